```python
import math
import jax, jax.numpy as jnp
from jax import lax
import numpy as np

D_MODEL = 1024
BATCH = 2
SEQ = 8192
DEPTH = 4
DEC_BATCH = 32
DEC_SEQ = 8
PAST_LEN = 8192
PAGE_SIZE = 128

N_A_LAYERS = DEPTH // 2
N_B_LAYERS = DEPTH - N_A_LAYERS
M_HEADS = 8
M_QK_DIM = D_MODEL // 2 // M_HEADS
M_V_DIM = D_MODEL // M_HEADS
M_CHUNK = 64
GATE_CAP = 15.0
N_HEADS = 16
HEAD_DIM = D_MODEL // N_HEADS
KV_HEADS = 4
HPG = N_HEADS // KV_HEADS
CMP_BLOCK = 32
CMP_STRIDE = 16
CMP_HIDDEN = 256
SEL_BLOCK = 64
N_SELECT = 16
WINDOW = 512
Q_BLOCK = 128
ROT_DIM = HEAD_DIM // 4
ROPE_THETA = 500000.0
N_KV_BRANCH = 6
D_FF = 2816
CONV_W = 3
EPS = 1e-6
BIG = 1e9
NEG = -1e30

kernel_name = 'yoco_mlstm_nsa_convffn_step'


def rmsnorm(x, g):
    xf = x.astype(jnp.float32)
    y = xf * lax.rsqrt(jnp.mean(xf * xf, axis=-1, keepdims=True) + EPS)
    return (y * g.astype(jnp.float32)).astype(x.dtype)


def rope_partial(x, pos):
    half = ROT_DIM // 2
    inv = jnp.power(jnp.float32(ROPE_THETA), -jnp.arange(0, ROT_DIM, 2, dtype=jnp.float32) / ROT_DIM)
    ang = pos.astype(jnp.float32)[:, None] * inv[None, :]
    cos = jnp.cos(ang)[None, :, None, :]
    sin = jnp.sin(ang)[None, :, None, :]
    xf = x.astype(jnp.float32)
    x1 = xf[..., :half]
    x2 = xf[..., half:ROT_DIM]
    out = jnp.concatenate([x1 * cos - x2 * sin, x2 * cos + x1 * sin, xf[..., ROT_DIM:]], axis=-1)
    return out.astype(x.dtype)


def soft_cap(a):
    return GATE_CAP * jnp.tanh(a / GATE_CAP)


def conv_ffn(xn, w_up, conv_w, conv_b, w_down, prev):
    T = xn.shape[1]
    u = xn @ w_up
    ue = jnp.concatenate([prev.astype(u.dtype), u], axis=1)
    c = conv_b
    for j in range(CONV_W):
        c = c + ue[:, j:j + T] * conv_w[j]
    a, g = jnp.split(c, 2, axis=-1)
    y = jax.nn.gelu(a, approximate=True) * g
    return y @ w_down, ue[:, -(CONV_W - 1):]


def mlstm_recurrence(q, k, v, i_log, logf, C0, n0, m0):
    B, T, H, _ = q.shape
    L = math.gcd(T, M_CHUNK)
    NC = T // L

    def chunks(a):
        return jnp.moveaxis(a.astype(jnp.float32).reshape((B, NC, L) + a.shape[2:]), 1, 0)

    xs = tuple(chunks(a) for a in (q, k, v, i_log, logf))
    causal = jnp.tril(jnp.ones((L, L), dtype=bool))

    def step(carry, inp):
        C, n, m = carry
        qq, kk, vv, ii, ff = inp
        bh = jnp.moveaxis(jnp.cumsum(ff, axis=1), 1, 2)
        ih = jnp.moveaxis(ii, 1, 2)
        D = bh[:, :, :, None] - bh[:, :, None, :] + ih[:, :, None, :]
        D = jnp.where(causal, D, -jnp.inf)
        g = bh + m[:, :, None]
        mt = jnp.maximum(g, jnp.max(D, axis=-1))
        w_inter = jnp.exp(g - mt)
        a = jnp.exp(D - mt[..., None]) * jnp.einsum('bthd,bshd->bhts', qq, kk)
        num = w_inter[..., None] * jnp.einsum('bthd,bhdv->bhtv', qq, C) + jnp.einsum('bhts,bshv->bhtv', a, vv)
        den = w_inter * jnp.einsum('bthd,bhd->bht', qq, n) + a.sum(-1)
        h = num / jnp.maximum(jnp.abs(den), jnp.exp(-mt))[..., None]
        m_new = mt[:, :, -1]
        b_last = bh[:, :, -1]
        ws = jnp.exp(b_last[..., None] - bh + ih - m_new[..., None])
        decay = jnp.exp(b_last + m - m_new)
        C_new = decay[..., None, None] * C + jnp.einsum('bhs,bshd,bshv->bhdv', ws, kk, vv)
        n_new = decay[..., None] * n + jnp.einsum('bhs,bshd->bhd', ws, kk)
        return (C_new, n_new, m_new), jnp.moveaxis(h, 1, 2)

    init = (C0.astype(jnp.float32), n0.astype(jnp.float32), m0.astype(jnp.float32))
    (C, n, m), hs = lax.scan(step, init, xs)
    h = jnp.moveaxis(hs, 0, 1).reshape(B, T, H, M_V_DIM)
    return h, C.astype(C0.dtype), n.astype(n0.dtype), m.astype(m0.dtype)


def mlstm_mixer(xn, w_in, b_gate, g_hnorm, w_out, C0, n0, m0):
    B, T, _ = xn.shape
    f32 = jnp.float32
    a_q = M_HEADS * M_QK_DIM
    a_v = M_HEADS * M_V_DIM
    proj = xn @ w_in
    q, k, v, i_pre, f_pre, o_pre = jnp.split(
        proj, [a_q, 2 * a_q, 2 * a_q + a_v, 2 * a_q + a_v + M_HEADS, 2 * a_q + a_v + 2 * M_HEADS], axis=-1)
    q = q.reshape(B, T, M_HEADS, M_QK_DIM) * (M_QK_DIM ** -0.5)
    k = k.reshape(B, T, M_HEADS, M_QK_DIM)
    v = v.reshape(B, T, M_HEADS, M_V_DIM)
    i_log = soft_cap(i_pre.astype(f32) + b_gate[:M_HEADS].astype(f32))
    logf = jax.nn.log_sigmoid(soft_cap(f_pre.astype(f32) + b_gate[M_HEADS:].astype(f32)))
    h, C, n, m = mlstm_recurrence(q, k, v, i_log, logf, C0, n0, m0)
    h = h * lax.rsqrt(jnp.mean(h * h, axis=-1, keepdims=True) + EPS)
    h = h.reshape(B, T, a_v) * g_hnorm.astype(f32) * jax.nn.sigmoid(o_pre.astype(f32))
    return h.astype(xn.dtype) @ w_out, C, n, m


def shared_kv_rows(x, pos, g_kv, w_kv):
    B, T, _ = x.shape
    kv = (rmsnorm(x, g_kv) @ w_kv).reshape(B, T, N_KV_BRANCH, KV_HEADS, HEAD_DIM)
    k_slc = rope_partial(kv[:, :, 2], pos)
    k_win = rope_partial(kv[:, :, 4], pos)
    return jnp.stack([kv[:, :, 0], kv[:, :, 1], k_slc, kv[:, :, 3], k_win, kv[:, :, 5]], axis=2)


def compress(k, pos_emb, w1, b1, w2, b2):
    B, Tp, G, hd = k.shape
    nb = Tp // CMP_STRIDE
    blk = k.reshape(B, nb, CMP_STRIDE, G, hd)
    w1r = w1.reshape(CMP_BLOCK, hd, CMP_HIDDEN)
    lo = jnp.einsum('bjrgd,rdh->bjgh', blk + pos_emb[None, None, :CMP_STRIDE, None, :], w1r[:CMP_STRIDE])
    hi = jnp.einsum('bjrgd,rdh->bjgh', blk + pos_emb[None, None, CMP_STRIDE:, None, :], w1r[CMP_STRIDE:])
    hi_next = jnp.concatenate([hi[:, 1:], jnp.zeros_like(hi[:, :1])], axis=1)
    h = jax.nn.gelu(lo + hi_next + b1, approximate=True)
    return h @ w2 + b2


def prepare_sparse(full, cmp_pos, cmp_w1, cmp_b1, cmp_w2, cmp_b2):
    B, Tk = full.shape[:2]
    Tp = -(-Tk // SEL_BLOCK) * SEL_BLOCK
    fp = jnp.pad(full, ((0, 0), (0, Tp - Tk), (0, 0), (0, 0), (0, 0)))
    kc = compress(fp[:, :, 0], cmp_pos[0], cmp_w1[0], cmp_b1[0], cmp_w2[0], cmp_b2[0])
    vc = compress(fp[:, :, 1], cmp_pos[1], cmp_w1[1], cmp_b1[1], cmp_w2[1], cmp_b2[1])
    nb = Tp // SEL_BLOCK
    blocks = fp[:, :, 2:4].reshape(B, nb, SEL_BLOCK, 2, KV_HEADS, HEAD_DIM)
    ks = jnp.transpose(blocks[:, :, :, 0], (0, 3, 1, 2, 4))
    vs = jnp.transpose(blocks[:, :, :, 1], (0, 3, 1, 2, 4))
    return kc, vc, ks, vs


def nsa_block(q_raw, q_rot, gates, q_pos, kc, vc, ks, vs, kw, vw, kw_pos):
    B, Q = q_raw.shape[:2]
    f32 = jnp.float32
    scale = HEAD_DIM ** -0.5
    qr = q_raw.reshape(B, Q, KV_HEADS, HPG, HEAD_DIM)
    qp = q_rot.reshape(B, Q, KV_HEADS, HPG, HEAD_DIM)
    n_cmp = kc.shape[1]
    cmp_end = jnp.arange(n_cmp) * CMP_STRIDE + (CMP_BLOCK - 1)
    valid = cmp_end[None, :] <= q_pos[:, None]
    s = jnp.where(valid, jnp.einsum('bqghd,bjgd->bghqj', qr, kc).astype(f32) * scale, NEG)
    e = jnp.where(valid, jnp.exp(s - jnp.max(s, axis=-1, keepdims=True)), 0.0)
    p_cmp = e / jnp.maximum(e.sum(-1, keepdims=True), 1e-30)
    o_cmp = jnp.einsum('bghqj,bjgd->bqghd', p_cmp.astype(vc.dtype), vc)
    nb = ks.shape[2]
    imp = p_cmp.sum(2).reshape(B, KV_HEADS, Q, nb, SEL_BLOCK // CMP_STRIDE)
    prev = jnp.concatenate([jnp.zeros_like(imp[..., :1, -1]), imp[..., :-1, -1]], axis=-1)
    score = imp.sum(-1) + prev
    blk = jnp.arange(nb)
    cur = q_pos // SEL_BLOCK
    forced = (blk[None, :] == 0) | (blk[None, :] == cur[:, None]) | (blk[None, :] == cur[:, None] - 1)
    blk_valid = blk[None, :] * SEL_BLOCK <= q_pos[:, None]
    score = jnp.where(forced, BIG, jnp.where(blk_valid, score, -BIG))
    n_sel = min(N_SELECT, nb)
    _, idx = lax.top_k(score, n_sel)
    bi = jnp.arange(B)[:, None, None, None]
    gi = jnp.arange(KV_HEADS)[None, :, None, None]
    ksel = ks[bi, gi, idx]
    vsel = vs[bi, gi, idx].reshape(B, KV_HEADS, Q, n_sel * SEL_BLOCK, HEAD_DIM)
    key_pos = idx[..., None] * SEL_BLOCK + jnp.arange(SEL_BLOCK)
    smask = (key_pos <= q_pos[None, None, :, None, None]).reshape(B, KV_HEADS, 1, Q, n_sel * SEL_BLOCK)
    s2 = jnp.einsum('bqghd,bgqnld->bghqnl', qp, ksel).astype(f32).reshape(B, KV_HEADS, HPG, Q, n_sel * SEL_BLOCK) * scale
    p2 = jax.nn.softmax(jnp.where(smask, s2, NEG), axis=-1)
    o_slc = jnp.einsum('bghqm,bgqmd->bqghd', p2.astype(vsel.dtype), vsel)
    wmask = (kw_pos[None, :] <= q_pos[:, None]) & (q_pos[:, None] - kw_pos[None, :] < WINDOW) & (kw_pos[None, :] >= 0)
    s3 = jnp.einsum('bqghd,blgd->bghql', qp, kw).astype(f32) * scale
    p3 = jax.nn.softmax(jnp.where(wmask, s3, NEG), axis=-1)
    o_win = jnp.einsum('bghql,blgd->bqghd', p3.astype(vw.dtype), vw)
    gt = gates.reshape(B, Q, KV_HEADS, HPG, 3)
    o = gt[..., 0:1] * o_cmp + gt[..., 1:2] * o_slc + gt[..., 2:3] * o_win
    return o.reshape(B, Q, N_HEADS * HEAD_DIM).astype(q_raw.dtype)


def attend_prompt(q_raw, q_rot, gates, kc, vc, ks, vs, kw, vw):
    B, T = q_raw.shape[:2]
    nq = T // Q_BLOCK
    kw_p = jnp.pad(kw, ((0, 0), (WINDOW, 0), (0, 0), (0, 0)))
    vw_p = jnp.pad(vw, ((0, 0), (WINDOW, 0), (0, 0), (0, 0)))

    def body(i):
        s = i * Q_BLOCK
        q_pos = s + jnp.arange(Q_BLOCK, dtype=jnp.int32)
        kwb = lax.dynamic_slice_in_dim(kw_p, s, WINDOW + Q_BLOCK, axis=1)
        vwb = lax.dynamic_slice_in_dim(vw_p, s, WINDOW + Q_BLOCK, axis=1)
        kw_pos = s - WINDOW + jnp.arange(WINDOW + Q_BLOCK, dtype=jnp.int32)
        return nsa_block(lax.dynamic_slice_in_dim(q_raw, s, Q_BLOCK, axis=1),
                         lax.dynamic_slice_in_dim(q_rot, s, Q_BLOCK, axis=1),
                         lax.dynamic_slice_in_dim(gates, s, Q_BLOCK, axis=1),
                         q_pos, kc, vc, ks, vs, kwb, vwb, kw_pos)

    out = lax.map(body, jnp.arange(nq, dtype=jnp.int32))
    return jnp.moveaxis(out, 0, 1).reshape(B, T, N_HEADS * HEAD_DIM)


def trunk(x, pos0, kv_past, win_past, conv_prev, mC, mn, mm,
          g_norms, w_a_in, b_a_gate, g_a_hnorm, w_a_out, g_kv, w_kv,
          cmp_pos, cmp_w1, cmp_b1, cmp_w2, cmp_b2, w_b_in, b_b_gate, w_b_out,
          w_ffn_up, ffn_conv_w, ffn_conv_b, w_ffn_down):
    B, T, _ = x.shape
    prompt = kv_past is None
    pos = pos0 + jnp.arange(T, dtype=jnp.int32)
    qd = N_HEADS * HEAD_DIM
    Cs, ns, ms, convs = [], [], [], []
    for layer in range(DEPTH):
        g = g_norms[layer]
        hn = rmsnorm(x, g[0])
        if layer < N_A_LAYERS:
            mix, C, n, m = mlstm_mixer(hn, w_a_in[layer], b_a_gate[layer], g_a_hnorm[layer], w_a_out[layer],
                                       mC[layer], mn[layer], mm[layer])
            Cs.append(C)
            ns.append(n)
            ms.append(m)
        else:
            if layer == N_A_LAYERS:
                new_kv = shared_kv_rows(x, pos, g_kv, w_kv)
                kv_rows = new_kv[:, :, :4]
                if prompt:
                    full = kv_rows
                    win_all = new_kv[:, :, 4:]
                    win_start = pos0
                else:
                    full = jnp.concatenate([kv_past.astype(kv_rows.dtype), kv_rows], axis=1)
                    win_all = jnp.concatenate([win_past.astype(kv_rows.dtype), new_kv[:, :, 4:]], axis=1)
                    win_start = pos0 - win_past.shape[1]
                kc, vc, ks, vs = prepare_sparse(full, cmp_pos, cmp_w1, cmp_b1, cmp_w2, cmp_b2)
                kw, vw = win_all[:, :, 0], win_all[:, :, 1]
                new_win = win_all[:, -WINDOW:]
            bl = layer - N_A_LAYERS
            proj = hn @ w_b_in[bl]
            q_raw = proj[..., :qd].reshape(B, T, N_HEADS, HEAD_DIM)
            gates = jax.nn.sigmoid(proj[..., qd:].astype(jnp.float32) + b_b_gate[bl].astype(jnp.float32)).reshape(B, T, N_HEADS, 3)
            q_rot = rope_partial(q_raw, pos)
            if prompt:
                o = attend_prompt(q_raw, q_rot, gates, kc, vc, ks, vs, kw, vw)
            else:
                kw_pos = win_start + jnp.arange(kw.shape[1], dtype=jnp.int32)
                o = nsa_block(q_raw, q_rot, gates, pos, kc, vc, ks, vs, kw, vw, kw_pos)
            mix = o @ w_b_out[bl]
        x = x + rmsnorm(mix, g[1])
        f, cp = conv_ffn(rmsnorm(x, g[2]), w_ffn_up[layer], ffn_conv_w[layer], ffn_conv_b[layer], w_ffn_down[layer], conv_prev[layer])
        convs.append(cp)
        x = x + rmsnorm(f, g[3])
    return x, kv_rows, new_win, jnp.stack(Cs), jnp.stack(ns), jnp.stack(ms), jnp.stack(convs)


def setup_inputs(seed: int = 0) -> dict:
    key = jax.random.key(seed)
    ks = jax.random.split(key, 40)
    f32 = jnp.float32

    def nrm(k, shape, scale):
        return jax.random.normal(k, shape, f32) * scale

    n_pages = PAST_LEN // PAGE_SIZE
    n_used = DEC_BATCH * n_pages
    n_pool = n_used + max(1, n_used // 4)
    win_len = min(WINDOW, PAST_LEN)
    a_q = M_HEADS * M_QK_DIM
    a_v = M_HEADS * M_V_DIM
    a_in = 2 * a_q + a_v + 2 * M_HEADS + a_v
    page_table = jax.random.permutation(ks[0], n_pool)[:n_used].astype(jnp.int32).reshape(DEC_BATCH, n_pages)
    return {
        'x_prompt': nrm(ks[1], (BATCH, SEQ, D_MODEL), 1.0),
        'x_sample': nrm(ks[2], (DEC_BATCH, DEC_SEQ, D_MODEL), 1.0),
        'cache_kv': nrm(ks[3], (n_pool, PAGE_SIZE, 4, KV_HEADS, HEAD_DIM), 1.0),
        'cache_win_kv': nrm(ks[4], (DEC_BATCH, win_len, 2, KV_HEADS, HEAD_DIM), 1.0),
        'state_mlstm_C': nrm(ks[5], (N_A_LAYERS, DEC_BATCH, M_HEADS, M_QK_DIM, M_V_DIM), 0.5),
        'state_mlstm_n': nrm(ks[6], (N_A_LAYERS, DEC_BATCH, M_HEADS, M_QK_DIM), 0.5),
        'state_mlstm_m': nrm(ks[7], (N_A_LAYERS, DEC_BATCH, M_HEADS), 1.0),
        'state_conv': nrm(ks[8], (DEPTH, DEC_BATCH, CONV_W - 1, 2 * D_FF), 1.0),
        'page_table': page_table,
        'g_norms': 1.0 + nrm(ks[9], (DEPTH, 4, D_MODEL), 0.05),
        'w_a_in': nrm(ks[10], (N_A_LAYERS, D_MODEL, a_in), D_MODEL ** -0.5),
        'b_a_gate': jnp.concatenate([nrm(ks[11], (N_A_LAYERS, M_HEADS), 0.1),
                                     3.0 + nrm(ks[12], (N_A_LAYERS, M_HEADS), 0.5)], axis=-1),
        'g_a_hnorm': 1.0 + nrm(ks[13], (N_A_LAYERS, a_v), 0.05),
        'w_a_out': nrm(ks[14], (N_A_LAYERS, a_v, D_MODEL), a_v ** -0.5),
        'g_kv': 1.0 + nrm(ks[15], (D_MODEL,), 0.05),
        'w_kv': nrm(ks[16], (D_MODEL, N_KV_BRANCH * KV_HEADS * HEAD_DIM), D_MODEL ** -0.5),
        'cmp_pos': nrm(ks[17], (2, CMP_BLOCK, HEAD_DIM), 0.1),
        'cmp_w1': nrm(ks[18], (2, CMP_BLOCK * HEAD_DIM, CMP_HIDDEN), (CMP_BLOCK * HEAD_DIM) ** -0.5),
        'cmp_b1': nrm(ks[19], (2, CMP_HIDDEN), 0.01),
        'cmp_w2': nrm(ks[20], (2, CMP_HIDDEN, HEAD_DIM), (CMP_HIDDEN / 2.0) ** -0.5),
        'cmp_b2': nrm(ks[21], (2, HEAD_DIM), 0.01),
        'w_b_in': nrm(ks[22], (N_B_LAYERS, D_MODEL, N_HEADS * HEAD_DIM + 3 * N_HEADS), D_MODEL ** -0.5),
        'b_b_gate': nrm(ks[23], (N_B_LAYERS, 3 * N_HEADS), 0.1),
        'w_b_out': nrm(ks[24], (N_B_LAYERS, N_HEADS * HEAD_DIM, D_MODEL), (N_HEADS * HEAD_DIM) ** -0.5),
        'w_ffn_up': nrm(ks[25], (DEPTH, D_MODEL, 2 * D_FF), D_MODEL ** -0.5),
        'ffn_conv_w': nrm(ks[26], (DEPTH, CONV_W, 2 * D_FF), CONV_W ** -0.5),
        'ffn_conv_b': nrm(ks[27], (DEPTH, 2 * D_FF), 0.01),
        'w_ffn_down': nrm(ks[28], (DEPTH, D_FF, D_MODEL), D_FF ** -0.5),
    }


def reference(x_prompt, x_sample, cache_kv, cache_win_kv, state_mlstm_C, state_mlstm_n, state_mlstm_m,
              state_conv, page_table, g_norms, w_a_in, b_a_gate, g_a_hnorm, w_a_out, g_kv, w_kv,
              cmp_pos, cmp_w1, cmp_b1, cmp_w2, cmp_b2, w_b_in, b_b_gate, w_b_out,
              w_ffn_up, ffn_conv_w, ffn_conv_b, w_ffn_down):
    dt = x_prompt.dtype
    zC = jnp.zeros((N_A_LAYERS, BATCH, M_HEADS, M_QK_DIM, M_V_DIM), dt)
    zn = jnp.zeros((N_A_LAYERS, BATCH, M_HEADS, M_QK_DIM), dt)
    zm = jnp.zeros((N_A_LAYERS, BATCH, M_HEADS), dt)
    zconv = jnp.zeros((DEPTH, BATCH, CONV_W - 1, 2 * D_FF), dt)
    y_p, kv_p, win_p, C_p, n_p, m_p, conv_p = trunk(
        x_prompt, 0, None, None, zconv, zC, zn, zm,
        g_norms, w_a_in, b_a_gate, g_a_hnorm, w_a_out, g_kv, w_kv,
        cmp_pos, cmp_w1, cmp_b1, cmp_w2, cmp_b2, w_b_in, b_b_gate, w_b_out,
        w_ffn_up, ffn_conv_w, ffn_conv_b, w_ffn_down)
    Bs = page_table.shape[0]
    past = cache_kv[page_table].reshape((Bs, page_table.shape[1] * PAGE_SIZE) + cache_kv.shape[2:])
    y_s, kv_s, win_s, C_s, n_s, m_s, conv_s = trunk(
        x_sample, PAST_LEN, past, cache_win_kv, state_conv, state_mlstm_C, state_mlstm_n, state_mlstm_m,
        g_norms, w_a_in, b_a_gate, g_a_hnorm, w_a_out, g_kv, w_kv,
        cmp_pos, cmp_w1, cmp_b1, cmp_w2, cmp_b2, w_b_in, b_b_gate, w_b_out,
        w_ffn_up, ffn_conv_w, ffn_conv_b, w_ffn_down)
    return (y_p, y_s, kv_p, kv_s, win_p, win_s, C_p, C_s, n_p, n_s, m_p, m_s, conv_p, conv_s)
```

```python
import functools
import math

import jax
import jax.numpy as jnp
from jax import lax
from jax.experimental import pallas as pl
from jax.experimental.pallas import tpu as pltpu

F32 = jnp.float32
BF16 = jnp.bfloat16
I32 = jnp.int32

D_MODEL = 1024
DEPTH = 4
N_A_LAYERS = 2
M_HEADS = 8
M_QK_DIM = 64
M_V_DIM = 128
GATE_CAP = 15.0
N_HEADS = 16
HEAD_DIM = 64
KV_HEADS = 4
HPG = 4
CMP_BLOCK = 32
CMP_STRIDE = 16
CMP_HIDDEN = 256
SEL_BLOCK = 64
N_SELECT = 16
WINDOW = 512
ROT_DIM = 16
ROPE_THETA = 500000.0
D_FF = 2816
CONV_W = 3
EPS = 1e-6
BIG = 1e9
NEG = -1e30
PAGE = 128

LANES = 128
SUBLANES = 8
VMEM_LIMIT = 56 * 1024 * 1024

MLSTM_CHUNK = 128
KEY_TILE = 512
Q_TILE = 128
SAMPLE_NQ = 32
CMP_PAGES = 8
ATT_PAGES = 8
FFN_TN = 256


def _cparams(sem):
    return pltpu.CompilerParams(dimension_semantics=sem, vmem_limit_bytes=VMEM_LIMIT)


def _rms(x, g):
    return x * lax.rsqrt(jnp.mean(x * x, axis=-1, keepdims=True) + EPS) * g


def _nt(a, b):
    return lax.dot_general(a, b, (((1,), (1,)), ((), ())), preferred_element_type=F32)


def _dot(a, b):
    return jnp.dot(a, b, preferred_element_type=F32)


def _row_tile(m, pref):
    t = min(m, pref)
    while m % t:
        t //= 2
    return t


def _norm_proj_kernel(x_ref, g_ref, *refs, n_w):
    xn = _rms(x_ref[...], g_ref[...]).astype(BF16)
    for w_ref, o_ref in zip(refs[:n_w], refs[n_w:]):
        o_ref[...] = _dot(xn, w_ref[...]).astype(o_ref.dtype)


def _norm_proj(x, g, ws, name):
    m, d = x.shape
    tm = _row_tile(m, 512)
    n_w = len(ws)
    in_specs = [pl.BlockSpec((tm, d), lambda i: (i, 0)), pl.BlockSpec((1, d), lambda i: (0, 0))]
    in_specs += [pl.BlockSpec(w.shape, lambda i: (0, 0)) for w in ws]
    out_specs = [pl.BlockSpec((tm, w.shape[1]), lambda i: (i, 0)) for w in ws]
    out_shape = [jax.ShapeDtypeStruct((m, w.shape[1]), F32) for w in ws]
    return pl.pallas_call(
        functools.partial(_norm_proj_kernel, n_w=n_w), grid=(m // tm,), in_specs=in_specs, out_specs=out_specs,
        out_shape=out_shape, compiler_params=_cparams(("parallel",)), name=name)(x, g.reshape(1, d), *ws)


def _proj_norm_res_kernel(a_ref, w_ref, g_ref, res_ref, o_ref):
    y = _dot(a_ref[...].astype(BF16), w_ref[...])
    o_ref[...] = res_ref[...] + _rms(y, g_ref[...])


def _proj_norm_res(a, w, g, res, name):
    m, k = a.shape
    d = w.shape[1]
    tm = _row_tile(m, 512)
    return pl.pallas_call(
        _proj_norm_res_kernel, grid=(m // tm,),
        in_specs=[pl.BlockSpec((tm, k), lambda i: (i, 0)), pl.BlockSpec((k, d), lambda i: (0, 0)),
                  pl.BlockSpec((1, d), lambda i: (0, 0)), pl.BlockSpec((tm, d), lambda i: (i, 0))],
        out_specs=pl.BlockSpec((tm, d), lambda i: (i, 0)),
        out_shape=jax.ShapeDtypeStruct((m, d), F32),
        compiler_params=_cparams(("parallel",)), name=name)(a, w, g.reshape(1, d), res)


def _mlstm_kernel(main_ref, gate_ref, bg_ref, gh_ref, c0_ref, n0_ref, m0_ref,
                  h_ref, c_ref, n_ref, m_ref, c_s, n_s, m_s, *, rows, nc):
    L = MLSTM_CHUNK
    cidx = pl.program_id(1)

    @pl.when(cidx == 0)
    def _():
        c_s[...] = c0_ref[...]
        n_s[...] = n0_ref[...]
        m_s[...] = m0_ref[...]

    main = main_ref[...]
    gp = gate_ref[...] + bg_ref[...]
    if rows < L:
        main = jnp.concatenate([main, jnp.zeros((L - rows, main.shape[1]), F32)], axis=0)
        gp = jnp.concatenate([gp, jnp.zeros((L - rows, LANES), F32)], axis=0)
    capped = GATE_CAP * jnp.tanh(gp / GATE_CAP)
    row1 = lax.broadcasted_iota(I32, (L, LANES), 0)
    real = row1 < rows
    ilog = jnp.where(real, capped, -jnp.inf)
    logf = jnp.where(real, jnp.minimum(capped, 0.0) - jnp.log1p(jnp.exp(-jnp.abs(capped))), 0.0)
    bh = logf
    k = 1
    while k < L:
        bh = bh + jnp.where(row1 >= k, pltpu.roll(bh, k, axis=0), 0.0)
        k *= 2

    rr = lax.broadcasted_iota(I32, (L, L), 0)
    cc = lax.broadcasted_iota(I32, (L, L), 1)
    eye = rr == cc
    causal = cc <= rr
    lane = lax.broadcasted_iota(I32, (L, LANES), 1)
    lo_half = lane < M_QK_DIM

    def to_row(col):
        return jnp.sum(jnp.where(eye, col, 0.0), axis=0, keepdims=True)

    for p in range(M_HEADS // 2):
        qp = main[:, p * LANES:(p + 1) * LANES] * (M_QK_DIM ** -0.5)
        kp = main[:, 512 + p * LANES:512 + (p + 1) * LANES]
        c_pair = c_s[2 * p:2 * p + 2].reshape(2 * M_QK_DIM, M_V_DIM)
        n_pair = n_s[p:p + 1, :]
        kp_b = kp.astype(BF16)
        c_b = c_pair.astype(BF16)
        ws_cols, decays, m_news = [], [], []
        for e in range(2):
            x = 2 * p + e
            qx = jnp.where(lo_half if e == 0 else jnp.logical_not(lo_half), qp, 0.0)
            vx = main[:, 1024 + x * M_V_DIM:1024 + (x + 1) * M_V_DIM]
            ox = main[:, 2048 + x * M_V_DIM:2048 + (x + 1) * M_V_DIM]
            bh_c = bh[:, M_HEADS + x:M_HEADS + x + 1]
            ii_c = ilog[:, x:x + 1]
            m_old = m_s[x:x + 1, 0:1]
            c_r = to_row(ii_c - bh_c)
            dmat = jnp.where(causal, bh_c + c_r, -jnp.inf)
            gcol = bh_c + m_old
            mt = jnp.maximum(gcol, jnp.max(dmat, axis=-1, keepdims=True))
            w_inter = jnp.exp(gcol - mt)
            qx_b = qx.astype(BF16)
            a = jnp.exp(dmat - mt) * _nt(qx_b, kp_b)
            num = w_inter * _dot(qx_b, c_b) + _dot(a.astype(BF16), vx.astype(BF16))
            den = w_inter * jnp.sum(qx * n_pair, axis=-1, keepdims=True) + jnp.sum(a, axis=-1, keepdims=True)
            h = num / jnp.maximum(jnp.abs(den), jnp.exp(-mt))
            m_new = mt[L - 1:L, :]
            b_last = bh_c[L - 1:L, :]
            ws_cols.append(jnp.exp(b_last - bh_c + ii_c - m_new))
            decays.append(jnp.exp(b_last + m_old - m_new))
            m_news.append(m_new)
            hn = h * lax.rsqrt(jnp.mean(h * h, axis=-1, keepdims=True) + EPS)
            hn = hn * gh_ref[:, x * M_V_DIM:(x + 1) * M_V_DIM] * jax.nn.sigmoid(ox)
            h_ref[:, x * M_V_DIM:(x + 1) * M_V_DIM] = hn[:rows]
        kw = kp * jnp.where(lo_half, ws_cols[0], ws_cols[1])
        vcat = main[:, 1024 + 2 * p * M_V_DIM:1024 + (2 * p + 2) * M_V_DIM]
        upd = lax.dot_general(kw.astype(BF16), vcat.astype(BF16), (((0,), (0,)), ((), ())),
                              preferred_element_type=F32)
        c_s[2 * p] = decays[0] * c_pair[:M_QK_DIM] + upd[:M_QK_DIM, :M_V_DIM]
        c_s[2 * p + 1] = decays[1] * c_pair[M_QK_DIM:] + upd[M_QK_DIM:, M_V_DIM:]
        dec_l = jnp.where(lo_half[0:1], decays[0], decays[1])
        n_s[p:p + 1, :] = dec_l * n_pair + jnp.sum(kw, axis=0, keepdims=True)
        m_s[2 * p:2 * p + 1, :] = jnp.broadcast_to(m_news[0], (1, LANES))
        m_s[2 * p + 1:2 * p + 2, :] = jnp.broadcast_to(m_news[1], (1, LANES))

    @pl.when(cidx == nc - 1)
    def _():
        c_ref[...] = c_s[...]
        n_ref[...] = n_s[...]
        m_ref[...] = m_s[...]


def _mlstm(main, gate, b_gate, g_hnorm, c0, n0, m0, batch, seq, name):
    rows = min(seq, MLSTM_CHUNK)
    nc = seq // rows
    a_v = M_HEADS * M_V_DIM
    n_in = n0.reshape(batch, M_HEADS // 2, LANES)
    m_in = jnp.broadcast_to(m0[:, :, None], (batch, M_HEADS, LANES))
    h, c, n, m = pl.pallas_call(
        functools.partial(_mlstm_kernel, rows=rows, nc=nc), grid=(batch, nc),
        in_specs=[pl.BlockSpec((rows, main.shape[1]), lambda b, c: (b * nc + c, 0)),
                  pl.BlockSpec((rows, LANES), lambda b, c: (b * nc + c, 0)),
                  pl.BlockSpec((1, LANES), lambda b, c: (0, 0)),
                  pl.BlockSpec((1, a_v), lambda b, c: (0, 0)),
                  pl.BlockSpec((None, M_HEADS, M_QK_DIM, M_V_DIM), lambda b, c: (b, 0, 0, 0)),
                  pl.BlockSpec((None, M_HEADS // 2, LANES), lambda b, c: (b, 0, 0)),
                  pl.BlockSpec((None, M_HEADS, LANES), lambda b, c: (b, 0, 0))],
        out_specs=[pl.BlockSpec((rows, a_v), lambda b, c: (b * nc + c, 0)),
                   pl.BlockSpec((None, M_HEADS, M_QK_DIM, M_V_DIM), lambda b, c: (b, 0, 0, 0)),
                   pl.BlockSpec((None, M_HEADS // 2, LANES), lambda b, c: (b, 0, 0)),
                   pl.BlockSpec((None, M_HEADS, LANES), lambda b, c: (b, 0, 0))],
        out_shape=[jax.ShapeDtypeStruct((batch * seq, a_v), F32),
                   jax.ShapeDtypeStruct((batch, M_HEADS, M_QK_DIM, M_V_DIM), F32),
                   jax.ShapeDtypeStruct((batch, M_HEADS // 2, LANES), F32),
                   jax.ShapeDtypeStruct((batch, M_HEADS, LANES), F32)],
        scratch_shapes=[pltpu.VMEM((M_HEADS, M_QK_DIM, M_V_DIM), F32), pltpu.VMEM((M_HEADS // 2, LANES), F32),
                        pltpu.VMEM((M_HEADS, LANES), F32)],
        compiler_params=_cparams(("parallel", "arbitrary")), name=name,
    )(main, gate, b_gate, g_hnorm.reshape(1, a_v), c0, n_in, m_in)
    return h, c, n.reshape(batch, M_HEADS, M_QK_DIM), m[:, :, 0]


def _ffn_kernel(*refs, carry, tm, n_j, tiles_per_seq, period):
    x_ref, g2_ref, wua_ref, wug_ref, cwa_ref, cwg_ref, cba_ref, cbg_ref, wd_ref, g3_ref = refs[:10]
    if carry:
        inita_ref, initg_ref, o_ref, sa_ref, sg_ref, xn_s, acc_s, ue_s, carry_s = refs[10:]
        branch_in = ((wua_ref, cwa_ref, cba_ref, inita_ref, sa_ref), (wug_ref, cwg_ref, cbg_ref, initg_ref, sg_ref))
    else:
        t1a_ref, t1g_ref, t2a_ref, t2g_ref, o_ref, sa_ref, sg_ref, xn_s, acc_s, ue_s = refs[10:]
        branch_in = ((wua_ref, cwa_ref, cba_ref, (t1a_ref, t2a_ref), sa_ref),
                     (wug_ref, cwg_ref, cbg_ref, (t1g_ref, t2g_ref), sg_ref))
    i = pl.program_id(0)
    j = pl.program_id(1)
    tn = wd_ref.shape[0]

    @pl.when(j == 0)
    def _():
        xn_s[...] = _rms(x_ref[...], g2_ref[...]).astype(BF16)
        acc_s[...] = jnp.zeros_like(acc_s)

    xn = xn_s[...]
    conv = []
    for which, (w_ref, cw_ref, cb_ref, boundary, s_ref) in enumerate(branch_in):
        u = _dot(xn, w_ref[...])
        ue_s[SUBLANES:, :] = u
        if carry:
            slot = which * n_j + j
            first = (i % tiles_per_seq) == 0
            ue_s[SUBLANES - 2:SUBLANES, :] = jnp.where(first, boundary[...], carry_s[slot])
            tap1 = ue_s[pl.ds(SUBLANES - 1, tm), :]
            tap2 = ue_s[pl.ds(SUBLANES - 2, tm), :]
            last2 = u[tm - 2:tm, :]
            carry_s[slot] = last2

            @pl.when((i % tiles_per_seq) == tiles_per_seq - 1)
            def _(s_ref=s_ref, last2=last2):
                s_ref[i // tiles_per_seq, j] = last2
        else:
            ue_s[0:SUBLANES, :] = jnp.zeros((SUBLANES, tn), F32)
            t = lax.broadcasted_iota(I32, (tm, tn), 0) & (period - 1)
            tap1 = jnp.where(t >= 1, ue_s[pl.ds(SUBLANES - 1, tm), :], boundary[0][...])
            tap2 = jnp.where(t >= 2, ue_s[pl.ds(SUBLANES - 2, tm), :], boundary[1][...])
            s_ref[...] = u
        conv.append(cb_ref[...] + tap2 * cw_ref[0:1, :] + tap1 * cw_ref[1:2, :] + u * cw_ref[2:3, :])
    y = jax.nn.gelu(conv[0], approximate=True) * conv[1]
    acc_s[...] += _dot(y.astype(BF16), wd_ref[...])

    @pl.when(j == n_j - 1)
    def _():
        o_ref[...] = x_ref[...] + _rms(acc_s[...], g3_ref[...])


def _ffn(x, g2, w_up, conv_w, conv_b, w_down, g3, prev, batch, seq, name):
    m, d = x.shape
    nf = w_down.shape[0]
    tn = FFN_TN
    n_j = nf // tn
    carry = seq >= 256
    cb = conv_b.reshape(1, 2 * nf)
    col_a = lambda i, j: (0, j)
    col_g = lambda i, j: (0, j + n_j)
    common_specs = [
        None,
        pl.BlockSpec((1, d), lambda i, j: (0, 0)),
        pl.BlockSpec((d, tn), col_a), pl.BlockSpec((d, tn), col_g),
        pl.BlockSpec((CONV_W, tn), col_a), pl.BlockSpec((CONV_W, tn), col_g),
        pl.BlockSpec((1, tn), col_a), pl.BlockSpec((1, tn), col_g),
        pl.BlockSpec((tn, d), lambda i, j: (j, 0)),
        pl.BlockSpec((1, d), lambda i, j: (0, 0)),
    ]
    common_args = [x, g2.reshape(1, d), w_up, w_up, conv_w, conv_w, cb, cb, w_down, g3.reshape(1, d)]
    if carry:
        tm = _row_tile(seq, 1024)
        tps = seq // tm
        common_specs[0] = pl.BlockSpec((tm, d), lambda i, j: (i, 0))
        st_a = lambda i, j: (i // tps, 0, j)
        st_g = lambda i, j: (i // tps, 0, j + n_j)
        out, sa, sg = pl.pallas_call(
            functools.partial(_ffn_kernel, carry=True, tm=tm, n_j=n_j, tiles_per_seq=tps, period=seq),
            grid=(m // tm, n_j),
            in_specs=common_specs + [pl.BlockSpec((None, 2, tn), st_a), pl.BlockSpec((None, 2, tn), st_g)],
            out_specs=[pl.BlockSpec((tm, d), lambda i, j: (i, 0)),
                       pl.BlockSpec((batch, n_j, 2, tn), lambda i, j: (0, 0, 0, 0)),
                       pl.BlockSpec((batch, n_j, 2, tn), lambda i, j: (0, 0, 0, 0))],
            out_shape=[jax.ShapeDtypeStruct((m, d), F32), jax.ShapeDtypeStruct((batch, n_j, 2, tn), F32),
                       jax.ShapeDtypeStruct((batch, n_j, 2, tn), F32)],
            scratch_shapes=[pltpu.VMEM((tm, d), BF16), pltpu.VMEM((tm, d), F32),
                            pltpu.VMEM((tm + SUBLANES, tn), F32), pltpu.VMEM((2 * n_j, 2, tn), F32)],
            compiler_params=_cparams(("arbitrary", "arbitrary")), name=name,
        )(*common_args, prev, prev)
        sa, sg = (s.transpose(0, 2, 1, 3).reshape(batch, 2, nf) for s in (sa, sg))
        return out, jnp.concatenate([sa, sg], axis=-1)
    tm = m
    assert seq >= 2 and seq & (seq - 1) == 0
    common_specs[0] = pl.BlockSpec((tm, d), lambda i, j: (i, 0))
    tap1 = jnp.pad(prev[:, 1:2], ((0, 0), (0, seq - 1), (0, 0))).reshape(m, 2 * nf)
    tap2 = jnp.pad(prev, ((0, 0), (0, seq - 2), (0, 0))).reshape(m, 2 * nf)
    row_a = lambda i, j: (i, j)
    row_g = lambda i, j: (i, j + n_j)
    out, ua, ug = pl.pallas_call(
        functools.partial(_ffn_kernel, carry=False, tm=tm, n_j=n_j, tiles_per_seq=1, period=seq),
        grid=(m // tm, n_j),
        in_specs=common_specs + [pl.BlockSpec((tm, tn), row_a), pl.BlockSpec((tm, tn), row_g),
                                 pl.BlockSpec((tm, tn), row_a), pl.BlockSpec((tm, tn), row_g)],
        out_specs=[pl.BlockSpec((tm, d), lambda i, j: (i, 0)), pl.BlockSpec((tm, tn), row_a),
                   pl.BlockSpec((tm, tn), row_a)],
        out_shape=[jax.ShapeDtypeStruct((m, d), F32), jax.ShapeDtypeStruct((m, nf), F32),
                   jax.ShapeDtypeStruct((m, nf), F32)],
        scratch_shapes=[pltpu.VMEM((tm, d), BF16), pltpu.VMEM((tm, d), F32), pltpu.VMEM((tm + SUBLANES, tn), F32)],
        compiler_params=_cparams(("arbitrary", "arbitrary")), name=name,
    )(*common_args, tap1, tap1, tap2, tap2)
    u = jnp.concatenate([ua, ug], axis=-1).reshape(batch, seq, 2 * nf)
    return out, u[:, seq - 2:]


def _rope_pair(x, cos, sin):
    half = ROT_DIM // 2
    lane = lax.broadcasted_iota(I32, x.shape, 1) & (HEAD_DIM - 1)
    partner = jnp.where(lane < half, pltpu.roll(x, LANES - half, axis=1), pltpu.roll(x, half, axis=1))
    return x * cos + partner * sin


def _rope(x, cos, sin):
    return jnp.concatenate(
        [_rope_pair(x[:, c * LANES:(c + 1) * LANES], cos, sin) for c in range(x.shape[1] // LANES)], axis=1)


def _rope_tables(pos0, seq):
    half = ROT_DIM // 2
    inv = jnp.power(jnp.float32(ROPE_THETA), -jnp.arange(0, ROT_DIM, 2, dtype=F32) / ROT_DIM)
    ang = (pos0 + jnp.arange(seq, dtype=I32)).astype(F32)[:, None] * inv[None, :]
    cos, sin = jnp.cos(ang), jnp.sin(ang)
    rest = HEAD_DIM - ROT_DIM
    cos_h = jnp.concatenate([cos, cos, jnp.ones((seq, rest), F32)], axis=1)
    sin_h = jnp.concatenate([-sin, sin, jnp.zeros((seq, rest), F32)], axis=1)
    return jnp.tile(cos_h, (1, LANES // HEAD_DIM)), jnp.tile(sin_h, (1, LANES // HEAD_DIM))


def _kv_kernel(x_ref, g_ref, w_ref, wvt_ref, cos_ref, sin_ref, kv_ref, win_ref, ks_ref, kw_ref, vst_ref, vwt_ref):
    xn = _rms(x_ref[...], g_ref[...]).astype(BF16)
    y = _dot(xn, w_ref[...])
    cos, sin = cos_ref[...], sin_ref[...]
    gw = KV_HEADS * HEAD_DIM
    ks = _rope(y[:, 2 * gw:3 * gw], cos, sin)
    kw = _rope(y[:, 4 * gw:5 * gw], cos, sin)
    kv_ref[:, 0:2 * gw] = y[:, 0:2 * gw]
    kv_ref[:, 2 * gw:3 * gw] = ks
    kv_ref[:, 3 * gw:4 * gw] = y[:, 3 * gw:4 * gw]
    win_ref[:, 0:gw] = kw
    win_ref[:, gw:2 * gw] = y[:, 5 * gw:6 * gw]
    ks_ref[...] = ks.astype(BF16)
    kw_ref[...] = kw.astype(BF16)
    vt = _nt(wvt_ref[...], xn).astype(BF16)
    vst_ref[...] = vt[0:gw]
    for u in range(vwt_ref.shape[0]):
        vwt_ref[u] = vt[gw:2 * gw, u * PAGE:(u + 1) * PAGE]


def _kv_rows(x, g_kv, w_kv, w_vt, cos, sin, name):
    m, d = x.shape
    tm = _row_tile(m, KEY_TILE)
    gw = KV_HEADS * HEAD_DIM
    ttiles = cos.shape[0] // tm
    return pl.pallas_call(
        _kv_kernel, grid=(m // tm,),
        in_specs=[pl.BlockSpec((tm, d), lambda i: (i, 0)), pl.BlockSpec((1, d), lambda i: (0, 0)),
                  pl.BlockSpec(w_kv.shape, lambda i: (0, 0)), pl.BlockSpec(w_vt.shape, lambda i: (0, 0)),
                  pl.BlockSpec((tm, LANES), lambda i: (i % ttiles, 0)),
                  pl.BlockSpec((tm, LANES), lambda i: (i % ttiles, 0))],
        out_specs=[pl.BlockSpec((tm, 4 * gw), lambda i: (i, 0)), pl.BlockSpec((tm, 2 * gw), lambda i: (i, 0)),
                   pl.BlockSpec((tm, gw), lambda i: (i, 0)), pl.BlockSpec((tm, gw), lambda i: (i, 0)),
                   pl.BlockSpec((None, gw, tm), lambda i: (i, 0, 0)),
                   pl.BlockSpec((tm // PAGE, gw, PAGE), lambda i: (i, 0, 0))],
        out_shape=[jax.ShapeDtypeStruct((m, 4 * gw), F32), jax.ShapeDtypeStruct((m, 2 * gw), F32),
                   jax.ShapeDtypeStruct((m, gw), BF16), jax.ShapeDtypeStruct((m, gw), BF16),
                   jax.ShapeDtypeStruct((m // tm, gw, tm), BF16), jax.ShapeDtypeStruct((m // PAGE, gw, PAGE), BF16)],
        compiler_params=_cparams(("parallel",)), name=name)(x, g_kv.reshape(1, d), w_kv, w_vt, cos, sin)


def _compress_kernel(tbl_ref, *refs, n_pg, rows, zero_after):
    page_refs = refs[:n_pg]
    (pos_lo_ref, pos_hi_ref, w1lo_ref, w1hi_ref, b1_ref, w2_ref, b2_ref, hi_init_ref,
     kc_ref, vc_ref, hi_first_ref, x_s, carry_s) = refs[n_pg:]
    c = pl.program_id(1)
    njp = rows // CMP_STRIDE
    njc = n_pg * njp
    n_bg = 2 * KV_HEADS

    @pl.when(c == 0)
    def _():
        carry_s[...] = hi_init_ref[...]

    for u in range(n_pg):
        y = page_refs[u][...].reshape(njp, CMP_STRIDE, n_bg * HEAD_DIM)
        for r in range(CMP_STRIDE):
            piece = y[:, r, :]
            for bg in range(n_bg):
                x_s[bg, u * njp:(u + 1) * njp, r * HEAD_DIM:(r + 1) * HEAD_DIM] = piece[:, bg * HEAD_DIM:(bg + 1) * HEAD_DIM]
    rowl = lax.broadcasted_iota(I32, (KV_HEADS * njc, CMP_HIDDEN), 0) & (njc - 1)
    for br, out_ref in enumerate((kc_ref, vc_ref)):
        x = x_s[br * KV_HEADS:(br + 1) * KV_HEADS].reshape(KV_HEADS * njc, CMP_STRIDE * HEAD_DIM)
        lo = _dot((x + pos_lo_ref[br]).astype(BF16), w1lo_ref[br])
        hi = _dot((x + pos_hi_ref[br]).astype(BF16), w1hi_ref[br])
        carry_rows = jnp.concatenate(
            [jnp.broadcast_to(carry_s[br * KV_HEADS + g:br * KV_HEADS + g + 1, :], (njc, CMP_HIDDEN))
             for g in range(KV_HEADS)], axis=0)
        hi_next = jnp.where(rowl == njc - 1, carry_rows, pltpu.roll(hi, KV_HEADS * njc - 1, axis=0))
        if zero_after is not None:
            hi_next = jnp.where(jnp.logical_and(rowl == zero_after, c == 0), 0.0, hi_next)
        for g in range(KV_HEADS):
            carry_s[br * KV_HEADS + g:br * KV_HEADS + g + 1, :] = hi[g * njc:g * njc + 1, :]
        h = jax.nn.gelu(lo + hi_next + b1_ref[br], approximate=True)
        o = _dot(h.astype(BF16), w2_ref[br]) + b2_ref[br]
        for g in range(KV_HEADS):
            out_ref[:, g * HEAD_DIM:(g + 1) * HEAD_DIM] = o[g * njc:(g + 1) * njc, :]
    hi_first_ref[...] = carry_s[...]


def _compress(cache3, table, batch, n_pages, n_pg, hi_init, cw, zero_after, name):
    rows = cache3.shape[1]
    njp = rows // CMP_STRIDE
    njc = n_pg * njp
    assert njc & (njc - 1) == 0 and njc % SUBLANES == 0 and n_pages % n_pg == 0
    n_ch = n_pages // n_pg
    gw = KV_HEADS * HEAD_DIM
    flat = CMP_STRIDE * HEAD_DIM

    def page_map(u):
        return lambda b, c, tbl: (tbl[b * n_pages + (n_ch - 1 - c) * n_pg + u], 0, 0)

    full3 = lambda b, c, tbl: (0, 0, 0)
    in_specs = [pl.BlockSpec((None, rows, 2 * gw), page_map(u)) for u in range(n_pg)]
    in_specs += [pl.BlockSpec((2, 1, flat), full3), pl.BlockSpec((2, 1, flat), full3),
                 pl.BlockSpec((2, flat, CMP_HIDDEN), full3), pl.BlockSpec((2, flat, CMP_HIDDEN), full3),
                 pl.BlockSpec((2, 1, CMP_HIDDEN), full3), pl.BlockSpec((2, CMP_HIDDEN, HEAD_DIM), full3),
                 pl.BlockSpec((2, 1, HEAD_DIM), full3),
                 pl.BlockSpec((None, 2 * KV_HEADS, CMP_HIDDEN), lambda b, c, tbl: (b, 0, 0))]
    out_map = lambda b, c, tbl: (b, n_ch - 1 - c, 0)
    grid_spec = pltpu.PrefetchScalarGridSpec(
        num_scalar_prefetch=1, grid=(batch, n_ch), in_specs=in_specs,
        out_specs=[pl.BlockSpec((None, njc, gw), out_map), pl.BlockSpec((None, njc, gw), out_map),
                   pl.BlockSpec((None, 2 * KV_HEADS, CMP_HIDDEN), lambda b, c, tbl: (b, 0, 0))],
        scratch_shapes=[pltpu.VMEM((2 * KV_HEADS, njc, flat), F32), pltpu.VMEM((2 * KV_HEADS, CMP_HIDDEN), F32)])
    return pl.pallas_call(
        functools.partial(_compress_kernel, n_pg=n_pg, rows=rows, zero_after=zero_after), grid_spec=grid_spec,
        out_shape=[jax.ShapeDtypeStruct((batch, n_ch * njc, gw), F32), jax.ShapeDtypeStruct((batch, n_ch * njc, gw), F32),
                   jax.ShapeDtypeStruct((batch, 2 * KV_HEADS, CMP_HIDDEN), F32)],
        compiler_params=_cparams(("parallel", "arbitrary")), name=name,
    )(table, *([cache3] * n_pg), cw["pos_lo"], cw["pos_hi"], cw["w1lo"], cw["w1hi"], cw["b1"], cw["w2"], cw["b2"], hi_init)


def _cmp_layout(kc_nat, nb, nbp):
    b = kc_nat.shape[0]
    x = kc_nat[:, :4 * nb].reshape(b, nb, 4, kc_nat.shape[-1]).transpose(0, 2, 1, 3)
    x = jnp.pad(x, ((0, 0), (0, 0), (0, nbp - nb), (0, 0)))
    return x.reshape(b, 4 * nbp, kc_nat.shape[-1]).astype(BF16)


SCALE = HEAD_DIM ** -0.5
QSCALE = SCALE * math.log2(math.e)


def _group_queries(q, g):
    nq = q.shape[0]
    keep = (lax.broadcasted_iota(I32, (nq, LANES), 1) >> 6) == (g % 2)
    pieces = []
    for hh in range(HPG):
        h = HPG * g + hh
        chunk = q[:, (h // 2) * LANES:(h // 2 + 1) * LANES]
        if h % 2 != g % 2:
            chunk = pltpu.roll(chunk, HEAD_DIM, axis=1)
        pieces.append(jnp.where(keep, chunk, 0.0))
    return (jnp.concatenate(pieces, axis=0) * QSCALE).astype(BF16)


def _kchunk(k, g):
    return k[:, (g // 2) * LANES:(g // 2 + 1) * LANES]


def _flash(s, vt, st, g):
    m_s, l_s, acc_s = st
    m_old = m_s[g]
    m_new = jnp.maximum(m_old, jnp.max(s, axis=0, keepdims=True))
    alpha = jnp.exp2(m_old - m_new)
    p = jnp.exp2(s - m_new)
    l_s[g] = l_s[g] * alpha + jnp.sum(p, axis=0, keepdims=True)
    acc_s[g] = acc_s[g] * alpha + _dot(vt, p.astype(BF16))
    m_s[g] = m_new


def _one_shot(s, vt):
    p = jnp.exp2(s - jnp.max(s, axis=0, keepdims=True))
    return _dot(vt, p.astype(BF16)), jnp.sum(p, axis=0, keepdims=True)


def _init_state(st):
    m_s, l_s, acc_s = st
    m_s[...] = jnp.full(m_s.shape, NEG, F32)
    l_s[...] = jnp.zeros(l_s.shape, F32)
    acc_s[...] = jnp.zeros(acc_s.shape, F32)


def _topk_bias(score, n_sel):
    n_iota = lax.broadcasted_iota(I32, score.shape, 0)

    def body(_, carry):
        sc, bias = carry
        mx = jnp.max(sc, axis=0, keepdims=True)
        idx = jnp.min(jnp.where(sc == mx, n_iota, score.shape[0]), axis=0, keepdims=True)
        hit = n_iota == idx
        return jnp.where(hit, -jnp.inf, sc), jnp.where(hit, 0.0, bias)

    return lax.fori_loop(0, n_sel, body, (score, jnp.full(score.shape, NEG, F32)))[1]


def _cmp_branch(kc_ref, vct_ref, qz_raw, g, qpos_w, qpos_q, nb, nbp, nq):
    w = qz_raw.shape[0]
    s = _nt(_kchunk(kc_ref[...], g), qz_raw)
    n_w = lax.broadcasted_iota(I32, (nbp, w), 0)
    valid = jnp.concatenate(
        [jnp.logical_and((4 * n_w + c) * CMP_STRIDE + (CMP_BLOCK - 1) <= qpos_w, n_w < nb) for c in range(4)], axis=0)
    s = jnp.where(valid, s, NEG)
    e = jnp.where(valid, jnp.exp2(s - jnp.max(s, axis=0, keepdims=True)), 0.0)
    p = e / jnp.maximum(jnp.sum(e, axis=0, keepdims=True), 1e-30)
    o_cmp = _dot(vct_ref[g * HEAD_DIM:(g + 1) * HEAD_DIM, :], p.astype(BF16))
    if nq == LANES:
        ps = ((p[:, 0:LANES] + p[:, LANES:2 * LANES]) + p[:, 2 * LANES:3 * LANES]) + p[:, 3 * LANES:4 * LANES]
    else:
        assert w == LANES
        ps = ((p + pltpu.roll(p, nq, axis=1)) + pltpu.roll(p, 2 * nq, axis=1)) + pltpu.roll(p, 3 * nq, axis=1)
    parts = [ps[c * nbp:(c + 1) * nbp] for c in range(4)]
    n_q = lax.broadcasted_iota(I32, (nbp, LANES), 0)
    prev = jnp.where(n_q >= 1, pltpu.roll(parts[3], 1, axis=0), 0.0)
    score = (((parts[0] + parts[1]) + parts[2]) + parts[3]) + prev
    cur = qpos_q >> 6
    forced = (n_q == 0) | (n_q == cur) | (n_q == cur - 1)
    score = jnp.where(forced, BIG, jnp.where(n_q * SEL_BLOCK <= qpos_q, score, -BIG))
    score = jnp.where(n_q < nb, score, -jnp.inf)
    return o_cmp, score


def _select(score_s, bias_s, nb, w):
    bias = _topk_bias(score_s[...], min(N_SELECT, nb))
    for g in range(KV_HEADS):
        b = bias[:, g * LANES:(g + 1) * LANES]
        bias_s[g] = b if w == LANES else jnp.concatenate([b] * (w // LANES), axis=1)


def _block_bias(bias_rows, n_blk):
    w = bias_rows.shape[1]
    return jnp.concatenate([jnp.broadcast_to(bias_rows[u:u + 1, :], (SEL_BLOCK, w)) for u in range(n_blk)], axis=0)


def _gate_row(gates_t, c, g, nq, w):
    rows = [gates_t[(c * HPG + hh) * KV_HEADS + g:(c * HPG + hh) * KV_HEADS + g + 1, :] for hh in range(HPG)]
    if nq == LANES:
        return jnp.concatenate(rows, axis=1)
    strip = lax.broadcasted_iota(I32, (1, LANES), 1) >> (nq.bit_length() - 1)
    out = jnp.zeros((1, LANES), F32)
    for hh in range(HPG):
        out = jnp.where(strip == hh, rows[hh] if hh == 0 else pltpu.roll(rows[hh], hh * nq, axis=1), out)
    return out


def _finish(o_ref, st, ocmp_s, win, ot_s, gates_t, nq, nq_real, w):
    m_s, l_s, acc_s = st
    for g in range(KV_HEADS):
        o = ocmp_s[g] * _gate_row(gates_t, 0, g, nq, w)
        o = o + acc_s[g] * (_gate_row(gates_t, 1, g, nq, w) / l_s[g])
        w_acc, w_l = win(g)
        o = o + w_acc * (_gate_row(gates_t, 2, g, nq, w) / w_l)
        ot_s[g * HEAD_DIM:(g + 1) * HEAD_DIM, :] = o
    o_t = ot_s[...].T
    for hh in range(HPG):
        o_ref[:, hh * KV_HEADS * HEAD_DIM:(hh + 1) * KV_HEADS * HEAD_DIM] = o_t[hh * nq:hh * nq + nq_real, :]


def _attn_prompt_kernel(q_ref, gp_ref, bg_ref, cos_ref, sin_ref, kc_ref, vct_ref, ks_ref, vst_ref, kw_ref, vwt_ref,
                        o_ref, m_s, l_s, acc_s, score_s, bias_s, ocmp_s, ot_s, qzr_s, *, nb, nbp):
    st = (m_s, l_s, acc_s)
    i = pl.program_id(1)
    nq = Q_TILE
    w = HPG * nq
    s0 = i * nq
    q = q_ref[...]
    q_rot = _rope(q, cos_ref[...], sin_ref[...])
    gates_t = jax.nn.sigmoid(gp_ref[...] + bg_ref[...]).T
    qpos_w = s0 + (lax.broadcasted_iota(I32, (1, w), 1) & (nq - 1))
    qpos_q = s0 + lax.broadcasted_iota(I32, (1, LANES), 1)
    _init_state(st)
    for g in range(KV_HEADS):
        qzr_s[g] = _group_queries(q_rot, g)
        o_cmp, score = _cmp_branch(kc_ref, vct_ref, _group_queries(q, g), g, qpos_w, qpos_q, nb, nbp, nq)
        ocmp_s[g] = o_cmp
        score_s[:, g * LANES:(g + 1) * LANES] = score
    _select(score_s, bias_s, nb, w)

    blk_per_tile = KEY_TILE // SEL_BLOCK

    def slc_tile(t, causal):
        k0 = pl.multiple_of(t * KEY_TILE, KEY_TILE)
        kt = ks_ref[pl.ds(k0, KEY_TILE), :]
        vt = vst_ref[t]
        for g in range(KV_HEADS):
            rows = bias_s[g, pl.ds(pl.multiple_of(t * blk_per_tile, blk_per_tile), blk_per_tile), :]
            s = _nt(_kchunk(kt, g), qzr_s[g]) + _block_bias(rows, blk_per_tile)
            if causal:
                s = jnp.where(k0 + lax.broadcasted_iota(I32, (KEY_TILE, w), 0) <= qpos_w, s, NEG)
            _flash(s, vt[g * HEAD_DIM:(g + 1) * HEAD_DIM, :], st, g)

    t_diag = s0 // KEY_TILE

    def full_tile(t, carry):
        slc_tile(t, False)
        return carry

    lax.fori_loop(0, t_diag, full_tile, 0)
    slc_tile(t_diag, True)

    n_wt = (WINDOW + nq) // PAGE
    t0 = jnp.maximum(i - WINDOW // PAGE, 0)
    k0 = pl.multiple_of(t0 * PAGE, PAGE)
    kwin = kw_ref[pl.ds(k0, n_wt * PAGE), :]
    vwin_tiles = vwt_ref[pl.ds(t0, n_wt)]
    vwin = jnp.concatenate([vwin_tiles[u] for u in range(n_wt)], axis=1)
    kpos = k0 + lax.broadcasted_iota(I32, (n_wt * PAGE, w), 0)
    wmask = jnp.logical_and(kpos <= qpos_w, qpos_w - kpos < WINDOW)

    def win(g):
        s = jnp.where(wmask, _nt(_kchunk(kwin, g), qzr_s[g]), NEG)
        return _one_shot(s, vwin[g * HEAD_DIM:(g + 1) * HEAD_DIM, :])

    _finish(o_ref, st, ocmp_s, win, ot_s, gates_t, nq, nq, w)


def _attn_prompt(q, gate_pre, b_gate, cos, sin, kc, vct, ks, vst, kw, vwt, batch, seq, name):
    assert seq % KEY_TILE == 0 and seq >= WINDOW + Q_TILE
    m, d = q.shape
    nq = Q_TILE
    w = HPG * nq
    nqb = seq // nq
    nb = seq // SEL_BLOCK
    nbp = kc.shape[1] // 4
    gw = KV_HEADS * HEAD_DIM
    ntile = seq // KEY_TILE
    row_map = lambda b, i: (b * nqb + i, 0)
    per_b2 = lambda b, i: (b, 0)
    per_b3 = lambda b, i: (b, 0, 0)
    return pl.pallas_call(
        functools.partial(_attn_prompt_kernel, nb=nb, nbp=nbp), grid=(batch, nqb),
        in_specs=[pl.BlockSpec((nq, d), row_map), pl.BlockSpec((nq, LANES), row_map),
                  pl.BlockSpec((1, LANES), lambda b, i: (0, 0)),
                  pl.BlockSpec((nq, LANES), lambda b, i: (i, 0)), pl.BlockSpec((nq, LANES), lambda b, i: (i, 0)),
                  pl.BlockSpec((None, 4 * nbp, gw), per_b3), pl.BlockSpec((None, gw, 4 * nbp), per_b3),
                  pl.BlockSpec((seq, gw), per_b2), pl.BlockSpec((ntile, gw, KEY_TILE), per_b3),
                  pl.BlockSpec((seq, gw), per_b2), pl.BlockSpec((seq // PAGE, gw, PAGE), per_b3)],
        out_specs=pl.BlockSpec((nq, d), row_map),
        out_shape=jax.ShapeDtypeStruct((m, d), F32),
        scratch_shapes=[pltpu.VMEM((KV_HEADS, 1, w), F32), pltpu.VMEM((KV_HEADS, 1, w), F32),
                        pltpu.VMEM((KV_HEADS, HEAD_DIM, w), F32), pltpu.VMEM((nbp, KV_HEADS * LANES), F32),
                        pltpu.VMEM((KV_HEADS, nbp, w), F32),
                        pltpu.VMEM((KV_HEADS, HEAD_DIM, w), F32), pltpu.VMEM((gw, w), F32),
                        pltpu.VMEM((KV_HEADS, w, LANES), BF16)],
        compiler_params=_cparams(("parallel", "arbitrary")), name=name,
    )(q, gate_pre, b_gate, cos, sin, kc, vct, ks, vst, kw, vwt)


def _pad_rows(x, n):
    return jnp.concatenate([x, jnp.zeros((n - x.shape[0], x.shape[1]), x.dtype)], axis=0)


def _attn_sample_kernel(tbl_ref, *refs, n_pg, n_steps, nb, nbp, pos0, n_new):
    page_refs = refs[:n_pg]
    (q_ref, gp_ref, bg_ref, cos_ref, sin_ref, kc_ref, vct_ref, kvn_ref, cwin_ref, wnew_ref,
     o_ref, m_s, l_s, acc_s, score_s, bias_s, ocmp_s, ot_s, qzr_s, gt_s) = refs[n_pg:]
    st = (m_s, l_s, acc_s)
    step = pl.program_id(1)
    nq = SAMPLE_NQ
    w = HPG * nq
    gw = KV_HEADS * HEAD_DIM
    qpos_w = pos0 + (lax.broadcasted_iota(I32, (1, w), 1) & (nq - 1))

    @pl.when(step == 0)
    def _():
        q = _pad_rows(q_ref[...], nq)
        q_rot = _pad_rows(_rope(q_ref[...], cos_ref[...], sin_ref[...]), nq)
        gt_s[...] = _pad_rows(jax.nn.sigmoid(gp_ref[...] + bg_ref[...]), LANES).T
        _init_state(st)
        for g in range(KV_HEADS):
            qzr_s[g] = _group_queries(q_rot, g)
            o_cmp, score = _cmp_branch(kc_ref, vct_ref, _group_queries(q, g), g, qpos_w, qpos_w, nb, nbp, nq)
            ocmp_s[g] = o_cmp
            score_s[:, g * LANES:(g + 1) * LANES] = score
        _select(score_s, bias_s, nb, w)

    blk_pp = PAGE // SEL_BLOCK
    blk_ps = n_pg * blk_pp
    pages = jnp.concatenate([r[...] for r in page_refs], axis=0)
    kb = pages[:, 0:gw].astype(BF16)
    vt = pages[:, gw:2 * gw].T.astype(BF16)
    for g in range(KV_HEADS):
        rows = bias_s[g, pl.ds(pl.multiple_of(step * blk_ps, blk_ps), blk_ps), :]
        s = _nt(_kchunk(kb, g), qzr_s[g]) + _block_bias(rows, blk_ps)
        _flash(s, vt[g * HEAD_DIM:(g + 1) * HEAD_DIM, :], st, g)

    @pl.when(step == n_steps - 1)
    def _():
        krow = lax.broadcasted_iota(I32, (PAGE, w), 0)
        kvn = _pad_rows(kvn_ref[...], PAGE)
        kn = kvn[:, 2 * gw:3 * gw].astype(BF16)
        vnt = kvn[:, 3 * gw:4 * gw].T.astype(BF16)
        nb0 = pos0 // SEL_BLOCK
        causal = pos0 + krow <= qpos_w
        for g in range(KV_HEADS):
            s = _nt(_kchunk(kn, g), qzr_s[g]) + _block_bias(bias_s[g, nb0:nb0 + blk_pp, :], blk_pp)
            _flash(jnp.where(causal, s, NEG), vnt[g * HEAD_DIM:(g + 1) * HEAD_DIM, :], st, g)

        n_cached = cwin_ref.shape[0]
        wall = jnp.concatenate([_pad_rows(wnew_ref[...], PAGE), cwin_ref[...]], axis=0)
        kwin = wall[:, 0:gw].astype(BF16)
        vwin = wall[:, gw:2 * gw].T.astype(BF16)
        kpos = jnp.concatenate([pos0 + krow, pos0 - n_cached + lax.broadcasted_iota(I32, (n_cached, w), 0)], axis=0)
        wmask = jnp.logical_and(jnp.logical_and(kpos <= qpos_w, qpos_w - kpos < WINDOW), kpos >= 0)

        def win(g):
            s = jnp.where(wmask, _nt(_kchunk(kwin, g), qzr_s[g]), NEG)
            return _one_shot(s, vwin[g * HEAD_DIM:(g + 1) * HEAD_DIM, :])

        _finish(o_ref, st, ocmp_s, win, ot_s, gt_s[...], nq, n_new, w)


def _attn_sample(q, gate_pre, b_gate, cos, sin, kc, vct, cache3, table, kv_new, cache_win2, win_new,
                 batch, n_new, n_pages, name):
    m, d = q.shape
    nq = SAMPLE_NQ
    w = HPG * nq
    assert w == LANES and n_new <= nq and n_new % SUBLANES == 0 and n_new <= SEL_BLOCK
    n_pg = ATT_PAGES
    assert n_pages % n_pg == 0
    n_steps = n_pages // n_pg
    pos0 = n_pages * PAGE
    nb = (pos0 + n_new + SEL_BLOCK - 1) // SEL_BLOCK
    nbp = kc.shape[1] // 4
    gw = KV_HEADS * HEAD_DIM
    n_cached = cache_win2.shape[1]
    assert n_cached % PAGE == 0

    def page_map(u):
        return lambda b, s, tbl: (tbl[b * n_pages + s * n_pg + u], 0, 1)

    row_map = lambda b, s, tbl: (b, 0)
    per_b3 = lambda b, s, tbl: (b, 0, 0)
    const2 = lambda b, s, tbl: (0, 0)
    in_specs = [pl.BlockSpec((None, PAGE, 2 * gw), page_map(u)) for u in range(n_pg)]
    in_specs += [pl.BlockSpec((n_new, d), row_map), pl.BlockSpec((n_new, LANES), row_map),
                 pl.BlockSpec((1, LANES), const2), pl.BlockSpec((n_new, LANES), const2),
                 pl.BlockSpec((n_new, LANES), const2),
                 pl.BlockSpec((None, 4 * nbp, gw), per_b3), pl.BlockSpec((None, gw, 4 * nbp), per_b3),
                 pl.BlockSpec((n_new, 4 * gw), row_map), pl.BlockSpec((None, n_cached, 2 * gw), per_b3),
                 pl.BlockSpec((n_new, 2 * gw), row_map)]
    grid_spec = pltpu.PrefetchScalarGridSpec(
        num_scalar_prefetch=1, grid=(batch, n_steps), in_specs=in_specs,
        out_specs=pl.BlockSpec((n_new, d), row_map),
        scratch_shapes=[pltpu.VMEM((KV_HEADS, 1, w), F32), pltpu.VMEM((KV_HEADS, 1, w), F32),
                        pltpu.VMEM((KV_HEADS, HEAD_DIM, w), F32), pltpu.VMEM((nbp, KV_HEADS * LANES), F32),
                        pltpu.VMEM((KV_HEADS, nbp, w), F32),
                        pltpu.VMEM((KV_HEADS, HEAD_DIM, w), F32), pltpu.VMEM((gw, w), F32),
                        pltpu.VMEM((KV_HEADS, w, LANES), BF16), pltpu.VMEM((LANES, LANES), F32)])
    return pl.pallas_call(
        functools.partial(_attn_sample_kernel, n_pg=n_pg, n_steps=n_steps, nb=nb, nbp=nbp, pos0=pos0, n_new=n_new),
        grid_spec=grid_spec, out_shape=jax.ShapeDtypeStruct((m, d), F32),
        compiler_params=_cparams(("parallel", "arbitrary")), name=name,
    )(table, *([cache3] * n_pg), q, gate_pre, b_gate, cos, sin, kc, vct, kv_new, cache_win2, win_new)


def _prep_weights(w_a_in, b_a_gate, w_a_out, w_kv, cmp_pos, cmp_w1, cmp_b1, cmp_w2, cmp_b2,
                  w_b_in, b_b_gate, w_b_out, w_ffn_up, w_ffn_down):
    a_q = M_HEADS * M_QK_DIM
    a_v = M_HEADS * M_V_DIM
    n_g = 2 * M_HEADS
    g0 = 2 * a_q + a_v
    gw = KV_HEADS * HEAD_DIM
    qd = N_HEADS * HEAD_DIM
    p = {}
    p["a_main"] = jnp.concatenate([w_a_in[:, :, :g0], w_a_in[:, :, g0 + n_g:]], axis=-1).astype(BF16)
    p["a_gate"] = jnp.pad(w_a_in[:, :, g0:g0 + n_g], ((0, 0), (0, 0), (0, LANES - n_g))).astype(BF16)
    p["a_bgate"] = jnp.pad(b_a_gate, ((0, 0), (0, LANES - n_g)))[:, None, :]
    p["a_out"] = w_a_out.astype(BF16)
    p["kv"] = w_kv.astype(BF16)
    p["kv_vt"] = jnp.concatenate([w_kv[:, 3 * gw:4 * gw], w_kv[:, 5 * gw:6 * gw]], axis=1).T.astype(BF16)
    hh, g, c = jnp.meshgrid(jnp.arange(HPG), jnp.arange(KV_HEADS), jnp.arange(3), indexing="ij")
    old_col = ((HPG * g + hh) * 3 + c)
    new_col = ((c * HPG + hh) * KV_HEADS + g)
    order = jnp.zeros((3 * N_HEADS,), I32).at[new_col.reshape(-1)].set(old_col.reshape(-1))
    p["b_q"] = w_b_in[:, :, :qd].astype(BF16)
    p["b_gate"] = jnp.pad(w_b_in[:, :, qd:][:, :, order], ((0, 0), (0, 0), (0, LANES - 3 * N_HEADS))).astype(BF16)
    p["b_bgate"] = jnp.pad(b_b_gate[:, order], ((0, 0), (0, LANES - 3 * N_HEADS)))[:, None, :]
    wo = w_b_out.reshape(w_b_out.shape[0], KV_HEADS, HPG, HEAD_DIM, D_MODEL).transpose(0, 2, 1, 3, 4)
    p["b_out"] = wo.reshape(w_b_out.shape[0], qd, D_MODEL).astype(BF16)
    p["up"] = w_ffn_up.astype(BF16)
    p["down"] = w_ffn_down.astype(BF16)
    flat = CMP_STRIDE * HEAD_DIM
    p["cmp"] = {
        "pos_lo": cmp_pos[:, :CMP_STRIDE].reshape(2, 1, flat), "pos_hi": cmp_pos[:, CMP_STRIDE:].reshape(2, 1, flat),
        "w1lo": cmp_w1[:, :flat].astype(BF16), "w1hi": cmp_w1[:, flat:].astype(BF16),
        "b1": cmp_b1[:, None, :], "w2": cmp_w2.astype(BF16), "b2": cmp_b2[:, None, :]}
    return p


def _trunk(x3, pos0, past, p, g_norms, g_a_hnorm, g_kv, ffn_conv_w, ffn_conv_b, conv_prev, m_c, m_n, m_m, tag):
    batch, seq, d = x3.shape
    x = x3.reshape(batch * seq, d)
    cs, ns, ms, convs = [], [], [], []
    gw = KV_HEADS * HEAD_DIM
    for layer in range(DEPTH):
        g = g_norms[layer]
        nm = f"{tag}{layer}"
        if layer < N_A_LAYERS:
            main, gate = _norm_proj(x, g[0], [p["a_main"][layer], p["a_gate"][layer]], nm + "_in")
            h, c_new, n_new, m_new = _mlstm(main, gate, p["a_bgate"][layer], g_a_hnorm[layer],
                                            m_c[layer], m_n[layer], m_m[layer], batch, seq, nm + "_mlstm")
            cs.append(c_new)
            ns.append(n_new)
            ms.append(m_new)
            x = _proj_norm_res(h, p["a_out"][layer], g[1], x, nm + "_out")
        else:
            if layer == N_A_LAYERS:
                cos, sin = _rope_tables(pos0, seq)
                reps = 1 if past is None else batch
                kv4, win, ks_b, kw_b, vst, vwt = _kv_rows(x, g_kv, p["kv"], p["kv_vt"], jnp.tile(cos, (reps, 1)),
                                                          jnp.tile(sin, (reps, 1)), tag + "_kv")
                zeros_hi = jnp.zeros((batch, 2 * KV_HEADS, CMP_HIDDEN), F32)
                if past is None:
                    n_pages = seq // PAGE
                    nb = seq // SEL_BLOCK
                    kc, vc, _ = _compress(kv4.reshape(batch * n_pages, PAGE, 4 * gw), jnp.arange(batch * n_pages, dtype=I32),
                                          batch, n_pages, CMP_PAGES, zeros_hi, p["cmp"], None, tag + "_cmp")
                else:
                    cache3, table, n_pages, cache_win2 = past
                    nb = (pos0 + seq + SEL_BLOCK - 1) // SEL_BLOCK
                    n_tail = nb * (SEL_BLOCK // CMP_STRIDE) - n_pages * (PAGE // CMP_STRIDE)
                    tail = jnp.pad(kv4.reshape(batch, seq, 4 * gw), ((0, 0), (0, PAGE - seq), (0, 0)))
                    kc_t, vc_t, hi_t = _compress(tail, jnp.arange(batch, dtype=I32), batch, 1, 1, zeros_hi, p["cmp"],
                                                 n_tail - 1, tag + "_cmpt")
                    kc_m, vc_m, _ = _compress(cache3, table, batch, n_pages, CMP_PAGES, hi_t, p["cmp"], None,
                                              tag + "_cmp")
                    kc = jnp.concatenate([kc_m, kc_t[:, :n_tail]], axis=1)
                    vc = jnp.concatenate([vc_m, vc_t[:, :n_tail]], axis=1)
                nbp = -(-nb // 32) * 32
                kc_l = _cmp_layout(kc, nb, nbp)
                vct_l = _cmp_layout(vc, nb, nbp).transpose(0, 2, 1)
                if past is not None:
                    cos_q, sin_q = cos, sin
            bl = layer - N_A_LAYERS
            q_raw, gate_pre = _norm_proj(x, g[0], [p["b_q"][bl], p["b_gate"][bl]], nm + "_in")
            if past is None:
                o = _attn_prompt(q_raw, gate_pre, p["b_bgate"][bl], cos, sin, kc_l, vct_l, ks_b, vst, kw_b, vwt,
                                 batch, seq, nm + "_attn")
            else:
                o = _attn_sample(q_raw, gate_pre, p["b_bgate"][bl], cos_q, sin_q, kc_l, vct_l, cache3, table,
                                 kv4, cache_win2, win, batch, seq, n_pages, nm + "_attn")
            x = _proj_norm_res(o, p["b_out"][bl], g[1], x, nm + "_out")
        x, conv_new = _ffn(x, g[2], p["up"][layer], ffn_conv_w[layer], ffn_conv_b[layer], p["down"][layer], g[3],
                           conv_prev[layer], batch, seq, nm + "_ffn")
        convs.append(conv_new)
    return (x.reshape(batch, seq, d), kv4, win, jnp.stack(cs), jnp.stack(ns), jnp.stack(ms), jnp.stack(convs))


def kernel(x_prompt, x_sample, cache_kv, cache_win_kv, state_mlstm_C, state_mlstm_n, state_mlstm_m, state_conv,
           page_table, g_norms, w_a_in, b_a_gate, g_a_hnorm, w_a_out, g_kv, w_kv, cmp_pos, cmp_w1, cmp_b1, cmp_w2,
           cmp_b2, w_b_in, b_b_gate, w_b_out, w_ffn_up, ffn_conv_w, ffn_conv_b, w_ffn_down):
    p = _prep_weights(w_a_in, b_a_gate, w_a_out, w_kv, cmp_pos, cmp_w1, cmp_b1, cmp_w2, cmp_b2,
                      w_b_in, b_b_gate, w_b_out, w_ffn_up, w_ffn_down)
    dt = x_prompt.dtype
    bp, tp, _ = x_prompt.shape
    bs, ts, _ = x_sample.shape
    n_pages = page_table.shape[1]
    past_len = n_pages * PAGE
    gw = KV_HEADS * HEAD_DIM
    shared = (p, g_norms, g_a_hnorm, g_kv, ffn_conv_w, ffn_conv_b)

    y_p, kv_p, win_p, c_p, n_p, m_p, conv_p = _trunk(
        x_prompt, 0, None, *shared,
        jnp.zeros((DEPTH, bp, CONV_W - 1, 2 * D_FF), dt),
        jnp.zeros((N_A_LAYERS, bp, M_HEADS, M_QK_DIM, M_V_DIM), dt),
        jnp.zeros((N_A_LAYERS, bp, M_HEADS, M_QK_DIM), dt), jnp.zeros((N_A_LAYERS, bp, M_HEADS), dt), "p")
    kv_p = kv_p.reshape(bp, tp, 4, KV_HEADS, HEAD_DIM)
    n_win_p = min(WINDOW, tp)
    win_p = win_p.reshape(bp, tp, 2, KV_HEADS, HEAD_DIM)[:, tp - n_win_p:]

    cache3 = cache_kv.reshape(cache_kv.shape[0], PAGE, 4 * gw)
    cache_win2 = cache_win_kv.reshape(bs, cache_win_kv.shape[1], 2 * gw)
    past = (cache3, page_table.reshape(-1), n_pages, cache_win2)
    y_s, kv_s, win_s, c_s, n_s, m_s, conv_s = _trunk(
        x_sample, past_len, past, *shared, state_conv, state_mlstm_C, state_mlstm_n, state_mlstm_m, "s")
    kv_s = kv_s.reshape(bs, ts, 4, KV_HEADS, HEAD_DIM)
    win_all = jnp.concatenate([cache_win2, win_s.reshape(bs, ts, 2 * gw)], axis=1)
    win_s = win_all[:, win_all.shape[1] - min(WINDOW, win_all.shape[1]):].reshape(bs, -1, 2, KV_HEADS, HEAD_DIM)
    return (y_p, y_s, kv_p, kv_s, win_p, win_s, c_p, c_s, n_p, n_s, m_p, m_s, conv_p, conv_s)
```

```python
import functools
import math

import jax
import jax.numpy as jnp
from jax import lax
from jax.experimental import pallas as pl
from jax.experimental.pallas import tpu as pltpu

F32 = jnp.float32
BF16 = jnp.bfloat16
I32 = jnp.int32

D_MODEL = 1024
DEPTH = 4
N_A_LAYERS = 2
M_HEADS = 8
M_QK_DIM = 64
M_V_DIM = 128
GATE_CAP = 15.0
N_HEADS = 16
HEAD_DIM = 64
KV_HEADS = 4
HPG = 4
CMP_BLOCK = 32
CMP_STRIDE = 16
CMP_HIDDEN = 256
SEL_BLOCK = 64
N_SELECT = 16
WINDOW = 512
ROT_DIM = 16
ROPE_THETA = 500000.0
D_FF = 2816
CONV_W = 3
EPS = 1e-6
BIG = 1e9
NEG = -1e30
PAGE = 128

LANES = 128
SUBLANES = 8
VMEM_LIMIT = 56 * 1024 * 1024

MLSTM_CHUNK = 128
KEY_TILE = 512
Q_TILE = 128
SAMPLE_NQ = 32
CMP_PAGES = 16
ATT_PAGES = 8
FFN_TM = 512
FFN_TN = 1408


def _cparams(sem):
    return pltpu.CompilerParams(dimension_semantics=sem, vmem_limit_bytes=VMEM_LIMIT)


def _rms(x, g):
    return x * lax.rsqrt(jnp.mean(x * x, axis=-1, keepdims=True) + EPS) * g


def _nt(a, b):
    return lax.dot_general(a, b, (((1,), (1,)), ((), ())), preferred_element_type=F32)


def _dot(a, b):
    return jnp.dot(a, b, preferred_element_type=F32)


def _row_tile(m, pref):
    t = min(m, pref)
    while m % t:
        t //= 2
    return t


def _norm_proj_kernel(x_ref, g_ref, *refs, n_w):
    xn = _rms(x_ref[...], g_ref[...]).astype(BF16)
    for w_ref, o_ref in zip(refs[:n_w], refs[n_w:]):
        o_ref[...] = _dot(xn, w_ref[...]).astype(o_ref.dtype)


def _norm_proj(x, g, ws, name):
    m, d = x.shape
    tm = _row_tile(m, 512)
    n_w = len(ws)
    in_specs = [pl.BlockSpec((tm, d), lambda i: (i, 0)), pl.BlockSpec((1, d), lambda i: (0, 0))]
    in_specs += [pl.BlockSpec(w.shape, lambda i: (0, 0)) for w in ws]
    out_specs = [pl.BlockSpec((tm, w.shape[1]), lambda i: (i, 0)) for w in ws]
    out_shape = [jax.ShapeDtypeStruct((m, w.shape[1]), F32) for w in ws]
    return pl.pallas_call(
        functools.partial(_norm_proj_kernel, n_w=n_w), grid=(m // tm,), in_specs=in_specs, out_specs=out_specs,
        out_shape=out_shape, compiler_params=_cparams(("parallel",)), name=name)(x, g.reshape(1, d), *ws)


def _proj_norm_res_kernel(a_ref, w_ref, g_ref, res_ref, o_ref):
    y = _dot(a_ref[...].astype(BF16), w_ref[...])
    o_ref[...] = res_ref[...] + _rms(y, g_ref[...])


def _proj_norm_res(a, w, g, res, name):
    m, k = a.shape
    d = w.shape[1]
    tm = _row_tile(m, 512)
    return pl.pallas_call(
        _proj_norm_res_kernel, grid=(m // tm,),
        in_specs=[pl.BlockSpec((tm, k), lambda i: (i, 0)), pl.BlockSpec((k, d), lambda i: (0, 0)),
                  pl.BlockSpec((1, d), lambda i: (0, 0)), pl.BlockSpec((tm, d), lambda i: (i, 0))],
        out_specs=pl.BlockSpec((tm, d), lambda i: (i, 0)),
        out_shape=jax.ShapeDtypeStruct((m, d), F32),
        compiler_params=_cparams(("parallel",)), name=name)(a, w, g.reshape(1, d), res)


def _mlstm_kernel(main_ref, gate_ref, bg_ref, gh_ref, c0_ref, n0_ref, m0_ref,
                  h_ref, c_ref, n_ref, m_ref, c_s, n_s, m_s, *, rows, nc):
    L = MLSTM_CHUNK
    nh = M_HEADS
    cidx = pl.program_id(1)

    @pl.when(cidx == 0)
    def _():
        c_s[...] = c0_ref[...]
        n_s[...] = n0_ref[...]
        m_s[...] = m0_ref[...]

    main = main_ref[...]
    gp = gate_ref[...] + bg_ref[...]
    if rows < L:
        main = jnp.concatenate([main, jnp.zeros((L - rows, main.shape[1]), F32)], axis=0)
        gp = jnp.concatenate([gp, jnp.zeros((L - rows, LANES), F32)], axis=0)
    capped = GATE_CAP * jnp.tanh(gp / GATE_CAP)
    row1 = lax.broadcasted_iota(I32, (L, LANES), 0)
    real = row1 < rows
    ilog = jnp.where(real, capped, -jnp.inf)
    logf = jnp.where(real, jnp.minimum(capped, 0.0) - jnp.log1p(jnp.exp(-jnp.abs(capped))), 0.0)
    bh = logf
    k = 1
    while k < L:
        bh = bh + jnp.where(row1 >= k, pltpu.roll(bh, k, axis=0), 0.0)
        k *= 2
    bh = pltpu.roll(bh, LANES - nh, axis=1)
    c_all = ilog - bh
    cm = c_all
    k = 1
    while k < L:
        cm = jnp.maximum(cm, jnp.where(row1 >= k, pltpu.roll(cm, k, axis=0), -jnp.inf))
        k *= 2
    m_row = m_s[...]
    mt = bh + jnp.maximum(m_row, cm)
    m_new = mt[L - 1:L, :]
    b_last = bh[L - 1:L, :]
    w_inter = jnp.exp(bh + m_row - mt)
    u_all = bh - mt
    emt = jnp.exp(-mt)
    ws_all = jnp.exp(b_last - bh + ilog - m_new)
    decay = jnp.exp(b_last + m_row - m_new)
    c_t = c_all.T

    rr = lax.broadcasted_iota(I32, (L, L), 0)
    cc = lax.broadcasted_iota(I32, (L, L), 1)
    causal = cc <= rr
    lo_half = lax.broadcasted_iota(I32, (L, LANES), 1) < M_QK_DIM
    lo_rows = lax.broadcasted_iota(I32, (2 * M_QK_DIM, LANES), 0) < M_QK_DIM
    ones_b = jnp.ones((L, LANES), BF16)

    def lanes(a, x):
        return jnp.broadcast_to(a[:, x:x + 1], (L, LANES))

    for p in range(nh // 2):
        qp = main[:, p * LANES:(p + 1) * LANES] * (M_QK_DIM ** -0.5)
        kp = main[:, 512 + p * LANES:512 + (p + 1) * LANES]
        c_pair = c_s[2 * p:2 * p + 2].reshape(2 * M_QK_DIM, M_V_DIM)
        n_pair = n_s[p]
        kp_b = kp.astype(BF16)
        state_b = jnp.concatenate([c_pair, n_pair], axis=1).astype(BF16)
        for e in range(2):
            x = 2 * p + e
            qx_b = jnp.where(lo_half if e == 0 else jnp.logical_not(lo_half), qp, 0.0).astype(BF16)
            vx = main[:, 1024 + x * M_V_DIM:1024 + (x + 1) * M_V_DIM]
            ox = main[:, 2048 + x * M_V_DIM:2048 + (x + 1) * M_V_DIM]
            a = jnp.exp(jnp.where(causal, lanes(u_all, x) + c_t[x:x + 1, :], -jnp.inf)) * _nt(qx_b, kp_b)
            inter = _dot(qx_b, state_b)
            intra = _dot(a.astype(BF16), jnp.concatenate([vx.astype(BF16), ones_b], axis=1))
            wi = lanes(w_inter, x)
            num = wi * inter[:, :M_V_DIM] + intra[:, :M_V_DIM]
            den = wi * inter[:, M_V_DIM:] + intra[:, M_V_DIM:]
            h = num / jnp.maximum(jnp.abs(den), lanes(emt, x))
            hn = h * lax.rsqrt(jnp.mean(h * h, axis=-1, keepdims=True) + EPS)
            hn = hn * gh_ref[:, x * M_V_DIM:(x + 1) * M_V_DIM] * jax.nn.sigmoid(ox)
            h_ref[:, x * M_V_DIM:(x + 1) * M_V_DIM] = hn[:rows]
        kw = kp * jnp.where(lo_half, lanes(ws_all, 2 * p), lanes(ws_all, 2 * p + 1))
        vcat = main[:, 1024 + 2 * p * M_V_DIM:1024 + (2 * p + 2) * M_V_DIM]
        upd = lax.dot_general(kw.astype(BF16), jnp.concatenate([vcat.astype(BF16), ones_b], axis=1),
                              (((0,), (0,)), ((), ())), preferred_element_type=F32)
        dec_e = decay[:, 2 * p:2 * p + 1]
        dec_o = decay[:, 2 * p + 1:2 * p + 2]
        c_s[2 * p] = dec_e * c_pair[:M_QK_DIM] + upd[:M_QK_DIM, :M_V_DIM]
        c_s[2 * p + 1] = dec_o * c_pair[M_QK_DIM:] + upd[M_QK_DIM:, M_V_DIM:2 * M_V_DIM]
        n_s[p] = jnp.where(lo_rows, dec_e, dec_o) * n_pair + upd[:, 2 * M_V_DIM:]
    m_s[...] = m_new

    @pl.when(cidx == nc - 1)
    def _():
        c_ref[...] = c_s[...]
        n_ref[...] = n_s[...]
        m_ref[...] = m_s[...]


def _mlstm(main, gate, b_gate, g_hnorm, c0, n0, m0, batch, seq, name):
    rows = min(seq, MLSTM_CHUNK)
    nc = seq // rows
    a_v = M_HEADS * M_V_DIM
    hp = M_HEADS // 2
    n_in = jnp.broadcast_to(n0.reshape(batch, hp, 2 * M_QK_DIM, 1), (batch, hp, 2 * M_QK_DIM, LANES))
    m_in = jnp.pad(m0, ((0, 0), (0, LANES - M_HEADS)))[:, None, :]
    h, c, n, m = pl.pallas_call(
        functools.partial(_mlstm_kernel, rows=rows, nc=nc), grid=(batch, nc),
        in_specs=[pl.BlockSpec((rows, main.shape[1]), lambda b, c: (b * nc + c, 0)),
                  pl.BlockSpec((rows, LANES), lambda b, c: (b * nc + c, 0)),
                  pl.BlockSpec((1, LANES), lambda b, c: (0, 0)),
                  pl.BlockSpec((1, a_v), lambda b, c: (0, 0)),
                  pl.BlockSpec((None, M_HEADS, M_QK_DIM, M_V_DIM), lambda b, c: (b, 0, 0, 0)),
                  pl.BlockSpec((None, hp, 2 * M_QK_DIM, LANES), lambda b, c: (b, 0, 0, 0)),
                  pl.BlockSpec((None, 1, LANES), lambda b, c: (b, 0, 0))],
        out_specs=[pl.BlockSpec((rows, a_v), lambda b, c: (b * nc + c, 0)),
                   pl.BlockSpec((None, M_HEADS, M_QK_DIM, M_V_DIM), lambda b, c: (b, 0, 0, 0)),
                   pl.BlockSpec((None, hp, 2 * M_QK_DIM, LANES), lambda b, c: (b, 0, 0, 0)),
                   pl.BlockSpec((None, 1, LANES), lambda b, c: (b, 0, 0))],
        out_shape=[jax.ShapeDtypeStruct((batch * seq, a_v), F32),
                   jax.ShapeDtypeStruct((batch, M_HEADS, M_QK_DIM, M_V_DIM), F32),
                   jax.ShapeDtypeStruct((batch, hp, 2 * M_QK_DIM, LANES), F32),
                   jax.ShapeDtypeStruct((batch, 1, LANES), F32)],
        scratch_shapes=[pltpu.VMEM((M_HEADS, M_QK_DIM, M_V_DIM), F32), pltpu.VMEM((hp, 2 * M_QK_DIM, LANES), F32),
                        pltpu.VMEM((1, LANES), F32)],
        compiler_params=_cparams(("parallel", "arbitrary")), name=name,
    )(main, gate, b_gate, g_hnorm.reshape(1, a_v), c0, n_in, m_in)
    return h, c, n[..., 0].reshape(batch, M_HEADS, M_QK_DIM), m[:, 0, :M_HEADS]


def _ffn_kernel(*refs, carry, tm, n_j, tiles_per_seq, period):
    x_ref, g2_ref, wua_ref, wug_ref, cwa_ref, cwg_ref, cba_ref, cbg_ref, wd_ref, g3_ref = refs[:10]
    if carry:
        inita_ref, initg_ref, o_ref, sa_ref, sg_ref, xn_s, acc_s, ue_s, carry_s = refs[10:]
        branch_in = ((wua_ref, cwa_ref, cba_ref, inita_ref, sa_ref), (wug_ref, cwg_ref, cbg_ref, initg_ref, sg_ref))
    else:
        t1a_ref, t1g_ref, t2a_ref, t2g_ref, o_ref, sa_ref, sg_ref, xn_s, acc_s, ue_s = refs[10:]
        branch_in = ((wua_ref, cwa_ref, cba_ref, (t1a_ref, t2a_ref), sa_ref),
                     (wug_ref, cwg_ref, cbg_ref, (t1g_ref, t2g_ref), sg_ref))
    i = pl.program_id(0)
    j = pl.program_id(1)
    tn = wd_ref.shape[0]

    @pl.when(j == 0)
    def _():
        xn_s[...] = _rms(x_ref[...], g2_ref[...]).astype(BF16)
        acc_s[...] = jnp.zeros_like(acc_s)

    xn = xn_s[...]
    conv = []
    for which, (w_ref, cw_ref, cb_ref, boundary, s_ref) in enumerate(branch_in):
        u = _dot(xn, w_ref[...])
        ue_s[SUBLANES:, :] = u
        if carry:
            slot = which * n_j + j
            first = (i % tiles_per_seq) == 0
            ue_s[SUBLANES - 2:SUBLANES, :] = jnp.where(first, boundary[...], carry_s[slot])
            tap1 = ue_s[pl.ds(SUBLANES - 1, tm), :]
            tap2 = ue_s[pl.ds(SUBLANES - 2, tm), :]
            last2 = u[tm - 2:tm, :]
            carry_s[slot] = last2
            s_ref[i // tiles_per_seq, j] = last2
        else:
            ue_s[0:SUBLANES, :] = jnp.zeros((SUBLANES, tn), F32)
            t = lax.broadcasted_iota(I32, (tm, tn), 0) & (period - 1)
            tap1 = jnp.where(t >= 1, ue_s[pl.ds(SUBLANES - 1, tm), :], boundary[0][...])
            tap2 = jnp.where(t >= 2, ue_s[pl.ds(SUBLANES - 2, tm), :], boundary[1][...])
            s_ref[...] = u
        conv.append(cb_ref[...] + tap2 * cw_ref[0:1, :] + tap1 * cw_ref[1:2, :] + u * cw_ref[2:3, :])
    y = jax.nn.gelu(conv[0], approximate=True) * conv[1]
    acc_s[...] += _dot(y.astype(BF16), wd_ref[...])

    @pl.when(j == n_j - 1)
    def _():
        o_ref[...] = x_ref[...] + _rms(acc_s[...], g3_ref[...])


def _ffn(x, g2, w_up, conv_w, conv_b, w_down, g3, prev, batch, seq, name):
    m, d = x.shape
    nf = w_down.shape[0]
    tn = FFN_TN
    n_j = nf // tn
    carry = seq >= 256
    cb = conv_b.reshape(1, 2 * nf)
    col_a = lambda i, j: (0, j)
    col_g = lambda i, j: (0, j + n_j)
    common_specs = [
        None,
        pl.BlockSpec((1, d), lambda i, j: (0, 0)),
        pl.BlockSpec((d, tn), col_a), pl.BlockSpec((d, tn), col_g),
        pl.BlockSpec((CONV_W, tn), col_a), pl.BlockSpec((CONV_W, tn), col_g),
        pl.BlockSpec((1, tn), col_a), pl.BlockSpec((1, tn), col_g),
        pl.BlockSpec((tn, d), lambda i, j: (j, 0)),
        pl.BlockSpec((1, d), lambda i, j: (0, 0)),
    ]
    common_args = [x, g2.reshape(1, d), w_up, w_up, conv_w, conv_w, cb, cb, w_down, g3.reshape(1, d)]
    if carry:
        tm = _row_tile(seq, FFN_TM)
        tps = seq // tm
        common_specs[0] = pl.BlockSpec((tm, d), lambda i, j: (i, 0))
        st_a = lambda i, j: (i // tps, 0, j)
        st_g = lambda i, j: (i // tps, 0, j + n_j)
        out, sa, sg = pl.pallas_call(
            functools.partial(_ffn_kernel, carry=True, tm=tm, n_j=n_j, tiles_per_seq=tps, period=seq),
            grid=(m // tm, n_j),
            in_specs=common_specs + [pl.BlockSpec((None, 2, tn), st_a), pl.BlockSpec((None, 2, tn), st_g)],
            out_specs=[pl.BlockSpec((tm, d), lambda i, j: (i, 0)),
                       pl.BlockSpec((batch, n_j, 2, tn), lambda i, j: (0, 0, 0, 0)),
                       pl.BlockSpec((batch, n_j, 2, tn), lambda i, j: (0, 0, 0, 0))],
            out_shape=[jax.ShapeDtypeStruct((m, d), F32), jax.ShapeDtypeStruct((batch, n_j, 2, tn), F32),
                       jax.ShapeDtypeStruct((batch, n_j, 2, tn), F32)],
            scratch_shapes=[pltpu.VMEM((tm, d), BF16), pltpu.VMEM((tm, d), F32),
                            pltpu.VMEM((tm + SUBLANES, tn), F32), pltpu.VMEM((2 * n_j, 2, tn), F32)],
            compiler_params=_cparams(("arbitrary", "arbitrary")), name=name,
        )(*common_args, prev, prev)
        sa, sg = (s.transpose(0, 2, 1, 3).reshape(batch, 2, nf) for s in (sa, sg))
        return out, jnp.concatenate([sa, sg], axis=-1)
    tm = m
    assert seq >= 2 and seq & (seq - 1) == 0
    common_specs[0] = pl.BlockSpec((tm, d), lambda i, j: (i, 0))
    tap1 = jnp.pad(prev[:, 1:2], ((0, 0), (0, seq - 1), (0, 0))).reshape(m, 2 * nf)
    tap2 = jnp.pad(prev, ((0, 0), (0, seq - 2), (0, 0))).reshape(m, 2 * nf)
    row_a = lambda i, j: (i, j)
    row_g = lambda i, j: (i, j + n_j)
    out, ua, ug = pl.pallas_call(
        functools.partial(_ffn_kernel, carry=False, tm=tm, n_j=n_j, tiles_per_seq=1, period=seq),
        grid=(m // tm, n_j),
        in_specs=common_specs + [pl.BlockSpec((tm, tn), row_a), pl.BlockSpec((tm, tn), row_g),
                                 pl.BlockSpec((tm, tn), row_a), pl.BlockSpec((tm, tn), row_g)],
        out_specs=[pl.BlockSpec((tm, d), lambda i, j: (i, 0)), pl.BlockSpec((tm, tn), row_a),
                   pl.BlockSpec((tm, tn), row_a)],
        out_shape=[jax.ShapeDtypeStruct((m, d), F32), jax.ShapeDtypeStruct((m, nf), F32),
                   jax.ShapeDtypeStruct((m, nf), F32)],
        scratch_shapes=[pltpu.VMEM((tm, d), BF16), pltpu.VMEM((tm, d), F32), pltpu.VMEM((tm + SUBLANES, tn), F32)],
        compiler_params=_cparams(("arbitrary", "arbitrary")), name=name,
    )(*common_args, tap1, tap1, tap2, tap2)
    u = jnp.concatenate([ua, ug], axis=-1).reshape(batch, seq, 2 * nf)
    return out, u[:, seq - 2:]


def _rope_pair(x, cos, sin):
    half = ROT_DIM // 2
    lane = lax.broadcasted_iota(I32, x.shape, 1) & (HEAD_DIM - 1)
    partner = jnp.where(lane < half, pltpu.roll(x, LANES - half, axis=1), pltpu.roll(x, half, axis=1))
    return x * cos + partner * sin


def _rope(x, cos, sin):
    return jnp.concatenate(
        [_rope_pair(x[:, c * LANES:(c + 1) * LANES], cos, sin) for c in range(x.shape[1] // LANES)], axis=1)


def _rope_tables(pos0, seq):
    half = ROT_DIM // 2
    inv = jnp.power(jnp.float32(ROPE_THETA), -jnp.arange(0, ROT_DIM, 2, dtype=F32) / ROT_DIM)
    ang = (pos0 + jnp.arange(seq, dtype=I32)).astype(F32)[:, None] * inv[None, :]
    cos, sin = jnp.cos(ang), jnp.sin(ang)
    rest = HEAD_DIM - ROT_DIM
    cos_h = jnp.concatenate([cos, cos, jnp.ones((seq, rest), F32)], axis=1)
    sin_h = jnp.concatenate([-sin, sin, jnp.zeros((seq, rest), F32)], axis=1)
    return jnp.tile(cos_h, (1, LANES // HEAD_DIM)), jnp.tile(sin_h, (1, LANES // HEAD_DIM))


def _kv_kernel(x_ref, g_ref, w_ref, wvt_ref, cos_ref, sin_ref, kv_ref, win_ref, ks_ref, kw_ref, vst_ref, vwt_ref):
    xn = _rms(x_ref[...], g_ref[...]).astype(BF16)
    y = _dot(xn, w_ref[...])
    cos, sin = cos_ref[...], sin_ref[...]
    gw = KV_HEADS * HEAD_DIM
    ks = _rope(y[:, 2 * gw:3 * gw], cos, sin)
    kw = _rope(y[:, 4 * gw:5 * gw], cos, sin)
    kv_ref[:, 0:2 * gw] = y[:, 0:2 * gw]
    kv_ref[:, 2 * gw:3 * gw] = ks
    kv_ref[:, 3 * gw:4 * gw] = y[:, 3 * gw:4 * gw]
    win_ref[:, 0:gw] = kw
    win_ref[:, gw:2 * gw] = y[:, 5 * gw:6 * gw]
    ks_ref[...] = ks.astype(BF16)
    kw_ref[...] = kw.astype(BF16)
    vt = _nt(wvt_ref[...], xn).astype(BF16)
    vst_ref[...] = vt[0:gw]
    for u in range(vwt_ref.shape[0]):
        vwt_ref[u] = vt[gw:2 * gw, u * PAGE:(u + 1) * PAGE]


def _kv_rows(x, g_kv, w_kv, w_vt, cos, sin, name):
    m, d = x.shape
    tm = _row_tile(m, KEY_TILE)
    gw = KV_HEADS * HEAD_DIM
    ttiles = cos.shape[0] // tm
    return pl.pallas_call(
        _kv_kernel, grid=(m // tm,),
        in_specs=[pl.BlockSpec((tm, d), lambda i: (i, 0)), pl.BlockSpec((1, d), lambda i: (0, 0)),
                  pl.BlockSpec(w_kv.shape, lambda i: (0, 0)), pl.BlockSpec(w_vt.shape, lambda i: (0, 0)),
                  pl.BlockSpec((tm, LANES), lambda i: (i % ttiles, 0)),
                  pl.BlockSpec((tm, LANES), lambda i: (i % ttiles, 0))],
        out_specs=[pl.BlockSpec((tm, 4 * gw), lambda i: (i, 0)), pl.BlockSpec((tm, 2 * gw), lambda i: (i, 0)),
                   pl.BlockSpec((tm, gw), lambda i: (i, 0)), pl.BlockSpec((tm, gw), lambda i: (i, 0)),
                   pl.BlockSpec((None, gw, tm), lambda i: (i, 0, 0)),
                   pl.BlockSpec((tm // PAGE, gw, PAGE), lambda i: (i, 0, 0))],
        out_shape=[jax.ShapeDtypeStruct((m, 4 * gw), F32), jax.ShapeDtypeStruct((m, 2 * gw), F32),
                   jax.ShapeDtypeStruct((m, gw), BF16), jax.ShapeDtypeStruct((m, gw), BF16),
                   jax.ShapeDtypeStruct((m // tm, gw, tm), BF16), jax.ShapeDtypeStruct((m // PAGE, gw, PAGE), BF16)],
        compiler_params=_cparams(("parallel",)), name=name)(x, g_kv.reshape(1, d), w_kv, w_vt, cos, sin)


def _cmp_blocks(rows):
    n, r = rows.shape[0], rows.shape[1]
    x = rows.reshape(n, r // CMP_STRIDE, CMP_STRIDE, 2 * KV_HEADS, HEAD_DIM).transpose(0, 3, 1, 2, 4)
    return x.reshape(n, 2 * KV_HEADS, r // CMP_STRIDE, CMP_STRIDE * HEAD_DIM)


def _compress_kernel(tbl_ref, *refs, n_pg, njp, zero_after):
    page_refs = refs[:n_pg]
    (pos_lo_ref, pos_hi_ref, w1lo_ref, w1hi_ref, b1_ref, w2_ref, b2_ref, hi_init_ref,
     kc_ref, vc_ref, hi_first_ref, carry_s) = refs[n_pg:]
    c = pl.program_id(1)
    njc = n_pg * njp

    @pl.when(c == 0)
    def _():
        carry_s[...] = hi_init_ref[...]

    rowl = lax.broadcasted_iota(I32, (KV_HEADS * njc, CMP_HIDDEN), 0) & (njc - 1)
    for br, out_ref in enumerate((kc_ref, vc_ref)):
        x = jnp.concatenate([page_refs[u][br * KV_HEADS + g] for g in range(KV_HEADS) for u in range(n_pg)], axis=0)
        lo = _dot((x + pos_lo_ref[br]).astype(BF16), w1lo_ref[br])
        hi = _dot((x + pos_hi_ref[br]).astype(BF16), w1hi_ref[br])
        carry_rows = jnp.concatenate(
            [jnp.broadcast_to(carry_s[br * KV_HEADS + g:br * KV_HEADS + g + 1, :], (njc, CMP_HIDDEN))
             for g in range(KV_HEADS)], axis=0)
        hi_next = jnp.where(rowl == njc - 1, carry_rows, pltpu.roll(hi, KV_HEADS * njc - 1, axis=0))
        if zero_after is not None:
            hi_next = jnp.where(jnp.logical_and(rowl == zero_after, c == 0), 0.0, hi_next)
        for g in range(KV_HEADS):
            carry_s[br * KV_HEADS + g:br * KV_HEADS + g + 1, :] = hi[g * njc:g * njc + 1, :]
        h = jax.nn.gelu(lo + hi_next + b1_ref[br], approximate=True)
        o = _dot(h.astype(BF16), w2_ref[br]) + b2_ref[br]
        for g in range(KV_HEADS):
            out_ref[:, g * HEAD_DIM:(g + 1) * HEAD_DIM] = o[g * njc:(g + 1) * njc, :]
    hi_first_ref[...] = carry_s[...]


def _compress(blocks, table, batch, n_pages, n_pg, hi_init, cw, zero_after, name):
    njp = blocks.shape[2]
    njc = n_pg * njp
    assert njc & (njc - 1) == 0 and njc % SUBLANES == 0 and n_pages % n_pg == 0
    n_ch = n_pages // n_pg
    gw = KV_HEADS * HEAD_DIM
    flat = CMP_STRIDE * HEAD_DIM

    def page_map(u):
        return lambda b, c, tbl: (tbl[b * n_pages + (n_ch - 1 - c) * n_pg + u], 0, 0, 0)

    full3 = lambda b, c, tbl: (0, 0, 0)
    in_specs = [pl.BlockSpec((None, 2 * KV_HEADS, njp, flat), page_map(u)) for u in range(n_pg)]
    in_specs += [pl.BlockSpec((2, 1, flat), full3), pl.BlockSpec((2, 1, flat), full3),
                 pl.BlockSpec((2, flat, CMP_HIDDEN), full3), pl.BlockSpec((2, flat, CMP_HIDDEN), full3),
                 pl.BlockSpec((2, 1, CMP_HIDDEN), full3), pl.BlockSpec((2, CMP_HIDDEN, HEAD_DIM), full3),
                 pl.BlockSpec((2, 1, HEAD_DIM), full3),
                 pl.BlockSpec((None, 2 * KV_HEADS, CMP_HIDDEN), lambda b, c, tbl: (b, 0, 0))]
    out_map = lambda b, c, tbl: (b, n_ch - 1 - c, 0)
    grid_spec = pltpu.PrefetchScalarGridSpec(
        num_scalar_prefetch=1, grid=(batch, n_ch), in_specs=in_specs,
        out_specs=[pl.BlockSpec((None, njc, gw), out_map), pl.BlockSpec((None, njc, gw), out_map),
                   pl.BlockSpec((None, 2 * KV_HEADS, CMP_HIDDEN), lambda b, c, tbl: (b, 0, 0))],
        scratch_shapes=[pltpu.VMEM((2 * KV_HEADS, CMP_HIDDEN), F32)])
    return pl.pallas_call(
        functools.partial(_compress_kernel, n_pg=n_pg, njp=njp, zero_after=zero_after), grid_spec=grid_spec,
        out_shape=[jax.ShapeDtypeStruct((batch, n_ch * njc, gw), F32), jax.ShapeDtypeStruct((batch, n_ch * njc, gw), F32),
                   jax.ShapeDtypeStruct((batch, 2 * KV_HEADS, CMP_HIDDEN), F32)],
        compiler_params=_cparams(("parallel", "arbitrary")), name=name,
    )(table, *([blocks] * n_pg), cw["pos_lo"], cw["pos_hi"], cw["w1lo"], cw["w1hi"], cw["b1"], cw["w2"], cw["b2"], hi_init)


def _cmp_layout(kc_nat, nb, nbp):
    b = kc_nat.shape[0]
    x = kc_nat[:, :4 * nb].reshape(b, nb, 4, kc_nat.shape[-1]).transpose(0, 2, 1, 3)
    x = jnp.pad(x, ((0, 0), (0, 0), (0, nbp - nb), (0, 0)))
    return x.reshape(b, 4 * nbp, kc_nat.shape[-1]).astype(BF16)


SCALE = HEAD_DIM ** -0.5
QSCALE = SCALE * math.log2(math.e)
SUM_ROWS = 16
ACC_ROWS = HEAD_DIM + SUM_ROWS


def _group_queries(q, g):
    nq = q.shape[0]
    keep = (lax.broadcasted_iota(I32, (nq, LANES), 1) >> 6) == (g % 2)
    pieces = []
    for hh in range(HPG):
        h = HPG * g + hh
        chunk = q[:, (h // 2) * LANES:(h // 2 + 1) * LANES]
        if h % 2 != g % 2:
            chunk = pltpu.roll(chunk, HEAD_DIM, axis=1)
        pieces.append(jnp.where(keep, chunk, 0.0))
    return (jnp.concatenate(pieces, axis=0) * QSCALE).astype(BF16)


def _kchunk(k, g):
    return k[:, (g // 2) * LANES:(g // 2 + 1) * LANES]


def _values(vt, g):
    return jnp.concatenate([vt[g * HEAD_DIM:(g + 1) * HEAD_DIM, :], jnp.ones((SUM_ROWS, vt.shape[1]), BF16)], axis=0)


def _flash_groups(ss, vt, m_s, acc_s):
    ps, alphas = [], []
    for g in range(KV_HEADS):
        m_old = m_s[g]
        m_new = jnp.maximum(m_old, jnp.max(ss[g], axis=0, keepdims=True))
        alphas.append(jnp.exp2(m_old - m_new))
        ps.append(jnp.exp2(ss[g] - m_new).astype(BF16))
        m_s[g] = m_new
    for g in range(KV_HEADS):
        acc_s[g] = acc_s[g] * alphas[g] + _dot(_values(vt, g), ps[g])


def _one_shot_groups(ss, vt):
    ps = [jnp.exp2(s - jnp.max(s, axis=0, keepdims=True)).astype(BF16) for s in ss]
    return [_dot(_values(vt, g), ps[g]) for g in range(KV_HEADS)]


def _init_state(m_s, acc_s):
    m_s[...] = jnp.full(m_s.shape, NEG, F32)
    acc_s[...] = jnp.zeros(acc_s.shape, F32)


def _topk_bias(score, n_sel):
    n_iota = lax.broadcasted_iota(I32, score.shape, 0)

    def body(_, sc):
        mx = jnp.max(sc, axis=0, keepdims=True)
        idx = jnp.min(jnp.where(sc == mx, n_iota, score.shape[0]), axis=0, keepdims=True)
        return jnp.where(n_iota == idx, -jnp.inf, sc)

    left = lax.fori_loop(0, n_sel, body, score)
    return jnp.where(jnp.logical_and(left == -jnp.inf, score > -jnp.inf), 0.0, NEG)


def _cmp_valid(qpos_w, nb, nbp):
    n_w = lax.broadcasted_iota(I32, (nbp, qpos_w.shape[1]), 0)
    return jnp.concatenate(
        [jnp.logical_and((4 * n_w + c) * CMP_STRIDE + (CMP_BLOCK - 1) <= qpos_w, n_w < nb) for c in range(4)], axis=0)


def _cmp_branch(s, valid, vct_ref, g, qpos_w, qpos_q, nb, nbp, nq):
    w = s.shape[1]
    s = jnp.where(valid, s, NEG)
    e = jnp.exp2(s - jnp.max(s, axis=0, keepdims=True))
    some = (qpos_w >= CMP_BLOCK - 1).astype(F32)
    p = e * (some / jnp.maximum(jnp.sum(e, axis=0, keepdims=True), 1e-30))
    o_cmp = _dot(vct_ref[g * HEAD_DIM:(g + 1) * HEAD_DIM, :], p.astype(BF16))
    if nq == LANES:
        ps = ((p[:, 0:LANES] + p[:, LANES:2 * LANES]) + p[:, 2 * LANES:3 * LANES]) + p[:, 3 * LANES:4 * LANES]
    else:
        assert w == LANES
        ps = ((p + pltpu.roll(p, nq, axis=1)) + pltpu.roll(p, 2 * nq, axis=1)) + pltpu.roll(p, 3 * nq, axis=1)
    parts = [ps[c * nbp:(c + 1) * nbp] for c in range(4)]
    n_q = lax.broadcasted_iota(I32, (nbp, LANES), 0)
    prev = jnp.where(n_q >= 1, pltpu.roll(parts[3], 1, axis=0), 0.0)
    score = (((parts[0] + parts[1]) + parts[2]) + parts[3]) + prev
    cur = qpos_q >> 6
    forced = (n_q == 0) | (n_q == cur) | (n_q == cur - 1)
    score = jnp.where(forced, BIG, jnp.where(n_q * SEL_BLOCK <= qpos_q, score, -BIG))
    score = jnp.where(n_q < nb, score, -jnp.inf)
    return o_cmp, score


def _compressed_and_select(q, kc_ref, vct_ref, qpos_w, qpos_q, ocmp_s, score_s, bias_s, nb, nbp, nq):
    w = HPG * nq
    kc_all = kc_ref[...]
    cs = [_nt(_kchunk(kc_all, g), _group_queries(q, g)) for g in range(KV_HEADS)]
    valid = _cmp_valid(qpos_w, nb, nbp)
    for g in range(KV_HEADS):
        o_cmp, score = _cmp_branch(cs[g], valid, vct_ref, g, qpos_w, qpos_q, nb, nbp, nq)
        ocmp_s[g] = o_cmp
        score_s[:, g * LANES:(g + 1) * LANES] = score
    bias = _topk_bias(score_s[...], min(N_SELECT, nb))
    for g in range(KV_HEADS):
        b = bias[:, g * LANES:(g + 1) * LANES]
        bias_s[g] = b if w == LANES else jnp.concatenate([b] * (w // LANES), axis=1)


def _block_bias(bias_rows, n_blk):
    w = bias_rows.shape[1]
    return jnp.concatenate([jnp.broadcast_to(bias_rows[u:u + 1, :], (SEL_BLOCK, w)) for u in range(n_blk)], axis=0)


def _gate_row(gates_t, c, g, nq, w):
    rows = [gates_t[(c * HPG + hh) * KV_HEADS + g:(c * HPG + hh) * KV_HEADS + g + 1, :] for hh in range(HPG)]
    if nq == LANES:
        return jnp.concatenate(rows, axis=1)
    strip = lax.broadcasted_iota(I32, (1, LANES), 1) >> (nq.bit_length() - 1)
    out = jnp.zeros((1, LANES), F32)
    for hh in range(HPG):
        out = jnp.where(strip == hh, rows[hh] if hh == 0 else pltpu.roll(rows[hh], hh * nq, axis=1), out)
    return out


def _finish(o_ref, acc_s, ocmp_s, wins, ot_s, gates_t, nq, nq_real, w):
    for g in range(KV_HEADS):
        o = ocmp_s[g] * _gate_row(gates_t, 0, g, nq, w)
        for c, acc in ((1, acc_s[g]), (2, wins[g])):
            o = o + acc[:HEAD_DIM] * (_gate_row(gates_t, c, g, nq, w) / acc[HEAD_DIM:HEAD_DIM + 1])
        ot_s[g * HEAD_DIM:(g + 1) * HEAD_DIM, :] = o
    o_t = ot_s[...].T
    for hh in range(HPG):
        o_ref[:, hh * KV_HEADS * HEAD_DIM:(hh + 1) * KV_HEADS * HEAD_DIM] = o_t[hh * nq:hh * nq + nq_real, :]


def _attn_prompt_kernel(q_ref, gp_ref, bg_ref, cos_ref, sin_ref, kc_ref, vct_ref, ks_ref, vst_ref, kw_ref, vwt_ref,
                        o_ref, m_s, acc_s, score_s, bias_s, ocmp_s, ot_s, qzr_s, *, nb, nbp):
    i = pl.program_id(1)
    nq = Q_TILE
    w = HPG * nq
    s0 = i * nq
    q = q_ref[...]
    q_rot = _rope(q, cos_ref[...], sin_ref[...])
    gates_t = jax.nn.sigmoid(gp_ref[...] + bg_ref[...]).T
    qpos_w = s0 + (lax.broadcasted_iota(I32, (1, w), 1) & (nq - 1))
    qpos_q = s0 + lax.broadcasted_iota(I32, (1, LANES), 1)
    _init_state(m_s, acc_s)
    for g in range(KV_HEADS):
        qzr_s[g] = _group_queries(q_rot, g)
    _compressed_and_select(q, kc_ref, vct_ref, qpos_w, qpos_q, ocmp_s, score_s, bias_s, nb, nbp, nq)

    blk_per_tile = KEY_TILE // SEL_BLOCK

    def slc_tile(t, causal):
        k0 = pl.multiple_of(t * KEY_TILE, KEY_TILE)
        kt = ks_ref[pl.ds(k0, KEY_TILE), :]
        ss = []
        for g in range(KV_HEADS):
            rows = bias_s[g, pl.ds(pl.multiple_of(t * blk_per_tile, blk_per_tile), blk_per_tile), :]
            s = _nt(_kchunk(kt, g), qzr_s[g]) + _block_bias(rows, blk_per_tile)
            if causal:
                s = jnp.where(k0 + lax.broadcasted_iota(I32, (KEY_TILE, w), 0) <= qpos_w, s, NEG)
            ss.append(s)
        _flash_groups(ss, vst_ref[t], m_s, acc_s)

    t_diag = s0 // KEY_TILE

    def full_tile(t, carry):
        slc_tile(t, False)
        return carry

    lax.fori_loop(0, t_diag, full_tile, 0)
    slc_tile(t_diag, True)

    n_wt = (WINDOW + nq) // PAGE
    t0 = jnp.maximum(i - WINDOW // PAGE, 0)
    k0 = pl.multiple_of(t0 * PAGE, PAGE)
    kwin = kw_ref[pl.ds(k0, n_wt * PAGE), :]
    vwin_tiles = vwt_ref[pl.ds(t0, n_wt)]
    vwin = jnp.concatenate([vwin_tiles[u] for u in range(n_wt)], axis=1)
    kpos = k0 + lax.broadcasted_iota(I32, (n_wt * PAGE, w), 0)
    wmask = jnp.logical_and(kpos <= qpos_w, qpos_w - kpos < WINDOW)
    ws = [jnp.where(wmask, _nt(_kchunk(kwin, g), qzr_s[g]), NEG) for g in range(KV_HEADS)]
    _finish(o_ref, acc_s, ocmp_s, _one_shot_groups(ws, vwin), ot_s, gates_t, nq, nq, w)


def _attn_prompt(q, gate_pre, b_gate, cos, sin, kc, vct, ks, vst, kw, vwt, batch, seq, name):
    assert seq % KEY_TILE == 0 and seq >= WINDOW + Q_TILE
    m, d = q.shape
    nq = Q_TILE
    w = HPG * nq
    nqb = seq // nq
    nb = seq // SEL_BLOCK
    nbp = kc.shape[1] // 4
    gw = KV_HEADS * HEAD_DIM
    ntile = seq // KEY_TILE
    row_map = lambda b, i: (b * nqb + i, 0)
    per_b2 = lambda b, i: (b, 0)
    per_b3 = lambda b, i: (b, 0, 0)
    return pl.pallas_call(
        functools.partial(_attn_prompt_kernel, nb=nb, nbp=nbp), grid=(batch, nqb),
        in_specs=[pl.BlockSpec((nq, d), row_map), pl.BlockSpec((nq, LANES), row_map),
                  pl.BlockSpec((1, LANES), lambda b, i: (0, 0)),
                  pl.BlockSpec((nq, LANES), lambda b, i: (i, 0)), pl.BlockSpec((nq, LANES), lambda b, i: (i, 0)),
                  pl.BlockSpec((None, 4 * nbp, gw), per_b3), pl.BlockSpec((None, gw, 4 * nbp), per_b3),
                  pl.BlockSpec((seq, gw), per_b2), pl.BlockSpec((ntile, gw, KEY_TILE), per_b3),
                  pl.BlockSpec((seq, gw), per_b2), pl.BlockSpec((seq // PAGE, gw, PAGE), per_b3)],
        out_specs=pl.BlockSpec((nq, d), row_map),
        out_shape=jax.ShapeDtypeStruct((m, d), F32),
        scratch_shapes=[pltpu.VMEM((KV_HEADS, 1, w), F32), pltpu.VMEM((KV_HEADS, ACC_ROWS, w), F32),
                        pltpu.VMEM((nbp, KV_HEADS * LANES), F32), pltpu.VMEM((KV_HEADS, nbp, w), F32),
                        pltpu.VMEM((KV_HEADS, HEAD_DIM, w), F32), pltpu.VMEM((gw, w), F32),
                        pltpu.VMEM((KV_HEADS, w, LANES), BF16)],
        compiler_params=_cparams(("parallel", "arbitrary")), name=name,
    )(q, gate_pre, b_gate, cos, sin, kc, vct, ks, vst, kw, vwt)


def _pad_rows(x, n):
    return jnp.concatenate([x, jnp.zeros((n - x.shape[0], x.shape[1]), x.dtype)], axis=0)


def _attn_sample_kernel(tbl_ref, *refs, n_pg, n_steps, nb, nbp, pos0, n_new):
    k_refs = refs[:n_pg]
    vt_refs = refs[n_pg:2 * n_pg]
    (q_ref, gp_ref, bg_ref, cos_ref, sin_ref, kc_ref, vct_ref, kvn_ref, cwin_ref, wnew_ref,
     o_ref, m_s, acc_s, score_s, bias_s, ocmp_s, ot_s, qzr_s, gt_s) = refs[2 * n_pg:]
    step = pl.program_id(1)
    nq = SAMPLE_NQ
    w = HPG * nq
    gw = KV_HEADS * HEAD_DIM
    qpos_w = pos0 + (lax.broadcasted_iota(I32, (1, w), 1) & (nq - 1))

    @pl.when(step == 0)
    def _():
        q = _pad_rows(q_ref[...], nq)
        q_rot = _pad_rows(_rope(q_ref[...], cos_ref[...], sin_ref[...]), nq)
        gt_s[...] = _pad_rows(jax.nn.sigmoid(gp_ref[...] + bg_ref[...]), LANES).T
        _init_state(m_s, acc_s)
        for g in range(KV_HEADS):
            qzr_s[g] = _group_queries(q_rot, g)
        _compressed_and_select(q, kc_ref, vct_ref, qpos_w, qpos_w, ocmp_s, score_s, bias_s, nb, nbp, nq)

    blk_pp = PAGE // SEL_BLOCK
    blk_ps = n_pg * blk_pp
    kb = jnp.concatenate([r[...] for r in k_refs], axis=0)
    vt = jnp.concatenate([r[...] for r in vt_refs], axis=1)
    ss = []
    for g in range(KV_HEADS):
        rows = bias_s[g, pl.ds(pl.multiple_of(step * blk_ps, blk_ps), blk_ps), :]
        ss.append(_nt(_kchunk(kb, g), qzr_s[g]) + _block_bias(rows, blk_ps))
    _flash_groups(ss, vt, m_s, acc_s)

    @pl.when(step == n_steps - 1)
    def _():
        krow = lax.broadcasted_iota(I32, (PAGE, w), 0)
        kvn = _pad_rows(kvn_ref[...], PAGE)
        kn = kvn[:, 2 * gw:3 * gw].astype(BF16)
        vnt = kvn[:, 3 * gw:4 * gw].T.astype(BF16)
        nb0 = pos0 // SEL_BLOCK
        causal = pos0 + krow <= qpos_w
        ts = [jnp.where(causal, _nt(_kchunk(kn, g), qzr_s[g]) + _block_bias(bias_s[g, nb0:nb0 + blk_pp, :], blk_pp), NEG)
              for g in range(KV_HEADS)]
        _flash_groups(ts, vnt, m_s, acc_s)

        n_cached = cwin_ref.shape[0]
        wall = jnp.concatenate([_pad_rows(wnew_ref[...], PAGE), cwin_ref[...]], axis=0)
        kwin = wall[:, 0:gw].astype(BF16)
        vwin = wall[:, gw:2 * gw].T.astype(BF16)
        kpos = jnp.concatenate([pos0 + krow, pos0 - n_cached + lax.broadcasted_iota(I32, (n_cached, w), 0)], axis=0)
        wmask = jnp.logical_and(jnp.logical_and(kpos <= qpos_w, qpos_w - kpos < WINDOW), kpos >= 0)
        ws = [jnp.where(wmask, _nt(_kchunk(kwin, g), qzr_s[g]), NEG) for g in range(KV_HEADS)]
        _finish(o_ref, acc_s, ocmp_s, _one_shot_groups(ws, vwin), ot_s, gt_s[...], nq, n_new, w)


def _attn_sample(q, gate_pre, b_gate, cos, sin, kc, vct, k_pages, vt_pages, table, kv_new, cache_win2, win_new,
                 batch, n_new, n_pages, name):
    m, d = q.shape
    nq = SAMPLE_NQ
    w = HPG * nq
    assert w == LANES and n_new <= nq and n_new % SUBLANES == 0 and n_new <= SEL_BLOCK
    n_pg = ATT_PAGES
    assert n_pages % n_pg == 0
    n_steps = n_pages // n_pg
    pos0 = n_pages * PAGE
    nb = (pos0 + n_new + SEL_BLOCK - 1) // SEL_BLOCK
    nbp = kc.shape[1] // 4
    gw = KV_HEADS * HEAD_DIM
    n_cached = cache_win2.shape[1]
    assert n_cached % PAGE == 0

    def page_map(u):
        return lambda b, s, tbl: (tbl[b * n_pages + s * n_pg + u], 0, 0)

    row_map = lambda b, s, tbl: (b, 0)
    per_b3 = lambda b, s, tbl: (b, 0, 0)
    const2 = lambda b, s, tbl: (0, 0)
    in_specs = [pl.BlockSpec((None, PAGE, gw), page_map(u)) for u in range(n_pg)]
    in_specs += [pl.BlockSpec((None, gw, PAGE), page_map(u)) for u in range(n_pg)]
    in_specs += [pl.BlockSpec((n_new, d), row_map), pl.BlockSpec((n_new, LANES), row_map),
                 pl.BlockSpec((1, LANES), const2), pl.BlockSpec((n_new, LANES), const2),
                 pl.BlockSpec((n_new, LANES), const2),
                 pl.BlockSpec((None, 4 * nbp, gw), per_b3), pl.BlockSpec((None, gw, 4 * nbp), per_b3),
                 pl.BlockSpec((n_new, 4 * gw), row_map), pl.BlockSpec((None, n_cached, 2 * gw), per_b3),
                 pl.BlockSpec((n_new, 2 * gw), row_map)]
    grid_spec = pltpu.PrefetchScalarGridSpec(
        num_scalar_prefetch=1, grid=(batch, n_steps), in_specs=in_specs,
        out_specs=pl.BlockSpec((n_new, d), row_map),
        scratch_shapes=[pltpu.VMEM((KV_HEADS, 1, w), F32), pltpu.VMEM((KV_HEADS, ACC_ROWS, w), F32),
                        pltpu.VMEM((nbp, KV_HEADS * LANES), F32), pltpu.VMEM((KV_HEADS, nbp, w), F32),
                        pltpu.VMEM((KV_HEADS, HEAD_DIM, w), F32), pltpu.VMEM((gw, w), F32),
                        pltpu.VMEM((KV_HEADS, w, LANES), BF16), pltpu.VMEM((LANES, LANES), F32)])
    return pl.pallas_call(
        functools.partial(_attn_sample_kernel, n_pg=n_pg, n_steps=n_steps, nb=nb, nbp=nbp, pos0=pos0, n_new=n_new),
        grid_spec=grid_spec, out_shape=jax.ShapeDtypeStruct((m, d), F32),
        compiler_params=_cparams(("parallel", "arbitrary")), name=name,
    )(table, *([k_pages] * n_pg), *([vt_pages] * n_pg), q, gate_pre, b_gate, cos, sin, kc, vct, kv_new, cache_win2,
      win_new)


def _prep_weights(w_a_in, b_a_gate, w_a_out, w_kv, cmp_pos, cmp_w1, cmp_b1, cmp_w2, cmp_b2,
                  w_b_in, b_b_gate, w_b_out, w_ffn_up, w_ffn_down):
    a_q = M_HEADS * M_QK_DIM
    a_v = M_HEADS * M_V_DIM
    n_g = 2 * M_HEADS
    g0 = 2 * a_q + a_v
    gw = KV_HEADS * HEAD_DIM
    qd = N_HEADS * HEAD_DIM
    p = {}
    p["a_main"] = jnp.concatenate([w_a_in[:, :, :g0], w_a_in[:, :, g0 + n_g:]], axis=-1).astype(BF16)
    p["a_gate"] = jnp.pad(w_a_in[:, :, g0:g0 + n_g], ((0, 0), (0, 0), (0, LANES - n_g))).astype(BF16)
    p["a_bgate"] = jnp.pad(b_a_gate, ((0, 0), (0, LANES - n_g)))[:, None, :]
    p["a_out"] = w_a_out.astype(BF16)
    p["kv"] = w_kv.astype(BF16)
    p["kv_vt"] = jnp.concatenate([w_kv[:, 3 * gw:4 * gw], w_kv[:, 5 * gw:6 * gw]], axis=1).T.astype(BF16)
    hh, g, c = jnp.meshgrid(jnp.arange(HPG), jnp.arange(KV_HEADS), jnp.arange(3), indexing="ij")
    old_col = ((HPG * g + hh) * 3 + c)
    new_col = ((c * HPG + hh) * KV_HEADS + g)
    order = jnp.zeros((3 * N_HEADS,), I32).at[new_col.reshape(-1)].set(old_col.reshape(-1))
    p["b_q"] = w_b_in[:, :, :qd].astype(BF16)
    p["b_gate"] = jnp.pad(w_b_in[:, :, qd:][:, :, order], ((0, 0), (0, 0), (0, LANES - 3 * N_HEADS))).astype(BF16)
    p["b_bgate"] = jnp.pad(b_b_gate[:, order], ((0, 0), (0, LANES - 3 * N_HEADS)))[:, None, :]
    wo = w_b_out.reshape(w_b_out.shape[0], KV_HEADS, HPG, HEAD_DIM, D_MODEL).transpose(0, 2, 1, 3, 4)
    p["b_out"] = wo.reshape(w_b_out.shape[0], qd, D_MODEL).astype(BF16)
    p["up"] = w_ffn_up.astype(BF16)
    p["down"] = w_ffn_down.astype(BF16)
    flat = CMP_STRIDE * HEAD_DIM
    p["cmp"] = {
        "pos_lo": cmp_pos[:, :CMP_STRIDE].reshape(2, 1, flat), "pos_hi": cmp_pos[:, CMP_STRIDE:].reshape(2, 1, flat),
        "w1lo": cmp_w1[:, :flat].astype(BF16), "w1hi": cmp_w1[:, flat:].astype(BF16),
        "b1": cmp_b1[:, None, :], "w2": cmp_w2.astype(BF16), "b2": cmp_b2[:, None, :]}
    return p


def _trunk(x3, pos0, past, p, g_norms, g_a_hnorm, g_kv, ffn_conv_w, ffn_conv_b, conv_prev, m_c, m_n, m_m, tag):
    batch, seq, d = x3.shape
    x = x3.reshape(batch * seq, d)
    cs, ns, ms, convs = [], [], [], []
    gw = KV_HEADS * HEAD_DIM
    for layer in range(DEPTH):
        g = g_norms[layer]
        nm = f"{tag}{layer}"
        if layer < N_A_LAYERS:
            main, gate = _norm_proj(x, g[0], [p["a_main"][layer], p["a_gate"][layer]], nm + "_in")
            h, c_new, n_new, m_new = _mlstm(main, gate, p["a_bgate"][layer], g_a_hnorm[layer],
                                            m_c[layer], m_n[layer], m_m[layer], batch, seq, nm + "_mlstm")
            cs.append(c_new)
            ns.append(n_new)
            ms.append(m_new)
            x = _proj_norm_res(h, p["a_out"][layer], g[1], x, nm + "_out")
        else:
            if layer == N_A_LAYERS:
                cos, sin = _rope_tables(pos0, seq)
                reps = 1 if past is None else batch
                kv4, win, ks_b, kw_b, vst, vwt = _kv_rows(x, g_kv, p["kv"], p["kv_vt"], jnp.tile(cos, (reps, 1)),
                                                          jnp.tile(sin, (reps, 1)), tag + "_kv")
                zeros_hi = jnp.zeros((batch, 2 * KV_HEADS, CMP_HIDDEN), F32)
                if past is None:
                    n_pages = seq // PAGE
                    nb = seq // SEL_BLOCK
                    own = _cmp_blocks(kv4[:, :2 * gw].reshape(batch * n_pages, PAGE, 2, KV_HEADS, HEAD_DIM))
                    kc, vc, _ = _compress(own, jnp.arange(batch * n_pages, dtype=I32), batch, n_pages,
                                          min(CMP_PAGES, n_pages), zeros_hi, p["cmp"], None, tag + "_cmp")
                else:
                    cmp_blocks, k_pages, vt_pages, table, n_pages, cache_win2 = past
                    nb = (pos0 + seq + SEL_BLOCK - 1) // SEL_BLOCK
                    n_tail = nb * (SEL_BLOCK // CMP_STRIDE) - n_pages * (PAGE // CMP_STRIDE)
                    tail = jnp.pad(kv4[:, :2 * gw].reshape(batch, seq, 2, KV_HEADS, HEAD_DIM),
                                   ((0, 0), (0, PAGE - seq), (0, 0), (0, 0), (0, 0)))
                    kc_t, vc_t, hi_t = _compress(_cmp_blocks(tail), jnp.arange(batch, dtype=I32), batch, 1, 1, zeros_hi,
                                                 p["cmp"], n_tail - 1, tag + "_cmpt")
                    kc_m, vc_m, _ = _compress(cmp_blocks, table, batch, n_pages, min(CMP_PAGES, n_pages), hi_t,
                                              p["cmp"], None, tag + "_cmp")
                    kc = jnp.concatenate([kc_m, kc_t[:, :n_tail]], axis=1)
                    vc = jnp.concatenate([vc_m, vc_t[:, :n_tail]], axis=1)
                nbp = -(-nb // 32) * 32
                kc_l = _cmp_layout(kc, nb, nbp)
                vct_l = _cmp_layout(vc, nb, nbp).transpose(0, 2, 1)
            bl = layer - N_A_LAYERS
            q_raw, gate_pre = _norm_proj(x, g[0], [p["b_q"][bl], p["b_gate"][bl]], nm + "_in")
            if past is None:
                o = _attn_prompt(q_raw, gate_pre, p["b_bgate"][bl], cos, sin, kc_l, vct_l, ks_b, vst, kw_b, vwt,
                                 batch, seq, nm + "_attn")
            else:
                o = _attn_sample(q_raw, gate_pre, p["b_bgate"][bl], cos, sin, kc_l, vct_l, k_pages, vt_pages, table,
                                 kv4, cache_win2, win, batch, seq, n_pages, nm + "_attn")
            x = _proj_norm_res(o, p["b_out"][bl], g[1], x, nm + "_out")
        x, conv_new = _ffn(x, g[2], p["up"][layer], ffn_conv_w[layer], ffn_conv_b[layer], p["down"][layer], g[3],
                           conv_prev[layer], batch, seq, nm + "_ffn")
        convs.append(conv_new)
    return (x.reshape(batch, seq, d), kv4, win, jnp.stack(cs), jnp.stack(ns), jnp.stack(ms), jnp.stack(convs))


def _past_views(cache_kv, cache_win_kv, page_table):
    n_pool = cache_kv.shape[0]
    gw = KV_HEADS * HEAD_DIM
    cmp_blocks = _cmp_blocks(cache_kv[:, :, 0:2])
    k_pages = cache_kv[:, :, 2].reshape(n_pool, PAGE, gw).astype(BF16)
    vt_pages = cache_kv[:, :, 3].reshape(n_pool, PAGE, gw).transpose(0, 2, 1).astype(BF16)
    cache_win2 = cache_win_kv.reshape(cache_win_kv.shape[0], cache_win_kv.shape[1], 2 * gw)
    return (cmp_blocks, k_pages, vt_pages, page_table.reshape(-1), page_table.shape[1], cache_win2)


def kernel(x_prompt, x_sample, cache_kv, cache_win_kv, state_mlstm_C, state_mlstm_n, state_mlstm_m, state_conv,
           page_table, g_norms, w_a_in, b_a_gate, g_a_hnorm, w_a_out, g_kv, w_kv, cmp_pos, cmp_w1, cmp_b1, cmp_w2,
           cmp_b2, w_b_in, b_b_gate, w_b_out, w_ffn_up, ffn_conv_w, ffn_conv_b, w_ffn_down):
    p = _prep_weights(w_a_in, b_a_gate, w_a_out, w_kv, cmp_pos, cmp_w1, cmp_b1, cmp_w2, cmp_b2,
                      w_b_in, b_b_gate, w_b_out, w_ffn_up, w_ffn_down)
    dt = x_prompt.dtype
    bp, tp, _ = x_prompt.shape
    bs, ts, _ = x_sample.shape
    past_len = page_table.shape[1] * PAGE
    gw = KV_HEADS * HEAD_DIM
    shared = (p, g_norms, g_a_hnorm, g_kv, ffn_conv_w, ffn_conv_b)

    y_p, kv_p, win_p, c_p, n_p, m_p, conv_p = _trunk(
        x_prompt, 0, None, *shared,
        jnp.zeros((DEPTH, bp, CONV_W - 1, 2 * D_FF), dt),
        jnp.zeros((N_A_LAYERS, bp, M_HEADS, M_QK_DIM, M_V_DIM), dt),
        jnp.zeros((N_A_LAYERS, bp, M_HEADS, M_QK_DIM), dt), jnp.zeros((N_A_LAYERS, bp, M_HEADS), dt), "p")
    kv_p = kv_p.reshape(bp, tp, 4, KV_HEADS, HEAD_DIM)
    n_win_p = min(WINDOW, tp)
    win_p = win_p.reshape(bp, tp, 2, KV_HEADS, HEAD_DIM)[:, tp - n_win_p:]

    past = _past_views(cache_kv, cache_win_kv, page_table)
    y_s, kv_s, win_s, c_s, n_s, m_s, conv_s = _trunk(
        x_sample, past_len, past, *shared, state_conv, state_mlstm_C, state_mlstm_n, state_mlstm_m, "s")
    kv_s = kv_s.reshape(bs, ts, 4, KV_HEADS, HEAD_DIM)
    win_all = jnp.concatenate([past[-1], win_s.reshape(bs, ts, 2 * gw)], axis=1)
    win_s = win_all[:, win_all.shape[1] - min(WINDOW, win_all.shape[1]):].reshape(bs, -1, 2, KV_HEADS, HEAD_DIM)
    return (y_p, y_s, kv_p, kv_s, win_p, win_s, c_p, c_s, n_p, n_s, m_p, m_s, conv_p, conv_s)
```

```python
import functools
import math

import jax
import jax.numpy as jnp
from jax import lax
from jax.experimental import pallas as pl
from jax.experimental.pallas import tpu as pltpu

F32 = jnp.float32
BF16 = jnp.bfloat16
I32 = jnp.int32

D_MODEL = 1024
DEPTH = 4
N_A_LAYERS = 2
M_HEADS = 8
M_QK_DIM = 64
M_V_DIM = 128
GATE_CAP = 15.0
N_HEADS = 16
HEAD_DIM = 64
KV_HEADS = 4
HPG = 4
CMP_BLOCK = 32
CMP_STRIDE = 16
CMP_HIDDEN = 256
SEL_BLOCK = 64
N_SELECT = 16
WINDOW = 512
ROT_DIM = 16
ROPE_THETA = 500000.0
D_FF = 2816
CONV_W = 3
EPS = 1e-6
BIG = 1e9
NEG = -1e30
PAGE = 128

LANES = 128
SUBLANES = 8
VMEM_LIMIT = 56 * 1024 * 1024

MLSTM_CHUNK = 128
KEY_TILE = 512
Q_TILE = 128
SAMPLE_NQ = 32
CMP_PAGES = 16
ATT_PAGES = 8
FFN_TM = 512
FFN_TN = 1408


def _cparams(sem):
    return pltpu.CompilerParams(dimension_semantics=sem, vmem_limit_bytes=VMEM_LIMIT)


def _rms(x, g):
    return x * lax.rsqrt(jnp.mean(x * x, axis=-1, keepdims=True) + EPS) * g


def _nt(a, b):
    return lax.dot_general(a, b, (((1,), (1,)), ((), ())), preferred_element_type=F32)


def _dot(a, b):
    return jnp.dot(a, b, preferred_element_type=F32)


def _row_tile(m, pref):
    t = min(m, pref)
    while m % t:
        t //= 2
    return t


def _norm_proj_kernel(x_ref, g_ref, *refs, n_w):
    xn = _rms(x_ref[...], g_ref[...]).astype(BF16)
    for w_ref, o_ref in zip(refs[:n_w], refs[n_w:]):
        o_ref[...] = _dot(xn, w_ref[...]).astype(o_ref.dtype)


def _norm_proj(x, g, ws, name):
    m, d = x.shape
    tm = _row_tile(m, 512)
    n_w = len(ws)
    in_specs = [pl.BlockSpec((tm, d), lambda i: (i, 0)), pl.BlockSpec((1, d), lambda i: (0, 0))]
    in_specs += [pl.BlockSpec(w.shape, lambda i: (0, 0)) for w in ws]
    out_specs = [pl.BlockSpec((tm, w.shape[1]), lambda i: (i, 0)) for w in ws]
    out_shape = [jax.ShapeDtypeStruct((m, w.shape[1]), F32) for w in ws]
    return pl.pallas_call(
        functools.partial(_norm_proj_kernel, n_w=n_w), grid=(m // tm,), in_specs=in_specs, out_specs=out_specs,
        out_shape=out_shape, compiler_params=_cparams(("parallel",)), name=name)(x, g.reshape(1, d), *ws)


def _proj_norm_res_kernel(a_ref, w_ref, g_ref, res_ref, o_ref):
    y = _dot(a_ref[...].astype(BF16), w_ref[...])
    o_ref[...] = res_ref[...] + _rms(y, g_ref[...])


def _proj_norm_res(a, w, g, res, name):
    m, k = a.shape
    d = w.shape[1]
    tm = _row_tile(m, 512)
    return pl.pallas_call(
        _proj_norm_res_kernel, grid=(m // tm,),
        in_specs=[pl.BlockSpec((tm, k), lambda i: (i, 0)), pl.BlockSpec((k, d), lambda i: (0, 0)),
                  pl.BlockSpec((1, d), lambda i: (0, 0)), pl.BlockSpec((tm, d), lambda i: (i, 0))],
        out_specs=pl.BlockSpec((tm, d), lambda i: (i, 0)),
        out_shape=jax.ShapeDtypeStruct((m, d), F32),
        compiler_params=_cparams(("parallel",)), name=name)(a, w, g.reshape(1, d), res)


def _mlstm_kernel(main_ref, gate_ref, bg_ref, gh_ref, c0_ref, n0_ref, m0_ref,
                  h_ref, c_ref, n_ref, m_ref, c_s, n_s, m_s, *, rows, nc):
    L = MLSTM_CHUNK
    nh = M_HEADS
    cidx = pl.program_id(1)

    @pl.when(cidx == 0)
    def _():
        c_s[...] = c0_ref[...]
        n_s[...] = n0_ref[...]
        m_s[...] = m0_ref[...]

    main = main_ref[...]
    gp = gate_ref[...] + bg_ref[...]
    if rows < L:
        main = jnp.concatenate([main, jnp.zeros((L - rows, main.shape[1]), F32)], axis=0)
        gp = jnp.concatenate([gp, jnp.zeros((L - rows, LANES), F32)], axis=0)
    capped = GATE_CAP * jnp.tanh(gp / GATE_CAP)
    row1 = lax.broadcasted_iota(I32, (L, LANES), 0)
    real = row1 < rows
    ilog = jnp.where(real, capped, -jnp.inf)
    logf = jnp.where(real, jnp.minimum(capped, 0.0) - jnp.log1p(jnp.exp(-jnp.abs(capped))), 0.0)
    bh = logf
    k = 1
    while k < L:
        bh = bh + jnp.where(row1 >= k, pltpu.roll(bh, k, axis=0), 0.0)
        k *= 2
    bh = pltpu.roll(bh, LANES - nh, axis=1)
    c_all = ilog - bh
    cm = c_all
    k = 1
    while k < L:
        cm = jnp.maximum(cm, jnp.where(row1 >= k, pltpu.roll(cm, k, axis=0), -jnp.inf))
        k *= 2
    m_row = m_s[...]
    mt = bh + jnp.maximum(m_row, cm)
    m_new = mt[L - 1:L, :]
    b_last = bh[L - 1:L, :]
    w_inter = jnp.exp(bh + m_row - mt)
    u_all = bh - mt
    emt = jnp.exp(-mt)
    ws_all = jnp.exp(b_last - bh + ilog - m_new)
    decay = jnp.exp(b_last + m_row - m_new)
    c_t = c_all.T

    rr = lax.broadcasted_iota(I32, (L, L), 0)
    cc = lax.broadcasted_iota(I32, (L, L), 1)
    causal = cc <= rr
    lo_half = lax.broadcasted_iota(I32, (L, LANES), 1) < M_QK_DIM
    lo_rows = lax.broadcasted_iota(I32, (2 * M_QK_DIM, LANES), 0) < M_QK_DIM
    ones_b = jnp.ones((L, LANES), BF16)

    def lanes(a, x):
        return jnp.broadcast_to(a[:, x:x + 1], (L, LANES))

    for p in range(nh // 2):
        qp = main[:, p * LANES:(p + 1) * LANES] * (M_QK_DIM ** -0.5)
        kp = main[:, 512 + p * LANES:512 + (p + 1) * LANES]
        c_pair = c_s[2 * p:2 * p + 2].reshape(2 * M_QK_DIM, M_V_DIM)
        n_pair = n_s[p]
        kp_b = kp.astype(BF16)
        state_b = jnp.concatenate([c_pair, n_pair], axis=1).astype(BF16)
        for e in range(2):
            x = 2 * p + e
            qx_b = jnp.where(lo_half if e == 0 else jnp.logical_not(lo_half), qp, 0.0).astype(BF16)
            vx = main[:, 1024 + x * M_V_DIM:1024 + (x + 1) * M_V_DIM]
            ox = main[:, 2048 + x * M_V_DIM:2048 + (x + 1) * M_V_DIM]
            a = jnp.exp(jnp.where(causal, lanes(u_all, x) + c_t[x:x + 1, :], -jnp.inf)) * _nt(qx_b, kp_b)
            inter = _dot(qx_b, state_b)
            intra = _dot(a.astype(BF16), jnp.concatenate([vx.astype(BF16), ones_b], axis=1))
            wi = lanes(w_inter, x)
            num = wi * inter[:, :M_V_DIM] + intra[:, :M_V_DIM]
            den = wi * inter[:, M_V_DIM:] + intra[:, M_V_DIM:]
            h = num / jnp.maximum(jnp.abs(den), lanes(emt, x))
            hn = h * lax.rsqrt(jnp.mean(h * h, axis=-1, keepdims=True) + EPS)
            hn = hn * gh_ref[:, x * M_V_DIM:(x + 1) * M_V_DIM] * jax.nn.sigmoid(ox)
            h_ref[:, x * M_V_DIM:(x + 1) * M_V_DIM] = hn[:rows]
        kw = kp * jnp.where(lo_half, lanes(ws_all, 2 * p), lanes(ws_all, 2 * p + 1))
        vcat = main[:, 1024 + 2 * p * M_V_DIM:1024 + (2 * p + 2) * M_V_DIM]
        upd = lax.dot_general(kw.astype(BF16), jnp.concatenate([vcat.astype(BF16), ones_b], axis=1),
                              (((0,), (0,)), ((), ())), preferred_element_type=F32)
        dec_e = decay[:, 2 * p:2 * p + 1]
        dec_o = decay[:, 2 * p + 1:2 * p + 2]
        c_s[2 * p] = dec_e * c_pair[:M_QK_DIM] + upd[:M_QK_DIM, :M_V_DIM]
        c_s[2 * p + 1] = dec_o * c_pair[M_QK_DIM:] + upd[M_QK_DIM:, M_V_DIM:2 * M_V_DIM]
        n_s[p] = jnp.where(lo_rows, dec_e, dec_o) * n_pair + upd[:, 2 * M_V_DIM:]
    m_s[...] = m_new

    @pl.when(cidx == nc - 1)
    def _():
        c_ref[...] = c_s[...]
        n_ref[...] = n_s[...]
        m_ref[...] = m_s[...]


def _mlstm(main, gate, b_gate, g_hnorm, c0, n0, m0, batch, seq, name):
    rows = min(seq, MLSTM_CHUNK)
    nc = seq // rows
    a_v = M_HEADS * M_V_DIM
    hp = M_HEADS // 2
    n_in = jnp.broadcast_to(n0.reshape(batch, hp, 2 * M_QK_DIM, 1), (batch, hp, 2 * M_QK_DIM, LANES))
    m_in = jnp.pad(m0, ((0, 0), (0, LANES - M_HEADS)))[:, None, :]
    h, c, n, m = pl.pallas_call(
        functools.partial(_mlstm_kernel, rows=rows, nc=nc), grid=(batch, nc),
        in_specs=[pl.BlockSpec((rows, main.shape[1]), lambda b, c: (b * nc + c, 0)),
                  pl.BlockSpec((rows, LANES), lambda b, c: (b * nc + c, 0)),
                  pl.BlockSpec((1, LANES), lambda b, c: (0, 0)),
                  pl.BlockSpec((1, a_v), lambda b, c: (0, 0)),
                  pl.BlockSpec((None, M_HEADS, M_QK_DIM, M_V_DIM), lambda b, c: (b, 0, 0, 0)),
                  pl.BlockSpec((None, hp, 2 * M_QK_DIM, LANES), lambda b, c: (b, 0, 0, 0)),
                  pl.BlockSpec((None, 1, LANES), lambda b, c: (b, 0, 0))],
        out_specs=[pl.BlockSpec((rows, a_v), lambda b, c: (b * nc + c, 0)),
                   pl.BlockSpec((None, M_HEADS, M_QK_DIM, M_V_DIM), lambda b, c: (b, 0, 0, 0)),
                   pl.BlockSpec((None, hp, 2 * M_QK_DIM, LANES), lambda b, c: (b, 0, 0, 0)),
                   pl.BlockSpec((None, 1, LANES), lambda b, c: (b, 0, 0))],
        out_shape=[jax.ShapeDtypeStruct((batch * seq, a_v), F32),
                   jax.ShapeDtypeStruct((batch, M_HEADS, M_QK_DIM, M_V_DIM), F32),
                   jax.ShapeDtypeStruct((batch, hp, 2 * M_QK_DIM, LANES), F32),
                   jax.ShapeDtypeStruct((batch, 1, LANES), F32)],
        scratch_shapes=[pltpu.VMEM((M_HEADS, M_QK_DIM, M_V_DIM), F32), pltpu.VMEM((hp, 2 * M_QK_DIM, LANES), F32),
                        pltpu.VMEM((1, LANES), F32)],
        compiler_params=_cparams(("parallel", "arbitrary")), name=name,
    )(main, gate, b_gate, g_hnorm.reshape(1, a_v), c0, n_in, m_in)
    return h, c, n[..., 0].reshape(batch, M_HEADS, M_QK_DIM), m[:, 0, :M_HEADS]


def _ffn_kernel(*refs, carry, tm, n_j, tiles_per_seq, period):
    x_ref, g2_ref, wua_ref, wug_ref, cwa_ref, cwg_ref, cba_ref, cbg_ref, wd_ref, g3_ref = refs[:10]
    if carry:
        inita_ref, initg_ref, o_ref, sa_ref, sg_ref, xn_s, acc_s, ue_s, carry_s = refs[10:]
        branch_in = ((wua_ref, cwa_ref, cba_ref, inita_ref, sa_ref), (wug_ref, cwg_ref, cbg_ref, initg_ref, sg_ref))
    else:
        t1a_ref, t1g_ref, t2a_ref, t2g_ref, o_ref, sa_ref, sg_ref, xn_s, acc_s, ue_s = refs[10:]
        branch_in = ((wua_ref, cwa_ref, cba_ref, (t1a_ref, t2a_ref), sa_ref),
                     (wug_ref, cwg_ref, cbg_ref, (t1g_ref, t2g_ref), sg_ref))
    i = pl.program_id(0)
    j = pl.program_id(1)
    tn = wd_ref.shape[0]

    @pl.when(j == 0)
    def _():
        xn_s[...] = _rms(x_ref[...], g2_ref[...]).astype(BF16)
        acc_s[...] = jnp.zeros_like(acc_s)

    xn = xn_s[...]
    conv = []
    for which, (w_ref, cw_ref, cb_ref, boundary, s_ref) in enumerate(branch_in):
        u = _dot(xn, w_ref[...])
        ue_s[SUBLANES:, :] = u
        if carry:
            slot = which * n_j + j
            first = (i % tiles_per_seq) == 0
            ue_s[SUBLANES - 2:SUBLANES, :] = jnp.where(first, boundary[...], carry_s[slot])
            tap1 = ue_s[pl.ds(SUBLANES - 1, tm), :]
            tap2 = ue_s[pl.ds(SUBLANES - 2, tm), :]
            last2 = u[tm - 2:tm, :]
            carry_s[slot] = last2
            s_ref[i // tiles_per_seq, j] = last2
        else:
            ue_s[0:SUBLANES, :] = jnp.zeros((SUBLANES, tn), F32)
            t = lax.broadcasted_iota(I32, (tm, tn), 0) & (period - 1)
            tap1 = jnp.where(t >= 1, ue_s[pl.ds(SUBLANES - 1, tm), :], boundary[0][...])
            tap2 = jnp.where(t >= 2, ue_s[pl.ds(SUBLANES - 2, tm), :], boundary[1][...])
            s_ref[...] = u
        conv.append(cb_ref[...] + tap2 * cw_ref[0:1, :] + tap1 * cw_ref[1:2, :] + u * cw_ref[2:3, :])
    y = jax.nn.gelu(conv[0], approximate=True) * conv[1]
    acc_s[...] += _dot(y.astype(BF16), wd_ref[...])

    @pl.when(j == n_j - 1)
    def _():
        o_ref[...] = x_ref[...] + _rms(acc_s[...], g3_ref[...])


def _ffn(x, g2, w_up, conv_w, conv_b, w_down, g3, prev, batch, seq, name):
    m, d = x.shape
    nf = w_down.shape[0]
    tn = FFN_TN
    n_j = nf // tn
    carry = seq >= 256
    cb = conv_b.reshape(1, 2 * nf)
    col_a = lambda i, j: (0, j)
    col_g = lambda i, j: (0, j + n_j)
    common_specs = [
        None,
        pl.BlockSpec((1, d), lambda i, j: (0, 0)),
        pl.BlockSpec((d, tn), col_a), pl.BlockSpec((d, tn), col_g),
        pl.BlockSpec((CONV_W, tn), col_a), pl.BlockSpec((CONV_W, tn), col_g),
        pl.BlockSpec((1, tn), col_a), pl.BlockSpec((1, tn), col_g),
        pl.BlockSpec((tn, d), lambda i, j: (j, 0)),
        pl.BlockSpec((1, d), lambda i, j: (0, 0)),
    ]
    common_args = [x, g2.reshape(1, d), w_up, w_up, conv_w, conv_w, cb, cb, w_down, g3.reshape(1, d)]
    if carry:
        tm = _row_tile(seq, FFN_TM)
        tps = seq // tm
        common_specs[0] = pl.BlockSpec((tm, d), lambda i, j: (i, 0))
        st_a = lambda i, j: (i // tps, 0, j)
        st_g = lambda i, j: (i // tps, 0, j + n_j)
        out, sa, sg = pl.pallas_call(
            functools.partial(_ffn_kernel, carry=True, tm=tm, n_j=n_j, tiles_per_seq=tps, period=seq),
            grid=(m // tm, n_j),
            in_specs=common_specs + [pl.BlockSpec((None, 2, tn), st_a), pl.BlockSpec((None, 2, tn), st_g)],
            out_specs=[pl.BlockSpec((tm, d), lambda i, j: (i, 0)),
                       pl.BlockSpec((batch, n_j, 2, tn), lambda i, j: (0, 0, 0, 0)),
                       pl.BlockSpec((batch, n_j, 2, tn), lambda i, j: (0, 0, 0, 0))],
            out_shape=[jax.ShapeDtypeStruct((m, d), F32), jax.ShapeDtypeStruct((batch, n_j, 2, tn), F32),
                       jax.ShapeDtypeStruct((batch, n_j, 2, tn), F32)],
            scratch_shapes=[pltpu.VMEM((tm, d), BF16), pltpu.VMEM((tm, d), F32),
                            pltpu.VMEM((tm + SUBLANES, tn), F32), pltpu.VMEM((2 * n_j, 2, tn), F32)],
            compiler_params=_cparams(("arbitrary", "arbitrary")), name=name,
        )(*common_args, prev, prev)
        sa, sg = (s.transpose(0, 2, 1, 3).reshape(batch, 2, nf) for s in (sa, sg))
        return out, jnp.concatenate([sa, sg], axis=-1)
    tm = m
    assert seq >= 2 and seq & (seq - 1) == 0
    common_specs[0] = pl.BlockSpec((tm, d), lambda i, j: (i, 0))
    tap1 = jnp.pad(prev[:, 1:2], ((0, 0), (0, seq - 1), (0, 0))).reshape(m, 2 * nf)
    tap2 = jnp.pad(prev, ((0, 0), (0, seq - 2), (0, 0))).reshape(m, 2 * nf)
    row_a = lambda i, j: (i, j)
    row_g = lambda i, j: (i, j + n_j)
    out, ua, ug = pl.pallas_call(
        functools.partial(_ffn_kernel, carry=False, tm=tm, n_j=n_j, tiles_per_seq=1, period=seq),
        grid=(m // tm, n_j),
        in_specs=common_specs + [pl.BlockSpec((tm, tn), row_a), pl.BlockSpec((tm, tn), row_g),
                                 pl.BlockSpec((tm, tn), row_a), pl.BlockSpec((tm, tn), row_g)],
        out_specs=[pl.BlockSpec((tm, d), lambda i, j: (i, 0)), pl.BlockSpec((tm, tn), row_a),
                   pl.BlockSpec((tm, tn), row_a)],
        out_shape=[jax.ShapeDtypeStruct((m, d), F32), jax.ShapeDtypeStruct((m, nf), F32),
                   jax.ShapeDtypeStruct((m, nf), F32)],
        scratch_shapes=[pltpu.VMEM((tm, d), BF16), pltpu.VMEM((tm, d), F32), pltpu.VMEM((tm + SUBLANES, tn), F32)],
        compiler_params=_cparams(("arbitrary", "arbitrary")), name=name,
    )(*common_args, tap1, tap1, tap2, tap2)
    u = jnp.concatenate([ua, ug], axis=-1).reshape(batch, seq, 2 * nf)
    return out, u[:, seq - 2:]


def _rope_pair(x, cos, sin):
    half = ROT_DIM // 2
    lane = lax.broadcasted_iota(I32, x.shape, 1) & (HEAD_DIM - 1)
    partner = jnp.where(lane < half, pltpu.roll(x, LANES - half, axis=1), pltpu.roll(x, half, axis=1))
    return x * cos + partner * sin


def _rope(x, cos, sin):
    return jnp.concatenate(
        [_rope_pair(x[:, c * LANES:(c + 1) * LANES], cos, sin) for c in range(x.shape[1] // LANES)], axis=1)


def _rope_tables(pos0, seq):
    half = ROT_DIM // 2
    inv = jnp.power(jnp.float32(ROPE_THETA), -jnp.arange(0, ROT_DIM, 2, dtype=F32) / ROT_DIM)
    ang = (pos0 + jnp.arange(seq, dtype=I32)).astype(F32)[:, None] * inv[None, :]
    cos, sin = jnp.cos(ang), jnp.sin(ang)
    rest = HEAD_DIM - ROT_DIM
    cos_h = jnp.concatenate([cos, cos, jnp.ones((seq, rest), F32)], axis=1)
    sin_h = jnp.concatenate([-sin, sin, jnp.zeros((seq, rest), F32)], axis=1)
    return jnp.tile(cos_h, (1, LANES // HEAD_DIM)), jnp.tile(sin_h, (1, LANES // HEAD_DIM))


def _kv_kernel(x_ref, g_ref, w_ref, wvt_ref, wct_ref, cos_ref, sin_ref, kv_ref, win_ref, ks_ref, kw_ref, vst_ref, vwt_ref,
               cmpt_ref):
    xn = _rms(x_ref[...], g_ref[...]).astype(BF16)
    y = _dot(xn, w_ref[...])
    cos, sin = cos_ref[...], sin_ref[...]
    gw = KV_HEADS * HEAD_DIM
    ks = _rope(y[:, 2 * gw:3 * gw], cos, sin)
    kw = _rope(y[:, 4 * gw:5 * gw], cos, sin)
    kv_ref[:, 0:2 * gw] = y[:, 0:2 * gw]
    kv_ref[:, 2 * gw:3 * gw] = ks
    kv_ref[:, 3 * gw:4 * gw] = y[:, 3 * gw:4 * gw]
    win_ref[:, 0:gw] = kw
    win_ref[:, gw:2 * gw] = y[:, 5 * gw:6 * gw]
    ks_ref[...] = ks.astype(BF16)
    kw_ref[...] = kw.astype(BF16)
    vt = _nt(wvt_ref[...], xn).astype(BF16)
    vst_ref[...] = vt[0:gw]
    ct = _nt(wct_ref[...], xn)
    for u in range(vwt_ref.shape[0]):
        vwt_ref[u] = vt[gw:2 * gw, u * PAGE:(u + 1) * PAGE]
        cmpt_ref[u, 0] = ct[0:gw, u * PAGE:(u + 1) * PAGE]
        cmpt_ref[u, 1] = ct[gw:2 * gw, u * PAGE:(u + 1) * PAGE]


def _kv_rows(x, g_kv, w_kv, w_vt, w_ct, cos, sin, name):
    m, d = x.shape
    tm = _row_tile(m, KEY_TILE)
    gw = KV_HEADS * HEAD_DIM
    ttiles = cos.shape[0] // tm
    return pl.pallas_call(
        _kv_kernel, grid=(m // tm,),
        in_specs=[pl.BlockSpec((tm, d), lambda i: (i, 0)), pl.BlockSpec((1, d), lambda i: (0, 0)),
                  pl.BlockSpec(w_kv.shape, lambda i: (0, 0)), pl.BlockSpec(w_vt.shape, lambda i: (0, 0)),
                  pl.BlockSpec(w_ct.shape, lambda i: (0, 0)),
                  pl.BlockSpec((tm, LANES), lambda i: (i % ttiles, 0)),
                  pl.BlockSpec((tm, LANES), lambda i: (i % ttiles, 0))],
        out_specs=[pl.BlockSpec((tm, 4 * gw), lambda i: (i, 0)), pl.BlockSpec((tm, 2 * gw), lambda i: (i, 0)),
                   pl.BlockSpec((tm, gw), lambda i: (i, 0)), pl.BlockSpec((tm, gw), lambda i: (i, 0)),
                   pl.BlockSpec((None, gw, tm), lambda i: (i, 0, 0)),
                   pl.BlockSpec((tm // PAGE, gw, PAGE), lambda i: (i, 0, 0)),
                   pl.BlockSpec((tm // PAGE, 2, gw, PAGE), lambda i: (i, 0, 0, 0))],
        out_shape=[jax.ShapeDtypeStruct((m, 4 * gw), F32), jax.ShapeDtypeStruct((m, 2 * gw), F32),
                   jax.ShapeDtypeStruct((m, gw), BF16), jax.ShapeDtypeStruct((m, gw), BF16),
                   jax.ShapeDtypeStruct((m // tm, gw, tm), BF16), jax.ShapeDtypeStruct((m // PAGE, gw, PAGE), BF16),
                   jax.ShapeDtypeStruct((m // PAGE, 2, gw, PAGE), F32)],
        compiler_params=_cparams(("parallel",)), name=name)(x, g_kv.reshape(1, d), w_kv, w_vt, w_ct, cos, sin)


def _token_minor(rows):
    n_p, r, n = rows.shape[0], rows.shape[1], rows.shape[2]
    return rows.transpose(0, 2, 3, 4, 1).reshape(n_p, n, KV_HEADS * HEAD_DIM, r)


def _compress_kernel(tbl_ref, *refs, n_pg, zero_after):
    page_refs = refs[:n_pg]
    (pos_lo_ref, pos_hi_ref, w1lo_ref, w1hi_ref, b1_ref, w2_ref, b2_ref, hi_init_ref,
     kc_ref, vc_ref, hi_first_ref, carry_s, t_s, x_s) = refs[n_pg:]
    c = pl.program_id(1)
    njp = PAGE // CMP_STRIDE
    njc = n_pg * njp

    @pl.when(c == 0)
    def _():
        carry_s[...] = hi_init_ref[...]

    lo_lanes = lax.broadcasted_iota(I32, (njp, LANES), 1) < HEAD_DIM
    for u in range(n_pg):
        for br in range(2):
            for cp in range(KV_HEADS // 2):
                tb = t_s.at[(2 * u + br) % 2 * 2 + cp]
                tb[...] = page_refs[u][br, cp * LANES:(cp + 1) * LANES, :].T
                for q in range(CMP_STRIDE // 2):
                    p0 = tb[pl.ds(2 * q, njp, stride=CMP_STRIDE), :]
                    p1 = tb[pl.ds(2 * q + 1, njp, stride=CMP_STRIDE), :]
                    rows = slice(u * njp, (u + 1) * njp)
                    cols = slice(q * LANES, (q + 1) * LANES)
                    x_s[br * KV_HEADS + 2 * cp, rows, cols] = jnp.where(lo_lanes, p0, pltpu.roll(p1, HEAD_DIM, axis=1))
                    x_s[br * KV_HEADS + 2 * cp + 1, rows, cols] = jnp.where(lo_lanes, pltpu.roll(p0, HEAD_DIM, axis=1), p1)

    rowl = lax.broadcasted_iota(I32, (KV_HEADS * njc, CMP_HIDDEN), 0) & (njc - 1)
    for br, out_ref in enumerate((kc_ref, vc_ref)):
        x = x_s[br * KV_HEADS:(br + 1) * KV_HEADS].reshape(KV_HEADS * njc, CMP_STRIDE * HEAD_DIM)
        lo = _dot((x + pos_lo_ref[br]).astype(BF16), w1lo_ref[br])
        hi = _dot((x + pos_hi_ref[br]).astype(BF16), w1hi_ref[br])
        carry_rows = jnp.concatenate(
            [jnp.broadcast_to(carry_s[br * KV_HEADS + g:br * KV_HEADS + g + 1, :], (njc, CMP_HIDDEN))
             for g in range(KV_HEADS)], axis=0)
        hi_next = jnp.where(rowl == njc - 1, carry_rows, pltpu.roll(hi, KV_HEADS * njc - 1, axis=0))
        if zero_after is not None:
            hi_next = jnp.where(jnp.logical_and(rowl == zero_after, c == 0), 0.0, hi_next)
        for g in range(KV_HEADS):
            carry_s[br * KV_HEADS + g:br * KV_HEADS + g + 1, :] = hi[g * njc:g * njc + 1, :]
        h = jax.nn.gelu(lo + hi_next + b1_ref[br], approximate=True)
        o = _dot(h.astype(BF16), w2_ref[br]) + b2_ref[br]
        for g in range(KV_HEADS):
            out_ref[:, g * HEAD_DIM:(g + 1) * HEAD_DIM] = o[g * njc:(g + 1) * njc, :]
    hi_first_ref[...] = carry_s[...]


def _compress(pages, table, batch, n_pages, n_pg, hi_init, cw, zero_after, name):
    njp = PAGE // CMP_STRIDE
    njc = n_pg * njp
    assert njc & (njc - 1) == 0 and njc % SUBLANES == 0 and n_pages % n_pg == 0
    n_ch = n_pages // n_pg
    gw = KV_HEADS * HEAD_DIM
    flat = CMP_STRIDE * HEAD_DIM

    def page_map(u):
        return lambda b, c, tbl: (tbl[b * n_pages + (n_ch - 1 - c) * n_pg + u], 0, 0, 0)

    full3 = lambda b, c, tbl: (0, 0, 0)
    in_specs = [pl.BlockSpec((None, 2, gw, PAGE), page_map(u)) for u in range(n_pg)]
    in_specs += [pl.BlockSpec((2, 1, flat), full3), pl.BlockSpec((2, 1, flat), full3),
                 pl.BlockSpec((2, flat, CMP_HIDDEN), full3), pl.BlockSpec((2, flat, CMP_HIDDEN), full3),
                 pl.BlockSpec((2, 1, CMP_HIDDEN), full3), pl.BlockSpec((2, CMP_HIDDEN, HEAD_DIM), full3),
                 pl.BlockSpec((2, 1, HEAD_DIM), full3),
                 pl.BlockSpec((None, 2 * KV_HEADS, CMP_HIDDEN), lambda b, c, tbl: (b, 0, 0))]
    out_map = lambda b, c, tbl: (b, n_ch - 1 - c, 0)
    grid_spec = pltpu.PrefetchScalarGridSpec(
        num_scalar_prefetch=1, grid=(batch, n_ch), in_specs=in_specs,
        out_specs=[pl.BlockSpec((None, njc, gw), out_map), pl.BlockSpec((None, njc, gw), out_map),
                   pl.BlockSpec((None, 2 * KV_HEADS, CMP_HIDDEN), lambda b, c, tbl: (b, 0, 0))],
        scratch_shapes=[pltpu.VMEM((2 * KV_HEADS, CMP_HIDDEN), F32), pltpu.VMEM((4, PAGE, LANES), F32),
                        pltpu.VMEM((2 * KV_HEADS, njc, flat), F32)])
    return pl.pallas_call(
        functools.partial(_compress_kernel, n_pg=n_pg, zero_after=zero_after), grid_spec=grid_spec,
        out_shape=[jax.ShapeDtypeStruct((batch, n_ch * njc, gw), F32), jax.ShapeDtypeStruct((batch, n_ch * njc, gw), F32),
                   jax.ShapeDtypeStruct((batch, 2 * KV_HEADS, CMP_HIDDEN), F32)],
        compiler_params=_cparams(("parallel", "arbitrary")), name=name,
    )(table, *([pages] * n_pg), cw["pos_lo"], cw["pos_hi"], cw["w1lo"], cw["w1hi"], cw["b1"], cw["w2"], cw["b2"], hi_init)


def _cmp_layout(kc_nat, nb, nbp):
    b = kc_nat.shape[0]
    x = kc_nat[:, :4 * nb].reshape(b, nb, 4, kc_nat.shape[-1]).transpose(0, 2, 1, 3)
    x = jnp.pad(x, ((0, 0), (0, 0), (0, nbp - nb), (0, 0)))
    return x.reshape(b, 4 * nbp, kc_nat.shape[-1]).astype(BF16)


SCALE = HEAD_DIM ** -0.5
QSCALE = SCALE * math.log2(math.e)
SUM_ROWS = 16
ACC_ROWS = HEAD_DIM + SUM_ROWS


def _group_queries(q, g):
    nq = q.shape[0]
    keep = (lax.broadcasted_iota(I32, (nq, LANES), 1) >> 6) == (g % 2)
    pieces = []
    for hh in range(HPG):
        h = HPG * g + hh
        chunk = q[:, (h // 2) * LANES:(h // 2 + 1) * LANES]
        if h % 2 != g % 2:
            chunk = pltpu.roll(chunk, HEAD_DIM, axis=1)
        pieces.append(jnp.where(keep, chunk, 0.0))
    return (jnp.concatenate(pieces, axis=0) * QSCALE).astype(BF16)


def _kchunk(k, g):
    return k[:, (g // 2) * LANES:(g // 2 + 1) * LANES]


def _values(vt, g):
    return jnp.concatenate([vt[g * HEAD_DIM:(g + 1) * HEAD_DIM, :], jnp.ones((SUM_ROWS, vt.shape[1]), BF16)], axis=0)


def _flash_groups(ss, vt, m_s, acc_s):
    ps, alphas = [], []
    for g in range(KV_HEADS):
        m_old = m_s[g]
        m_new = jnp.maximum(m_old, jnp.max(ss[g], axis=0, keepdims=True))
        alphas.append(jnp.exp2(m_old - m_new))
        ps.append(jnp.exp2(ss[g] - m_new).astype(BF16))
        m_s[g] = m_new
    for g in range(KV_HEADS):
        acc_s[g] = acc_s[g] * alphas[g] + _dot(_values(vt, g), ps[g])


def _one_shot_groups(ss, vt):
    ps = [jnp.exp2(s - jnp.max(s, axis=0, keepdims=True)).astype(BF16) for s in ss]
    return [_dot(_values(vt, g), ps[g]) for g in range(KV_HEADS)]


def _init_state(m_s, acc_s):
    m_s[...] = jnp.full(m_s.shape, NEG, F32)
    acc_s[...] = jnp.zeros(acc_s.shape, F32)


def _topk_bias(score, n_sel):
    n_iota = lax.broadcasted_iota(I32, score.shape, 0)

    def body(_, sc):
        mx = jnp.max(sc, axis=0, keepdims=True)
        idx = jnp.min(jnp.where(sc == mx, n_iota, score.shape[0]), axis=0, keepdims=True)
        return jnp.where(n_iota == idx, -jnp.inf, sc)

    left = lax.fori_loop(0, n_sel, body, score)
    return jnp.where(jnp.logical_and(left == -jnp.inf, score > -jnp.inf), 0.0, NEG)


def _cmp_valid(qpos_w, nb, nbp):
    n_w = lax.broadcasted_iota(I32, (nbp, qpos_w.shape[1]), 0)
    return jnp.concatenate(
        [jnp.logical_and((4 * n_w + c) * CMP_STRIDE + (CMP_BLOCK - 1) <= qpos_w, n_w < nb) for c in range(4)], axis=0)


def _cmp_branch(s, valid, vct_ref, g, qpos_w, qpos_q, nb, nbp, nq):
    w = s.shape[1]
    s = jnp.where(valid, s, NEG)
    e = jnp.exp2(s - jnp.max(s, axis=0, keepdims=True))
    some = (qpos_w >= CMP_BLOCK - 1).astype(F32)
    p = e * (some / jnp.maximum(jnp.sum(e, axis=0, keepdims=True), 1e-30))
    o_cmp = _dot(vct_ref[g * HEAD_DIM:(g + 1) * HEAD_DIM, :], p.astype(BF16))
    if nq == LANES:
        ps = ((p[:, 0:LANES] + p[:, LANES:2 * LANES]) + p[:, 2 * LANES:3 * LANES]) + p[:, 3 * LANES:4 * LANES]
    else:
        assert w == LANES
        ps = ((p + pltpu.roll(p, nq, axis=1)) + pltpu.roll(p, 2 * nq, axis=1)) + pltpu.roll(p, 3 * nq, axis=1)
    parts = [ps[c * nbp:(c + 1) * nbp] for c in range(4)]
    n_q = lax.broadcasted_iota(I32, (nbp, LANES), 0)
    prev = jnp.where(n_q >= 1, pltpu.roll(parts[3], 1, axis=0), 0.0)
    score = (((parts[0] + parts[1]) + parts[2]) + parts[3]) + prev
    cur = qpos_q >> 6
    forced = (n_q == 0) | (n_q == cur) | (n_q == cur - 1)
    score = jnp.where(forced, BIG, jnp.where(n_q * SEL_BLOCK <= qpos_q, score, -BIG))
    score = jnp.where(n_q < nb, score, -jnp.inf)
    return o_cmp, score


def _compressed_and_select(q, kc_ref, vct_ref, qpos_w, qpos_q, ocmp_s, score_s, bias_s, nb, nbp, nq):
    w = HPG * nq
    kc_all = kc_ref[...]
    cs = [_nt(_kchunk(kc_all, g), _group_queries(q, g)) for g in range(KV_HEADS)]
    valid = _cmp_valid(qpos_w, nb, nbp)
    for g in range(KV_HEADS):
        o_cmp, score = _cmp_branch(cs[g], valid, vct_ref, g, qpos_w, qpos_q, nb, nbp, nq)
        ocmp_s[g] = o_cmp
        score_s[:, g * LANES:(g + 1) * LANES] = score
    bias = _topk_bias(score_s[...], min(N_SELECT, nb))
    for g in range(KV_HEADS):
        b = bias[:, g * LANES:(g + 1) * LANES]
        bias_s[g] = b if w == LANES else jnp.concatenate([b] * (w // LANES), axis=1)


def _block_bias(bias_rows, n_blk):
    w = bias_rows.shape[1]
    return jnp.concatenate([jnp.broadcast_to(bias_rows[u:u + 1, :], (SEL_BLOCK, w)) for u in range(n_blk)], axis=0)


def _gate_row(gates_t, c, g, nq, w):
    rows = [gates_t[(c * HPG + hh) * KV_HEADS + g:(c * HPG + hh) * KV_HEADS + g + 1, :] for hh in range(HPG)]
    if nq == LANES:
        return jnp.concatenate(rows, axis=1)
    strip = lax.broadcasted_iota(I32, (1, LANES), 1) >> (nq.bit_length() - 1)
    out = jnp.zeros((1, LANES), F32)
    for hh in range(HPG):
        out = jnp.where(strip == hh, rows[hh] if hh == 0 else pltpu.roll(rows[hh], hh * nq, axis=1), out)
    return out


def _finish(o_ref, acc_s, ocmp_s, wins, ot_s, gates_t, nq, nq_real, w):
    for g in range(KV_HEADS):
        o = ocmp_s[g] * _gate_row(gates_t, 0, g, nq, w)
        for c, acc in ((1, acc_s[g]), (2, wins[g])):
            o = o + acc[:HEAD_DIM] * (_gate_row(gates_t, c, g, nq, w) / acc[HEAD_DIM:HEAD_DIM + 1])
        ot_s[g * HEAD_DIM:(g + 1) * HEAD_DIM, :] = o
    o_t = ot_s[...].T
    for hh in range(HPG):
        o_ref[:, hh * KV_HEADS * HEAD_DIM:(hh + 1) * KV_HEADS * HEAD_DIM] = o_t[hh * nq:hh * nq + nq_real, :]


def _attn_prompt_kernel(q_ref, gp_ref, bg_ref, cos_ref, sin_ref, kc_ref, vct_ref, ks_ref, vst_ref, kw_ref, vwt_ref,
                        o_ref, m_s, acc_s, score_s, bias_s, ocmp_s, ot_s, qzr_s, *, nb, nbp):
    i = pl.program_id(1)
    nq = Q_TILE
    w = HPG * nq
    s0 = i * nq
    q = q_ref[...]
    q_rot = _rope(q, cos_ref[...], sin_ref[...])
    gates_t = jax.nn.sigmoid(gp_ref[...] + bg_ref[...]).T
    qpos_w = s0 + (lax.broadcasted_iota(I32, (1, w), 1) & (nq - 1))
    qpos_q = s0 + lax.broadcasted_iota(I32, (1, LANES), 1)
    _init_state(m_s, acc_s)
    for g in range(KV_HEADS):
        qzr_s[g] = _group_queries(q_rot, g)
    _compressed_and_select(q, kc_ref, vct_ref, qpos_w, qpos_q, ocmp_s, score_s, bias_s, nb, nbp, nq)

    blk_per_tile = KEY_TILE // SEL_BLOCK

    def slc_tile(t, causal):
        k0 = pl.multiple_of(t * KEY_TILE, KEY_TILE)
        kt = ks_ref[pl.ds(k0, KEY_TILE), :]
        ss = []
        for g in range(KV_HEADS):
            rows = bias_s[g, pl.ds(pl.multiple_of(t * blk_per_tile, blk_per_tile), blk_per_tile), :]
            s = _nt(_kchunk(kt, g), qzr_s[g]) + _block_bias(rows, blk_per_tile)
            if causal:
                s = jnp.where(k0 + lax.broadcasted_iota(I32, (KEY_TILE, w), 0) <= qpos_w, s, NEG)
            ss.append(s)
        _flash_groups(ss, vst_ref[t], m_s, acc_s)

    t_diag = s0 // KEY_TILE

    def full_tile(t, carry):
        slc_tile(t, False)
        return carry

    lax.fori_loop(0, t_diag, full_tile, 0)
    slc_tile(t_diag, True)

    n_wt = (WINDOW + nq) // PAGE
    t0 = jnp.maximum(i - WINDOW // PAGE, 0)
    k0 = pl.multiple_of(t0 * PAGE, PAGE)
    kwin = kw_ref[pl.ds(k0, n_wt * PAGE), :]
    vwin_tiles = vwt_ref[pl.ds(t0, n_wt)]
    vwin = jnp.concatenate([vwin_tiles[u] for u in range(n_wt)], axis=1)
    kpos = k0 + lax.broadcasted_iota(I32, (n_wt * PAGE, w), 0)
    wmask = jnp.logical_and(kpos <= qpos_w, qpos_w - kpos < WINDOW)
    ws = [jnp.where(wmask, _nt(_kchunk(kwin, g), qzr_s[g]), NEG) for g in range(KV_HEADS)]
    _finish(o_ref, acc_s, ocmp_s, _one_shot_groups(ws, vwin), ot_s, gates_t, nq, nq, w)


def _attn_prompt(q, gate_pre, b_gate, cos, sin, kc, vct, ks, vst, kw, vwt, batch, seq, name):
    assert seq % KEY_TILE == 0 and seq >= WINDOW + Q_TILE
    m, d = q.shape
    nq = Q_TILE
    w = HPG * nq
    nqb = seq // nq
    nb = seq // SEL_BLOCK
    nbp = kc.shape[1] // 4
    gw = KV_HEADS * HEAD_DIM
    ntile = seq // KEY_TILE
    row_map = lambda b, i: (b * nqb + i, 0)
    per_b2 = lambda b, i: (b, 0)
    per_b3 = lambda b, i: (b, 0, 0)
    return pl.pallas_call(
        functools.partial(_attn_prompt_kernel, nb=nb, nbp=nbp), grid=(batch, nqb),
        in_specs=[pl.BlockSpec((nq, d), row_map), pl.BlockSpec((nq, LANES), row_map),
                  pl.BlockSpec((1, LANES), lambda b, i: (0, 0)),
                  pl.BlockSpec((nq, LANES), lambda b, i: (i, 0)), pl.BlockSpec((nq, LANES), lambda b, i: (i, 0)),
                  pl.BlockSpec((None, 4 * nbp, gw), per_b3), pl.BlockSpec((None, gw, 4 * nbp), per_b3),
                  pl.BlockSpec((seq, gw), per_b2), pl.BlockSpec((ntile, gw, KEY_TILE), per_b3),
                  pl.BlockSpec((seq, gw), per_b2), pl.BlockSpec((seq // PAGE, gw, PAGE), per_b3)],
        out_specs=pl.BlockSpec((nq, d), row_map),
        out_shape=jax.ShapeDtypeStruct((m, d), F32),
        scratch_shapes=[pltpu.VMEM((KV_HEADS, 1, w), F32), pltpu.VMEM((KV_HEADS, ACC_ROWS, w), F32),
                        pltpu.VMEM((nbp, KV_HEADS * LANES), F32), pltpu.VMEM((KV_HEADS, nbp, w), F32),
                        pltpu.VMEM((KV_HEADS, HEAD_DIM, w), F32), pltpu.VMEM((gw, w), F32),
                        pltpu.VMEM((KV_HEADS, w, LANES), BF16)],
        compiler_params=_cparams(("parallel", "arbitrary")), name=name,
    )(q, gate_pre, b_gate, cos, sin, kc, vct, ks, vst, kw, vwt)


def _pad_rows(x, n):
    return jnp.concatenate([x, jnp.zeros((n - x.shape[0], x.shape[1]), x.dtype)], axis=0)


def _attn_sample_kernel(tbl_ref, *refs, n_pg, n_steps, nb, nbp, pos0, n_new):
    k_refs = refs[:n_pg]
    vt_refs = refs[n_pg:2 * n_pg]
    (q_ref, gp_ref, bg_ref, cos_ref, sin_ref, kc_ref, vct_ref, kvn_ref, cwin_ref, wnew_ref,
     o_ref, m_s, acc_s, score_s, bias_s, ocmp_s, ot_s, qzr_s, gt_s) = refs[2 * n_pg:]
    step = pl.program_id(1)
    nq = SAMPLE_NQ
    w = HPG * nq
    gw = KV_HEADS * HEAD_DIM
    qpos_w = pos0 + (lax.broadcasted_iota(I32, (1, w), 1) & (nq - 1))

    @pl.when(step == 0)
    def _():
        q = _pad_rows(q_ref[...], nq)
        q_rot = _pad_rows(_rope(q_ref[...], cos_ref[...], sin_ref[...]), nq)
        gt_s[...] = _pad_rows(jax.nn.sigmoid(gp_ref[...] + bg_ref[...]), LANES).T
        _init_state(m_s, acc_s)
        for g in range(KV_HEADS):
            qzr_s[g] = _group_queries(q_rot, g)
        _compressed_and_select(q, kc_ref, vct_ref, qpos_w, qpos_w, ocmp_s, score_s, bias_s, nb, nbp, nq)

    blk_pp = PAGE // SEL_BLOCK
    blk_ps = n_pg * blk_pp
    kcs = [jnp.concatenate([r[cp * LANES:(cp + 1) * LANES, :].T for r in k_refs], axis=0).astype(BF16)
           for cp in range(KV_HEADS // 2)]
    vt = jnp.concatenate([r[...] for r in vt_refs], axis=1).astype(BF16)
    ss = []
    for g in range(KV_HEADS):
        rows = bias_s[g, pl.ds(pl.multiple_of(step * blk_ps, blk_ps), blk_ps), :]
        ss.append(_nt(kcs[g // 2], qzr_s[g]) + _block_bias(rows, blk_ps))
    _flash_groups(ss, vt, m_s, acc_s)

    @pl.when(step == n_steps - 1)
    def _():
        krow = lax.broadcasted_iota(I32, (PAGE, w), 0)
        kvn = _pad_rows(kvn_ref[...], PAGE)
        kn = kvn[:, 2 * gw:3 * gw].astype(BF16)
        vnt = kvn[:, 3 * gw:4 * gw].T.astype(BF16)
        nb0 = pos0 // SEL_BLOCK
        causal = pos0 + krow <= qpos_w
        ts = [jnp.where(causal, _nt(_kchunk(kn, g), qzr_s[g]) + _block_bias(bias_s[g, nb0:nb0 + blk_pp, :], blk_pp), NEG)
              for g in range(KV_HEADS)]
        _flash_groups(ts, vnt, m_s, acc_s)

        n_cached = cwin_ref.shape[2]
        wn = _pad_rows(wnew_ref[...], PAGE)
        kws = [jnp.concatenate([wn[:, cp * LANES:(cp + 1) * LANES], cwin_ref[0, cp * LANES:(cp + 1) * LANES, :].T],
                               axis=0).astype(BF16) for cp in range(KV_HEADS // 2)]
        vwin = jnp.concatenate([wn[:, gw:2 * gw].T, cwin_ref[1]], axis=1).astype(BF16)
        kpos = jnp.concatenate([pos0 + krow, pos0 - n_cached + lax.broadcasted_iota(I32, (n_cached, w), 0)], axis=0)
        wmask = jnp.logical_and(jnp.logical_and(kpos <= qpos_w, qpos_w - kpos < WINDOW), kpos >= 0)
        ws = [jnp.where(wmask, _nt(kws[g // 2], qzr_s[g]), NEG) for g in range(KV_HEADS)]
        _finish(o_ref, acc_s, ocmp_s, _one_shot_groups(ws, vwin), ot_s, gt_s[...], nq, n_new, w)


def _attn_sample(q, gate_pre, b_gate, cos, sin, kc, vct, pages, table, kv_new, cache_win_t, win_new,
                 batch, n_new, n_pages, name):
    m, d = q.shape
    nq = SAMPLE_NQ
    w = HPG * nq
    assert w == LANES and n_new <= nq and n_new % SUBLANES == 0 and n_new <= SEL_BLOCK
    n_pg = ATT_PAGES
    assert n_pages % n_pg == 0
    n_steps = n_pages // n_pg
    pos0 = n_pages * PAGE
    nb = (pos0 + n_new + SEL_BLOCK - 1) // SEL_BLOCK
    nbp = kc.shape[1] // 4
    gw = KV_HEADS * HEAD_DIM
    n_cached = cache_win_t.shape[3]
    assert n_cached % PAGE == 0

    def page_map(u, branch):
        return lambda b, s, tbl: (tbl[b * n_pages + s * n_pg + u], branch, 0, 0)

    row_map = lambda b, s, tbl: (b, 0)
    per_b3 = lambda b, s, tbl: (b, 0, 0)
    const2 = lambda b, s, tbl: (0, 0)
    in_specs = [pl.BlockSpec((None, None, gw, PAGE), page_map(u, 2)) for u in range(n_pg)]
    in_specs += [pl.BlockSpec((None, None, gw, PAGE), page_map(u, 3)) for u in range(n_pg)]
    in_specs += [pl.BlockSpec((n_new, d), row_map), pl.BlockSpec((n_new, LANES), row_map),
                 pl.BlockSpec((1, LANES), const2), pl.BlockSpec((n_new, LANES), const2),
                 pl.BlockSpec((n_new, LANES), const2),
                 pl.BlockSpec((None, 4 * nbp, gw), per_b3), pl.BlockSpec((None, gw, 4 * nbp), per_b3),
                 pl.BlockSpec((n_new, 4 * gw), row_map),
                 pl.BlockSpec((None, 2, gw, n_cached), lambda b, s, tbl: (b, 0, 0, 0)),
                 pl.BlockSpec((n_new, 2 * gw), row_map)]
    grid_spec = pltpu.PrefetchScalarGridSpec(
        num_scalar_prefetch=1, grid=(batch, n_steps), in_specs=in_specs,
        out_specs=pl.BlockSpec((n_new, d), row_map),
        scratch_shapes=[pltpu.VMEM((KV_HEADS, 1, w), F32), pltpu.VMEM((KV_HEADS, ACC_ROWS, w), F32),
                        pltpu.VMEM((nbp, KV_HEADS * LANES), F32), pltpu.VMEM((KV_HEADS, nbp, w), F32),
                        pltpu.VMEM((KV_HEADS, HEAD_DIM, w), F32), pltpu.VMEM((gw, w), F32),
                        pltpu.VMEM((KV_HEADS, w, LANES), BF16), pltpu.VMEM((LANES, LANES), F32)])
    return pl.pallas_call(
        functools.partial(_attn_sample_kernel, n_pg=n_pg, n_steps=n_steps, nb=nb, nbp=nbp, pos0=pos0, n_new=n_new),
        grid_spec=grid_spec, out_shape=jax.ShapeDtypeStruct((m, d), F32),
        compiler_params=_cparams(("parallel", "arbitrary")), name=name,
    )(table, *([pages] * 2 * n_pg), q, gate_pre, b_gate, cos, sin, kc, vct, kv_new, cache_win_t, win_new)


def _prep_weights(w_a_in, b_a_gate, w_a_out, w_kv, cmp_pos, cmp_w1, cmp_b1, cmp_w2, cmp_b2,
                  w_b_in, b_b_gate, w_b_out, w_ffn_up, w_ffn_down):
    a_q = M_HEADS * M_QK_DIM
    a_v = M_HEADS * M_V_DIM
    n_g = 2 * M_HEADS
    g0 = 2 * a_q + a_v
    gw = KV_HEADS * HEAD_DIM
    qd = N_HEADS * HEAD_DIM
    p = {}
    p["a_main"] = jnp.concatenate([w_a_in[:, :, :g0], w_a_in[:, :, g0 + n_g:]], axis=-1).astype(BF16)
    p["a_gate"] = jnp.pad(w_a_in[:, :, g0:g0 + n_g], ((0, 0), (0, 0), (0, LANES - n_g))).astype(BF16)
    p["a_bgate"] = jnp.pad(b_a_gate, ((0, 0), (0, LANES - n_g)))[:, None, :]
    p["a_out"] = w_a_out.astype(BF16)
    p["kv"] = w_kv.astype(BF16)
    p["kv_vt"] = jnp.concatenate([w_kv[:, 3 * gw:4 * gw], w_kv[:, 5 * gw:6 * gw]], axis=1).T.astype(BF16)
    p["kv_ct"] = w_kv[:, 0:2 * gw].T.astype(BF16)
    hh, g, c = jnp.meshgrid(jnp.arange(HPG), jnp.arange(KV_HEADS), jnp.arange(3), indexing="ij")
    old_col = ((HPG * g + hh) * 3 + c)
    new_col = ((c * HPG + hh) * KV_HEADS + g)
    order = jnp.zeros((3 * N_HEADS,), I32).at[new_col.reshape(-1)].set(old_col.reshape(-1))
    p["b_q"] = w_b_in[:, :, :qd].astype(BF16)
    p["b_gate"] = jnp.pad(w_b_in[:, :, qd:][:, :, order], ((0, 0), (0, 0), (0, LANES - 3 * N_HEADS))).astype(BF16)
    p["b_bgate"] = jnp.pad(b_b_gate[:, order], ((0, 0), (0, LANES - 3 * N_HEADS)))[:, None, :]
    wo = w_b_out.reshape(w_b_out.shape[0], KV_HEADS, HPG, HEAD_DIM, D_MODEL).transpose(0, 2, 1, 3, 4)
    p["b_out"] = wo.reshape(w_b_out.shape[0], qd, D_MODEL).astype(BF16)
    p["up"] = w_ffn_up.astype(BF16)
    p["down"] = w_ffn_down.astype(BF16)
    flat = CMP_STRIDE * HEAD_DIM
    p["cmp"] = {
        "pos_lo": cmp_pos[:, :CMP_STRIDE].reshape(2, 1, flat), "pos_hi": cmp_pos[:, CMP_STRIDE:].reshape(2, 1, flat),
        "w1lo": cmp_w1[:, :flat].astype(BF16), "w1hi": cmp_w1[:, flat:].astype(BF16),
        "b1": cmp_b1[:, None, :], "w2": cmp_w2.astype(BF16), "b2": cmp_b2[:, None, :]}
    return p


def _trunk(x3, pos0, past, p, g_norms, g_a_hnorm, g_kv, ffn_conv_w, ffn_conv_b, conv_prev, m_c, m_n, m_m, tag):
    batch, seq, d = x3.shape
    x = x3.reshape(batch * seq, d)
    cs, ns, ms, convs = [], [], [], []
    gw = KV_HEADS * HEAD_DIM
    for layer in range(DEPTH):
        g = g_norms[layer]
        nm = f"{tag}{layer}"
        if layer < N_A_LAYERS:
            main, gate = _norm_proj(x, g[0], [p["a_main"][layer], p["a_gate"][layer]], nm + "_in")
            h, c_new, n_new, m_new = _mlstm(main, gate, p["a_bgate"][layer], g_a_hnorm[layer],
                                            m_c[layer], m_n[layer], m_m[layer], batch, seq, nm + "_mlstm")
            cs.append(c_new)
            ns.append(n_new)
            ms.append(m_new)
            x = _proj_norm_res(h, p["a_out"][layer], g[1], x, nm + "_out")
        else:
            if layer == N_A_LAYERS:
                cos, sin = _rope_tables(pos0, seq)
                reps = 1 if past is None else batch
                kv4, win, ks_b, kw_b, vst, vwt, cmpt = _kv_rows(x, g_kv, p["kv"], p["kv_vt"], p["kv_ct"],
                                                                jnp.tile(cos, (reps, 1)), jnp.tile(sin, (reps, 1)),
                                                                tag + "_kv")
                zeros_hi = jnp.zeros((batch, 2 * KV_HEADS, CMP_HIDDEN), F32)
                if past is None:
                    n_pages = seq // PAGE
                    nb = seq // SEL_BLOCK
                    kc, vc, _ = _compress(cmpt, jnp.arange(batch * n_pages, dtype=I32), batch, n_pages,
                                          min(CMP_PAGES, n_pages), zeros_hi, p["cmp"], None, tag + "_cmp")
                else:
                    pages, table, n_pages, cache_win_t = past
                    nb = (pos0 + seq + SEL_BLOCK - 1) // SEL_BLOCK
                    n_tail = nb * (SEL_BLOCK // CMP_STRIDE) - n_pages * (PAGE // CMP_STRIDE)
                    tail = jnp.pad(kv4[:, :2 * gw].reshape(batch, seq, 2, KV_HEADS, HEAD_DIM),
                                   ((0, 0), (0, PAGE - seq), (0, 0), (0, 0), (0, 0)))
                    kc_t, vc_t, hi_t = _compress(_token_minor(tail), jnp.arange(batch, dtype=I32), batch, 1, 1, zeros_hi,
                                                 p["cmp"], n_tail - 1, tag + "_cmpt")
                    kc_m, vc_m, _ = _compress(pages, table, batch, n_pages, min(CMP_PAGES, n_pages), hi_t,
                                              p["cmp"], None, tag + "_cmp")
                    kc = jnp.concatenate([kc_m, kc_t[:, :n_tail]], axis=1)
                    vc = jnp.concatenate([vc_m, vc_t[:, :n_tail]], axis=1)
                nbp = -(-nb // 32) * 32
                kc_l = _cmp_layout(kc, nb, nbp)
                vct_l = _cmp_layout(vc, nb, nbp).transpose(0, 2, 1)
            bl = layer - N_A_LAYERS
            q_raw, gate_pre = _norm_proj(x, g[0], [p["b_q"][bl], p["b_gate"][bl]], nm + "_in")
            if past is None:
                o = _attn_prompt(q_raw, gate_pre, p["b_bgate"][bl], cos, sin, kc_l, vct_l, ks_b, vst, kw_b, vwt,
                                 batch, seq, nm + "_attn")
            else:
                o = _attn_sample(q_raw, gate_pre, p["b_bgate"][bl], cos, sin, kc_l, vct_l, pages, table,
                                 kv4, cache_win_t, win, batch, seq, n_pages, nm + "_attn")
            x = _proj_norm_res(o, p["b_out"][bl], g[1], x, nm + "_out")
        x, conv_new = _ffn(x, g[2], p["up"][layer], ffn_conv_w[layer], ffn_conv_b[layer], p["down"][layer], g[3],
                           conv_prev[layer], batch, seq, nm + "_ffn")
        convs.append(conv_new)
    return (x.reshape(batch, seq, d), kv4, win, jnp.stack(cs), jnp.stack(ns), jnp.stack(ms), jnp.stack(convs))


def _past_views(cache_kv, cache_win_kv, page_table):
    return (_token_minor(cache_kv), page_table.reshape(-1), page_table.shape[1], _token_minor(cache_win_kv))


def kernel(x_prompt, x_sample, cache_kv, cache_win_kv, state_mlstm_C, state_mlstm_n, state_mlstm_m, state_conv,
           page_table, g_norms, w_a_in, b_a_gate, g_a_hnorm, w_a_out, g_kv, w_kv, cmp_pos, cmp_w1, cmp_b1, cmp_w2,
           cmp_b2, w_b_in, b_b_gate, w_b_out, w_ffn_up, ffn_conv_w, ffn_conv_b, w_ffn_down):
    p = _prep_weights(w_a_in, b_a_gate, w_a_out, w_kv, cmp_pos, cmp_w1, cmp_b1, cmp_w2, cmp_b2,
                      w_b_in, b_b_gate, w_b_out, w_ffn_up, w_ffn_down)
    dt = x_prompt.dtype
    bp, tp, _ = x_prompt.shape
    bs, ts, _ = x_sample.shape
    past_len = page_table.shape[1] * PAGE
    gw = KV_HEADS * HEAD_DIM
    shared = (p, g_norms, g_a_hnorm, g_kv, ffn_conv_w, ffn_conv_b)

    y_p, kv_p, win_p, c_p, n_p, m_p, conv_p = _trunk(
        x_prompt, 0, None, *shared,
        jnp.zeros((DEPTH, bp, CONV_W - 1, 2 * D_FF), dt),
        jnp.zeros((N_A_LAYERS, bp, M_HEADS, M_QK_DIM, M_V_DIM), dt),
        jnp.zeros((N_A_LAYERS, bp, M_HEADS, M_QK_DIM), dt), jnp.zeros((N_A_LAYERS, bp, M_HEADS), dt), "p")
    kv_p = kv_p.reshape(bp, tp, 4, KV_HEADS, HEAD_DIM)
    n_win_p = min(WINDOW, tp)
    win_p = win_p.reshape(bp, tp, 2, KV_HEADS, HEAD_DIM)[:, tp - n_win_p:]

    past = _past_views(cache_kv, cache_win_kv, page_table)
    y_s, kv_s, win_s, c_s, n_s, m_s, conv_s = _trunk(
        x_sample, past_len, past, *shared, state_conv, state_mlstm_C, state_mlstm_n, state_mlstm_m, "s")
    kv_s = kv_s.reshape(bs, ts, 4, KV_HEADS, HEAD_DIM)
    win_all = jnp.concatenate([cache_win_kv, win_s.reshape(bs, ts, 2, KV_HEADS, HEAD_DIM)], axis=1)
    win_s = win_all[:, win_all.shape[1] - min(WINDOW, win_all.shape[1]):]
    return (y_p, y_s, kv_p, kv_s, win_p, win_s, c_p, c_s, n_p, n_s, m_p, m_s, conv_p, conv_s)
```

```python
import functools
import math

import jax
import jax.numpy as jnp
from jax import lax
from jax.experimental import pallas as pl
from jax.experimental.pallas import tpu as pltpu

F32 = jnp.float32
BF16 = jnp.bfloat16
I32 = jnp.int32

D_MODEL = 1024
DEPTH = 4
N_A_LAYERS = 2
M_HEADS = 8
M_QK_DIM = 64
M_V_DIM = 128
GATE_CAP = 15.0
N_HEADS = 16
HEAD_DIM = 64
KV_HEADS = 4
HPG = 4
CMP_BLOCK = 32
CMP_STRIDE = 16
CMP_HIDDEN = 256
SEL_BLOCK = 64
N_SELECT = 16
WINDOW = 512
ROT_DIM = 16
ROPE_THETA = 500000.0
D_FF = 2816
CONV_W = 3
EPS = 1e-6
BIG = 1e9
NEG = -1e30
PAGE = 128

LANES = 128
SUBLANES = 8
VMEM_LIMIT = 56 * 1024 * 1024

MLSTM_CHUNK = 128
KEY_TILE = 512
Q_TILE = 128
SAMPLE_NQ = 32
CMP_PAGES = 16
ATT_PAGES = 8
FFN_TM = 512
FFN_TN = 1408


def _cparams(sem):
    return pltpu.CompilerParams(dimension_semantics=sem, vmem_limit_bytes=VMEM_LIMIT)


def _rms(x, g):
    return x * lax.rsqrt(jnp.mean(x * x, axis=-1, keepdims=True) + EPS) * g


def _nt(a, b):
    return lax.dot_general(a, b, (((1,), (1,)), ((), ())), preferred_element_type=F32)


def _dot(a, b):
    return jnp.dot(a, b, preferred_element_type=F32)


def _row_tile(m, pref):
    t = min(m, pref)
    while m % t:
        t //= 2
    return t


def _norm_proj_kernel(x_ref, g_ref, *refs, n_w, n_t):
    xn = _rms(x_ref[...], g_ref[...]).astype(BF16)
    n_in = n_w + n_t
    for w_ref, o_ref in zip(refs[:n_w], refs[n_in:n_in + n_w]):
        o_ref[...] = _dot(xn, w_ref[...]).astype(o_ref.dtype)
    for w_ref, o_ref in zip(refs[n_w:n_in], refs[n_in + n_w:]):
        yt = _nt(w_ref[...], xn)
        for u in range(o_ref.shape[0]):
            o_ref[u] = yt[:, u * LANES:(u + 1) * LANES]


def _norm_proj(x, g, ws, name, wts=()):
    m, d = x.shape
    tm = _row_tile(m, 512)
    assert not wts or tm % LANES == 0
    n_w, n_t = len(ws), len(wts)
    in_specs = [pl.BlockSpec((tm, d), lambda i: (i, 0)), pl.BlockSpec((1, d), lambda i: (0, 0))]
    in_specs += [pl.BlockSpec(w.shape, lambda i: (0, 0)) for w in (*ws, *wts)]
    out_specs = [pl.BlockSpec((tm, w.shape[1]), lambda i: (i, 0)) for w in ws]
    out_specs += [pl.BlockSpec((tm // LANES, w.shape[0], LANES), lambda i: (i, 0, 0)) for w in wts]
    out_shape = [jax.ShapeDtypeStruct((m, w.shape[1]), F32) for w in ws]
    out_shape += [jax.ShapeDtypeStruct((m // LANES, w.shape[0], LANES), F32) for w in wts]
    return pl.pallas_call(
        functools.partial(_norm_proj_kernel, n_w=n_w, n_t=n_t), grid=(m // tm,), in_specs=in_specs,
        out_specs=out_specs, out_shape=out_shape, compiler_params=_cparams(("parallel",)),
        name=name)(x, g.reshape(1, d), *ws, *wts)


def _proj_norm_res_kernel(a_ref, w_ref, g_ref, res_ref, o_ref):
    y = _dot(a_ref[...].astype(BF16), w_ref[...])
    o_ref[...] = res_ref[...] + _rms(y, g_ref[...])


def _proj_norm_res(a, w, g, res, name):
    m, k = a.shape
    d = w.shape[1]
    tm = _row_tile(m, 512)
    return pl.pallas_call(
        _proj_norm_res_kernel, grid=(m // tm,),
        in_specs=[pl.BlockSpec((tm, k), lambda i: (i, 0)), pl.BlockSpec((k, d), lambda i: (0, 0)),
                  pl.BlockSpec((1, d), lambda i: (0, 0)), pl.BlockSpec((tm, d), lambda i: (i, 0))],
        out_specs=pl.BlockSpec((tm, d), lambda i: (i, 0)),
        out_shape=jax.ShapeDtypeStruct((m, d), F32),
        compiler_params=_cparams(("parallel",)), name=name)(a, w, g.reshape(1, d), res)


def _mlstm_kernel(main_ref, gate_ref, *refs, rows, nc, k_minor):
    if k_minor:
        kt_ref, *refs = refs
    bg_ref, gh_ref, c0_ref, n0_ref, m0_ref, h_ref, c_ref, n_ref, m_ref, c_s, n_s, m_s = refs
    L = MLSTM_CHUNK
    nh = M_HEADS
    a_q = nh * M_QK_DIM
    v0 = a_q if k_minor else 2 * a_q
    o0 = v0 + nh * M_V_DIM
    cidx = pl.program_id(1)

    @pl.when(cidx == 0)
    def _():
        c_s[...] = c0_ref[...]
        n_s[...] = n0_ref[...]
        m_s[...] = m0_ref[...]

    main = main_ref[...]
    gp = gate_ref[...] + bg_ref[...]
    if rows < L:
        main = jnp.concatenate([main, jnp.zeros((L - rows, main.shape[1]), F32)], axis=0)
        gp = jnp.concatenate([gp, jnp.zeros((L - rows, LANES), F32)], axis=0)
    capped = GATE_CAP * jnp.tanh(gp / GATE_CAP)
    row1 = lax.broadcasted_iota(I32, (L, LANES), 0)
    real = row1 < rows
    ilog = jnp.where(real, capped, -jnp.inf)
    logf = jnp.where(real, jnp.minimum(capped, 0.0) - jnp.log1p(jnp.exp(-jnp.abs(capped))), 0.0)
    bh = logf
    k = 1
    while k < L:
        bh = bh + jnp.where(row1 >= k, pltpu.roll(bh, k, axis=0), 0.0)
        k *= 2
    bh = pltpu.roll(bh, LANES - nh, axis=1)
    c_all = ilog - bh
    cm = c_all
    k = 1
    while k < L:
        cm = jnp.maximum(cm, jnp.where(row1 >= k, pltpu.roll(cm, k, axis=0), -jnp.inf))
        k *= 2
    m_row = m_s[...]
    mt = bh + jnp.maximum(m_row, cm)
    m_new = mt[L - 1:L, :]
    b_last = bh[L - 1:L, :]
    w_inter = jnp.exp(bh + m_row - mt)
    u_all = bh - mt
    emt = jnp.exp(-mt)
    ws_all = jnp.exp(b_last - bh + ilog - m_new)
    decay = jnp.exp(b_last + m_row - m_new)
    c_t = c_all.T
    ws_t = ws_all.T

    rr = lax.broadcasted_iota(I32, (L, L), 0)
    cc = lax.broadcasted_iota(I32, (L, L), 1)
    causal = cc <= rr
    lo_half = lax.broadcasted_iota(I32, (L, LANES), 1) < M_QK_DIM
    lo_rows = lax.broadcasted_iota(I32, (2 * M_QK_DIM, LANES), 0) < M_QK_DIM
    ones_b = jnp.ones((L, LANES), BF16)
    assert L == LANES

    def lanes(a, x):
        return jnp.broadcast_to(a[:, x:x + 1], (L, LANES))

    for p in range(nh // 2):
        qp = main[:, p * LANES:(p + 1) * LANES] * (M_QK_DIM ** -0.5)
        if k_minor:
            kt = kt_ref[0, p * LANES:(p + 1) * LANES, :]
        else:
            kt = main[:, a_q + p * LANES:a_q + (p + 1) * LANES].T
        c_pair = c_s[2 * p:2 * p + 2].reshape(2 * M_QK_DIM, M_V_DIM)
        n_pair = n_s[p]
        kt_b = kt.astype(BF16)
        state_b = jnp.concatenate([c_pair, n_pair], axis=1).astype(BF16)
        for e in range(2):
            x = 2 * p + e
            qx_b = jnp.where(lo_half if e == 0 else jnp.logical_not(lo_half), qp, 0.0).astype(BF16)
            vx = main[:, v0 + x * M_V_DIM:v0 + (x + 1) * M_V_DIM]
            ox = main[:, o0 + x * M_V_DIM:o0 + (x + 1) * M_V_DIM]
            a = jnp.exp(jnp.where(causal, lanes(u_all, x) + c_t[x:x + 1, :], -jnp.inf)) * _dot(qx_b, kt_b)
            inter = _dot(qx_b, state_b)
            intra = _dot(a.astype(BF16), jnp.concatenate([vx.astype(BF16), ones_b], axis=1))
            wi = lanes(w_inter, x)
            num = wi * inter[:, :M_V_DIM] + intra[:, :M_V_DIM]
            den = wi * inter[:, M_V_DIM:] + intra[:, M_V_DIM:]
            h = num / jnp.maximum(jnp.abs(den), lanes(emt, x))
            hn = h * lax.rsqrt(jnp.mean(h * h, axis=-1, keepdims=True) + EPS)
            hn = hn * gh_ref[:, x * M_V_DIM:(x + 1) * M_V_DIM] * jax.nn.sigmoid(ox)
            h_ref[:, x * M_V_DIM:(x + 1) * M_V_DIM] = hn[:rows]
        kwt = kt * jnp.where(lo_rows, ws_t[2 * p:2 * p + 1, :], ws_t[2 * p + 1:2 * p + 2, :])
        vcat = main[:, v0 + 2 * p * M_V_DIM:v0 + (2 * p + 2) * M_V_DIM]
        upd = _dot(kwt.astype(BF16), jnp.concatenate([vcat.astype(BF16), ones_b], axis=1))
        dec_e = decay[:, 2 * p:2 * p + 1]
        dec_o = decay[:, 2 * p + 1:2 * p + 2]
        c_s[2 * p] = dec_e * c_pair[:M_QK_DIM] + upd[:M_QK_DIM, :M_V_DIM]
        c_s[2 * p + 1] = dec_o * c_pair[M_QK_DIM:] + upd[M_QK_DIM:, M_V_DIM:2 * M_V_DIM]
        n_s[p] = jnp.where(lo_rows, dec_e, dec_o) * n_pair + upd[:, 2 * M_V_DIM:]
    m_s[...] = m_new

    @pl.when(cidx == nc - 1)
    def _():
        c_ref[...] = c_s[...]
        n_ref[...] = n_s[...]
        m_ref[...] = m_s[...]


def _mlstm(main, gate, kt, b_gate, g_hnorm, c0, n0, m0, batch, seq, name):
    rows = min(seq, MLSTM_CHUNK)
    nc = seq // rows
    a_v = M_HEADS * M_V_DIM
    hp = M_HEADS // 2
    n_in = jnp.broadcast_to(n0.reshape(batch, hp, 2 * M_QK_DIM, 1), (batch, hp, 2 * M_QK_DIM, LANES))
    m_in = jnp.pad(m0, ((0, 0), (0, LANES - M_HEADS)))[:, None, :]
    k_minor = kt is not None
    assert not k_minor or rows == MLSTM_CHUNK == LANES
    kt_specs = [pl.BlockSpec((1, kt.shape[1], LANES), lambda b, c: (b * nc + c, 0, 0))] if k_minor else []
    h, c, n, m = pl.pallas_call(
        functools.partial(_mlstm_kernel, rows=rows, nc=nc, k_minor=k_minor), grid=(batch, nc),
        in_specs=[pl.BlockSpec((rows, main.shape[1]), lambda b, c: (b * nc + c, 0)),
                  pl.BlockSpec((rows, LANES), lambda b, c: (b * nc + c, 0)), *kt_specs,
                  pl.BlockSpec((1, LANES), lambda b, c: (0, 0)),
                  pl.BlockSpec((1, a_v), lambda b, c: (0, 0)),
                  pl.BlockSpec((None, M_HEADS, M_QK_DIM, M_V_DIM), lambda b, c: (b, 0, 0, 0)),
                  pl.BlockSpec((None, hp, 2 * M_QK_DIM, LANES), lambda b, c: (b, 0, 0, 0)),
                  pl.BlockSpec((None, 1, LANES), lambda b, c: (b, 0, 0))],
        out_specs=[pl.BlockSpec((rows, a_v), lambda b, c: (b * nc + c, 0)),
                   pl.BlockSpec((None, M_HEADS, M_QK_DIM, M_V_DIM), lambda b, c: (b, 0, 0, 0)),
                   pl.BlockSpec((None, hp, 2 * M_QK_DIM, LANES), lambda b, c: (b, 0, 0, 0)),
                   pl.BlockSpec((None, 1, LANES), lambda b, c: (b, 0, 0))],
        out_shape=[jax.ShapeDtypeStruct((batch * seq, a_v), F32),
                   jax.ShapeDtypeStruct((batch, M_HEADS, M_QK_DIM, M_V_DIM), F32),
                   jax.ShapeDtypeStruct((batch, hp, 2 * M_QK_DIM, LANES), F32),
                   jax.ShapeDtypeStruct((batch, 1, LANES), F32)],
        scratch_shapes=[pltpu.VMEM((M_HEADS, M_QK_DIM, M_V_DIM), F32), pltpu.VMEM((hp, 2 * M_QK_DIM, LANES), F32),
                        pltpu.VMEM((1, LANES), F32)],
        compiler_params=_cparams(("parallel", "arbitrary")), name=name,
    )(main, gate, *([kt] if k_minor else []), b_gate, g_hnorm.reshape(1, a_v), c0, n_in, m_in)
    return h, c, n[..., 0].reshape(batch, M_HEADS, M_QK_DIM), m[:, 0, :M_HEADS]


def _ffn_kernel(*refs, carry, tm, n_j, tiles_per_seq, period):
    x_ref, g2_ref, wua_ref, wug_ref, cwa_ref, cwg_ref, cba_ref, cbg_ref, wd_ref, g3_ref = refs[:10]
    if carry:
        inita_ref, initg_ref, o_ref, sa_ref, sg_ref, xn_s, acc_s, ue_s, carry_s = refs[10:]
        branch_in = ((wua_ref, cwa_ref, cba_ref, inita_ref, sa_ref), (wug_ref, cwg_ref, cbg_ref, initg_ref, sg_ref))
    else:
        t1a_ref, t1g_ref, t2a_ref, t2g_ref, o_ref, sa_ref, sg_ref, xn_s, acc_s, ue_s = refs[10:]
        branch_in = ((wua_ref, cwa_ref, cba_ref, (t1a_ref, t2a_ref), sa_ref),
                     (wug_ref, cwg_ref, cbg_ref, (t1g_ref, t2g_ref), sg_ref))
    i = pl.program_id(0)
    j = pl.program_id(1)
    tn = wd_ref.shape[0]

    @pl.when(j == 0)
    def _():
        xn_s[...] = _rms(x_ref[...], g2_ref[...]).astype(BF16)
        acc_s[...] = jnp.zeros_like(acc_s)

    xn = xn_s[...]
    conv = []
    for which, (w_ref, cw_ref, cb_ref, boundary, s_ref) in enumerate(branch_in):
        u = _dot(xn, w_ref[...])
        ue_s[SUBLANES:, :] = u
        if carry:
            slot = which * n_j + j
            first = (i % tiles_per_seq) == 0
            ue_s[SUBLANES - 2:SUBLANES, :] = jnp.where(first, boundary[...], carry_s[slot])
            tap1 = ue_s[pl.ds(SUBLANES - 1, tm), :]
            tap2 = ue_s[pl.ds(SUBLANES - 2, tm), :]
            last2 = u[tm - 2:tm, :]
            carry_s[slot] = last2
            s_ref[i // tiles_per_seq, j] = last2
        else:
            ue_s[0:SUBLANES, :] = jnp.zeros((SUBLANES, tn), F32)
            t = lax.broadcasted_iota(I32, (tm, tn), 0) & (period - 1)
            tap1 = jnp.where(t >= 1, ue_s[pl.ds(SUBLANES - 1, tm), :], boundary[0][...])
            tap2 = jnp.where(t >= 2, ue_s[pl.ds(SUBLANES - 2, tm), :], boundary[1][...])
            s_ref[...] = u
        conv.append(cb_ref[...] + tap2 * cw_ref[0:1, :] + tap1 * cw_ref[1:2, :] + u * cw_ref[2:3, :])
    y = jax.nn.gelu(conv[0], approximate=True) * conv[1]
    acc_s[...] += _dot(y.astype(BF16), wd_ref[...])

    @pl.when(j == n_j - 1)
    def _():
        o_ref[...] = x_ref[...] + _rms(acc_s[...], g3_ref[...])


def _ffn(x, g2, w_up, conv_w, conv_b, w_down, g3, prev, batch, seq, name):
    m, d = x.shape
    nf = w_down.shape[0]
    tn = FFN_TN
    n_j = nf // tn
    carry = seq >= 256
    cb = conv_b.reshape(1, 2 * nf)
    col_a = lambda i, j: (0, j)
    col_g = lambda i, j: (0, j + n_j)
    common_specs = [
        None,
        pl.BlockSpec((1, d), lambda i, j: (0, 0)),
        pl.BlockSpec((d, tn), col_a), pl.BlockSpec((d, tn), col_g),
        pl.BlockSpec((CONV_W, tn), col_a), pl.BlockSpec((CONV_W, tn), col_g),
        pl.BlockSpec((1, tn), col_a), pl.BlockSpec((1, tn), col_g),
        pl.BlockSpec((tn, d), lambda i, j: (j, 0)),
        pl.BlockSpec((1, d), lambda i, j: (0, 0)),
    ]
    common_args = [x, g2.reshape(1, d), w_up, w_up, conv_w, conv_w, cb, cb, w_down, g3.reshape(1, d)]
    if carry:
        tm = _row_tile(seq, FFN_TM)
        tps = seq // tm
        common_specs[0] = pl.BlockSpec((tm, d), lambda i, j: (i, 0))
        st_a = lambda i, j: (i // tps, 0, j)
        st_g = lambda i, j: (i // tps, 0, j + n_j)
        out, sa, sg = pl.pallas_call(
            functools.partial(_ffn_kernel, carry=True, tm=tm, n_j=n_j, tiles_per_seq=tps, period=seq),
            grid=(m // tm, n_j),
            in_specs=common_specs + [pl.BlockSpec((None, 2, tn), st_a), pl.BlockSpec((None, 2, tn), st_g)],
            out_specs=[pl.BlockSpec((tm, d), lambda i, j: (i, 0)),
                       pl.BlockSpec((batch, n_j, 2, tn), lambda i, j: (0, 0, 0, 0)),
                       pl.BlockSpec((batch, n_j, 2, tn), lambda i, j: (0, 0, 0, 0))],
            out_shape=[jax.ShapeDtypeStruct((m, d), F32), jax.ShapeDtypeStruct((batch, n_j, 2, tn), F32),
                       jax.ShapeDtypeStruct((batch, n_j, 2, tn), F32)],
            scratch_shapes=[pltpu.VMEM((tm, d), BF16), pltpu.VMEM((tm, d), F32),
                            pltpu.VMEM((tm + SUBLANES, tn), F32), pltpu.VMEM((2 * n_j, 2, tn), F32)],
            compiler_params=_cparams(("arbitrary", "arbitrary")), name=name,
        )(*common_args, prev, prev)
        sa, sg = (s.transpose(0, 2, 1, 3).reshape(batch, 2, nf) for s in (sa, sg))
        return out, jnp.concatenate([sa, sg], axis=-1)
    tm = m
    assert seq >= 2 and seq & (seq - 1) == 0
    common_specs[0] = pl.BlockSpec((tm, d), lambda i, j: (i, 0))
    tap1 = jnp.pad(prev[:, 1:2], ((0, 0), (0, seq - 1), (0, 0))).reshape(m, 2 * nf)
    tap2 = jnp.pad(prev, ((0, 0), (0, seq - 2), (0, 0))).reshape(m, 2 * nf)
    row_a = lambda i, j: (i, j)
    row_g = lambda i, j: (i, j + n_j)
    out, ua, ug = pl.pallas_call(
        functools.partial(_ffn_kernel, carry=False, tm=tm, n_j=n_j, tiles_per_seq=1, period=seq),
        grid=(m // tm, n_j),
        in_specs=common_specs + [pl.BlockSpec((tm, tn), row_a), pl.BlockSpec((tm, tn), row_g),
                                 pl.BlockSpec((tm, tn), row_a), pl.BlockSpec((tm, tn), row_g)],
        out_specs=[pl.BlockSpec((tm, d), lambda i, j: (i, 0)), pl.BlockSpec((tm, tn), row_a),
                   pl.BlockSpec((tm, tn), row_a)],
        out_shape=[jax.ShapeDtypeStruct((m, d), F32), jax.ShapeDtypeStruct((m, nf), F32),
                   jax.ShapeDtypeStruct((m, nf), F32)],
        scratch_shapes=[pltpu.VMEM((tm, d), BF16), pltpu.VMEM((tm, d), F32), pltpu.VMEM((tm + SUBLANES, tn), F32)],
        compiler_params=_cparams(("arbitrary", "arbitrary")), name=name,
    )(*common_args, tap1, tap1, tap2, tap2)
    u = jnp.concatenate([ua, ug], axis=-1).reshape(batch, seq, 2 * nf)
    return out, u[:, seq - 2:]


def _rope_pair(x, cos, sin):
    half = ROT_DIM // 2
    lane = lax.broadcasted_iota(I32, x.shape, 1) & (HEAD_DIM - 1)
    partner = jnp.where(lane < half, pltpu.roll(x, LANES - half, axis=1), pltpu.roll(x, half, axis=1))
    return x * cos + partner * sin


def _rope(x, cos, sin):
    return jnp.concatenate(
        [_rope_pair(x[:, c * LANES:(c + 1) * LANES], cos, sin) for c in range(x.shape[1] // LANES)], axis=1)


def _rope_tables(pos0, seq):
    half = ROT_DIM // 2
    inv = jnp.power(jnp.float32(ROPE_THETA), -jnp.arange(0, ROT_DIM, 2, dtype=F32) / ROT_DIM)
    ang = (pos0 + jnp.arange(seq, dtype=I32)).astype(F32)[:, None] * inv[None, :]
    cos, sin = jnp.cos(ang), jnp.sin(ang)
    rest = HEAD_DIM - ROT_DIM
    cos_h = jnp.concatenate([cos, cos, jnp.ones((seq, rest), F32)], axis=1)
    sin_h = jnp.concatenate([-sin, sin, jnp.zeros((seq, rest), F32)], axis=1)
    return jnp.tile(cos_h, (1, LANES // HEAD_DIM)), jnp.tile(sin_h, (1, LANES // HEAD_DIM))


def _kv_kernel(x_ref, g_ref, w_ref, wvt_ref, wct_ref, cos_ref, sin_ref, kv_ref, win_ref, ks_ref, kw_ref, vst_ref, vwt_ref,
               cmpt_ref):
    xn = _rms(x_ref[...], g_ref[...]).astype(BF16)
    y = _dot(xn, w_ref[...])
    cos, sin = cos_ref[...], sin_ref[...]
    gw = KV_HEADS * HEAD_DIM
    ks = _rope(y[:, 2 * gw:3 * gw], cos, sin)
    kw = _rope(y[:, 4 * gw:5 * gw], cos, sin)
    kv_ref[:, 0:2 * gw] = y[:, 0:2 * gw]
    kv_ref[:, 2 * gw:3 * gw] = ks
    kv_ref[:, 3 * gw:4 * gw] = y[:, 3 * gw:4 * gw]
    win_ref[:, 0:gw] = kw
    win_ref[:, gw:2 * gw] = y[:, 5 * gw:6 * gw]
    ks_ref[...] = ks.astype(BF16)
    kw_ref[...] = kw.astype(BF16)
    vt = _nt(wvt_ref[...], xn).astype(BF16)
    vst_ref[...] = vt[0:gw]
    ct = _nt(wct_ref[...], xn)
    for u in range(vwt_ref.shape[0]):
        vwt_ref[u] = vt[gw:2 * gw, u * PAGE:(u + 1) * PAGE]
        cmpt_ref[u, 0] = ct[0:gw, u * PAGE:(u + 1) * PAGE]
        cmpt_ref[u, 1] = ct[gw:2 * gw, u * PAGE:(u + 1) * PAGE]


def _kv_rows(x, g_kv, w_kv, w_vt, w_ct, cos, sin, name):
    m, d = x.shape
    tm = _row_tile(m, KEY_TILE)
    gw = KV_HEADS * HEAD_DIM
    ttiles = cos.shape[0] // tm
    return pl.pallas_call(
        _kv_kernel, grid=(m // tm,),
        in_specs=[pl.BlockSpec((tm, d), lambda i: (i, 0)), pl.BlockSpec((1, d), lambda i: (0, 0)),
                  pl.BlockSpec(w_kv.shape, lambda i: (0, 0)), pl.BlockSpec(w_vt.shape, lambda i: (0, 0)),
                  pl.BlockSpec(w_ct.shape, lambda i: (0, 0)),
                  pl.BlockSpec((tm, LANES), lambda i: (i % ttiles, 0)),
                  pl.BlockSpec((tm, LANES), lambda i: (i % ttiles, 0))],
        out_specs=[pl.BlockSpec((tm, 4 * gw), lambda i: (i, 0)), pl.BlockSpec((tm, 2 * gw), lambda i: (i, 0)),
                   pl.BlockSpec((tm, gw), lambda i: (i, 0)), pl.BlockSpec((tm, gw), lambda i: (i, 0)),
                   pl.BlockSpec((None, gw, tm), lambda i: (i, 0, 0)),
                   pl.BlockSpec((tm // PAGE, gw, PAGE), lambda i: (i, 0, 0)),
                   pl.BlockSpec((tm // PAGE, 2, gw, PAGE), lambda i: (i, 0, 0, 0))],
        out_shape=[jax.ShapeDtypeStruct((m, 4 * gw), F32), jax.ShapeDtypeStruct((m, 2 * gw), F32),
                   jax.ShapeDtypeStruct((m, gw), BF16), jax.ShapeDtypeStruct((m, gw), BF16),
                   jax.ShapeDtypeStruct((m // tm, gw, tm), BF16), jax.ShapeDtypeStruct((m // PAGE, gw, PAGE), BF16),
                   jax.ShapeDtypeStruct((m // PAGE, 2, gw, PAGE), F32)],
        compiler_params=_cparams(("parallel",)), name=name)(x, g_kv.reshape(1, d), w_kv, w_vt, w_ct, cos, sin)


def _token_minor(rows):
    n_p, r, n = rows.shape[0], rows.shape[1], rows.shape[2]
    return rows.transpose(0, 2, 3, 4, 1).reshape(n_p, n, KV_HEADS * HEAD_DIM, r)


def _compress_kernel(tbl_ref, *refs, n_pg, zero_after):
    page_refs = refs[:n_pg]
    (pos_lo_ref, pos_hi_ref, w1lo_ref, w1hi_ref, b1_ref, w2_ref, b2_ref, hi_init_ref,
     kc_ref, vc_ref, hi_first_ref, carry_s, t_s, x_s) = refs[n_pg:]
    c = pl.program_id(1)
    njp = PAGE // CMP_STRIDE
    njc = n_pg * njp

    @pl.when(c == 0)
    def _():
        carry_s[...] = hi_init_ref[...]

    lo_lanes = lax.broadcasted_iota(I32, (njp, LANES), 1) < HEAD_DIM
    for u in range(n_pg):
        for br in range(2):
            for cp in range(KV_HEADS // 2):
                tb = t_s.at[(2 * u + br) % 2 * 2 + cp]
                tb[...] = page_refs[u][br, cp * LANES:(cp + 1) * LANES, :].T
                for q in range(CMP_STRIDE // 2):
                    p0 = tb[pl.ds(2 * q, njp, stride=CMP_STRIDE), :]
                    p1 = tb[pl.ds(2 * q + 1, njp, stride=CMP_STRIDE), :]
                    rows = slice(u * njp, (u + 1) * njp)
                    cols = slice(q * LANES, (q + 1) * LANES)
                    x_s[br * KV_HEADS + 2 * cp, rows, cols] = jnp.where(lo_lanes, p0, pltpu.roll(p1, HEAD_DIM, axis=1))
                    x_s[br * KV_HEADS + 2 * cp + 1, rows, cols] = jnp.where(lo_lanes, pltpu.roll(p0, HEAD_DIM, axis=1), p1)

    rowl = lax.broadcasted_iota(I32, (KV_HEADS * njc, CMP_HIDDEN), 0) & (njc - 1)
    for br, out_ref in enumerate((kc_ref, vc_ref)):
        x = x_s[br * KV_HEADS:(br + 1) * KV_HEADS].reshape(KV_HEADS * njc, CMP_STRIDE * HEAD_DIM)
        lo = _dot((x + pos_lo_ref[br]).astype(BF16), w1lo_ref[br])
        hi = _dot((x + pos_hi_ref[br]).astype(BF16), w1hi_ref[br])
        carry_rows = jnp.concatenate(
            [jnp.broadcast_to(carry_s[br * KV_HEADS + g:br * KV_HEADS + g + 1, :], (njc, CMP_HIDDEN))
             for g in range(KV_HEADS)], axis=0)
        hi_next = jnp.where(rowl == njc - 1, carry_rows, pltpu.roll(hi, KV_HEADS * njc - 1, axis=0))
        if zero_after is not None:
            hi_next = jnp.where(jnp.logical_and(rowl == zero_after, c == 0), 0.0, hi_next)
        for g in range(KV_HEADS):
            carry_s[br * KV_HEADS + g:br * KV_HEADS + g + 1, :] = hi[g * njc:g * njc + 1, :]
        h = jax.nn.gelu(lo + hi_next + b1_ref[br], approximate=True)
        o = _dot(h.astype(BF16), w2_ref[br]) + b2_ref[br]
        for g in range(KV_HEADS):
            out_ref[:, g * HEAD_DIM:(g + 1) * HEAD_DIM] = o[g * njc:(g + 1) * njc, :]
    hi_first_ref[...] = carry_s[...]


def _compress(pages, table, batch, n_pages, n_pg, hi_init, cw, zero_after, name):
    njp = PAGE // CMP_STRIDE
    njc = n_pg * njp
    assert njc & (njc - 1) == 0 and njc % SUBLANES == 0 and n_pages % n_pg == 0
    n_ch = n_pages // n_pg
    gw = KV_HEADS * HEAD_DIM
    flat = CMP_STRIDE * HEAD_DIM

    def page_map(u):
        return lambda b, c, tbl: (tbl[b * n_pages + (n_ch - 1 - c) * n_pg + u], 0, 0, 0)

    full3 = lambda b, c, tbl: (0, 0, 0)
    in_specs = [pl.BlockSpec((None, 2, gw, PAGE), page_map(u)) for u in range(n_pg)]
    in_specs += [pl.BlockSpec((2, 1, flat), full3), pl.BlockSpec((2, 1, flat), full3),
                 pl.BlockSpec((2, flat, CMP_HIDDEN), full3), pl.BlockSpec((2, flat, CMP_HIDDEN), full3),
                 pl.BlockSpec((2, 1, CMP_HIDDEN), full3), pl.BlockSpec((2, CMP_HIDDEN, HEAD_DIM), full3),
                 pl.BlockSpec((2, 1, HEAD_DIM), full3),
                 pl.BlockSpec((None, 2 * KV_HEADS, CMP_HIDDEN), lambda b, c, tbl: (b, 0, 0))]
    out_map = lambda b, c, tbl: (b, n_ch - 1 - c, 0)
    grid_spec = pltpu.PrefetchScalarGridSpec(
        num_scalar_prefetch=1, grid=(batch, n_ch), in_specs=in_specs,
        out_specs=[pl.BlockSpec((None, njc, gw), out_map), pl.BlockSpec((None, njc, gw), out_map),
                   pl.BlockSpec((None, 2 * KV_HEADS, CMP_HIDDEN), lambda b, c, tbl: (b, 0, 0))],
        scratch_shapes=[pltpu.VMEM((2 * KV_HEADS, CMP_HIDDEN), F32), pltpu.VMEM((4, PAGE, LANES), F32),
                        pltpu.VMEM((2 * KV_HEADS, njc, flat), F32)])
    return pl.pallas_call(
        functools.partial(_compress_kernel, n_pg=n_pg, zero_after=zero_after), grid_spec=grid_spec,
        out_shape=[jax.ShapeDtypeStruct((batch, n_ch * njc, gw), F32), jax.ShapeDtypeStruct((batch, n_ch * njc, gw), F32),
                   jax.ShapeDtypeStruct((batch, 2 * KV_HEADS, CMP_HIDDEN), F32)],
        compiler_params=_cparams(("parallel", "arbitrary")), name=name,
    )(table, *([pages] * n_pg), cw["pos_lo"], cw["pos_hi"], cw["w1lo"], cw["w1hi"], cw["b1"], cw["w2"], cw["b2"], hi_init)


def _cmp_layout(kc_nat, nb, nbp):
    b = kc_nat.shape[0]
    x = kc_nat[:, :4 * nb].reshape(b, nb, 4, kc_nat.shape[-1]).transpose(0, 2, 1, 3)
    x = jnp.pad(x, ((0, 0), (0, 0), (0, nbp - nb), (0, 0)))
    return x.reshape(b, 4 * nbp, kc_nat.shape[-1]).astype(BF16)


SCALE = HEAD_DIM ** -0.5
QSCALE = SCALE * math.log2(math.e)
SUM_ROWS = 16
ACC_ROWS = HEAD_DIM + SUM_ROWS


def _group_queries(q, g):
    nq = q.shape[0]
    keep = (lax.broadcasted_iota(I32, (nq, LANES), 1) >> 6) == (g % 2)
    pieces = []
    for hh in range(HPG):
        h = HPG * g + hh
        chunk = q[:, (h // 2) * LANES:(h // 2 + 1) * LANES]
        if h % 2 != g % 2:
            chunk = pltpu.roll(chunk, HEAD_DIM, axis=1)
        pieces.append(jnp.where(keep, chunk, 0.0))
    return (jnp.concatenate(pieces, axis=0) * QSCALE).astype(BF16)


def _kchunk(k, g):
    return k[:, (g // 2) * LANES:(g // 2 + 1) * LANES]


def _values(vt, g):
    return jnp.concatenate([vt[g * HEAD_DIM:(g + 1) * HEAD_DIM, :], jnp.ones((SUM_ROWS, vt.shape[1]), BF16)], axis=0)


def _flash_groups(ss, vt, m_s, acc_s):
    ps, alphas = [], []
    for g in range(KV_HEADS):
        m_old = m_s[g]
        m_new = jnp.maximum(m_old, jnp.max(ss[g], axis=0, keepdims=True))
        alphas.append(jnp.exp2(m_old - m_new))
        ps.append(jnp.exp2(ss[g] - m_new).astype(BF16))
        m_s[g] = m_new
    for g in range(KV_HEADS):
        acc_s[g] = acc_s[g] * alphas[g] + _dot(_values(vt, g), ps[g])


def _one_shot_groups(ss, vt):
    ps = [jnp.exp2(s - jnp.max(s, axis=0, keepdims=True)).astype(BF16) for s in ss]
    return [_dot(_values(vt, g), ps[g]) for g in range(KV_HEADS)]


def _init_state(m_s, acc_s):
    m_s[...] = jnp.full(m_s.shape, NEG, F32)
    acc_s[...] = jnp.zeros(acc_s.shape, F32)


def _topk_bias(score, n_sel):
    n_iota = lax.broadcasted_iota(I32, score.shape, 0)

    def body(_, sc):
        mx = jnp.max(sc, axis=0, keepdims=True)
        idx = jnp.min(jnp.where(sc == mx, n_iota, score.shape[0]), axis=0, keepdims=True)
        return jnp.where(n_iota == idx, -jnp.inf, sc)

    left = lax.fori_loop(0, n_sel, body, score)
    return jnp.where(jnp.logical_and(left == -jnp.inf, score > -jnp.inf), 0.0, NEG)


def _cmp_valid(qpos_w, nb, nbp):
    n_w = lax.broadcasted_iota(I32, (nbp, qpos_w.shape[1]), 0)
    return jnp.concatenate(
        [jnp.logical_and((4 * n_w + c) * CMP_STRIDE + (CMP_BLOCK - 1) <= qpos_w, n_w < nb) for c in range(4)], axis=0)


def _cmp_branch(s, valid, vct_ref, g, qpos_w, qpos_q, nb, nbp, nq):
    w = s.shape[1]
    s = jnp.where(valid, s, NEG)
    e = jnp.exp2(s - jnp.max(s, axis=0, keepdims=True))
    some = (qpos_w >= CMP_BLOCK - 1).astype(F32)
    p = e * (some / jnp.maximum(jnp.sum(e, axis=0, keepdims=True), 1e-30))
    o_cmp = _dot(vct_ref[g * HEAD_DIM:(g + 1) * HEAD_DIM, :], p.astype(BF16))
    if nq == LANES:
        ps = ((p[:, 0:LANES] + p[:, LANES:2 * LANES]) + p[:, 2 * LANES:3 * LANES]) + p[:, 3 * LANES:4 * LANES]
    else:
        assert w == LANES
        ps = ((p + pltpu.roll(p, nq, axis=1)) + pltpu.roll(p, 2 * nq, axis=1)) + pltpu.roll(p, 3 * nq, axis=1)
    parts = [ps[c * nbp:(c + 1) * nbp] for c in range(4)]
    n_q = lax.broadcasted_iota(I32, (nbp, LANES), 0)
    prev = jnp.where(n_q >= 1, pltpu.roll(parts[3], 1, axis=0), 0.0)
    score = (((parts[0] + parts[1]) + parts[2]) + parts[3]) + prev
    cur = qpos_q >> 6
    forced = (n_q == 0) | (n_q == cur) | (n_q == cur - 1)
    score = jnp.where(forced, BIG, jnp.where(n_q * SEL_BLOCK <= qpos_q, score, -BIG))
    score = jnp.where(n_q < nb, score, -jnp.inf)
    return o_cmp, score


def _compressed_and_select(q, kc_ref, vct_ref, qpos_w, qpos_q, ocmp_s, score_s, bias_s, nb, nbp, nq, n_eff=None):
    w = HPG * nq
    n_eff = nbp if n_eff is None else n_eff
    kc_all = jnp.concatenate([kc_ref[c * nbp:c * nbp + n_eff, :] for c in range(4)], axis=0)
    vct_all = jnp.concatenate([vct_ref[:, c * nbp:c * nbp + n_eff] for c in range(4)], axis=1)
    cs = [_nt(_kchunk(kc_all, g), _group_queries(q, g)) for g in range(KV_HEADS)]
    valid = _cmp_valid(qpos_w, nb, n_eff)
    for g in range(KV_HEADS):
        o_cmp, score = _cmp_branch(cs[g], valid, vct_all, g, qpos_w, qpos_q, nb, n_eff, nq)
        ocmp_s[g] = o_cmp
        score_s[0:n_eff, g * LANES:(g + 1) * LANES] = score
    bias = _topk_bias(score_s[0:n_eff, :], min(N_SELECT, nb))
    for g in range(KV_HEADS):
        b = bias[:, g * LANES:(g + 1) * LANES]
        bias_s[g, 0:n_eff, :] = b if w == LANES else jnp.concatenate([b] * (w // LANES), axis=1)
        if n_eff < nbp:
            bias_s[g, n_eff:nbp, :] = jnp.full((nbp - n_eff, w), NEG, F32)


def _block_bias(bias_rows, n_blk):
    w = bias_rows.shape[1]
    return jnp.concatenate([jnp.broadcast_to(bias_rows[u:u + 1, :], (SEL_BLOCK, w)) for u in range(n_blk)], axis=0)


def _gate_row(gates_t, c, g, nq, w):
    rows = [gates_t[(c * HPG + hh) * KV_HEADS + g:(c * HPG + hh) * KV_HEADS + g + 1, :] for hh in range(HPG)]
    if nq == LANES:
        return jnp.concatenate(rows, axis=1)
    strip = lax.broadcasted_iota(I32, (1, LANES), 1) >> (nq.bit_length() - 1)
    out = jnp.zeros((1, LANES), F32)
    for hh in range(HPG):
        out = jnp.where(strip == hh, rows[hh] if hh == 0 else pltpu.roll(rows[hh], hh * nq, axis=1), out)
    return out


def _finish(o_ref, acc_s, ocmp_s, wins, ot_s, gates_t, nq, nq_real, w):
    for g in range(KV_HEADS):
        o = ocmp_s[g] * _gate_row(gates_t, 0, g, nq, w)
        for c, acc in ((1, acc_s[g]), (2, wins[g])):
            o = o + acc[:HEAD_DIM] * (_gate_row(gates_t, c, g, nq, w) / acc[HEAD_DIM:HEAD_DIM + 1])
        ot_s[g * HEAD_DIM:(g + 1) * HEAD_DIM, :] = o
    o_t = ot_s[...].T
    for hh in range(HPG):
        o_ref[:, hh * KV_HEADS * HEAD_DIM:(hh + 1) * KV_HEADS * HEAD_DIM] = o_t[hh * nq:hh * nq + nq_real, :]


def _attn_prompt_kernel(q_ref, gp_ref, bg_ref, cos_ref, sin_ref, kc_ref, vct_ref, ks_ref, vst_ref, kw_ref, vwt_ref,
                        o_ref, m_s, acc_s, score_s, bias_s, ocmp_s, ot_s, qzr_s, *, nb, nbp):
    i = pl.program_id(1)
    nq = Q_TILE
    w = HPG * nq
    s0 = i * nq
    q = q_ref[...]
    q_rot = _rope(q, cos_ref[...], sin_ref[...])
    gates_t = jax.nn.sigmoid(gp_ref[...] + bg_ref[...]).T
    qpos_w = s0 + (lax.broadcasted_iota(I32, (1, w), 1) & (nq - 1))
    qpos_q = s0 + lax.broadcasted_iota(I32, (1, LANES), 1)
    _init_state(m_s, acc_s)
    for g in range(KV_HEADS):
        qzr_s[g] = _group_queries(q_rot, g)
    n_cls = 4 if nbp % (4 * SUBLANES) == 0 and nbp // 4 >= N_SELECT else 1
    per_cls = nbp // n_cls
    cls = jnp.minimum((2 * (i + 1) + per_cls - 1) // per_cls, n_cls) - 1
    for k in range(n_cls):
        @pl.when(cls == k)
        def _(k=k):
            _compressed_and_select(q, kc_ref, vct_ref, qpos_w, qpos_q, ocmp_s, score_s, bias_s, nb, nbp, nq,
                                   n_eff=(k + 1) * per_cls)

    blk_per_tile = KEY_TILE // SEL_BLOCK

    def slc_tile(t, causal):
        k0 = pl.multiple_of(t * KEY_TILE, KEY_TILE)
        kt = ks_ref[pl.ds(k0, KEY_TILE), :]
        ss = []
        for g in range(KV_HEADS):
            rows = bias_s[g, pl.ds(pl.multiple_of(t * blk_per_tile, blk_per_tile), blk_per_tile), :]
            s = _nt(_kchunk(kt, g), qzr_s[g]) + _block_bias(rows, blk_per_tile)
            if causal:
                s = jnp.where(k0 + lax.broadcasted_iota(I32, (KEY_TILE, w), 0) <= qpos_w, s, NEG)
            ss.append(s)
        _flash_groups(ss, vst_ref[t], m_s, acc_s)

    t_diag = s0 // KEY_TILE

    def full_tile(t, carry):
        slc_tile(t, False)
        return carry

    lax.fori_loop(0, t_diag, full_tile, 0)
    slc_tile(t_diag, True)

    n_wt = (WINDOW + nq) // PAGE
    t0 = jnp.maximum(i - WINDOW // PAGE, 0)
    k0 = pl.multiple_of(t0 * PAGE, PAGE)
    kwin = kw_ref[pl.ds(k0, n_wt * PAGE), :]
    vwin_tiles = vwt_ref[pl.ds(t0, n_wt)]
    vwin = jnp.concatenate([vwin_tiles[u] for u in range(n_wt)], axis=1)
    kpos = k0 + lax.broadcasted_iota(I32, (n_wt * PAGE, w), 0)
    wmask = jnp.logical_and(kpos <= qpos_w, qpos_w - kpos < WINDOW)
    ws = [jnp.where(wmask, _nt(_kchunk(kwin, g), qzr_s[g]), NEG) for g in range(KV_HEADS)]
    _finish(o_ref, acc_s, ocmp_s, _one_shot_groups(ws, vwin), ot_s, gates_t, nq, nq, w)


def _attn_prompt(q, gate_pre, b_gate, cos, sin, kc, vct, ks, vst, kw, vwt, batch, seq, name):
    assert seq % KEY_TILE == 0 and seq >= WINDOW + Q_TILE
    m, d = q.shape
    nq = Q_TILE
    w = HPG * nq
    nqb = seq // nq
    nb = seq // SEL_BLOCK
    nbp = kc.shape[1] // 4
    gw = KV_HEADS * HEAD_DIM
    ntile = seq // KEY_TILE
    row_map = lambda b, i: (b * nqb + i, 0)
    per_b2 = lambda b, i: (b, 0)
    per_b3 = lambda b, i: (b, 0, 0)
    return pl.pallas_call(
        functools.partial(_attn_prompt_kernel, nb=nb, nbp=nbp), grid=(batch, nqb),
        in_specs=[pl.BlockSpec((nq, d), row_map), pl.BlockSpec((nq, LANES), row_map),
                  pl.BlockSpec((1, LANES), lambda b, i: (0, 0)),
                  pl.BlockSpec((nq, LANES), lambda b, i: (i, 0)), pl.BlockSpec((nq, LANES), lambda b, i: (i, 0)),
                  pl.BlockSpec((None, 4 * nbp, gw), per_b3), pl.BlockSpec((None, gw, 4 * nbp), per_b3),
                  pl.BlockSpec((seq, gw), per_b2), pl.BlockSpec((ntile, gw, KEY_TILE), per_b3),
                  pl.BlockSpec((seq, gw), per_b2), pl.BlockSpec((seq // PAGE, gw, PAGE), per_b3)],
        out_specs=pl.BlockSpec((nq, d), row_map),
        out_shape=jax.ShapeDtypeStruct((m, d), F32),
        scratch_shapes=[pltpu.VMEM((KV_HEADS, 1, w), F32), pltpu.VMEM((KV_HEADS, ACC_ROWS, w), F32),
                        pltpu.VMEM((nbp, KV_HEADS * LANES), F32), pltpu.VMEM((KV_HEADS, nbp, w), F32),
                        pltpu.VMEM((KV_HEADS, HEAD_DIM, w), F32), pltpu.VMEM((gw, w), F32),
                        pltpu.VMEM((KV_HEADS, w, LANES), BF16)],
        compiler_params=_cparams(("parallel", "arbitrary")), name=name,
    )(q, gate_pre, b_gate, cos, sin, kc, vct, ks, vst, kw, vwt)


def _pad_rows(x, n):
    return jnp.concatenate([x, jnp.zeros((n - x.shape[0], x.shape[1]), x.dtype)], axis=0)


def _attn_sample_kernel(tbl_ref, *refs, n_pg, n_steps, nb, nbp, pos0, n_new):
    k_refs = refs[:n_pg]
    vt_refs = refs[n_pg:2 * n_pg]
    (q_ref, gp_ref, bg_ref, cos_ref, sin_ref, kc_ref, vct_ref, kvn_ref, cwin_ref, wnew_ref,
     o_ref, m_s, acc_s, score_s, bias_s, ocmp_s, ot_s, qzr_s, gt_s) = refs[2 * n_pg:]
    step = pl.program_id(1)
    nq = SAMPLE_NQ
    w = HPG * nq
    gw = KV_HEADS * HEAD_DIM
    qpos_w = pos0 + (lax.broadcasted_iota(I32, (1, w), 1) & (nq - 1))

    @pl.when(step == 0)
    def _():
        q = _pad_rows(q_ref[...], nq)
        q_rot = _pad_rows(_rope(q_ref[...], cos_ref[...], sin_ref[...]), nq)
        gt_s[...] = _pad_rows(jax.nn.sigmoid(gp_ref[...] + bg_ref[...]), LANES).T
        _init_state(m_s, acc_s)
        for g in range(KV_HEADS):
            qzr_s[g] = _group_queries(q_rot, g)
        _compressed_and_select(q, kc_ref, vct_ref, qpos_w, qpos_w, ocmp_s, score_s, bias_s, nb, nbp, nq)

    blk_pp = PAGE // SEL_BLOCK
    blk_ps = n_pg * blk_pp
    kcs = [jnp.concatenate([r[cp * LANES:(cp + 1) * LANES, :].T for r in k_refs], axis=0).astype(BF16)
           for cp in range(KV_HEADS // 2)]
    vt = jnp.concatenate([r[...] for r in vt_refs], axis=1).astype(BF16)
    ss = []
    for g in range(KV_HEADS):
        rows = bias_s[g, pl.ds(pl.multiple_of(step * blk_ps, blk_ps), blk_ps), :]
        ss.append(_nt(kcs[g // 2], qzr_s[g]) + _block_bias(rows, blk_ps))
    _flash_groups(ss, vt, m_s, acc_s)

    @pl.when(step == n_steps - 1)
    def _():
        krow = lax.broadcasted_iota(I32, (PAGE, w), 0)
        kvn = _pad_rows(kvn_ref[...], PAGE)
        kn = kvn[:, 2 * gw:3 * gw].astype(BF16)
        vnt = kvn[:, 3 * gw:4 * gw].T.astype(BF16)
        nb0 = pos0 // SEL_BLOCK
        causal = pos0 + krow <= qpos_w
        ts = [jnp.where(causal, _nt(_kchunk(kn, g), qzr_s[g]) + _block_bias(bias_s[g, nb0:nb0 + blk_pp, :], blk_pp), NEG)
              for g in range(KV_HEADS)]
        _flash_groups(ts, vnt, m_s, acc_s)

        n_cached = cwin_ref.shape[2]
        wn = _pad_rows(wnew_ref[...], PAGE)
        kws = [jnp.concatenate([wn[:, cp * LANES:(cp + 1) * LANES], cwin_ref[0, cp * LANES:(cp + 1) * LANES, :].T],
                               axis=0).astype(BF16) for cp in range(KV_HEADS // 2)]
        vwin = jnp.concatenate([wn[:, gw:2 * gw].T, cwin_ref[1]], axis=1).astype(BF16)
        kpos = jnp.concatenate([pos0 + krow, pos0 - n_cached + lax.broadcasted_iota(I32, (n_cached, w), 0)], axis=0)
        wmask = jnp.logical_and(jnp.logical_and(kpos <= qpos_w, qpos_w - kpos < WINDOW), kpos >= 0)
        ws = [jnp.where(wmask, _nt(kws[g // 2], qzr_s[g]), NEG) for g in range(KV_HEADS)]
        _finish(o_ref, acc_s, ocmp_s, _one_shot_groups(ws, vwin), ot_s, gt_s[...], nq, n_new, w)


def _attn_sample(q, gate_pre, b_gate, cos, sin, kc, vct, pages, table, kv_new, cache_win_t, win_new,
                 batch, n_new, n_pages, name):
    m, d = q.shape
    nq = SAMPLE_NQ
    w = HPG * nq
    assert w == LANES and n_new <= nq and n_new % SUBLANES == 0 and n_new <= SEL_BLOCK
    n_pg = ATT_PAGES
    assert n_pages % n_pg == 0
    n_steps = n_pages // n_pg
    pos0 = n_pages * PAGE
    nb = (pos0 + n_new + SEL_BLOCK - 1) // SEL_BLOCK
    nbp = kc.shape[1] // 4
    gw = KV_HEADS * HEAD_DIM
    n_cached = cache_win_t.shape[3]
    assert n_cached % PAGE == 0

    def page_map(u, branch):
        return lambda b, s, tbl: (tbl[b * n_pages + s * n_pg + u], branch, 0, 0)

    row_map = lambda b, s, tbl: (b, 0)
    per_b3 = lambda b, s, tbl: (b, 0, 0)
    const2 = lambda b, s, tbl: (0, 0)
    in_specs = [pl.BlockSpec((None, None, gw, PAGE), page_map(u, 2)) for u in range(n_pg)]
    in_specs += [pl.BlockSpec((None, None, gw, PAGE), page_map(u, 3)) for u in range(n_pg)]
    in_specs += [pl.BlockSpec((n_new, d), row_map), pl.BlockSpec((n_new, LANES), row_map),
                 pl.BlockSpec((1, LANES), const2), pl.BlockSpec((n_new, LANES), const2),
                 pl.BlockSpec((n_new, LANES), const2),
                 pl.BlockSpec((None, 4 * nbp, gw), per_b3), pl.BlockSpec((None, gw, 4 * nbp), per_b3),
                 pl.BlockSpec((n_new, 4 * gw), row_map),
                 pl.BlockSpec((None, 2, gw, n_cached), lambda b, s, tbl: (b, 0, 0, 0)),
                 pl.BlockSpec((n_new, 2 * gw), row_map)]
    grid_spec = pltpu.PrefetchScalarGridSpec(
        num_scalar_prefetch=1, grid=(batch, n_steps), in_specs=in_specs,
        out_specs=pl.BlockSpec((n_new, d), row_map),
        scratch_shapes=[pltpu.VMEM((KV_HEADS, 1, w), F32), pltpu.VMEM((KV_HEADS, ACC_ROWS, w), F32),
                        pltpu.VMEM((nbp, KV_HEADS * LANES), F32), pltpu.VMEM((KV_HEADS, nbp, w), F32),
                        pltpu.VMEM((KV_HEADS, HEAD_DIM, w), F32), pltpu.VMEM((gw, w), F32),
                        pltpu.VMEM((KV_HEADS, w, LANES), BF16), pltpu.VMEM((LANES, LANES), F32)])
    return pl.pallas_call(
        functools.partial(_attn_sample_kernel, n_pg=n_pg, n_steps=n_steps, nb=nb, nbp=nbp, pos0=pos0, n_new=n_new),
        grid_spec=grid_spec, out_shape=jax.ShapeDtypeStruct((m, d), F32),
        compiler_params=_cparams(("parallel", "arbitrary")), name=name,
    )(table, *([pages] * 2 * n_pg), q, gate_pre, b_gate, cos, sin, kc, vct, kv_new, cache_win_t, win_new)


def _prep_weights(w_a_in, b_a_gate, w_a_out, w_kv, cmp_pos, cmp_w1, cmp_b1, cmp_w2, cmp_b2,
                  w_b_in, b_b_gate, w_b_out, w_ffn_up, w_ffn_down):
    a_q = M_HEADS * M_QK_DIM
    a_v = M_HEADS * M_V_DIM
    n_g = 2 * M_HEADS
    g0 = 2 * a_q + a_v
    gw = KV_HEADS * HEAD_DIM
    qd = N_HEADS * HEAD_DIM
    p = {}
    p["a_main"] = jnp.concatenate([w_a_in[:, :, :g0], w_a_in[:, :, g0 + n_g:]], axis=-1).astype(BF16)
    p["a_qvo"] = jnp.concatenate([w_a_in[:, :, :a_q], w_a_in[:, :, 2 * a_q:g0], w_a_in[:, :, g0 + n_g:]],
                                 axis=-1).astype(BF16)
    p["a_kt"] = w_a_in[:, :, a_q:2 * a_q].transpose(0, 2, 1).astype(BF16)
    p["a_gate"] = jnp.pad(w_a_in[:, :, g0:g0 + n_g], ((0, 0), (0, 0), (0, LANES - n_g))).astype(BF16)
    p["a_bgate"] = jnp.pad(b_a_gate, ((0, 0), (0, LANES - n_g)))[:, None, :]
    p["a_out"] = w_a_out.astype(BF16)
    p["kv"] = w_kv.astype(BF16)
    p["kv_vt"] = jnp.concatenate([w_kv[:, 3 * gw:4 * gw], w_kv[:, 5 * gw:6 * gw]], axis=1).T.astype(BF16)
    p["kv_ct"] = w_kv[:, 0:2 * gw].T.astype(BF16)
    hh, g, c = jnp.meshgrid(jnp.arange(HPG), jnp.arange(KV_HEADS), jnp.arange(3), indexing="ij")
    old_col = ((HPG * g + hh) * 3 + c)
    new_col = ((c * HPG + hh) * KV_HEADS + g)
    order = jnp.zeros((3 * N_HEADS,), I32).at[new_col.reshape(-1)].set(old_col.reshape(-1))
    p["b_q"] = w_b_in[:, :, :qd].astype(BF16)
    p["b_gate"] = jnp.pad(w_b_in[:, :, qd:][:, :, order], ((0, 0), (0, 0), (0, LANES - 3 * N_HEADS))).astype(BF16)
    p["b_bgate"] = jnp.pad(b_b_gate[:, order], ((0, 0), (0, LANES - 3 * N_HEADS)))[:, None, :]
    wo = w_b_out.reshape(w_b_out.shape[0], KV_HEADS, HPG, HEAD_DIM, D_MODEL).transpose(0, 2, 1, 3, 4)
    p["b_out"] = wo.reshape(w_b_out.shape[0], qd, D_MODEL).astype(BF16)
    p["up"] = w_ffn_up.astype(BF16)
    p["down"] = w_ffn_down.astype(BF16)
    flat = CMP_STRIDE * HEAD_DIM
    p["cmp"] = {
        "pos_lo": cmp_pos[:, :CMP_STRIDE].reshape(2, 1, flat), "pos_hi": cmp_pos[:, CMP_STRIDE:].reshape(2, 1, flat),
        "w1lo": cmp_w1[:, :flat].astype(BF16), "w1hi": cmp_w1[:, flat:].astype(BF16),
        "b1": cmp_b1[:, None, :], "w2": cmp_w2.astype(BF16), "b2": cmp_b2[:, None, :]}
    return p


def _trunk(x3, pos0, past, p, g_norms, g_a_hnorm, g_kv, ffn_conv_w, ffn_conv_b, conv_prev, m_c, m_n, m_m, tag):
    batch, seq, d = x3.shape
    x = x3.reshape(batch * seq, d)
    cs, ns, ms, convs = [], [], [], []
    gw = KV_HEADS * HEAD_DIM
    for layer in range(DEPTH):
        g = g_norms[layer]
        nm = f"{tag}{layer}"
        if layer < N_A_LAYERS:
            if seq % MLSTM_CHUNK == 0:
                main, gate, kt = _norm_proj(x, g[0], [p["a_qvo"][layer], p["a_gate"][layer]], nm + "_in",
                                            wts=[p["a_kt"][layer]])
            else:
                main, gate = _norm_proj(x, g[0], [p["a_main"][layer], p["a_gate"][layer]], nm + "_in")
                kt = None
            h, c_new, n_new, m_new = _mlstm(main, gate, kt, p["a_bgate"][layer], g_a_hnorm[layer],
                                            m_c[layer], m_n[layer], m_m[layer], batch, seq, nm + "_mlstm")
            cs.append(c_new)
            ns.append(n_new)
            ms.append(m_new)
            x = _proj_norm_res(h, p["a_out"][layer], g[1], x, nm + "_out")
        else:
            if layer == N_A_LAYERS:
                cos, sin = _rope_tables(pos0, seq)
                reps = 1 if past is None else batch
                kv4, win, ks_b, kw_b, vst, vwt, cmpt = _kv_rows(x, g_kv, p["kv"], p["kv_vt"], p["kv_ct"],
                                                                jnp.tile(cos, (reps, 1)), jnp.tile(sin, (reps, 1)),
                                                                tag + "_kv")
                zeros_hi = jnp.zeros((batch, 2 * KV_HEADS, CMP_HIDDEN), F32)
                if past is None:
                    n_pages = seq // PAGE
                    nb = seq // SEL_BLOCK
                    kc, vc, _ = _compress(cmpt, jnp.arange(batch * n_pages, dtype=I32), batch, n_pages,
                                          min(CMP_PAGES, n_pages), zeros_hi, p["cmp"], None, tag + "_cmp")
                else:
                    pages, table, n_pages, cache_win_t = past
                    nb = (pos0 + seq + SEL_BLOCK - 1) // SEL_BLOCK
                    n_tail = nb * (SEL_BLOCK // CMP_STRIDE) - n_pages * (PAGE // CMP_STRIDE)
                    tail = jnp.pad(kv4[:, :2 * gw].reshape(batch, seq, 2, KV_HEADS, HEAD_DIM),
                                   ((0, 0), (0, PAGE - seq), (0, 0), (0, 0), (0, 0)))
                    kc_t, vc_t, hi_t = _compress(_token_minor(tail), jnp.arange(batch, dtype=I32), batch, 1, 1, zeros_hi,
                                                 p["cmp"], n_tail - 1, tag + "_cmpt")
                    kc_m, vc_m, _ = _compress(pages, table, batch, n_pages, min(CMP_PAGES, n_pages), hi_t,
                                              p["cmp"], None, tag + "_cmp")
                    kc = jnp.concatenate([kc_m, kc_t[:, :n_tail]], axis=1)
                    vc = jnp.concatenate([vc_m, vc_t[:, :n_tail]], axis=1)
                nbp = -(-nb // 32) * 32
                kc_l = _cmp_layout(kc, nb, nbp)
                vct_l = _cmp_layout(vc, nb, nbp).transpose(0, 2, 1)
            bl = layer - N_A_LAYERS
            q_raw, gate_pre = _norm_proj(x, g[0], [p["b_q"][bl], p["b_gate"][bl]], nm + "_in")
            if past is None:
                o = _attn_prompt(q_raw, gate_pre, p["b_bgate"][bl], cos, sin, kc_l, vct_l, ks_b, vst, kw_b, vwt,
                                 batch, seq, nm + "_attn")
            else:
                o = _attn_sample(q_raw, gate_pre, p["b_bgate"][bl], cos, sin, kc_l, vct_l, pages, table,
                                 kv4, cache_win_t, win, batch, seq, n_pages, nm + "_attn")
            x = _proj_norm_res(o, p["b_out"][bl], g[1], x, nm + "_out")
        x, conv_new = _ffn(x, g[2], p["up"][layer], ffn_conv_w[layer], ffn_conv_b[layer], p["down"][layer], g[3],
                           conv_prev[layer], batch, seq, nm + "_ffn")
        convs.append(conv_new)
    return (x.reshape(batch, seq, d), kv4, win, jnp.stack(cs), jnp.stack(ns), jnp.stack(ms), jnp.stack(convs))


def _past_views(cache_kv, cache_win_kv, page_table):
    return (_token_minor(cache_kv), page_table.reshape(-1), page_table.shape[1], _token_minor(cache_win_kv))


def kernel(x_prompt, x_sample, cache_kv, cache_win_kv, state_mlstm_C, state_mlstm_n, state_mlstm_m, state_conv,
           page_table, g_norms, w_a_in, b_a_gate, g_a_hnorm, w_a_out, g_kv, w_kv, cmp_pos, cmp_w1, cmp_b1, cmp_w2,
           cmp_b2, w_b_in, b_b_gate, w_b_out, w_ffn_up, ffn_conv_w, ffn_conv_b, w_ffn_down):
    p = _prep_weights(w_a_in, b_a_gate, w_a_out, w_kv, cmp_pos, cmp_w1, cmp_b1, cmp_w2, cmp_b2,
                      w_b_in, b_b_gate, w_b_out, w_ffn_up, w_ffn_down)
    dt = x_prompt.dtype
    bp, tp, _ = x_prompt.shape
    bs, ts, _ = x_sample.shape
    past_len = page_table.shape[1] * PAGE
    gw = KV_HEADS * HEAD_DIM
    shared = (p, g_norms, g_a_hnorm, g_kv, ffn_conv_w, ffn_conv_b)

    y_p, kv_p, win_p, c_p, n_p, m_p, conv_p = _trunk(
        x_prompt, 0, None, *shared,
        jnp.zeros((DEPTH, bp, CONV_W - 1, 2 * D_FF), dt),
        jnp.zeros((N_A_LAYERS, bp, M_HEADS, M_QK_DIM, M_V_DIM), dt),
        jnp.zeros((N_A_LAYERS, bp, M_HEADS, M_QK_DIM), dt), jnp.zeros((N_A_LAYERS, bp, M_HEADS), dt), "p")
    kv_p = kv_p.reshape(bp, tp, 4, KV_HEADS, HEAD_DIM)
    n_win_p = min(WINDOW, tp)
    win_p = win_p.reshape(bp, tp, 2, KV_HEADS, HEAD_DIM)[:, tp - n_win_p:]

    past = _past_views(cache_kv, cache_win_kv, page_table)
    y_s, kv_s, win_s, c_s, n_s, m_s, conv_s = _trunk(
        x_sample, past_len, past, *shared, state_conv, state_mlstm_C, state_mlstm_n, state_mlstm_m, "s")
    kv_s = kv_s.reshape(bs, ts, 4, KV_HEADS, HEAD_DIM)
    win_all = jnp.concatenate([cache_win_kv, win_s.reshape(bs, ts, 2, KV_HEADS, HEAD_DIM)], axis=1)
    win_s = win_all[:, win_all.shape[1] - min(WINDOW, win_all.shape[1]):]
    return (y_p, y_s, kv_p, kv_s, win_p, win_s, c_p, c_s, n_p, n_s, m_p, m_s, conv_p, conv_s)
```

```python
import functools
import math

import jax
import jax.numpy as jnp
from jax import lax
from jax.experimental import pallas as pl
from jax.experimental.pallas import tpu as pltpu

F32 = jnp.float32
BF16 = jnp.bfloat16
I32 = jnp.int32

D_MODEL = 1024
DEPTH = 4
N_A_LAYERS = 2
M_HEADS = 8
M_QK_DIM = 64
M_V_DIM = 128
GATE_CAP = 15.0
N_HEADS = 16
HEAD_DIM = 64
KV_HEADS = 4
HPG = 4
CMP_BLOCK = 32
CMP_STRIDE = 16
CMP_HIDDEN = 256
SEL_BLOCK = 64
N_SELECT = 16
WINDOW = 512
ROT_DIM = 16
ROPE_THETA = 500000.0
D_FF = 2816
CONV_W = 3
EPS = 1e-6
BIG = 1e9
NEG = -1e30
PAGE = 128

LANES = 128
SUBLANES = 8
VMEM_LIMIT = 56 * 1024 * 1024

MLSTM_CHUNK = 128
KEY_TILE = 512
Q_TILE = 128
SAMPLE_NQ = 32
CMP_PAGES = 16
ATT_PAGES = 8
FFN_TM = 512
FFN_TN = 1408


def _cparams(sem):
    return pltpu.CompilerParams(dimension_semantics=sem, vmem_limit_bytes=VMEM_LIMIT)


def _rms(x, g):
    return x * lax.rsqrt(jnp.mean(x * x, axis=-1, keepdims=True) + EPS) * g


def _nt(a, b):
    return lax.dot_general(a, b, (((1,), (1,)), ((), ())), preferred_element_type=F32)


def _dot(a, b):
    return jnp.dot(a, b, preferred_element_type=F32)


def _row_tile(m, pref):
    t = min(m, pref)
    while m % t:
        t //= 2
    return t


def _norm_proj_kernel(x_ref, g_ref, *refs, n_w, n_t):
    xn = _rms(x_ref[...], g_ref[...]).astype(BF16)
    n_in = n_w + n_t
    for w_ref, o_ref in zip(refs[:n_w], refs[n_in:n_in + n_w]):
        o_ref[...] = _dot(xn, w_ref[...]).astype(o_ref.dtype)
    for w_ref, o_ref in zip(refs[n_w:n_in], refs[n_in + n_w:]):
        yt = _nt(w_ref[...], xn)
        for u in range(o_ref.shape[0]):
            o_ref[u] = yt[:, u * LANES:(u + 1) * LANES]


def _norm_proj(x, g, ws, name, wts=()):
    m, d = x.shape
    tm = _row_tile(m, 512)
    assert not wts or tm % LANES == 0
    n_w, n_t = len(ws), len(wts)
    in_specs = [pl.BlockSpec((tm, d), lambda i: (i, 0)), pl.BlockSpec((1, d), lambda i: (0, 0))]
    in_specs += [pl.BlockSpec(w.shape, lambda i: (0, 0)) for w in (*ws, *wts)]
    out_specs = [pl.BlockSpec((tm, w.shape[1]), lambda i: (i, 0)) for w in ws]
    out_specs += [pl.BlockSpec((tm // LANES, w.shape[0], LANES), lambda i: (i, 0, 0)) for w in wts]
    out_shape = [jax.ShapeDtypeStruct((m, w.shape[1]), F32) for w in ws]
    out_shape += [jax.ShapeDtypeStruct((m // LANES, w.shape[0], LANES), F32) for w in wts]
    return pl.pallas_call(
        functools.partial(_norm_proj_kernel, n_w=n_w, n_t=n_t), grid=(m // tm,), in_specs=in_specs,
        out_specs=out_specs, out_shape=out_shape, compiler_params=_cparams(("parallel",)),
        name=name)(x, g.reshape(1, d), *ws, *wts)


def _proj_norm_res_kernel(a_ref, w_ref, g_ref, res_ref, o_ref):
    y = _dot(a_ref[...].astype(BF16), w_ref[...])
    o_ref[...] = res_ref[...] + _rms(y, g_ref[...])


def _proj_norm_res(a, w, g, res, name):
    m, k = a.shape
    d = w.shape[1]
    tm = _row_tile(m, 512)
    return pl.pallas_call(
        _proj_norm_res_kernel, grid=(m // tm,),
        in_specs=[pl.BlockSpec((tm, k), lambda i: (i, 0)), pl.BlockSpec((k, d), lambda i: (0, 0)),
                  pl.BlockSpec((1, d), lambda i: (0, 0)), pl.BlockSpec((tm, d), lambda i: (i, 0))],
        out_specs=pl.BlockSpec((tm, d), lambda i: (i, 0)),
        out_shape=jax.ShapeDtypeStruct((m, d), F32),
        compiler_params=_cparams(("parallel",)), name=name)(a, w, g.reshape(1, d), res)


def _mlstm_kernel(main_ref, gate_ref, *refs, rows, nc, k_minor):
    if k_minor:
        kt_ref, *refs = refs
    bg_ref, gh_ref, c0_ref, n0_ref, m0_ref, h_ref, c_ref, n_ref, m_ref, c_s, n_s, m_s = refs
    L = MLSTM_CHUNK
    nh = M_HEADS
    a_q = nh * M_QK_DIM
    v0 = a_q if k_minor else 2 * a_q
    o0 = v0 + nh * M_V_DIM
    cidx = pl.program_id(1)

    @pl.when(cidx == 0)
    def _():
        c_s[...] = c0_ref[...]
        n_s[...] = n0_ref[...]
        m_s[...] = m0_ref[...]

    main = main_ref[...]
    gp = gate_ref[...] + bg_ref[...]
    if rows < L:
        main = jnp.concatenate([main, jnp.zeros((L - rows, main.shape[1]), F32)], axis=0)
        gp = jnp.concatenate([gp, jnp.zeros((L - rows, LANES), F32)], axis=0)
    capped = GATE_CAP * jnp.tanh(gp / GATE_CAP)
    row1 = lax.broadcasted_iota(I32, (L, LANES), 0)
    real = row1 < rows
    ilog = jnp.where(real, capped, -jnp.inf)
    logf = jnp.where(real, jnp.minimum(capped, 0.0) - jnp.log1p(jnp.exp(-jnp.abs(capped))), 0.0)
    bh = logf
    k = 1
    while k < L:
        bh = bh + jnp.where(row1 >= k, pltpu.roll(bh, k, axis=0), 0.0)
        k *= 2
    bh = pltpu.roll(bh, LANES - nh, axis=1)
    c_all = ilog - bh
    cm = c_all
    k = 1
    while k < L:
        cm = jnp.maximum(cm, jnp.where(row1 >= k, pltpu.roll(cm, k, axis=0), -jnp.inf))
        k *= 2
    m_row = m_s[...]
    mt = bh + jnp.maximum(m_row, cm)
    m_new = mt[L - 1:L, :]
    b_last = bh[L - 1:L, :]
    w_inter = jnp.exp(bh + m_row - mt)
    u_all = bh - mt
    emt = jnp.exp(-mt)
    ws_all = jnp.exp(b_last - bh + ilog - m_new)
    decay = jnp.exp(b_last + m_row - m_new)
    c_t = c_all.T
    ws_t = ws_all.T

    rr = lax.broadcasted_iota(I32, (L, L), 0)
    cc = lax.broadcasted_iota(I32, (L, L), 1)
    causal = cc <= rr
    lo_half = lax.broadcasted_iota(I32, (L, LANES), 1) < M_QK_DIM
    lo_rows = lax.broadcasted_iota(I32, (2 * M_QK_DIM, LANES), 0) < M_QK_DIM
    ones_b = jnp.ones((L, LANES), BF16)
    assert L == LANES

    def lanes(a, x):
        return jnp.broadcast_to(a[:, x:x + 1], (L, LANES))

    for p in range(nh // 2):
        qp = main[:, p * LANES:(p + 1) * LANES] * (M_QK_DIM ** -0.5)
        if k_minor:
            kt = kt_ref[0, p * LANES:(p + 1) * LANES, :]
        else:
            kt = main[:, a_q + p * LANES:a_q + (p + 1) * LANES].T
        c_pair = c_s[2 * p:2 * p + 2].reshape(2 * M_QK_DIM, M_V_DIM)
        n_pair = n_s[p]
        kt_b = kt.astype(BF16)
        state_b = jnp.concatenate([c_pair, n_pair], axis=1).astype(BF16)
        for e in range(2):
            x = 2 * p + e
            qx_b = jnp.where(lo_half if e == 0 else jnp.logical_not(lo_half), qp, 0.0).astype(BF16)
            vx = main[:, v0 + x * M_V_DIM:v0 + (x + 1) * M_V_DIM]
            ox = main[:, o0 + x * M_V_DIM:o0 + (x + 1) * M_V_DIM]
            a = jnp.exp(jnp.where(causal, lanes(u_all, x) + c_t[x:x + 1, :], -jnp.inf)) * _dot(qx_b, kt_b)
            inter = _dot(qx_b, state_b)
            intra = _dot(a.astype(BF16), jnp.concatenate([vx.astype(BF16), ones_b], axis=1))
            wi = lanes(w_inter, x)
            num = wi * inter[:, :M_V_DIM] + intra[:, :M_V_DIM]
            den = wi * inter[:, M_V_DIM:] + intra[:, M_V_DIM:]
            h = num / jnp.maximum(jnp.abs(den), lanes(emt, x))
            hn = h * lax.rsqrt(jnp.mean(h * h, axis=-1, keepdims=True) + EPS)
            hn = hn * gh_ref[:, x * M_V_DIM:(x + 1) * M_V_DIM] * jax.nn.sigmoid(ox)
            h_ref[:, x * M_V_DIM:(x + 1) * M_V_DIM] = hn[:rows]
        kwt = kt * jnp.where(lo_rows, ws_t[2 * p:2 * p + 1, :], ws_t[2 * p + 1:2 * p + 2, :])
        vcat = main[:, v0 + 2 * p * M_V_DIM:v0 + (2 * p + 2) * M_V_DIM]
        upd = _dot(kwt.astype(BF16), jnp.concatenate([vcat.astype(BF16), ones_b], axis=1))
        dec_e = decay[:, 2 * p:2 * p + 1]
        dec_o = decay[:, 2 * p + 1:2 * p + 2]
        c_s[2 * p] = dec_e * c_pair[:M_QK_DIM] + upd[:M_QK_DIM, :M_V_DIM]
        c_s[2 * p + 1] = dec_o * c_pair[M_QK_DIM:] + upd[M_QK_DIM:, M_V_DIM:2 * M_V_DIM]
        n_s[p] = jnp.where(lo_rows, dec_e, dec_o) * n_pair + upd[:, 2 * M_V_DIM:]
    m_s[...] = m_new

    @pl.when(cidx == nc - 1)
    def _():
        c_ref[...] = c_s[...]
        n_ref[...] = n_s[...]
        m_ref[...] = m_s[...]


def _mlstm(main, gate, kt, b_gate, g_hnorm, c0, n0, m0, batch, seq, name):
    rows = min(seq, MLSTM_CHUNK)
    nc = seq // rows
    a_v = M_HEADS * M_V_DIM
    hp = M_HEADS // 2
    n_in = jnp.broadcast_to(n0.reshape(batch, hp, 2 * M_QK_DIM, 1), (batch, hp, 2 * M_QK_DIM, LANES))
    m_in = jnp.pad(m0, ((0, 0), (0, LANES - M_HEADS)))[:, None, :]
    k_minor = kt is not None
    assert not k_minor or rows == MLSTM_CHUNK == LANES
    kt_specs = [pl.BlockSpec((1, kt.shape[1], LANES), lambda b, c: (b * nc + c, 0, 0))] if k_minor else []
    h, c, n, m = pl.pallas_call(
        functools.partial(_mlstm_kernel, rows=rows, nc=nc, k_minor=k_minor), grid=(batch, nc),
        in_specs=[pl.BlockSpec((rows, main.shape[1]), lambda b, c: (b * nc + c, 0)),
                  pl.BlockSpec((rows, LANES), lambda b, c: (b * nc + c, 0)), *kt_specs,
                  pl.BlockSpec((1, LANES), lambda b, c: (0, 0)),
                  pl.BlockSpec((1, a_v), lambda b, c: (0, 0)),
                  pl.BlockSpec((None, M_HEADS, M_QK_DIM, M_V_DIM), lambda b, c: (b, 0, 0, 0)),
                  pl.BlockSpec((None, hp, 2 * M_QK_DIM, LANES), lambda b, c: (b, 0, 0, 0)),
                  pl.BlockSpec((None, 1, LANES), lambda b, c: (b, 0, 0))],
        out_specs=[pl.BlockSpec((rows, a_v), lambda b, c: (b * nc + c, 0)),
                   pl.BlockSpec((None, M_HEADS, M_QK_DIM, M_V_DIM), lambda b, c: (b, 0, 0, 0)),
                   pl.BlockSpec((None, hp, 2 * M_QK_DIM, LANES), lambda b, c: (b, 0, 0, 0)),
                   pl.BlockSpec((None, 1, LANES), lambda b, c: (b, 0, 0))],
        out_shape=[jax.ShapeDtypeStruct((batch * seq, a_v), F32),
                   jax.ShapeDtypeStruct((batch, M_HEADS, M_QK_DIM, M_V_DIM), F32),
                   jax.ShapeDtypeStruct((batch, hp, 2 * M_QK_DIM, LANES), F32),
                   jax.ShapeDtypeStruct((batch, 1, LANES), F32)],
        scratch_shapes=[pltpu.VMEM((M_HEADS, M_QK_DIM, M_V_DIM), F32), pltpu.VMEM((hp, 2 * M_QK_DIM, LANES), F32),
                        pltpu.VMEM((1, LANES), F32)],
        compiler_params=_cparams(("parallel", "arbitrary")), name=name,
    )(main, gate, *([kt] if k_minor else []), b_gate, g_hnorm.reshape(1, a_v), c0, n_in, m_in)
    return h, c, n[..., 0].reshape(batch, M_HEADS, M_QK_DIM), m[:, 0, :M_HEADS]


def _ffn_kernel(*refs, carry, tm, n_j, tiles_per_seq, period):
    x_ref, g2_ref, wua_ref, wug_ref, cwa_ref, cwg_ref, cba_ref, cbg_ref, wd_ref, g3_ref = refs[:10]
    if carry:
        inita_ref, initg_ref, o_ref, sa_ref, sg_ref, xn_s, acc_s, ue_s, carry_s = refs[10:]
        branch_in = ((wua_ref, cwa_ref, cba_ref, inita_ref, sa_ref), (wug_ref, cwg_ref, cbg_ref, initg_ref, sg_ref))
    else:
        t1a_ref, t1g_ref, t2a_ref, t2g_ref, o_ref, sa_ref, sg_ref, xn_s, acc_s, ue_s = refs[10:]
        branch_in = ((wua_ref, cwa_ref, cba_ref, (t1a_ref, t2a_ref), sa_ref),
                     (wug_ref, cwg_ref, cbg_ref, (t1g_ref, t2g_ref), sg_ref))
    i = pl.program_id(0)
    j = pl.program_id(1)
    tn = wd_ref.shape[0]

    @pl.when(j == 0)
    def _():
        xn_s[...] = _rms(x_ref[...], g2_ref[...]).astype(BF16)
        acc_s[...] = jnp.zeros_like(acc_s)

    xn = xn_s[...]
    conv = []
    for which, (w_ref, cw_ref, cb_ref, boundary, s_ref) in enumerate(branch_in):
        u = _dot(xn, w_ref[...])
        ue_s[SUBLANES:, :] = u
        if carry:
            slot = which * n_j + j
            first = (i % tiles_per_seq) == 0
            ue_s[SUBLANES - 2:SUBLANES, :] = jnp.where(first, boundary[...], carry_s[slot])
            tap1 = ue_s[pl.ds(SUBLANES - 1, tm), :]
            tap2 = ue_s[pl.ds(SUBLANES - 2, tm), :]
            last2 = u[tm - 2:tm, :]
            carry_s[slot] = last2
            s_ref[i // tiles_per_seq, j] = last2
        else:
            ue_s[0:SUBLANES, :] = jnp.zeros((SUBLANES, tn), F32)
            t = lax.broadcasted_iota(I32, (tm, tn), 0) & (period - 1)
            tap1 = jnp.where(t >= 1, ue_s[pl.ds(SUBLANES - 1, tm), :], boundary[0][...])
            tap2 = jnp.where(t >= 2, ue_s[pl.ds(SUBLANES - 2, tm), :], boundary[1][...])
            s_ref[...] = u
        conv.append(cb_ref[...] + tap2 * cw_ref[0:1, :] + tap1 * cw_ref[1:2, :] + u * cw_ref[2:3, :])
    y = jax.nn.gelu(conv[0], approximate=True) * conv[1]
    acc_s[...] += _dot(y.astype(BF16), wd_ref[...])

    @pl.when(j == n_j - 1)
    def _():
        o_ref[...] = x_ref[...] + _rms(acc_s[...], g3_ref[...])


def _ffn(x, g2, w_up, conv_w, conv_b, w_down, g3, prev, batch, seq, name):
    m, d = x.shape
    nf = w_down.shape[0]
    tn = FFN_TN
    n_j = nf // tn
    carry = seq >= 256
    cb = conv_b.reshape(1, 2 * nf)
    col_a = lambda i, j: (0, j)
    col_g = lambda i, j: (0, j + n_j)
    common_specs = [
        None,
        pl.BlockSpec((1, d), lambda i, j: (0, 0)),
        pl.BlockSpec((d, tn), col_a), pl.BlockSpec((d, tn), col_g),
        pl.BlockSpec((CONV_W, tn), col_a), pl.BlockSpec((CONV_W, tn), col_g),
        pl.BlockSpec((1, tn), col_a), pl.BlockSpec((1, tn), col_g),
        pl.BlockSpec((tn, d), lambda i, j: (j, 0)),
        pl.BlockSpec((1, d), lambda i, j: (0, 0)),
    ]
    common_args = [x, g2.reshape(1, d), w_up, w_up, conv_w, conv_w, cb, cb, w_down, g3.reshape(1, d)]
    if carry:
        tm = _row_tile(seq, FFN_TM)
        tps = seq // tm
        common_specs[0] = pl.BlockSpec((tm, d), lambda i, j: (i, 0))
        st_a = lambda i, j: (i // tps, 0, j)
        st_g = lambda i, j: (i // tps, 0, j + n_j)
        out, sa, sg = pl.pallas_call(
            functools.partial(_ffn_kernel, carry=True, tm=tm, n_j=n_j, tiles_per_seq=tps, period=seq),
            grid=(m // tm, n_j),
            in_specs=common_specs + [pl.BlockSpec((None, 2, tn), st_a), pl.BlockSpec((None, 2, tn), st_g)],
            out_specs=[pl.BlockSpec((tm, d), lambda i, j: (i, 0)),
                       pl.BlockSpec((batch, n_j, 2, tn), lambda i, j: (0, 0, 0, 0)),
                       pl.BlockSpec((batch, n_j, 2, tn), lambda i, j: (0, 0, 0, 0))],
            out_shape=[jax.ShapeDtypeStruct((m, d), F32), jax.ShapeDtypeStruct((batch, n_j, 2, tn), F32),
                       jax.ShapeDtypeStruct((batch, n_j, 2, tn), F32)],
            scratch_shapes=[pltpu.VMEM((tm, d), BF16), pltpu.VMEM((tm, d), F32),
                            pltpu.VMEM((tm + SUBLANES, tn), F32), pltpu.VMEM((2 * n_j, 2, tn), F32)],
            compiler_params=_cparams(("arbitrary", "arbitrary")), name=name,
        )(*common_args, prev, prev)
        sa, sg = (s.transpose(0, 2, 1, 3).reshape(batch, 2, nf) for s in (sa, sg))
        return out, jnp.concatenate([sa, sg], axis=-1)
    tm = m
    assert seq >= 2 and seq & (seq - 1) == 0
    common_specs[0] = pl.BlockSpec((tm, d), lambda i, j: (i, 0))
    tap1 = jnp.pad(prev[:, 1:2], ((0, 0), (0, seq - 1), (0, 0))).reshape(m, 2 * nf)
    tap2 = jnp.pad(prev, ((0, 0), (0, seq - 2), (0, 0))).reshape(m, 2 * nf)
    row_a = lambda i, j: (i, j)
    row_g = lambda i, j: (i, j + n_j)
    out, ua, ug = pl.pallas_call(
        functools.partial(_ffn_kernel, carry=False, tm=tm, n_j=n_j, tiles_per_seq=1, period=seq),
        grid=(m // tm, n_j),
        in_specs=common_specs + [pl.BlockSpec((tm, tn), row_a), pl.BlockSpec((tm, tn), row_g),
                                 pl.BlockSpec((tm, tn), row_a), pl.BlockSpec((tm, tn), row_g)],
        out_specs=[pl.BlockSpec((tm, d), lambda i, j: (i, 0)), pl.BlockSpec((tm, tn), row_a),
                   pl.BlockSpec((tm, tn), row_a)],
        out_shape=[jax.ShapeDtypeStruct((m, d), F32), jax.ShapeDtypeStruct((m, nf), F32),
                   jax.ShapeDtypeStruct((m, nf), F32)],
        scratch_shapes=[pltpu.VMEM((tm, d), BF16), pltpu.VMEM((tm, d), F32), pltpu.VMEM((tm + SUBLANES, tn), F32)],
        compiler_params=_cparams(("arbitrary", "arbitrary")), name=name,
    )(*common_args, tap1, tap1, tap2, tap2)
    u = jnp.concatenate([ua, ug], axis=-1).reshape(batch, seq, 2 * nf)
    return out, u[:, seq - 2:]


def _rope_pair(x, cos, sin):
    half = ROT_DIM // 2
    lane = lax.broadcasted_iota(I32, x.shape, 1) & (HEAD_DIM - 1)
    partner = jnp.where(lane < half, pltpu.roll(x, LANES - half, axis=1), pltpu.roll(x, half, axis=1))
    return x * cos + partner * sin


def _rope(x, cos, sin):
    return jnp.concatenate(
        [_rope_pair(x[:, c * LANES:(c + 1) * LANES], cos, sin) for c in range(x.shape[1] // LANES)], axis=1)


def _rope_tables(pos0, seq):
    half = ROT_DIM // 2
    inv = jnp.power(jnp.float32(ROPE_THETA), -jnp.arange(0, ROT_DIM, 2, dtype=F32) / ROT_DIM)
    ang = (pos0 + jnp.arange(seq, dtype=I32)).astype(F32)[:, None] * inv[None, :]
    cos, sin = jnp.cos(ang), jnp.sin(ang)
    rest = HEAD_DIM - ROT_DIM
    cos_h = jnp.concatenate([cos, cos, jnp.ones((seq, rest), F32)], axis=1)
    sin_h = jnp.concatenate([-sin, sin, jnp.zeros((seq, rest), F32)], axis=1)
    return jnp.tile(cos_h, (1, LANES // HEAD_DIM)), jnp.tile(sin_h, (1, LANES // HEAD_DIM))


def _kv_kernel(x_ref, g_ref, w_ref, cos_ref, sin_ref, kv_ref, win_ref):
    xn = _rms(x_ref[...], g_ref[...]).astype(BF16)
    y = _dot(xn, w_ref[...])
    cos, sin = cos_ref[...], sin_ref[...]
    gw = KV_HEADS * HEAD_DIM
    ks = _rope(y[:, 2 * gw:3 * gw], cos, sin)
    kw = _rope(y[:, 4 * gw:5 * gw], cos, sin)
    kv_ref[:, 0:2 * gw] = y[:, 0:2 * gw]
    kv_ref[:, 2 * gw:3 * gw] = ks
    kv_ref[:, 3 * gw:4 * gw] = y[:, 3 * gw:4 * gw]
    win_ref[:, 0:gw] = kw
    win_ref[:, gw:2 * gw] = y[:, 5 * gw:6 * gw]


def _rope_rows(x, cos_t, sin_t):
    half = ROT_DIM // 2
    row = lax.broadcasted_iota(I32, x.shape, 0) & (HEAD_DIM - 1)
    partner = jnp.where(row < half, pltpu.roll(x, x.shape[0] - half, axis=0), pltpu.roll(x, half, axis=0))
    return x * cos_t + partner * sin_t


def _kv_minor_kernel(x_ref, g_ref, wt_ref, wk_ref, cos_ref, sin_ref, cost_ref, sint_ref,
                     kvt_ref, wint_ref, ks_ref, kw_ref, vst_ref, vwt_ref):
    xn = _rms(x_ref[...], g_ref[...]).astype(BF16)
    gw = KV_HEADS * HEAD_DIM
    yt = _nt(wt_ref[...], xn)
    cos_t, sin_t = cost_ref[...], sint_ref[...]

    def rope_t(a):
        return jnp.concatenate(
            [_rope_rows(a[c * LANES:(c + 1) * LANES], cos_t, sin_t) for c in range(gw // LANES)], axis=0)

    kvt_ref[0:2 * gw, :] = yt[0:2 * gw]
    kvt_ref[2 * gw:3 * gw, :] = rope_t(yt[2 * gw:3 * gw])
    kvt_ref[3 * gw:4 * gw, :] = yt[3 * gw:4 * gw]
    wint_ref[0:gw, :] = rope_t(yt[4 * gw:5 * gw])
    wint_ref[gw:2 * gw, :] = yt[5 * gw:6 * gw]
    vst_ref[...] = yt[3 * gw:4 * gw].astype(BF16)
    for u in range(vwt_ref.shape[0]):
        vwt_ref[u] = yt[5 * gw:6 * gw, u * PAGE:(u + 1) * PAGE].astype(BF16)
    yk = _dot(xn, wk_ref[...])
    cos, sin = cos_ref[...], sin_ref[...]
    ks_ref[...] = _rope(yk[:, 0:gw], cos, sin).astype(BF16)
    kw_ref[...] = _rope(yk[:, gw:2 * gw], cos, sin).astype(BF16)


def _kv_rows_minor(x, g_kv, w_t, w_k, cos, sin, batch, seq, name):
    m, d = x.shape
    tm = _row_tile(seq, KEY_TILE)
    gw = KV_HEADS * HEAD_DIM
    tt = seq // tm
    tok = lambda i: (i % tt, 0)
    tok_t = lambda i: (0, i % tt)
    return pl.pallas_call(
        _kv_minor_kernel, grid=(m // tm,),
        in_specs=[pl.BlockSpec((tm, d), lambda i: (i, 0)), pl.BlockSpec((1, d), lambda i: (0, 0)),
                  pl.BlockSpec(w_t.shape, lambda i: (0, 0)), pl.BlockSpec(w_k.shape, lambda i: (0, 0)),
                  pl.BlockSpec((tm, LANES), tok), pl.BlockSpec((tm, LANES), tok),
                  pl.BlockSpec((LANES, tm), tok_t), pl.BlockSpec((LANES, tm), tok_t)],
        out_specs=[pl.BlockSpec((None, 4 * gw, tm), lambda i: (i // tt, 0, i % tt)),
                   pl.BlockSpec((None, 2 * gw, tm), lambda i: (i // tt, 0, i % tt)),
                   pl.BlockSpec((tm, gw), lambda i: (i, 0)), pl.BlockSpec((tm, gw), lambda i: (i, 0)),
                   pl.BlockSpec((None, gw, tm), lambda i: (i, 0, 0)),
                   pl.BlockSpec((tm // PAGE, gw, PAGE), lambda i: (i, 0, 0))],
        out_shape=[jax.ShapeDtypeStruct((batch, 4 * gw, seq), F32), jax.ShapeDtypeStruct((batch, 2 * gw, seq), F32),
                   jax.ShapeDtypeStruct((m, gw), BF16), jax.ShapeDtypeStruct((m, gw), BF16),
                   jax.ShapeDtypeStruct((m // tm, gw, tm), BF16), jax.ShapeDtypeStruct((m // PAGE, gw, PAGE), BF16)],
        compiler_params=_cparams(("parallel",)), name=name)(x, g_kv.reshape(1, d), w_t, w_k, cos, sin, cos.T, sin.T)


def _kv_rows(x, g_kv, w_kv, cos, sin, name):
    m, d = x.shape
    gw = KV_HEADS * HEAD_DIM
    return pl.pallas_call(
        _kv_kernel, grid=(1,),
        in_specs=[pl.BlockSpec((m, d), lambda i: (0, 0)), pl.BlockSpec((1, d), lambda i: (0, 0)),
                  pl.BlockSpec(w_kv.shape, lambda i: (0, 0)),
                  pl.BlockSpec((m, LANES), lambda i: (0, 0)), pl.BlockSpec((m, LANES), lambda i: (0, 0))],
        out_specs=[pl.BlockSpec((m, 4 * gw), lambda i: (0, 0)), pl.BlockSpec((m, 2 * gw), lambda i: (0, 0))],
        out_shape=[jax.ShapeDtypeStruct((m, 4 * gw), F32), jax.ShapeDtypeStruct((m, 2 * gw), F32)],
        compiler_params=_cparams(("arbitrary",)), name=name)(x, g_kv.reshape(1, d), w_kv, cos, sin)


def _token_minor(rows):
    n_p, r, n = rows.shape[0], rows.shape[1], rows.shape[2]
    return rows.transpose(0, 2, 3, 4, 1).reshape(n_p, n, KV_HEADS * HEAD_DIM, r)


def _compress_kernel(tbl_ref, *refs, n_pg, zero_after):
    page_refs = refs[:n_pg]
    (pos_lo_ref, pos_hi_ref, w1lo_ref, w1hi_ref, b1_ref, w2_ref, b2_ref, hi_init_ref,
     kc_ref, vc_ref, hi_first_ref, carry_s, t_s, x_s) = refs[n_pg:]
    c = pl.program_id(1)
    njp = PAGE // CMP_STRIDE
    njc = n_pg * njp

    @pl.when(c == 0)
    def _():
        carry_s[...] = hi_init_ref[...]

    lo_lanes = lax.broadcasted_iota(I32, (njp, LANES), 1) < HEAD_DIM
    for u in range(n_pg):
        for br in range(2):
            for cp in range(KV_HEADS // 2):
                tb = t_s.at[(2 * u + br) % 2 * 2 + cp]
                tb[...] = page_refs[u][br, cp * LANES:(cp + 1) * LANES, :].T
                for q in range(CMP_STRIDE // 2):
                    p0 = tb[pl.ds(2 * q, njp, stride=CMP_STRIDE), :]
                    p1 = tb[pl.ds(2 * q + 1, njp, stride=CMP_STRIDE), :]
                    rows = slice(u * njp, (u + 1) * njp)
                    cols = slice(q * LANES, (q + 1) * LANES)
                    x_s[br * KV_HEADS + 2 * cp, rows, cols] = jnp.where(lo_lanes, p0, pltpu.roll(p1, HEAD_DIM, axis=1))
                    x_s[br * KV_HEADS + 2 * cp + 1, rows, cols] = jnp.where(lo_lanes, pltpu.roll(p0, HEAD_DIM, axis=1), p1)

    rowl = lax.broadcasted_iota(I32, (KV_HEADS * njc, CMP_HIDDEN), 0) & (njc - 1)
    for br, out_ref in enumerate((kc_ref, vc_ref)):
        x = x_s[br * KV_HEADS:(br + 1) * KV_HEADS].reshape(KV_HEADS * njc, CMP_STRIDE * HEAD_DIM)
        lo = _dot((x + pos_lo_ref[br]).astype(BF16), w1lo_ref[br])
        hi = _dot((x + pos_hi_ref[br]).astype(BF16), w1hi_ref[br])
        carry_rows = jnp.concatenate(
            [jnp.broadcast_to(carry_s[br * KV_HEADS + g:br * KV_HEADS + g + 1, :], (njc, CMP_HIDDEN))
             for g in range(KV_HEADS)], axis=0)
        hi_next = jnp.where(rowl == njc - 1, carry_rows, pltpu.roll(hi, KV_HEADS * njc - 1, axis=0))
        if zero_after is not None:
            hi_next = jnp.where(jnp.logical_and(rowl == zero_after, c == 0), 0.0, hi_next)
        for g in range(KV_HEADS):
            carry_s[br * KV_HEADS + g:br * KV_HEADS + g + 1, :] = hi[g * njc:g * njc + 1, :]
        h = jax.nn.gelu(lo + hi_next + b1_ref[br], approximate=True)
        o = _dot(h.astype(BF16), w2_ref[br]) + b2_ref[br]
        for g in range(KV_HEADS):
            out_ref[:, g * HEAD_DIM:(g + 1) * HEAD_DIM] = o[g * njc:(g + 1) * njc, :]
    hi_first_ref[...] = carry_s[...]


def _compress(pages, table, batch, n_pages, n_pg, hi_init, cw, zero_after, name):
    njp = PAGE // CMP_STRIDE
    njc = n_pg * njp
    assert njc & (njc - 1) == 0 and njc % SUBLANES == 0 and n_pages % n_pg == 0
    n_ch = n_pages // n_pg
    gw = KV_HEADS * HEAD_DIM
    flat = CMP_STRIDE * HEAD_DIM

    def page_map(u):
        if table is None:
            return lambda b, c, tbl: (b, 0, 0, (n_ch - 1 - c) * n_pg + u)
        return lambda b, c, tbl: (tbl[b * n_pages + (n_ch - 1 - c) * n_pg + u], 0, 0, 0)

    full3 = lambda b, c, tbl: (0, 0, 0)
    in_specs = [pl.BlockSpec((None, 2, gw, PAGE), page_map(u)) for u in range(n_pg)]
    in_specs += [pl.BlockSpec((2, 1, flat), full3), pl.BlockSpec((2, 1, flat), full3),
                 pl.BlockSpec((2, flat, CMP_HIDDEN), full3), pl.BlockSpec((2, flat, CMP_HIDDEN), full3),
                 pl.BlockSpec((2, 1, CMP_HIDDEN), full3), pl.BlockSpec((2, CMP_HIDDEN, HEAD_DIM), full3),
                 pl.BlockSpec((2, 1, HEAD_DIM), full3),
                 pl.BlockSpec((None, 2 * KV_HEADS, CMP_HIDDEN), lambda b, c, tbl: (b, 0, 0))]
    out_map = lambda b, c, tbl: (b, n_ch - 1 - c, 0)
    grid_spec = pltpu.PrefetchScalarGridSpec(
        num_scalar_prefetch=1, grid=(batch, n_ch), in_specs=in_specs,
        out_specs=[pl.BlockSpec((None, njc, gw), out_map), pl.BlockSpec((None, njc, gw), out_map),
                   pl.BlockSpec((None, 2 * KV_HEADS, CMP_HIDDEN), lambda b, c, tbl: (b, 0, 0))],
        scratch_shapes=[pltpu.VMEM((2 * KV_HEADS, CMP_HIDDEN), F32), pltpu.VMEM((4, PAGE, LANES), F32),
                        pltpu.VMEM((2 * KV_HEADS, njc, flat), F32)])
    return pl.pallas_call(
        functools.partial(_compress_kernel, n_pg=n_pg, zero_after=zero_after), grid_spec=grid_spec,
        out_shape=[jax.ShapeDtypeStruct((batch, n_ch * njc, gw), F32), jax.ShapeDtypeStruct((batch, n_ch * njc, gw), F32),
                   jax.ShapeDtypeStruct((batch, 2 * KV_HEADS, CMP_HIDDEN), F32)],
        compiler_params=_cparams(("parallel", "arbitrary")), name=name,
    )(jnp.zeros((1,), I32) if table is None else table, *([pages] * n_pg), cw["pos_lo"], cw["pos_hi"], cw["w1lo"], cw["w1hi"], cw["b1"], cw["w2"], cw["b2"], hi_init)


def _cmp_layout(kc_nat, nb, nbp):
    b = kc_nat.shape[0]
    x = kc_nat[:, :4 * nb].reshape(b, nb, 4, kc_nat.shape[-1]).transpose(0, 2, 1, 3)
    x = jnp.pad(x, ((0, 0), (0, 0), (0, nbp - nb), (0, 0)))
    return x.reshape(b, 4 * nbp, kc_nat.shape[-1]).astype(BF16)


SCALE = HEAD_DIM ** -0.5
QSCALE = SCALE * math.log2(math.e)
SUM_ROWS = 16
ACC_ROWS = HEAD_DIM + SUM_ROWS


def _group_queries(q, g):
    nq = q.shape[0]
    keep = (lax.broadcasted_iota(I32, (nq, LANES), 1) >> 6) == (g % 2)
    pieces = []
    for hh in range(HPG):
        h = HPG * g + hh
        chunk = q[:, (h // 2) * LANES:(h // 2 + 1) * LANES]
        if h % 2 != g % 2:
            chunk = pltpu.roll(chunk, HEAD_DIM, axis=1)
        pieces.append(jnp.where(keep, chunk, 0.0))
    return (jnp.concatenate(pieces, axis=0) * QSCALE).astype(BF16)


def _kchunk(k, g):
    return k[:, (g // 2) * LANES:(g // 2 + 1) * LANES]


def _values(vt, g):
    return jnp.concatenate([vt[g * HEAD_DIM:(g + 1) * HEAD_DIM, :], jnp.ones((SUM_ROWS, vt.shape[1]), BF16)], axis=0)


def _flash_groups(ss, vt, m_s, acc_s):
    ps, alphas = [], []
    for g in range(KV_HEADS):
        m_old = m_s[g]
        m_new = jnp.maximum(m_old, jnp.max(ss[g], axis=0, keepdims=True))
        alphas.append(jnp.exp2(m_old - m_new))
        ps.append(jnp.exp2(ss[g] - m_new).astype(BF16))
        m_s[g] = m_new
    for g in range(KV_HEADS):
        acc_s[g] = acc_s[g] * alphas[g] + _dot(_values(vt, g), ps[g])


def _one_shot_groups(ss, vt):
    ps = [jnp.exp2(s - jnp.max(s, axis=0, keepdims=True)).astype(BF16) for s in ss]
    return [_dot(_values(vt, g), ps[g]) for g in range(KV_HEADS)]


def _init_state(m_s, acc_s):
    m_s[...] = jnp.full(m_s.shape, NEG, F32)
    acc_s[...] = jnp.zeros(acc_s.shape, F32)


def _topk_bias(score, n_sel):
    n_iota = lax.broadcasted_iota(I32, score.shape, 0)

    def body(_, sc):
        mx = jnp.max(sc, axis=0, keepdims=True)
        idx = jnp.min(jnp.where(sc == mx, n_iota, score.shape[0]), axis=0, keepdims=True)
        return jnp.where(n_iota == idx, -jnp.inf, sc)

    left = lax.fori_loop(0, n_sel, body, score)
    return jnp.where(jnp.logical_and(left == -jnp.inf, score > -jnp.inf), 0.0, NEG)


def _cmp_valid(qpos_w, nb, nbp):
    n_w = lax.broadcasted_iota(I32, (nbp, qpos_w.shape[1]), 0)
    return jnp.concatenate(
        [jnp.logical_and((4 * n_w + c) * CMP_STRIDE + (CMP_BLOCK - 1) <= qpos_w, n_w < nb) for c in range(4)], axis=0)


def _cmp_branch(s, valid, vct_ref, g, qpos_w, qpos_q, nb, nbp, nq):
    w = s.shape[1]
    s = jnp.where(valid, s, NEG)
    e = jnp.exp2(s - jnp.max(s, axis=0, keepdims=True))
    some = (qpos_w >= CMP_BLOCK - 1).astype(F32)
    p = e * (some / jnp.maximum(jnp.sum(e, axis=0, keepdims=True), 1e-30))
    o_cmp = _dot(vct_ref[g * HEAD_DIM:(g + 1) * HEAD_DIM, :], p.astype(BF16))
    if nq == LANES:
        ps = ((p[:, 0:LANES] + p[:, LANES:2 * LANES]) + p[:, 2 * LANES:3 * LANES]) + p[:, 3 * LANES:4 * LANES]
    else:
        assert w == LANES
        ps = ((p + pltpu.roll(p, nq, axis=1)) + pltpu.roll(p, 2 * nq, axis=1)) + pltpu.roll(p, 3 * nq, axis=1)
    parts = [ps[c * nbp:(c + 1) * nbp] for c in range(4)]
    n_q = lax.broadcasted_iota(I32, (nbp, LANES), 0)
    prev = jnp.where(n_q >= 1, pltpu.roll(parts[3], 1, axis=0), 0.0)
    score = (((parts[0] + parts[1]) + parts[2]) + parts[3]) + prev
    cur = qpos_q >> 6
    forced = (n_q == 0) | (n_q == cur) | (n_q == cur - 1)
    score = jnp.where(forced, BIG, jnp.where(n_q * SEL_BLOCK <= qpos_q, score, -BIG))
    score = jnp.where(n_q < nb, score, -jnp.inf)
    return o_cmp, score


def _compressed_and_select(q, kc_ref, vct_ref, qpos_w, qpos_q, ocmp_s, score_s, bias_s, nb, nbp, nq, n_eff=None):
    w = HPG * nq
    n_eff = nbp if n_eff is None else n_eff
    kc_all = jnp.concatenate([kc_ref[c * nbp:c * nbp + n_eff, :] for c in range(4)], axis=0)
    vct_all = jnp.concatenate([vct_ref[:, c * nbp:c * nbp + n_eff] for c in range(4)], axis=1)
    cs = [_nt(_kchunk(kc_all, g), _group_queries(q, g)) for g in range(KV_HEADS)]
    valid = _cmp_valid(qpos_w, nb, n_eff)
    for g in range(KV_HEADS):
        o_cmp, score = _cmp_branch(cs[g], valid, vct_all, g, qpos_w, qpos_q, nb, n_eff, nq)
        ocmp_s[g] = o_cmp
        score_s[0:n_eff, g * LANES:(g + 1) * LANES] = score
    bias = _topk_bias(score_s[0:n_eff, :], min(N_SELECT, nb))
    for g in range(KV_HEADS):
        b = bias[:, g * LANES:(g + 1) * LANES]
        bias_s[g, 0:n_eff, :] = b if w == LANES else jnp.concatenate([b] * (w // LANES), axis=1)
        if n_eff < nbp:
            bias_s[g, n_eff:nbp, :] = jnp.full((nbp - n_eff, w), NEG, F32)


def _block_bias(bias_rows, n_blk):
    w = bias_rows.shape[1]
    return jnp.concatenate([jnp.broadcast_to(bias_rows[u:u + 1, :], (SEL_BLOCK, w)) for u in range(n_blk)], axis=0)


def _gate_row(gates_t, c, g, nq, w):
    rows = [gates_t[(c * HPG + hh) * KV_HEADS + g:(c * HPG + hh) * KV_HEADS + g + 1, :] for hh in range(HPG)]
    if nq == LANES:
        return jnp.concatenate(rows, axis=1)
    strip = lax.broadcasted_iota(I32, (1, LANES), 1) >> (nq.bit_length() - 1)
    out = jnp.zeros((1, LANES), F32)
    for hh in range(HPG):
        out = jnp.where(strip == hh, rows[hh] if hh == 0 else pltpu.roll(rows[hh], hh * nq, axis=1), out)
    return out


def _finish(o_ref, acc_s, ocmp_s, wins, ot_s, gates_t, nq, nq_real, w):
    for g in range(KV_HEADS):
        o = ocmp_s[g] * _gate_row(gates_t, 0, g, nq, w)
        for c, acc in ((1, acc_s[g]), (2, wins[g])):
            o = o + acc[:HEAD_DIM] * (_gate_row(gates_t, c, g, nq, w) / acc[HEAD_DIM:HEAD_DIM + 1])
        ot_s[g * HEAD_DIM:(g + 1) * HEAD_DIM, :] = o
    o_t = ot_s[...].T
    for hh in range(HPG):
        o_ref[:, hh * KV_HEADS * HEAD_DIM:(hh + 1) * KV_HEADS * HEAD_DIM] = o_t[hh * nq:hh * nq + nq_real, :]


def _attn_prompt_kernel(q_ref, gp_ref, bg_ref, cos_ref, sin_ref, kc_ref, vct_ref, ks_ref, vst_ref, kw_ref, vwt_ref,
                        o_ref, m_s, acc_s, score_s, bias_s, ocmp_s, ot_s, qzr_s, *, nb, nbp):
    i = pl.program_id(1)
    nq = Q_TILE
    w = HPG * nq
    s0 = i * nq
    q = q_ref[...]
    q_rot = _rope(q, cos_ref[...], sin_ref[...])
    gates_t = jax.nn.sigmoid(gp_ref[...] + bg_ref[...]).T
    qpos_w = s0 + (lax.broadcasted_iota(I32, (1, w), 1) & (nq - 1))
    qpos_q = s0 + lax.broadcasted_iota(I32, (1, LANES), 1)
    _init_state(m_s, acc_s)
    for g in range(KV_HEADS):
        qzr_s[g] = _group_queries(q_rot, g)
    n_cls = 4 if nbp % (4 * SUBLANES) == 0 and nbp // 4 >= N_SELECT else 1
    per_cls = nbp // n_cls
    cls = jnp.minimum((2 * (i + 1) + per_cls - 1) // per_cls, n_cls) - 1
    for k in range(n_cls):
        @pl.when(cls == k)
        def _(k=k):
            _compressed_and_select(q, kc_ref, vct_ref, qpos_w, qpos_q, ocmp_s, score_s, bias_s, nb, nbp, nq,
                                   n_eff=(k + 1) * per_cls)

    blk_per_tile = KEY_TILE // SEL_BLOCK

    def slc_tile(t, causal):
        k0 = pl.multiple_of(t * KEY_TILE, KEY_TILE)
        kt = ks_ref[pl.ds(k0, KEY_TILE), :]
        ss = []
        for g in range(KV_HEADS):
            rows = bias_s[g, pl.ds(pl.multiple_of(t * blk_per_tile, blk_per_tile), blk_per_tile), :]
            s = _nt(_kchunk(kt, g), qzr_s[g]) + _block_bias(rows, blk_per_tile)
            if causal:
                s = jnp.where(k0 + lax.broadcasted_iota(I32, (KEY_TILE, w), 0) <= qpos_w, s, NEG)
            ss.append(s)
        _flash_groups(ss, vst_ref[t], m_s, acc_s)

    t_diag = s0 // KEY_TILE

    def full_tile(t, carry):
        slc_tile(t, False)
        return carry

    lax.fori_loop(0, t_diag, full_tile, 0)
    slc_tile(t_diag, True)

    n_wt = (WINDOW + nq) // PAGE
    t0 = jnp.maximum(i - WINDOW // PAGE, 0)
    k0 = pl.multiple_of(t0 * PAGE, PAGE)
    kwin = kw_ref[pl.ds(k0, n_wt * PAGE), :]
    vwin_tiles = vwt_ref[pl.ds(t0, n_wt)]
    vwin = jnp.concatenate([vwin_tiles[u] for u in range(n_wt)], axis=1)
    kpos = k0 + lax.broadcasted_iota(I32, (n_wt * PAGE, w), 0)
    wmask = jnp.logical_and(kpos <= qpos_w, qpos_w - kpos < WINDOW)
    ws = [jnp.where(wmask, _nt(_kchunk(kwin, g), qzr_s[g]), NEG) for g in range(KV_HEADS)]
    _finish(o_ref, acc_s, ocmp_s, _one_shot_groups(ws, vwin), ot_s, gates_t, nq, nq, w)


def _attn_prompt(q, gate_pre, b_gate, cos, sin, kc, vct, ks, vst, kw, vwt, batch, seq, name):
    assert seq % KEY_TILE == 0 and seq >= WINDOW + Q_TILE
    m, d = q.shape
    nq = Q_TILE
    w = HPG * nq
    nqb = seq // nq
    nb = seq // SEL_BLOCK
    nbp = kc.shape[1] // 4
    gw = KV_HEADS * HEAD_DIM
    ntile = seq // KEY_TILE
    row_map = lambda b, i: (b * nqb + i, 0)
    per_b2 = lambda b, i: (b, 0)
    per_b3 = lambda b, i: (b, 0, 0)
    return pl.pallas_call(
        functools.partial(_attn_prompt_kernel, nb=nb, nbp=nbp), grid=(batch, nqb),
        in_specs=[pl.BlockSpec((nq, d), row_map), pl.BlockSpec((nq, LANES), row_map),
                  pl.BlockSpec((1, LANES), lambda b, i: (0, 0)),
                  pl.BlockSpec((nq, LANES), lambda b, i: (i, 0)), pl.BlockSpec((nq, LANES), lambda b, i: (i, 0)),
                  pl.BlockSpec((None, 4 * nbp, gw), per_b3), pl.BlockSpec((None, gw, 4 * nbp), per_b3),
                  pl.BlockSpec((seq, gw), per_b2), pl.BlockSpec((ntile, gw, KEY_TILE), per_b3),
                  pl.BlockSpec((seq, gw), per_b2), pl.BlockSpec((seq // PAGE, gw, PAGE), per_b3)],
        out_specs=pl.BlockSpec((nq, d), row_map),
        out_shape=jax.ShapeDtypeStruct((m, d), F32),
        scratch_shapes=[pltpu.VMEM((KV_HEADS, 1, w), F32), pltpu.VMEM((KV_HEADS, ACC_ROWS, w), F32),
                        pltpu.VMEM((nbp, KV_HEADS * LANES), F32), pltpu.VMEM((KV_HEADS, nbp, w), F32),
                        pltpu.VMEM((KV_HEADS, HEAD_DIM, w), F32), pltpu.VMEM((gw, w), F32),
                        pltpu.VMEM((KV_HEADS, w, LANES), BF16)],
        compiler_params=_cparams(("parallel", "arbitrary")), name=name,
    )(q, gate_pre, b_gate, cos, sin, kc, vct, ks, vst, kw, vwt)


def _pad_rows(x, n):
    return jnp.concatenate([x, jnp.zeros((n - x.shape[0], x.shape[1]), x.dtype)], axis=0)


def _attn_sample_kernel(tbl_ref, *refs, n_pg, n_steps, nb, nbp, pos0, n_new):
    k_refs = refs[:n_pg]
    vt_refs = refs[n_pg:2 * n_pg]
    (q_ref, gp_ref, bg_ref, cos_ref, sin_ref, kc_ref, vct_ref, kvn_ref, cwin_ref, wnew_ref,
     o_ref, m_s, acc_s, score_s, bias_s, ocmp_s, ot_s, qzr_s, gt_s) = refs[2 * n_pg:]
    step = pl.program_id(1)
    nq = SAMPLE_NQ
    w = HPG * nq
    gw = KV_HEADS * HEAD_DIM
    qpos_w = pos0 + (lax.broadcasted_iota(I32, (1, w), 1) & (nq - 1))

    @pl.when(step == 0)
    def _():
        q = _pad_rows(q_ref[...], nq)
        q_rot = _pad_rows(_rope(q_ref[...], cos_ref[...], sin_ref[...]), nq)
        gt_s[...] = _pad_rows(jax.nn.sigmoid(gp_ref[...] + bg_ref[...]), LANES).T
        _init_state(m_s, acc_s)
        for g in range(KV_HEADS):
            qzr_s[g] = _group_queries(q_rot, g)
        _compressed_and_select(q, kc_ref, vct_ref, qpos_w, qpos_w, ocmp_s, score_s, bias_s, nb, nbp, nq)

    blk_pp = PAGE // SEL_BLOCK
    blk_ps = n_pg * blk_pp
    kcs = [jnp.concatenate([r[cp * LANES:(cp + 1) * LANES, :].T for r in k_refs], axis=0).astype(BF16)
           for cp in range(KV_HEADS // 2)]
    vt = jnp.concatenate([r[...] for r in vt_refs], axis=1).astype(BF16)
    ss = []
    for g in range(KV_HEADS):
        rows = bias_s[g, pl.ds(pl.multiple_of(step * blk_ps, blk_ps), blk_ps), :]
        ss.append(_nt(kcs[g // 2], qzr_s[g]) + _block_bias(rows, blk_ps))
    _flash_groups(ss, vt, m_s, acc_s)

    @pl.when(step == n_steps - 1)
    def _():
        krow = lax.broadcasted_iota(I32, (PAGE, w), 0)
        kvn = _pad_rows(kvn_ref[...], PAGE)
        kn = kvn[:, 2 * gw:3 * gw].astype(BF16)
        vnt = kvn[:, 3 * gw:4 * gw].T.astype(BF16)
        nb0 = pos0 // SEL_BLOCK
        causal = pos0 + krow <= qpos_w
        ts = [jnp.where(causal, _nt(_kchunk(kn, g), qzr_s[g]) + _block_bias(bias_s[g, nb0:nb0 + blk_pp, :], blk_pp), NEG)
              for g in range(KV_HEADS)]
        _flash_groups(ts, vnt, m_s, acc_s)

        n_cached = cwin_ref.shape[2]
        wn = _pad_rows(wnew_ref[...], PAGE)
        kws = [jnp.concatenate([wn[:, cp * LANES:(cp + 1) * LANES], cwin_ref[0, cp * LANES:(cp + 1) * LANES, :].T],
                               axis=0).astype(BF16) for cp in range(KV_HEADS // 2)]
        vwin = jnp.concatenate([wn[:, gw:2 * gw].T, cwin_ref[1]], axis=1).astype(BF16)
        kpos = jnp.concatenate([pos0 + krow, pos0 - n_cached + lax.broadcasted_iota(I32, (n_cached, w), 0)], axis=0)
        wmask = jnp.logical_and(jnp.logical_and(kpos <= qpos_w, qpos_w - kpos < WINDOW), kpos >= 0)
        ws = [jnp.where(wmask, _nt(kws[g // 2], qzr_s[g]), NEG) for g in range(KV_HEADS)]
        _finish(o_ref, acc_s, ocmp_s, _one_shot_groups(ws, vwin), ot_s, gt_s[...], nq, n_new, w)


def _attn_sample(q, gate_pre, b_gate, cos, sin, kc, vct, pages, table, kv_new, cache_win_t, win_new,
                 batch, n_new, n_pages, name):
    m, d = q.shape
    nq = SAMPLE_NQ
    w = HPG * nq
    assert w == LANES and n_new <= nq and n_new % SUBLANES == 0 and n_new <= SEL_BLOCK
    n_pg = ATT_PAGES
    assert n_pages % n_pg == 0
    n_steps = n_pages // n_pg
    pos0 = n_pages * PAGE
    nb = (pos0 + n_new + SEL_BLOCK - 1) // SEL_BLOCK
    nbp = kc.shape[1] // 4
    gw = KV_HEADS * HEAD_DIM
    n_cached = cache_win_t.shape[3]
    assert n_cached % PAGE == 0

    def page_map(u, branch):
        return lambda b, s, tbl: (tbl[b * n_pages + s * n_pg + u], branch, 0, 0)

    row_map = lambda b, s, tbl: (b, 0)
    per_b3 = lambda b, s, tbl: (b, 0, 0)
    const2 = lambda b, s, tbl: (0, 0)
    in_specs = [pl.BlockSpec((None, None, gw, PAGE), page_map(u, 2)) for u in range(n_pg)]
    in_specs += [pl.BlockSpec((None, None, gw, PAGE), page_map(u, 3)) for u in range(n_pg)]
    in_specs += [pl.BlockSpec((n_new, d), row_map), pl.BlockSpec((n_new, LANES), row_map),
                 pl.BlockSpec((1, LANES), const2), pl.BlockSpec((n_new, LANES), const2),
                 pl.BlockSpec((n_new, LANES), const2),
                 pl.BlockSpec((None, 4 * nbp, gw), per_b3), pl.BlockSpec((None, gw, 4 * nbp), per_b3),
                 pl.BlockSpec((n_new, 4 * gw), row_map),
                 pl.BlockSpec((None, 2, gw, n_cached), lambda b, s, tbl: (b, 0, 0, 0)),
                 pl.BlockSpec((n_new, 2 * gw), row_map)]
    grid_spec = pltpu.PrefetchScalarGridSpec(
        num_scalar_prefetch=1, grid=(batch, n_steps), in_specs=in_specs,
        out_specs=pl.BlockSpec((n_new, d), row_map),
        scratch_shapes=[pltpu.VMEM((KV_HEADS, 1, w), F32), pltpu.VMEM((KV_HEADS, ACC_ROWS, w), F32),
                        pltpu.VMEM((nbp, KV_HEADS * LANES), F32), pltpu.VMEM((KV_HEADS, nbp, w), F32),
                        pltpu.VMEM((KV_HEADS, HEAD_DIM, w), F32), pltpu.VMEM((gw, w), F32),
                        pltpu.VMEM((KV_HEADS, w, LANES), BF16), pltpu.VMEM((LANES, LANES), F32)])
    return pl.pallas_call(
        functools.partial(_attn_sample_kernel, n_pg=n_pg, n_steps=n_steps, nb=nb, nbp=nbp, pos0=pos0, n_new=n_new),
        grid_spec=grid_spec, out_shape=jax.ShapeDtypeStruct((m, d), F32),
        compiler_params=_cparams(("parallel", "arbitrary")), name=name,
    )(table, *([pages] * 2 * n_pg), q, gate_pre, b_gate, cos, sin, kc, vct, kv_new, cache_win_t, win_new)


def _prep_weights(w_a_in, b_a_gate, w_a_out, w_kv, cmp_pos, cmp_w1, cmp_b1, cmp_w2, cmp_b2,
                  w_b_in, b_b_gate, w_b_out, w_ffn_up, w_ffn_down):
    a_q = M_HEADS * M_QK_DIM
    a_v = M_HEADS * M_V_DIM
    n_g = 2 * M_HEADS
    g0 = 2 * a_q + a_v
    gw = KV_HEADS * HEAD_DIM
    qd = N_HEADS * HEAD_DIM
    p = {}
    p["a_main"] = jnp.concatenate([w_a_in[:, :, :g0], w_a_in[:, :, g0 + n_g:]], axis=-1).astype(BF16)
    p["a_qvo"] = jnp.concatenate([w_a_in[:, :, :a_q], w_a_in[:, :, 2 * a_q:g0], w_a_in[:, :, g0 + n_g:]],
                                 axis=-1).astype(BF16)
    p["a_kt"] = w_a_in[:, :, a_q:2 * a_q].transpose(0, 2, 1).astype(BF16)
    p["a_gate"] = jnp.pad(w_a_in[:, :, g0:g0 + n_g], ((0, 0), (0, 0), (0, LANES - n_g))).astype(BF16)
    p["a_bgate"] = jnp.pad(b_a_gate, ((0, 0), (0, LANES - n_g)))[:, None, :]
    p["a_out"] = w_a_out.astype(BF16)
    p["kv"] = w_kv.astype(BF16)
    p["kv_t"] = w_kv.T.astype(BF16)
    p["kv_k"] = jnp.concatenate([w_kv[:, 2 * gw:3 * gw], w_kv[:, 4 * gw:5 * gw]], axis=1).astype(BF16)
    hh, g, c = jnp.meshgrid(jnp.arange(HPG), jnp.arange(KV_HEADS), jnp.arange(3), indexing="ij")
    old_col = ((HPG * g + hh) * 3 + c)
    new_col = ((c * HPG + hh) * KV_HEADS + g)
    order = jnp.zeros((3 * N_HEADS,), I32).at[new_col.reshape(-1)].set(old_col.reshape(-1))
    p["b_q"] = w_b_in[:, :, :qd].astype(BF16)
    p["b_gate"] = jnp.pad(w_b_in[:, :, qd:][:, :, order], ((0, 0), (0, 0), (0, LANES - 3 * N_HEADS))).astype(BF16)
    p["b_bgate"] = jnp.pad(b_b_gate[:, order], ((0, 0), (0, LANES - 3 * N_HEADS)))[:, None, :]
    wo = w_b_out.reshape(w_b_out.shape[0], KV_HEADS, HPG, HEAD_DIM, D_MODEL).transpose(0, 2, 1, 3, 4)
    p["b_out"] = wo.reshape(w_b_out.shape[0], qd, D_MODEL).astype(BF16)
    p["up"] = w_ffn_up.astype(BF16)
    p["down"] = w_ffn_down.astype(BF16)
    flat = CMP_STRIDE * HEAD_DIM
    p["cmp"] = {
        "pos_lo": cmp_pos[:, :CMP_STRIDE].reshape(2, 1, flat), "pos_hi": cmp_pos[:, CMP_STRIDE:].reshape(2, 1, flat),
        "w1lo": cmp_w1[:, :flat].astype(BF16), "w1hi": cmp_w1[:, flat:].astype(BF16),
        "b1": cmp_b1[:, None, :], "w2": cmp_w2.astype(BF16), "b2": cmp_b2[:, None, :]}
    return p


def _trunk(x3, pos0, past, p, g_norms, g_a_hnorm, g_kv, ffn_conv_w, ffn_conv_b, conv_prev, m_c, m_n, m_m, tag):
    batch, seq, d = x3.shape
    x = x3.reshape(batch * seq, d)
    cs, ns, ms, convs = [], [], [], []
    gw = KV_HEADS * HEAD_DIM
    for layer in range(DEPTH):
        g = g_norms[layer]
        nm = f"{tag}{layer}"
        if layer < N_A_LAYERS:
            if seq % MLSTM_CHUNK == 0:
                main, gate, kt = _norm_proj(x, g[0], [p["a_qvo"][layer], p["a_gate"][layer]], nm + "_in",
                                            wts=[p["a_kt"][layer]])
            else:
                main, gate = _norm_proj(x, g[0], [p["a_main"][layer], p["a_gate"][layer]], nm + "_in")
                kt = None
            h, c_new, n_new, m_new = _mlstm(main, gate, kt, p["a_bgate"][layer], g_a_hnorm[layer],
                                            m_c[layer], m_n[layer], m_m[layer], batch, seq, nm + "_mlstm")
            cs.append(c_new)
            ns.append(n_new)
            ms.append(m_new)
            x = _proj_norm_res(h, p["a_out"][layer], g[1], x, nm + "_out")
        else:
            if layer == N_A_LAYERS:
                cos, sin = _rope_tables(pos0, seq)
                zeros_hi = jnp.zeros((batch, 2 * KV_HEADS, CMP_HIDDEN), F32)
                if past is None:
                    kvt, wint, ks_b, kw_b, vst, vwt = _kv_rows_minor(x, g_kv, p["kv_t"], p["kv_k"], cos, sin,
                                                                     batch, seq, tag + "_kv")
                    kv_out = kvt.reshape(batch, 4, KV_HEADS, HEAD_DIM, seq).transpose(0, 4, 1, 2, 3)
                    win_out = wint.reshape(batch, 2, KV_HEADS, HEAD_DIM, seq).transpose(0, 4, 1, 2, 3)
                    n_pages = seq // PAGE
                    nb = seq // SEL_BLOCK
                    kc, vc, _ = _compress(kvt.reshape(batch, 4, gw, seq), None, batch, n_pages,
                                          min(CMP_PAGES, n_pages), zeros_hi, p["cmp"], None, tag + "_cmp")
                else:
                    kv4, win = _kv_rows(x, g_kv, p["kv"], jnp.tile(cos, (batch, 1)), jnp.tile(sin, (batch, 1)),
                                        tag + "_kv")
                    kv_out = kv4.reshape(batch, seq, 4, KV_HEADS, HEAD_DIM)
                    win_out = win.reshape(batch, seq, 2, KV_HEADS, HEAD_DIM)
                    pages, table, n_pages, cache_win_t = past
                    nb = (pos0 + seq + SEL_BLOCK - 1) // SEL_BLOCK
                    n_tail = nb * (SEL_BLOCK // CMP_STRIDE) - n_pages * (PAGE // CMP_STRIDE)
                    tail = jnp.pad(kv4[:, :2 * gw].reshape(batch, seq, 2, KV_HEADS, HEAD_DIM),
                                   ((0, 0), (0, PAGE - seq), (0, 0), (0, 0), (0, 0)))
                    kc_t, vc_t, hi_t = _compress(_token_minor(tail), jnp.arange(batch, dtype=I32), batch, 1, 1, zeros_hi,
                                                 p["cmp"], n_tail - 1, tag + "_cmpt")
                    kc_m, vc_m, _ = _compress(pages, table, batch, n_pages, min(CMP_PAGES, n_pages), hi_t,
                                              p["cmp"], None, tag + "_cmp")
                    kc = jnp.concatenate([kc_m, kc_t[:, :n_tail]], axis=1)
                    vc = jnp.concatenate([vc_m, vc_t[:, :n_tail]], axis=1)
                nbp = -(-nb // 32) * 32
                kc_l = _cmp_layout(kc, nb, nbp)
                vct_l = _cmp_layout(vc, nb, nbp).transpose(0, 2, 1)
            bl = layer - N_A_LAYERS
            q_raw, gate_pre = _norm_proj(x, g[0], [p["b_q"][bl], p["b_gate"][bl]], nm + "_in")
            if past is None:
                o = _attn_prompt(q_raw, gate_pre, p["b_bgate"][bl], cos, sin, kc_l, vct_l, ks_b, vst, kw_b, vwt,
                                 batch, seq, nm + "_attn")
            else:
                o = _attn_sample(q_raw, gate_pre, p["b_bgate"][bl], cos, sin, kc_l, vct_l, pages, table,
                                 kv4, cache_win_t, win, batch, seq, n_pages, nm + "_attn")
            x = _proj_norm_res(o, p["b_out"][bl], g[1], x, nm + "_out")
        x, conv_new = _ffn(x, g[2], p["up"][layer], ffn_conv_w[layer], ffn_conv_b[layer], p["down"][layer], g[3],
                           conv_prev[layer], batch, seq, nm + "_ffn")
        convs.append(conv_new)
    return (x.reshape(batch, seq, d), kv_out, win_out, jnp.stack(cs), jnp.stack(ns), jnp.stack(ms), jnp.stack(convs))


def _past_views(cache_kv, cache_win_kv, page_table):
    return (_token_minor(cache_kv), page_table.reshape(-1), page_table.shape[1], _token_minor(cache_win_kv))


def kernel(x_prompt, x_sample, cache_kv, cache_win_kv, state_mlstm_C, state_mlstm_n, state_mlstm_m, state_conv,
           page_table, g_norms, w_a_in, b_a_gate, g_a_hnorm, w_a_out, g_kv, w_kv, cmp_pos, cmp_w1, cmp_b1, cmp_w2,
           cmp_b2, w_b_in, b_b_gate, w_b_out, w_ffn_up, ffn_conv_w, ffn_conv_b, w_ffn_down):
    p = _prep_weights(w_a_in, b_a_gate, w_a_out, w_kv, cmp_pos, cmp_w1, cmp_b1, cmp_w2, cmp_b2,
                      w_b_in, b_b_gate, w_b_out, w_ffn_up, w_ffn_down)
    dt = x_prompt.dtype
    bp, tp, _ = x_prompt.shape
    bs, ts, _ = x_sample.shape
    past_len = page_table.shape[1] * PAGE
    gw = KV_HEADS * HEAD_DIM
    shared = (p, g_norms, g_a_hnorm, g_kv, ffn_conv_w, ffn_conv_b)

    y_p, kv_p, win_p, c_p, n_p, m_p, conv_p = _trunk(
        x_prompt, 0, None, *shared,
        jnp.zeros((DEPTH, bp, CONV_W - 1, 2 * D_FF), dt),
        jnp.zeros((N_A_LAYERS, bp, M_HEADS, M_QK_DIM, M_V_DIM), dt),
        jnp.zeros((N_A_LAYERS, bp, M_HEADS, M_QK_DIM), dt), jnp.zeros((N_A_LAYERS, bp, M_HEADS), dt), "p")
    win_p = win_p[:, tp - min(WINDOW, tp):]

    past = _past_views(cache_kv, cache_win_kv, page_table)
    y_s, kv_s, win_s, c_s, n_s, m_s, conv_s = _trunk(
        x_sample, past_len, past, *shared, state_conv, state_mlstm_C, state_mlstm_n, state_mlstm_m, "s")
    win_all = jnp.concatenate([cache_win_kv, win_s], axis=1)
    win_s = win_all[:, win_all.shape[1] - min(WINDOW, win_all.shape[1]):]
    return (y_p, y_s, kv_p, kv_s, win_p, win_s, c_p, c_s, n_p, n_s, m_p, m_s, conv_p, conv_s)
```

```python
import functools
import math

import jax
import jax.numpy as jnp
from jax import lax
from jax.experimental import pallas as pl
from jax.experimental.pallas import tpu as pltpu

F32 = jnp.float32
BF16 = jnp.bfloat16
I32 = jnp.int32

D_MODEL = 1024
DEPTH = 4
N_A_LAYERS = 2
M_HEADS = 8
M_QK_DIM = 64
M_V_DIM = 128
GATE_CAP = 15.0
N_HEADS = 16
HEAD_DIM = 64
KV_HEADS = 4
HPG = 4
CMP_BLOCK = 32
CMP_STRIDE = 16
CMP_HIDDEN = 256
SEL_BLOCK = 64
N_SELECT = 16
WINDOW = 512
ROT_DIM = 16
ROPE_THETA = 500000.0
D_FF = 2816
CONV_W = 3
EPS = 1e-6
BIG = 1e9
NEG = -1e30
PAGE = 128

LANES = 128
SUBLANES = 8
VMEM_LIMIT = 56 * 1024 * 1024

MLSTM_CHUNK = 128
KEY_TILE = 512
Q_TILE = 128
CMP_PAGES = 16
ATT_PAGES = 16
FFN_TM = 512
FFN_TN = 1408


def _cparams(sem):
    return pltpu.CompilerParams(dimension_semantics=sem, vmem_limit_bytes=VMEM_LIMIT)


def _rms(x, g):
    return x * lax.rsqrt(jnp.mean(x * x, axis=-1, keepdims=True) + EPS) * g


def _nt(a, b):
    return lax.dot_general(a, b, (((1,), (1,)), ((), ())), preferred_element_type=F32)


def _dot(a, b):
    return jnp.dot(a, b, preferred_element_type=F32)


def _row_tile(m, pref):
    t = min(m, pref)
    while m % t:
        t //= 2
    return t


def _norm_proj_kernel(x_ref, g_ref, *refs, n_w, n_t):
    xn = _rms(x_ref[...], g_ref[...]).astype(BF16)
    n_in = n_w + n_t
    for w_ref, o_ref in zip(refs[:n_w], refs[n_in:n_in + n_w]):
        o_ref[...] = _dot(xn, w_ref[...]).astype(o_ref.dtype)
    for w_ref, o_ref in zip(refs[n_w:n_in], refs[n_in + n_w:]):
        yt = _nt(w_ref[...], xn)
        for u in range(o_ref.shape[0]):
            o_ref[u] = yt[:, u * LANES:(u + 1) * LANES]


def _norm_proj(x, g, ws, name, wts=()):
    m, d = x.shape
    tm = _row_tile(m, 512)
    assert not wts or tm % LANES == 0
    n_w, n_t = len(ws), len(wts)
    in_specs = [pl.BlockSpec((tm, d), lambda i: (i, 0)), pl.BlockSpec((1, d), lambda i: (0, 0))]
    in_specs += [pl.BlockSpec(w.shape, lambda i: (0, 0)) for w in (*ws, *wts)]
    out_specs = [pl.BlockSpec((tm, w.shape[1]), lambda i: (i, 0)) for w in ws]
    out_specs += [pl.BlockSpec((tm // LANES, w.shape[0], LANES), lambda i: (i, 0, 0)) for w in wts]
    out_shape = [jax.ShapeDtypeStruct((m, w.shape[1]), F32) for w in ws]
    out_shape += [jax.ShapeDtypeStruct((m // LANES, w.shape[0], LANES), F32) for w in wts]
    return pl.pallas_call(
        functools.partial(_norm_proj_kernel, n_w=n_w, n_t=n_t), grid=(m // tm,), in_specs=in_specs,
        out_specs=out_specs, out_shape=out_shape, compiler_params=_cparams(("parallel",)),
        name=name)(x, g.reshape(1, d), *ws, *wts)


def _proj_norm_res_kernel(a_ref, w_ref, g_ref, res_ref, o_ref):
    y = _dot(a_ref[...].astype(BF16), w_ref[...])
    o_ref[...] = res_ref[...] + _rms(y, g_ref[...])


def _proj_norm_res(a, w, g, res, name):
    m, k = a.shape
    d = w.shape[1]
    tm = _row_tile(m, 512)
    return pl.pallas_call(
        _proj_norm_res_kernel, grid=(m // tm,),
        in_specs=[pl.BlockSpec((tm, k), lambda i: (i, 0)), pl.BlockSpec((k, d), lambda i: (0, 0)),
                  pl.BlockSpec((1, d), lambda i: (0, 0)), pl.BlockSpec((tm, d), lambda i: (i, 0))],
        out_specs=pl.BlockSpec((tm, d), lambda i: (i, 0)),
        out_shape=jax.ShapeDtypeStruct((m, d), F32),
        compiler_params=_cparams(("parallel",)), name=name)(a, w, g.reshape(1, d), res)


def _mlstm_kernel(main_ref, gate_ref, *refs, rows, nc, k_minor):
    if k_minor:
        kt_ref, *refs = refs
    bg_ref, gh_ref, c0_ref, n0_ref, m0_ref, h_ref, c_ref, n_ref, m_ref, c_s, n_s, m_s = refs
    L = MLSTM_CHUNK
    nh = M_HEADS
    a_q = nh * M_QK_DIM
    v0 = a_q if k_minor else 2 * a_q
    o0 = v0 + nh * M_V_DIM
    cidx = pl.program_id(1)

    @pl.when(cidx == 0)
    def _():
        c_s[...] = c0_ref[...]
        n_s[...] = n0_ref[...]
        m_s[...] = m0_ref[...]

    main = main_ref[...]
    gp = gate_ref[...] + bg_ref[...]
    if rows < L:
        main = jnp.concatenate([main, jnp.zeros((L - rows, main.shape[1]), F32)], axis=0)
        gp = jnp.concatenate([gp, jnp.zeros((L - rows, LANES), F32)], axis=0)
    capped = GATE_CAP * jnp.tanh(gp / GATE_CAP)
    row1 = lax.broadcasted_iota(I32, (L, LANES), 0)
    real = row1 < rows
    ilog = jnp.where(real, capped, -jnp.inf)
    logf = jnp.where(real, jnp.minimum(capped, 0.0) - jnp.log1p(jnp.exp(-jnp.abs(capped))), 0.0)
    bh = logf
    k = 1
    while k < L:
        bh = bh + jnp.where(row1 >= k, pltpu.roll(bh, k, axis=0), 0.0)
        k *= 2
    bh = pltpu.roll(bh, LANES - nh, axis=1)
    c_all = ilog - bh
    cm = c_all
    k = 1
    while k < L:
        cm = jnp.maximum(cm, jnp.where(row1 >= k, pltpu.roll(cm, k, axis=0), -jnp.inf))
        k *= 2
    m_row = m_s[...]
    mt = bh + jnp.maximum(m_row, cm)
    m_new = mt[L - 1:L, :]
    b_last = bh[L - 1:L, :]
    w_inter = jnp.exp(bh + m_row - mt)
    u_all = bh - mt
    emt = jnp.exp(-mt)
    ws_all = jnp.exp(b_last - bh + ilog - m_new)
    decay = jnp.exp(b_last + m_row - m_new)
    c_t = c_all.T
    ws_t = ws_all.T

    rr = lax.broadcasted_iota(I32, (L, L), 0)
    cc = lax.broadcasted_iota(I32, (L, L), 1)
    causal = cc <= rr
    lo_half = lax.broadcasted_iota(I32, (L, LANES), 1) < M_QK_DIM
    lo_rows = lax.broadcasted_iota(I32, (2 * M_QK_DIM, LANES), 0) < M_QK_DIM
    ones_b = jnp.ones((L, LANES), BF16)
    assert L == LANES

    def lanes(a, x):
        return jnp.broadcast_to(a[:, x:x + 1], (L, LANES))

    for p in range(nh // 2):
        qp = main[:, p * LANES:(p + 1) * LANES] * (M_QK_DIM ** -0.5)
        if k_minor:
            kt = kt_ref[0, p * LANES:(p + 1) * LANES, :]
        else:
            kt = main[:, a_q + p * LANES:a_q + (p + 1) * LANES].T
        c_pair = c_s[2 * p:2 * p + 2].reshape(2 * M_QK_DIM, M_V_DIM)
        n_pair = n_s[p]
        kt_b = kt.astype(BF16)
        state_b = jnp.concatenate([c_pair, n_pair], axis=1).astype(BF16)
        for e in range(2):
            x = 2 * p + e
            qx_b = jnp.where(lo_half if e == 0 else jnp.logical_not(lo_half), qp, 0.0).astype(BF16)
            vx = main[:, v0 + x * M_V_DIM:v0 + (x + 1) * M_V_DIM]
            ox = main[:, o0 + x * M_V_DIM:o0 + (x + 1) * M_V_DIM]
            a = jnp.exp(jnp.where(causal, lanes(u_all, x) + c_t[x:x + 1, :], -jnp.inf)) * _dot(qx_b, kt_b)
            inter = _dot(qx_b, state_b)
            intra = _dot(a.astype(BF16), jnp.concatenate([vx.astype(BF16), ones_b], axis=1))
            wi = lanes(w_inter, x)
            num = wi * inter[:, :M_V_DIM] + intra[:, :M_V_DIM]
            den = wi * inter[:, M_V_DIM:] + intra[:, M_V_DIM:]
            h = num / jnp.maximum(jnp.abs(den), lanes(emt, x))
            hn = h * lax.rsqrt(jnp.mean(h * h, axis=-1, keepdims=True) + EPS)
            hn = hn * gh_ref[:, x * M_V_DIM:(x + 1) * M_V_DIM] * jax.nn.sigmoid(ox)
            h_ref[:, x * M_V_DIM:(x + 1) * M_V_DIM] = hn[:rows]
        kwt = kt * jnp.where(lo_rows, ws_t[2 * p:2 * p + 1, :], ws_t[2 * p + 1:2 * p + 2, :])
        vcat = main[:, v0 + 2 * p * M_V_DIM:v0 + (2 * p + 2) * M_V_DIM]
        upd = _dot(kwt.astype(BF16), jnp.concatenate([vcat.astype(BF16), ones_b], axis=1))
        dec_e = decay[:, 2 * p:2 * p + 1]
        dec_o = decay[:, 2 * p + 1:2 * p + 2]
        c_s[2 * p] = dec_e * c_pair[:M_QK_DIM] + upd[:M_QK_DIM, :M_V_DIM]
        c_s[2 * p + 1] = dec_o * c_pair[M_QK_DIM:] + upd[M_QK_DIM:, M_V_DIM:2 * M_V_DIM]
        n_s[p] = jnp.where(lo_rows, dec_e, dec_o) * n_pair + upd[:, 2 * M_V_DIM:]
    m_s[...] = m_new

    @pl.when(cidx == nc - 1)
    def _():
        c_ref[...] = c_s[...]
        n_ref[...] = n_s[...]
        m_ref[...] = m_s[...]


def _mlstm(main, gate, kt, b_gate, g_hnorm, c0, n0, m0, batch, seq, name):
    rows = min(seq, MLSTM_CHUNK)
    nc = seq // rows
    a_v = M_HEADS * M_V_DIM
    hp = M_HEADS // 2
    n_in = jnp.broadcast_to(n0.reshape(batch, hp, 2 * M_QK_DIM, 1), (batch, hp, 2 * M_QK_DIM, LANES))
    m_in = jnp.pad(m0, ((0, 0), (0, LANES - M_HEADS)))[:, None, :]
    k_minor = kt is not None
    assert not k_minor or rows == MLSTM_CHUNK == LANES
    kt_specs = [pl.BlockSpec((1, kt.shape[1], LANES), lambda b, c: (b * nc + c, 0, 0))] if k_minor else []
    h, c, n, m = pl.pallas_call(
        functools.partial(_mlstm_kernel, rows=rows, nc=nc, k_minor=k_minor), grid=(batch, nc),
        in_specs=[pl.BlockSpec((rows, main.shape[1]), lambda b, c: (b * nc + c, 0)),
                  pl.BlockSpec((rows, LANES), lambda b, c: (b * nc + c, 0)), *kt_specs,
                  pl.BlockSpec((1, LANES), lambda b, c: (0, 0)),
                  pl.BlockSpec((1, a_v), lambda b, c: (0, 0)),
                  pl.BlockSpec((None, M_HEADS, M_QK_DIM, M_V_DIM), lambda b, c: (b, 0, 0, 0)),
                  pl.BlockSpec((None, hp, 2 * M_QK_DIM, LANES), lambda b, c: (b, 0, 0, 0)),
                  pl.BlockSpec((None, 1, LANES), lambda b, c: (b, 0, 0))],
        out_specs=[pl.BlockSpec((rows, a_v), lambda b, c: (b * nc + c, 0)),
                   pl.BlockSpec((None, M_HEADS, M_QK_DIM, M_V_DIM), lambda b, c: (b, 0, 0, 0)),
                   pl.BlockSpec((None, hp, 2 * M_QK_DIM, LANES), lambda b, c: (b, 0, 0, 0)),
                   pl.BlockSpec((None, 1, LANES), lambda b, c: (b, 0, 0))],
        out_shape=[jax.ShapeDtypeStruct((batch * seq, a_v), F32),
                   jax.ShapeDtypeStruct((batch, M_HEADS, M_QK_DIM, M_V_DIM), F32),
                   jax.ShapeDtypeStruct((batch, hp, 2 * M_QK_DIM, LANES), F32),
                   jax.ShapeDtypeStruct((batch, 1, LANES), F32)],
        scratch_shapes=[pltpu.VMEM((M_HEADS, M_QK_DIM, M_V_DIM), F32), pltpu.VMEM((hp, 2 * M_QK_DIM, LANES), F32),
                        pltpu.VMEM((1, LANES), F32)],
        compiler_params=_cparams(("parallel", "arbitrary")), name=name,
    )(main, gate, *([kt] if k_minor else []), b_gate, g_hnorm.reshape(1, a_v), c0, n_in, m_in)
    return h, c, n[..., 0].reshape(batch, M_HEADS, M_QK_DIM), m[:, 0, :M_HEADS]


def _ffn_kernel(*refs, carry, tm, n_j, tiles_per_seq, period):
    x_ref, g2_ref, wua_ref, wug_ref, cwa_ref, cwg_ref, cba_ref, cbg_ref, wd_ref, g3_ref = refs[:10]
    if carry:
        inita_ref, initg_ref, o_ref, sa_ref, sg_ref, xn_s, acc_s, ue_s, carry_s = refs[10:]
        branch_in = ((wua_ref, cwa_ref, cba_ref, inita_ref, sa_ref), (wug_ref, cwg_ref, cbg_ref, initg_ref, sg_ref))
    else:
        t1a_ref, t1g_ref, t2a_ref, t2g_ref, o_ref, sa_ref, sg_ref, xn_s, acc_s, ue_s = refs[10:]
        branch_in = ((wua_ref, cwa_ref, cba_ref, (t1a_ref, t2a_ref), sa_ref),
                     (wug_ref, cwg_ref, cbg_ref, (t1g_ref, t2g_ref), sg_ref))
    i = pl.program_id(0)
    j = pl.program_id(1)
    tn = wd_ref.shape[0]

    @pl.when(j == 0)
    def _():
        xn_s[...] = _rms(x_ref[...], g2_ref[...]).astype(BF16)
        acc_s[...] = jnp.zeros_like(acc_s)

    xn = xn_s[...]
    conv = []
    for which, (w_ref, cw_ref, cb_ref, boundary, s_ref) in enumerate(branch_in):
        u = _dot(xn, w_ref[...])
        ue_s[SUBLANES:, :] = u
        if carry:
            slot = which * n_j + j
            first = (i % tiles_per_seq) == 0
            ue_s[SUBLANES - 2:SUBLANES, :] = jnp.where(first, boundary[...], carry_s[slot])
            tap1 = ue_s[pl.ds(SUBLANES - 1, tm), :]
            tap2 = ue_s[pl.ds(SUBLANES - 2, tm), :]
            last2 = u[tm - 2:tm, :]
            carry_s[slot] = last2
            s_ref[i // tiles_per_seq, j] = last2
        else:
            ue_s[0:SUBLANES, :] = jnp.zeros((SUBLANES, tn), F32)
            t = lax.broadcasted_iota(I32, (tm, tn), 0) & (period - 1)
            tap1 = jnp.where(t >= 1, ue_s[pl.ds(SUBLANES - 1, tm), :], boundary[0][...])
            tap2 = jnp.where(t >= 2, ue_s[pl.ds(SUBLANES - 2, tm), :], boundary[1][...])
            s_ref[...] = u
        conv.append(cb_ref[...] + tap2 * cw_ref[0:1, :] + tap1 * cw_ref[1:2, :] + u * cw_ref[2:3, :])
    y = jax.nn.gelu(conv[0], approximate=True) * conv[1]
    acc_s[...] += _dot(y.astype(BF16), wd_ref[...])

    @pl.when(j == n_j - 1)
    def _():
        o_ref[...] = x_ref[...] + _rms(acc_s[...], g3_ref[...])


def _ffn(x, g2, w_up, conv_w, conv_b, w_down, g3, prev, batch, seq, name):
    m, d = x.shape
    nf = w_down.shape[0]
    tn = FFN_TN
    n_j = nf // tn
    carry = seq >= 256
    cb = conv_b.reshape(1, 2 * nf)
    col_a = lambda i, j: (0, j)
    col_g = lambda i, j: (0, j + n_j)
    common_specs = [
        None,
        pl.BlockSpec((1, d), lambda i, j: (0, 0)),
        pl.BlockSpec((d, tn), col_a), pl.BlockSpec((d, tn), col_g),
        pl.BlockSpec((CONV_W, tn), col_a), pl.BlockSpec((CONV_W, tn), col_g),
        pl.BlockSpec((1, tn), col_a), pl.BlockSpec((1, tn), col_g),
        pl.BlockSpec((tn, d), lambda i, j: (j, 0)),
        pl.BlockSpec((1, d), lambda i, j: (0, 0)),
    ]
    common_args = [x, g2.reshape(1, d), w_up, w_up, conv_w, conv_w, cb, cb, w_down, g3.reshape(1, d)]
    if carry:
        tm = _row_tile(seq, FFN_TM)
        tps = seq // tm
        common_specs[0] = pl.BlockSpec((tm, d), lambda i, j: (i, 0))
        st_a = lambda i, j: (i // tps, 0, j)
        st_g = lambda i, j: (i // tps, 0, j + n_j)
        out, sa, sg = pl.pallas_call(
            functools.partial(_ffn_kernel, carry=True, tm=tm, n_j=n_j, tiles_per_seq=tps, period=seq),
            grid=(m // tm, n_j),
            in_specs=common_specs + [pl.BlockSpec((None, 2, tn), st_a), pl.BlockSpec((None, 2, tn), st_g)],
            out_specs=[pl.BlockSpec((tm, d), lambda i, j: (i, 0)),
                       pl.BlockSpec((batch, n_j, 2, tn), lambda i, j: (0, 0, 0, 0)),
                       pl.BlockSpec((batch, n_j, 2, tn), lambda i, j: (0, 0, 0, 0))],
            out_shape=[jax.ShapeDtypeStruct((m, d), F32), jax.ShapeDtypeStruct((batch, n_j, 2, tn), F32),
                       jax.ShapeDtypeStruct((batch, n_j, 2, tn), F32)],
            scratch_shapes=[pltpu.VMEM((tm, d), BF16), pltpu.VMEM((tm, d), F32),
                            pltpu.VMEM((tm + SUBLANES, tn), F32), pltpu.VMEM((2 * n_j, 2, tn), F32)],
            compiler_params=_cparams(("arbitrary", "arbitrary")), name=name,
        )(*common_args, prev, prev)
        sa, sg = (s.transpose(0, 2, 1, 3).reshape(batch, 2, nf) for s in (sa, sg))
        return out, jnp.concatenate([sa, sg], axis=-1)
    tm = m
    assert seq >= 2 and seq & (seq - 1) == 0
    common_specs[0] = pl.BlockSpec((tm, d), lambda i, j: (i, 0))
    tap1 = jnp.pad(prev[:, 1:2], ((0, 0), (0, seq - 1), (0, 0))).reshape(m, 2 * nf)
    tap2 = jnp.pad(prev, ((0, 0), (0, seq - 2), (0, 0))).reshape(m, 2 * nf)
    row_a = lambda i, j: (i, j)
    row_g = lambda i, j: (i, j + n_j)
    out, ua, ug = pl.pallas_call(
        functools.partial(_ffn_kernel, carry=False, tm=tm, n_j=n_j, tiles_per_seq=1, period=seq),
        grid=(m // tm, n_j),
        in_specs=common_specs + [pl.BlockSpec((tm, tn), row_a), pl.BlockSpec((tm, tn), row_g),
                                 pl.BlockSpec((tm, tn), row_a), pl.BlockSpec((tm, tn), row_g)],
        out_specs=[pl.BlockSpec((tm, d), lambda i, j: (i, 0)), pl.BlockSpec((tm, tn), row_a),
                   pl.BlockSpec((tm, tn), row_a)],
        out_shape=[jax.ShapeDtypeStruct((m, d), F32), jax.ShapeDtypeStruct((m, nf), F32),
                   jax.ShapeDtypeStruct((m, nf), F32)],
        scratch_shapes=[pltpu.VMEM((tm, d), BF16), pltpu.VMEM((tm, d), F32), pltpu.VMEM((tm + SUBLANES, tn), F32)],
        compiler_params=_cparams(("arbitrary", "arbitrary")), name=name,
    )(*common_args, tap1, tap1, tap2, tap2)
    u = jnp.concatenate([ua, ug], axis=-1).reshape(batch, seq, 2 * nf)
    return out, u[:, seq - 2:]


def _rope_pair(x, cos, sin):
    half = ROT_DIM // 2
    lane = lax.broadcasted_iota(I32, x.shape, 1) & (HEAD_DIM - 1)
    partner = jnp.where(lane < half, pltpu.roll(x, LANES - half, axis=1), pltpu.roll(x, half, axis=1))
    return x * cos + partner * sin


def _rope(x, cos, sin):
    return jnp.concatenate(
        [_rope_pair(x[:, c * LANES:(c + 1) * LANES], cos, sin) for c in range(x.shape[1] // LANES)], axis=1)


def _rope_tables(pos0, seq):
    half = ROT_DIM // 2
    inv = jnp.power(jnp.float32(ROPE_THETA), -jnp.arange(0, ROT_DIM, 2, dtype=F32) / ROT_DIM)
    ang = (pos0 + jnp.arange(seq, dtype=I32)).astype(F32)[:, None] * inv[None, :]
    cos, sin = jnp.cos(ang), jnp.sin(ang)
    rest = HEAD_DIM - ROT_DIM
    cos_h = jnp.concatenate([cos, cos, jnp.ones((seq, rest), F32)], axis=1)
    sin_h = jnp.concatenate([-sin, sin, jnp.zeros((seq, rest), F32)], axis=1)
    return jnp.tile(cos_h, (1, LANES // HEAD_DIM)), jnp.tile(sin_h, (1, LANES // HEAD_DIM))


def _kv_kernel(x_ref, g_ref, w_ref, cos_ref, sin_ref, kv_ref, win_ref):
    xn = _rms(x_ref[...], g_ref[...]).astype(BF16)
    y = _dot(xn, w_ref[...])
    cos, sin = cos_ref[...], sin_ref[...]
    gw = KV_HEADS * HEAD_DIM
    ks = _rope(y[:, 2 * gw:3 * gw], cos, sin)
    kw = _rope(y[:, 4 * gw:5 * gw], cos, sin)
    kv_ref[:, 0:2 * gw] = y[:, 0:2 * gw]
    kv_ref[:, 2 * gw:3 * gw] = ks
    kv_ref[:, 3 * gw:4 * gw] = y[:, 3 * gw:4 * gw]
    win_ref[:, 0:gw] = kw
    win_ref[:, gw:2 * gw] = y[:, 5 * gw:6 * gw]


def _rope_rows(x, cos_t, sin_t):
    half = ROT_DIM // 2
    row = lax.broadcasted_iota(I32, x.shape, 0) & (HEAD_DIM - 1)
    partner = jnp.where(row < half, pltpu.roll(x, x.shape[0] - half, axis=0), pltpu.roll(x, half, axis=0))
    return x * cos_t + partner * sin_t


def _kv_minor_kernel(x_ref, g_ref, wt_ref, wk_ref, cos_ref, sin_ref, cost_ref, sint_ref,
                     kvt_ref, wint_ref, ks_ref, kw_ref, vst_ref, vwt_ref):
    xn = _rms(x_ref[...], g_ref[...]).astype(BF16)
    gw = KV_HEADS * HEAD_DIM
    yt = _nt(wt_ref[...], xn)
    cos_t, sin_t = cost_ref[...], sint_ref[...]

    def rope_t(a):
        return jnp.concatenate(
            [_rope_rows(a[c * LANES:(c + 1) * LANES], cos_t, sin_t) for c in range(gw // LANES)], axis=0)

    kvt_ref[0:2 * gw, :] = yt[0:2 * gw]
    kvt_ref[2 * gw:3 * gw, :] = rope_t(yt[2 * gw:3 * gw])
    kvt_ref[3 * gw:4 * gw, :] = yt[3 * gw:4 * gw]
    wint_ref[0:gw, :] = rope_t(yt[4 * gw:5 * gw])
    wint_ref[gw:2 * gw, :] = yt[5 * gw:6 * gw]
    vst_ref[...] = yt[3 * gw:4 * gw].astype(BF16)
    for u in range(vwt_ref.shape[0]):
        vwt_ref[u] = yt[5 * gw:6 * gw, u * PAGE:(u + 1) * PAGE].astype(BF16)
    yk = _dot(xn, wk_ref[...])
    cos, sin = cos_ref[...], sin_ref[...]
    ks_ref[...] = _rope(yk[:, 0:gw], cos, sin).astype(BF16)
    kw_ref[...] = _rope(yk[:, gw:2 * gw], cos, sin).astype(BF16)


def _kv_rows_minor(x, g_kv, w_t, w_k, cos, sin, batch, seq, name):
    m, d = x.shape
    tm = _row_tile(seq, KEY_TILE)
    gw = KV_HEADS * HEAD_DIM
    tt = seq // tm
    tok = lambda i: (i % tt, 0)
    tok_t = lambda i: (0, i % tt)
    return pl.pallas_call(
        _kv_minor_kernel, grid=(m // tm,),
        in_specs=[pl.BlockSpec((tm, d), lambda i: (i, 0)), pl.BlockSpec((1, d), lambda i: (0, 0)),
                  pl.BlockSpec(w_t.shape, lambda i: (0, 0)), pl.BlockSpec(w_k.shape, lambda i: (0, 0)),
                  pl.BlockSpec((tm, LANES), tok), pl.BlockSpec((tm, LANES), tok),
                  pl.BlockSpec((LANES, tm), tok_t), pl.BlockSpec((LANES, tm), tok_t)],
        out_specs=[pl.BlockSpec((None, 4 * gw, tm), lambda i: (i // tt, 0, i % tt)),
                   pl.BlockSpec((None, 2 * gw, tm), lambda i: (i // tt, 0, i % tt)),
                   pl.BlockSpec((tm, gw), lambda i: (i, 0)), pl.BlockSpec((tm, gw), lambda i: (i, 0)),
                   pl.BlockSpec((None, gw, tm), lambda i: (i, 0, 0)),
                   pl.BlockSpec((tm // PAGE, gw, PAGE), lambda i: (i, 0, 0))],
        out_shape=[jax.ShapeDtypeStruct((batch, 4 * gw, seq), F32), jax.ShapeDtypeStruct((batch, 2 * gw, seq), F32),
                   jax.ShapeDtypeStruct((m, gw), BF16), jax.ShapeDtypeStruct((m, gw), BF16),
                   jax.ShapeDtypeStruct((m // tm, gw, tm), BF16), jax.ShapeDtypeStruct((m // PAGE, gw, PAGE), BF16)],
        compiler_params=_cparams(("parallel",)), name=name)(x, g_kv.reshape(1, d), w_t, w_k, cos, sin, cos.T, sin.T)


def _kv_rows(x, g_kv, w_kv, cos, sin, name):
    m, d = x.shape
    gw = KV_HEADS * HEAD_DIM
    return pl.pallas_call(
        _kv_kernel, grid=(1,),
        in_specs=[pl.BlockSpec((m, d), lambda i: (0, 0)), pl.BlockSpec((1, d), lambda i: (0, 0)),
                  pl.BlockSpec(w_kv.shape, lambda i: (0, 0)),
                  pl.BlockSpec((m, LANES), lambda i: (0, 0)), pl.BlockSpec((m, LANES), lambda i: (0, 0))],
        out_specs=[pl.BlockSpec((m, 4 * gw), lambda i: (0, 0)), pl.BlockSpec((m, 2 * gw), lambda i: (0, 0))],
        out_shape=[jax.ShapeDtypeStruct((m, 4 * gw), F32), jax.ShapeDtypeStruct((m, 2 * gw), F32)],
        compiler_params=_cparams(("arbitrary",)), name=name)(x, g_kv.reshape(1, d), w_kv, cos, sin)


def _token_minor(rows):
    n_p, r, n = rows.shape[0], rows.shape[1], rows.shape[2]
    return rows.transpose(0, 2, 3, 4, 1).reshape(n_p, n, KV_HEADS * HEAD_DIM, r)


def _compress_kernel(tbl_ref, *refs, n_pg, zero_after):
    page_refs = refs[:n_pg]
    (pos_lo_ref, pos_hi_ref, w1lo_ref, w1hi_ref, b1_ref, w2_ref, b2_ref, hi_init_ref,
     kc_ref, vc_ref, hi_first_ref, carry_s, t_s, x_s) = refs[n_pg:]
    c = pl.program_id(1)
    njp = PAGE // CMP_STRIDE
    njc = n_pg * njp

    @pl.when(c == 0)
    def _():
        carry_s[...] = hi_init_ref[...]

    lo_lanes = lax.broadcasted_iota(I32, (njp, LANES), 1) < HEAD_DIM
    for u in range(n_pg):
        for br in range(2):
            for cp in range(KV_HEADS // 2):
                tb = t_s.at[(2 * u + br) % 2 * 2 + cp]
                tb[...] = page_refs[u][br, cp * LANES:(cp + 1) * LANES, :].T
                for q in range(CMP_STRIDE // 2):
                    p0 = tb[pl.ds(2 * q, njp, stride=CMP_STRIDE), :]
                    p1 = tb[pl.ds(2 * q + 1, njp, stride=CMP_STRIDE), :]
                    rows = slice(u * njp, (u + 1) * njp)
                    cols = slice(q * LANES, (q + 1) * LANES)
                    x_s[br * KV_HEADS + 2 * cp, rows, cols] = jnp.where(lo_lanes, p0, pltpu.roll(p1, HEAD_DIM, axis=1))
                    x_s[br * KV_HEADS + 2 * cp + 1, rows, cols] = jnp.where(lo_lanes, pltpu.roll(p0, HEAD_DIM, axis=1), p1)

    rowl = lax.broadcasted_iota(I32, (KV_HEADS * njc, CMP_HIDDEN), 0) & (njc - 1)
    for br, out_ref in enumerate((kc_ref, vc_ref)):
        x = x_s[br * KV_HEADS:(br + 1) * KV_HEADS].reshape(KV_HEADS * njc, CMP_STRIDE * HEAD_DIM)
        lo = _dot((x + pos_lo_ref[br]).astype(BF16), w1lo_ref[br])
        hi = _dot((x + pos_hi_ref[br]).astype(BF16), w1hi_ref[br])
        carry_rows = jnp.concatenate(
            [jnp.broadcast_to(carry_s[br * KV_HEADS + g:br * KV_HEADS + g + 1, :], (njc, CMP_HIDDEN))
             for g in range(KV_HEADS)], axis=0)
        hi_next = jnp.where(rowl == njc - 1, carry_rows, pltpu.roll(hi, KV_HEADS * njc - 1, axis=0))
        if zero_after is not None:
            hi_next = jnp.where(jnp.logical_and(rowl == zero_after, c == 0), 0.0, hi_next)
        for g in range(KV_HEADS):
            carry_s[br * KV_HEADS + g:br * KV_HEADS + g + 1, :] = hi[g * njc:g * njc + 1, :]
        h = jax.nn.gelu(lo + hi_next + b1_ref[br], approximate=True)
        o = _dot(h.astype(BF16), w2_ref[br]) + b2_ref[br]
        for g in range(KV_HEADS):
            out_ref[:, g * HEAD_DIM:(g + 1) * HEAD_DIM] = o[g * njc:(g + 1) * njc, :]
    hi_first_ref[...] = carry_s[...]


def _compress(pages, table, batch, n_pages, n_pg, hi_init, cw, zero_after, name):
    njp = PAGE // CMP_STRIDE
    njc = n_pg * njp
    assert njc & (njc - 1) == 0 and njc % SUBLANES == 0 and n_pages % n_pg == 0
    n_ch = n_pages // n_pg
    gw = KV_HEADS * HEAD_DIM
    flat = CMP_STRIDE * HEAD_DIM

    def page_map(u):
        if table is None:
            return lambda b, c, tbl: (b, 0, 0, (n_ch - 1 - c) * n_pg + u)
        return lambda b, c, tbl: (tbl[b * n_pages + (n_ch - 1 - c) * n_pg + u], 0, 0, 0)

    full3 = lambda b, c, tbl: (0, 0, 0)
    in_specs = [pl.BlockSpec((None, 2, gw, PAGE), page_map(u)) for u in range(n_pg)]
    in_specs += [pl.BlockSpec((2, 1, flat), full3), pl.BlockSpec((2, 1, flat), full3),
                 pl.BlockSpec((2, flat, CMP_HIDDEN), full3), pl.BlockSpec((2, flat, CMP_HIDDEN), full3),
                 pl.BlockSpec((2, 1, CMP_HIDDEN), full3), pl.BlockSpec((2, CMP_HIDDEN, HEAD_DIM), full3),
                 pl.BlockSpec((2, 1, HEAD_DIM), full3),
                 pl.BlockSpec((None, 2 * KV_HEADS, CMP_HIDDEN), lambda b, c, tbl: (b, 0, 0))]
    out_map = lambda b, c, tbl: (b, n_ch - 1 - c, 0)
    grid_spec = pltpu.PrefetchScalarGridSpec(
        num_scalar_prefetch=1, grid=(batch, n_ch), in_specs=in_specs,
        out_specs=[pl.BlockSpec((None, njc, gw), out_map), pl.BlockSpec((None, njc, gw), out_map),
                   pl.BlockSpec((None, 2 * KV_HEADS, CMP_HIDDEN), lambda b, c, tbl: (b, 0, 0))],
        scratch_shapes=[pltpu.VMEM((2 * KV_HEADS, CMP_HIDDEN), F32), pltpu.VMEM((4, PAGE, LANES), F32),
                        pltpu.VMEM((2 * KV_HEADS, njc, flat), F32)])
    return pl.pallas_call(
        functools.partial(_compress_kernel, n_pg=n_pg, zero_after=zero_after), grid_spec=grid_spec,
        out_shape=[jax.ShapeDtypeStruct((batch, n_ch * njc, gw), F32), jax.ShapeDtypeStruct((batch, n_ch * njc, gw), F32),
                   jax.ShapeDtypeStruct((batch, 2 * KV_HEADS, CMP_HIDDEN), F32)],
        compiler_params=_cparams(("parallel", "arbitrary")), name=name,
    )(jnp.zeros((1,), I32) if table is None else table, *([pages] * n_pg), cw["pos_lo"], cw["pos_hi"], cw["w1lo"], cw["w1hi"], cw["b1"], cw["w2"], cw["b2"], hi_init)


def _cmp_layout(kc_nat, nb, nbp):
    b = kc_nat.shape[0]
    x = kc_nat[:, :4 * nb].reshape(b, nb, 4, kc_nat.shape[-1]).transpose(0, 2, 1, 3)
    x = jnp.pad(x, ((0, 0), (0, 0), (0, nbp - nb), (0, 0)))
    return x.reshape(b, 4 * nbp, kc_nat.shape[-1]).astype(BF16)


SCALE = HEAD_DIM ** -0.5
QSCALE = SCALE * math.log2(math.e)
SUM_ROWS = 16
ACC_ROWS = HEAD_DIM + SUM_ROWS


def _group_queries(q, g):
    nq = q.shape[0]
    keep = (lax.broadcasted_iota(I32, (nq, LANES), 1) >> 6) == (g % 2)
    pieces = []
    for hh in range(HPG):
        h = HPG * g + hh
        chunk = q[:, (h // 2) * LANES:(h // 2 + 1) * LANES]
        if h % 2 != g % 2:
            chunk = pltpu.roll(chunk, HEAD_DIM, axis=1)
        pieces.append(jnp.where(keep, chunk, 0.0))
    return (jnp.concatenate(pieces, axis=0) * QSCALE).astype(BF16)


def _kchunk(k, g):
    return k[:, (g // 2) * LANES:(g // 2 + 1) * LANES]


def _values(vt, g):
    return jnp.concatenate([vt[g * HEAD_DIM:(g + 1) * HEAD_DIM, :], jnp.ones((SUM_ROWS, vt.shape[1]), BF16)], axis=0)


def _flash_groups(ss, vt, m_s, acc_s):
    ps, alphas = [], []
    for g in range(KV_HEADS):
        m_old = m_s[g]
        m_new = jnp.maximum(m_old, jnp.max(ss[g], axis=0, keepdims=True))
        alphas.append(jnp.exp2(m_old - m_new))
        ps.append(jnp.exp2(ss[g] - m_new).astype(BF16))
        m_s[g] = m_new
    for g in range(KV_HEADS):
        acc_s[g] = acc_s[g] * alphas[g] + _dot(_values(vt, g), ps[g])


def _one_shot_groups(ss, vt):
    ps = [jnp.exp2(s - jnp.max(s, axis=0, keepdims=True)).astype(BF16) for s in ss]
    return [_dot(_values(vt, g), ps[g]) for g in range(KV_HEADS)]


def _init_state(m_s, acc_s):
    m_s[...] = jnp.full(m_s.shape, NEG, F32)
    acc_s[...] = jnp.zeros(acc_s.shape, F32)


def _topk_bias(score, n_sel):
    n_iota = lax.broadcasted_iota(I32, score.shape, 0)

    def body(_, sc):
        mx = jnp.max(sc, axis=0, keepdims=True)
        idx = jnp.min(jnp.where(sc == mx, n_iota, score.shape[0]), axis=0, keepdims=True)
        return jnp.where(n_iota == idx, -jnp.inf, sc)

    left = lax.fori_loop(0, n_sel, body, score)
    return jnp.where(jnp.logical_and(left == -jnp.inf, score > -jnp.inf), 0.0, NEG)


def _cmp_valid(qpos_w, nb, nbp):
    n_w = lax.broadcasted_iota(I32, (nbp, qpos_w.shape[1]), 0)
    return jnp.concatenate(
        [jnp.logical_and((4 * n_w + c) * CMP_STRIDE + (CMP_BLOCK - 1) <= qpos_w, n_w < nb) for c in range(4)], axis=0)


def _cmp_branch(s, valid, vct_ref, g, qpos_w, qpos_q, nb, nbp, nq):
    w = s.shape[1]
    s = jnp.where(valid, s, NEG)
    e = jnp.exp2(s - jnp.max(s, axis=0, keepdims=True))
    some = (qpos_w >= CMP_BLOCK - 1).astype(F32)
    p = e * (some / jnp.maximum(jnp.sum(e, axis=0, keepdims=True), 1e-30))
    o_cmp = _dot(vct_ref[g * HEAD_DIM:(g + 1) * HEAD_DIM, :], p.astype(BF16))
    if nq == LANES:
        ps = ((p[:, 0:LANES] + p[:, LANES:2 * LANES]) + p[:, 2 * LANES:3 * LANES]) + p[:, 3 * LANES:4 * LANES]
    else:
        assert w == LANES
        ps = ((p + pltpu.roll(p, nq, axis=1)) + pltpu.roll(p, 2 * nq, axis=1)) + pltpu.roll(p, 3 * nq, axis=1)
    parts = [ps[c * nbp:(c + 1) * nbp] for c in range(4)]
    n_q = lax.broadcasted_iota(I32, (nbp, LANES), 0)
    prev = jnp.where(n_q >= 1, pltpu.roll(parts[3], 1, axis=0), 0.0)
    score = (((parts[0] + parts[1]) + parts[2]) + parts[3]) + prev
    cur = qpos_q >> 6
    forced = (n_q == 0) | (n_q == cur) | (n_q == cur - 1)
    score = jnp.where(forced, BIG, jnp.where(n_q * SEL_BLOCK <= qpos_q, score, -BIG))
    score = jnp.where(n_q < nb, score, -jnp.inf)
    return o_cmp, score


def _compressed_and_select(q, kc_ref, vct_ref, qpos_w, qpos_q, ocmp_s, score_s, bias_s, nb, nbp, nq, n_eff=None):
    w = HPG * nq
    n_eff = nbp if n_eff is None else n_eff
    kc_all = jnp.concatenate([kc_ref[c * nbp:c * nbp + n_eff, :] for c in range(4)], axis=0)
    vct_all = jnp.concatenate([vct_ref[:, c * nbp:c * nbp + n_eff] for c in range(4)], axis=1)
    cs = [_nt(_kchunk(kc_all, g), _group_queries(q, g)) for g in range(KV_HEADS)]
    valid = _cmp_valid(qpos_w, nb, n_eff)
    for g in range(KV_HEADS):
        o_cmp, score = _cmp_branch(cs[g], valid, vct_all, g, qpos_w, qpos_q, nb, n_eff, nq)
        ocmp_s[g] = o_cmp
        score_s[0:n_eff, g * LANES:(g + 1) * LANES] = score
    bias = _topk_bias(score_s[0:n_eff, :], min(N_SELECT, nb))
    for g in range(KV_HEADS):
        b = bias[:, g * LANES:(g + 1) * LANES]
        bias_s[g, 0:n_eff, :] = b if w == LANES else jnp.concatenate([b] * (w // LANES), axis=1)
        if n_eff < nbp:
            bias_s[g, n_eff:nbp, :] = jnp.full((nbp - n_eff, w), NEG, F32)


def _block_bias(bias_rows, n_blk):
    w = bias_rows.shape[1]
    return jnp.concatenate([jnp.broadcast_to(bias_rows[u:u + 1, :], (SEL_BLOCK, w)) for u in range(n_blk)], axis=0)


def _gate_row(gates_t, c, g, nq, w):
    rows = [gates_t[(c * HPG + hh) * KV_HEADS + g:(c * HPG + hh) * KV_HEADS + g + 1, :] for hh in range(HPG)]
    if nq == LANES:
        return jnp.concatenate(rows, axis=1)
    strip = lax.broadcasted_iota(I32, (1, LANES), 1) >> (nq.bit_length() - 1)
    out = jnp.zeros((1, LANES), F32)
    for hh in range(HPG):
        out = jnp.where(strip == hh, rows[hh] if hh == 0 else pltpu.roll(rows[hh], hh * nq, axis=1), out)
    return out


def _finish(o_ref, acc_s, ocmp_s, wins, ot_s, gates_t, nq, nq_real, w):
    for g in range(KV_HEADS):
        o = ocmp_s[g] * _gate_row(gates_t, 0, g, nq, w)
        for c, acc in ((1, acc_s[g]), (2, wins[g])):
            o = o + acc[:HEAD_DIM] * (_gate_row(gates_t, c, g, nq, w) / acc[HEAD_DIM:HEAD_DIM + 1])
        ot_s[g * HEAD_DIM:(g + 1) * HEAD_DIM, :] = o
    o_t = ot_s[...].T
    for hh in range(HPG):
        o_ref[:, hh * KV_HEADS * HEAD_DIM:(hh + 1) * KV_HEADS * HEAD_DIM] = o_t[hh * nq:hh * nq + nq_real, :]


def _attn_prompt_kernel(q_ref, gp_ref, bg_ref, cos_ref, sin_ref, kc_ref, vct_ref, ks_ref, vst_ref, kw_ref, vwt_ref,
                        o_ref, m_s, acc_s, score_s, bias_s, ocmp_s, ot_s, qzr_s, *, nb, nbp):
    i = pl.program_id(1)
    nq = Q_TILE
    w = HPG * nq
    s0 = i * nq
    q = q_ref[...]
    q_rot = _rope(q, cos_ref[...], sin_ref[...])
    gates_t = jax.nn.sigmoid(gp_ref[...] + bg_ref[...]).T
    qpos_w = s0 + (lax.broadcasted_iota(I32, (1, w), 1) & (nq - 1))
    qpos_q = s0 + lax.broadcasted_iota(I32, (1, LANES), 1)
    _init_state(m_s, acc_s)
    for g in range(KV_HEADS):
        qzr_s[g] = _group_queries(q_rot, g)
    n_cls = 4 if nbp % (4 * SUBLANES) == 0 and nbp // 4 >= N_SELECT else 1
    per_cls = nbp // n_cls
    cls = jnp.minimum((2 * (i + 1) + per_cls - 1) // per_cls, n_cls) - 1
    for k in range(n_cls):
        @pl.when(cls == k)
        def _(k=k):
            _compressed_and_select(q, kc_ref, vct_ref, qpos_w, qpos_q, ocmp_s, score_s, bias_s, nb, nbp, nq,
                                   n_eff=(k + 1) * per_cls)

    blk_per_tile = KEY_TILE // SEL_BLOCK

    def slc_tile(t, causal):
        k0 = pl.multiple_of(t * KEY_TILE, KEY_TILE)
        kt = ks_ref[pl.ds(k0, KEY_TILE), :]
        ss = []
        for g in range(KV_HEADS):
            rows = bias_s[g, pl.ds(pl.multiple_of(t * blk_per_tile, blk_per_tile), blk_per_tile), :]
            s = _nt(_kchunk(kt, g), qzr_s[g]) + _block_bias(rows, blk_per_tile)
            if causal:
                s = jnp.where(k0 + lax.broadcasted_iota(I32, (KEY_TILE, w), 0) <= qpos_w, s, NEG)
            ss.append(s)
        _flash_groups(ss, vst_ref[t], m_s, acc_s)

    t_diag = s0 // KEY_TILE

    def full_tile(t, carry):
        slc_tile(t, False)
        return carry

    lax.fori_loop(0, t_diag, full_tile, 0)
    slc_tile(t_diag, True)

    n_wt = (WINDOW + nq) // PAGE
    t0 = jnp.maximum(i - WINDOW // PAGE, 0)
    k0 = pl.multiple_of(t0 * PAGE, PAGE)
    kwin = kw_ref[pl.ds(k0, n_wt * PAGE), :]
    vwin_tiles = vwt_ref[pl.ds(t0, n_wt)]
    vwin = jnp.concatenate([vwin_tiles[u] for u in range(n_wt)], axis=1)
    kpos = k0 + lax.broadcasted_iota(I32, (n_wt * PAGE, w), 0)
    wmask = jnp.logical_and(kpos <= qpos_w, qpos_w - kpos < WINDOW)
    ws = [jnp.where(wmask, _nt(_kchunk(kwin, g), qzr_s[g]), NEG) for g in range(KV_HEADS)]
    _finish(o_ref, acc_s, ocmp_s, _one_shot_groups(ws, vwin), ot_s, gates_t, nq, nq, w)


def _attn_prompt(q, gate_pre, b_gate, cos, sin, kc, vct, ks, vst, kw, vwt, batch, seq, name):
    assert seq % KEY_TILE == 0 and seq >= WINDOW + Q_TILE
    m, d = q.shape
    nq = Q_TILE
    w = HPG * nq
    nqb = seq // nq
    nb = seq // SEL_BLOCK
    nbp = kc.shape[1] // 4
    gw = KV_HEADS * HEAD_DIM
    ntile = seq // KEY_TILE
    row_map = lambda b, i: (b * nqb + i, 0)
    per_b2 = lambda b, i: (b, 0)
    per_b3 = lambda b, i: (b, 0, 0)
    return pl.pallas_call(
        functools.partial(_attn_prompt_kernel, nb=nb, nbp=nbp), grid=(batch, nqb),
        in_specs=[pl.BlockSpec((nq, d), row_map), pl.BlockSpec((nq, LANES), row_map),
                  pl.BlockSpec((1, LANES), lambda b, i: (0, 0)),
                  pl.BlockSpec((nq, LANES), lambda b, i: (i, 0)), pl.BlockSpec((nq, LANES), lambda b, i: (i, 0)),
                  pl.BlockSpec((None, 4 * nbp, gw), per_b3), pl.BlockSpec((None, gw, 4 * nbp), per_b3),
                  pl.BlockSpec((seq, gw), per_b2), pl.BlockSpec((ntile, gw, KEY_TILE), per_b3),
                  pl.BlockSpec((seq, gw), per_b2), pl.BlockSpec((seq // PAGE, gw, PAGE), per_b3)],
        out_specs=pl.BlockSpec((nq, d), row_map),
        out_shape=jax.ShapeDtypeStruct((m, d), F32),
        scratch_shapes=[pltpu.VMEM((KV_HEADS, 1, w), F32), pltpu.VMEM((KV_HEADS, ACC_ROWS, w), F32),
                        pltpu.VMEM((nbp, KV_HEADS * LANES), F32), pltpu.VMEM((KV_HEADS, nbp, w), F32),
                        pltpu.VMEM((KV_HEADS, HEAD_DIM, w), F32), pltpu.VMEM((gw, w), F32),
                        pltpu.VMEM((KV_HEADS, w, LANES), BF16)],
        compiler_params=_cparams(("parallel", "arbitrary")), name=name,
    )(q, gate_pre, b_gate, cos, sin, kc, vct, ks, vst, kw, vwt)


def _pad_rows(x, n):
    return jnp.concatenate([x, jnp.zeros((n - x.shape[0], x.shape[1]), x.dtype)], axis=0)


def _sample_queries(q):
    n = q.shape[0]
    half = lax.broadcasted_iota(I32, (n, LANES), 1) >> 6
    zeros = jnp.zeros((n, LANES), F32)
    blocks = []
    for g in range(KV_HEADS):
        for hh in range(HPG):
            h = HPG * g + hh
            chunk = q[:, (h // 2) * LANES:(h // 2 + 1) * LANES]
            if h % 2 != g % 2:
                chunk = pltpu.roll(chunk, HEAD_DIM, axis=1)
            blk = jnp.where(half == (g % 2), chunk, 0.0)
            blocks.append(jnp.concatenate([blk, zeros] if g < 2 else [zeros, blk], axis=1))
    return (jnp.concatenate(blocks, axis=0) * QSCALE).astype(BF16)


def _with_ones(vt):
    return jnp.concatenate([vt, jnp.ones((SUM_ROWS, vt.shape[1]), BF16)], axis=0)


def _attn_sample_kernel(tbl_ref, *refs, n_pg, n_steps, nb, nbp, pos0, n_new):
    k_refs = refs[:n_pg]
    vt_refs = refs[n_pg:2 * n_pg]
    (q_ref, gp_ref, bg_ref, cos_ref, sin_ref, kc_ref, vct_ref, kvn_ref, cwin_ref, wnew_ref,
     o_ref, m_s, acc_s, bias_s, ocmp_s, qzr_s, gt_s) = refs[2 * n_pg:]
    step = pl.program_id(1)
    gw = KV_HEADS * HEAD_DIM
    per_g = HPG * n_new
    lane1 = lax.broadcasted_iota(I32, (1, LANES), 1)
    qpos = pos0 + (lane1 & (n_new - 1))

    @pl.when(step == 0)
    def _():
        q = q_ref[...]
        qzr_s[...] = _sample_queries(_rope(q, cos_ref[...], sin_ref[...]))
        gt_s[...] = _pad_rows(jax.nn.sigmoid(gp_ref[...] + bg_ref[...]), LANES).T
        m_s[...] = jnp.full(m_s.shape, NEG, F32)
        acc_s[...] = jnp.zeros(acc_s.shape, F32)
        valid = _cmp_valid(qpos, nb, nbp)
        sc = jnp.where(valid, _nt(kc_ref[...], _sample_queries(q)), NEG)
        e = jnp.exp2(sc - jnp.max(sc, axis=0, keepdims=True))
        some = (qpos >= CMP_BLOCK - 1).astype(F32)
        p = e * (some / jnp.maximum(jnp.sum(e, axis=0, keepdims=True), 1e-30))
        ocmp_s[...] = _dot(vct_ref[...], p.astype(BF16))
        in_strip = lax.broadcasted_iota(I32, p.shape, 1) & (per_g - 1)
        ps = p
        for k in range(1, HPG):
            sh = k * n_new
            ps = ps + jnp.where(in_strip >= sh, pltpu.roll(p, sh, axis=1), pltpu.roll(p, LANES - per_g + sh, axis=1))
        parts = [ps[c * nbp:(c + 1) * nbp] for c in range(4)]
        n_q = lax.broadcasted_iota(I32, (nbp, LANES), 0)
        prev = jnp.where(n_q >= 1, pltpu.roll(parts[3], 1, axis=0), 0.0)
        score = (((parts[0] + parts[1]) + parts[2]) + parts[3]) + prev
        cur = qpos >> 6
        forced = (n_q == 0) | (n_q == cur) | (n_q == cur - 1)
        score = jnp.where(forced, BIG, jnp.where(n_q * SEL_BLOCK <= qpos, score, -BIG))
        score = jnp.where(n_q < nb, score, -jnp.inf)
        bias_s[...] = _topk_bias(score, min(N_SELECT, nb))

    def flash(k_rows, vt, s_bias):
        s = _nt(k_rows, qzr_s[...]) + s_bias
        m_old = m_s[...]
        m_new = jnp.maximum(m_old, jnp.max(s, axis=0, keepdims=True))
        acc_s[...] = acc_s[...] * jnp.exp2(m_old - m_new) + _dot(_with_ones(vt), jnp.exp2(s - m_new).astype(BF16))
        m_s[...] = m_new

    blk_pp = PAGE // SEL_BLOCK
    blk_ps = n_pg * blk_pp
    k_all = jnp.concatenate(
        [jnp.concatenate([r[cp * LANES:(cp + 1) * LANES, :].T for r in k_refs], axis=0) for cp in range(gw // LANES)],
        axis=1).astype(BF16)
    vt = jnp.concatenate([r[...] for r in vt_refs], axis=1).astype(BF16)
    flash(k_all, vt, _block_bias(bias_s[pl.ds(pl.multiple_of(step * blk_ps, blk_ps), blk_ps), :], blk_ps))

    @pl.when(step == n_steps - 1)
    def _():
        krow = lax.broadcasted_iota(I32, (PAGE, LANES), 0)
        kvn = _pad_rows(kvn_ref[...], PAGE)
        nb0 = pos0 // SEL_BLOCK
        tail_bias = jnp.where(pos0 + krow <= qpos, _block_bias(bias_s[nb0:nb0 + blk_pp, :], blk_pp), NEG)
        flash(kvn[:, 2 * gw:3 * gw].astype(BF16), kvn[:, 3 * gw:4 * gw].T.astype(BF16), tail_bias)

        n_cached = cwin_ref.shape[2]
        wn = _pad_rows(wnew_ref[...], PAGE)
        kwin = jnp.concatenate([wn[:, 0:gw], cwin_ref[0].T], axis=0).astype(BF16)
        vwin = jnp.concatenate([wn[:, gw:2 * gw].T, cwin_ref[1]], axis=1).astype(BF16)
        kpos = jnp.concatenate([pos0 + krow, pos0 - n_cached + lax.broadcasted_iota(I32, (n_cached, LANES), 0)], axis=0)
        wmask = jnp.logical_and(jnp.logical_and(kpos <= qpos, qpos - kpos < WINDOW), kpos >= 0)
        ws = jnp.where(wmask, _nt(kwin, qzr_s[...]), NEG)
        wacc = _dot(_with_ones(vwin), jnp.exp2(ws - jnp.max(ws, axis=0, keepdims=True)).astype(BF16))

        gates_t = gt_s[...]
        grows = []
        for c in range(3):
            row = jnp.zeros((1, LANES), F32)
            for g in range(KV_HEADS):
                for hh in range(HPG):
                    col = (c * HPG + hh) * KV_HEADS + g
                    off = g * per_g + hh * n_new
                    src = gates_t[col:col + 1, :]
                    row = jnp.where((lane1 >= off) & (lane1 < off + n_new), src if off == 0 else pltpu.roll(src, off, axis=1), row)
            grows.append(row)
        acc = acc_s[...]
        o = (ocmp_s[...] * grows[0] + acc[0:gw] * (grows[1] / acc[gw:gw + 1])
             + wacc[0:gw] * (grows[2] / wacc[gw:gw + 1]))
        o_t = o.T
        for g in range(KV_HEADS):
            for hh in range(HPG):
                r0 = g * per_g + hh * n_new
                c0 = hh * gw + g * HEAD_DIM
                o_ref[:, c0:c0 + HEAD_DIM] = o_t[r0:r0 + n_new, g * HEAD_DIM:(g + 1) * HEAD_DIM]


def _attn_sample(q, gate_pre, b_gate, cos, sin, kc, vct, pages, table, kv_new, cache_win_t, win_new,
                 batch, n_new, n_pages, name):
    m, d = q.shape
    assert N_HEADS * n_new == LANES and n_new % SUBLANES == 0
    n_pg = ATT_PAGES
    assert n_pages % n_pg == 0
    n_steps = n_pages // n_pg
    pos0 = n_pages * PAGE
    nb = (pos0 + n_new + SEL_BLOCK - 1) // SEL_BLOCK
    nbp = kc.shape[1] // 4
    gw = KV_HEADS * HEAD_DIM
    n_cached = cache_win_t.shape[3]
    assert n_cached % PAGE == 0

    def page_map(u, branch):
        return lambda b, s, tbl: (tbl[b * n_pages + s * n_pg + u], branch, 0, 0)

    row_map = lambda b, s, tbl: (b, 0)
    per_b3 = lambda b, s, tbl: (b, 0, 0)
    const2 = lambda b, s, tbl: (0, 0)
    in_specs = [pl.BlockSpec((None, None, gw, PAGE), page_map(u, 2)) for u in range(n_pg)]
    in_specs += [pl.BlockSpec((None, None, gw, PAGE), page_map(u, 3)) for u in range(n_pg)]
    in_specs += [pl.BlockSpec((n_new, d), row_map), pl.BlockSpec((n_new, LANES), row_map),
                 pl.BlockSpec((1, LANES), const2), pl.BlockSpec((n_new, LANES), const2),
                 pl.BlockSpec((n_new, LANES), const2),
                 pl.BlockSpec((None, 4 * nbp, gw), per_b3), pl.BlockSpec((None, gw, 4 * nbp), per_b3),
                 pl.BlockSpec((n_new, 4 * gw), row_map),
                 pl.BlockSpec((None, 2, gw, n_cached), lambda b, s, tbl: (b, 0, 0, 0)),
                 pl.BlockSpec((n_new, 2 * gw), row_map)]
    grid_spec = pltpu.PrefetchScalarGridSpec(
        num_scalar_prefetch=1, grid=(batch, n_steps), in_specs=in_specs,
        out_specs=pl.BlockSpec((n_new, d), row_map),
        scratch_shapes=[pltpu.VMEM((1, LANES), F32), pltpu.VMEM((gw + SUM_ROWS, LANES), F32),
                        pltpu.VMEM((nbp, LANES), F32), pltpu.VMEM((gw, LANES), F32),
                        pltpu.VMEM((LANES, gw), BF16), pltpu.VMEM((LANES, LANES), F32)])
    return pl.pallas_call(
        functools.partial(_attn_sample_kernel, n_pg=n_pg, n_steps=n_steps, nb=nb, nbp=nbp, pos0=pos0, n_new=n_new),
        grid_spec=grid_spec, out_shape=jax.ShapeDtypeStruct((m, d), F32),
        compiler_params=_cparams(("parallel", "arbitrary")), name=name,
    )(table, *([pages] * 2 * n_pg), q, gate_pre, b_gate, cos, sin, kc, vct, kv_new, cache_win_t, win_new)


def _prep_weights(w_a_in, b_a_gate, w_a_out, w_kv, cmp_pos, cmp_w1, cmp_b1, cmp_w2, cmp_b2,
                  w_b_in, b_b_gate, w_b_out, w_ffn_up, w_ffn_down):
    a_q = M_HEADS * M_QK_DIM
    a_v = M_HEADS * M_V_DIM
    n_g = 2 * M_HEADS
    g0 = 2 * a_q + a_v
    gw = KV_HEADS * HEAD_DIM
    qd = N_HEADS * HEAD_DIM
    p = {}
    p["a_main"] = jnp.concatenate([w_a_in[:, :, :g0], w_a_in[:, :, g0 + n_g:]], axis=-1).astype(BF16)
    p["a_qvo"] = jnp.concatenate([w_a_in[:, :, :a_q], w_a_in[:, :, 2 * a_q:g0], w_a_in[:, :, g0 + n_g:]],
                                 axis=-1).astype(BF16)
    p["a_kt"] = w_a_in[:, :, a_q:2 * a_q].transpose(0, 2, 1).astype(BF16)
    p["a_gate"] = jnp.pad(w_a_in[:, :, g0:g0 + n_g], ((0, 0), (0, 0), (0, LANES - n_g))).astype(BF16)
    p["a_bgate"] = jnp.pad(b_a_gate, ((0, 0), (0, LANES - n_g)))[:, None, :]
    p["a_out"] = w_a_out.astype(BF16)
    p["kv"] = w_kv.astype(BF16)
    p["kv_t"] = w_kv.T.astype(BF16)
    p["kv_k"] = jnp.concatenate([w_kv[:, 2 * gw:3 * gw], w_kv[:, 4 * gw:5 * gw]], axis=1).astype(BF16)
    hh, g, c = jnp.meshgrid(jnp.arange(HPG), jnp.arange(KV_HEADS), jnp.arange(3), indexing="ij")
    old_col = ((HPG * g + hh) * 3 + c)
    new_col = ((c * HPG + hh) * KV_HEADS + g)
    order = jnp.zeros((3 * N_HEADS,), I32).at[new_col.reshape(-1)].set(old_col.reshape(-1))
    p["b_q"] = w_b_in[:, :, :qd].astype(BF16)
    p["b_gate"] = jnp.pad(w_b_in[:, :, qd:][:, :, order], ((0, 0), (0, 0), (0, LANES - 3 * N_HEADS))).astype(BF16)
    p["b_bgate"] = jnp.pad(b_b_gate[:, order], ((0, 0), (0, LANES - 3 * N_HEADS)))[:, None, :]
    wo = w_b_out.reshape(w_b_out.shape[0], KV_HEADS, HPG, HEAD_DIM, D_MODEL).transpose(0, 2, 1, 3, 4)
    p["b_out"] = wo.reshape(w_b_out.shape[0], qd, D_MODEL).astype(BF16)
    p["up"] = w_ffn_up.astype(BF16)
    p["down"] = w_ffn_down.astype(BF16)
    flat = CMP_STRIDE * HEAD_DIM
    p["cmp"] = {
        "pos_lo": cmp_pos[:, :CMP_STRIDE].reshape(2, 1, flat), "pos_hi": cmp_pos[:, CMP_STRIDE:].reshape(2, 1, flat),
        "w1lo": cmp_w1[:, :flat].astype(BF16), "w1hi": cmp_w1[:, flat:].astype(BF16),
        "b1": cmp_b1[:, None, :], "w2": cmp_w2.astype(BF16), "b2": cmp_b2[:, None, :]}
    return p


def _trunk(x3, pos0, past, p, g_norms, g_a_hnorm, g_kv, ffn_conv_w, ffn_conv_b, conv_prev, m_c, m_n, m_m, tag):
    batch, seq, d = x3.shape
    x = x3.reshape(batch * seq, d)
    cs, ns, ms, convs = [], [], [], []
    gw = KV_HEADS * HEAD_DIM
    for layer in range(DEPTH):
        g = g_norms[layer]
        nm = f"{tag}{layer}"
        if layer < N_A_LAYERS:
            if seq % MLSTM_CHUNK == 0:
                main, gate, kt = _norm_proj(x, g[0], [p["a_qvo"][layer], p["a_gate"][layer]], nm + "_in",
                                            wts=[p["a_kt"][layer]])
            else:
                main, gate = _norm_proj(x, g[0], [p["a_main"][layer], p["a_gate"][layer]], nm + "_in")
                kt = None
            h, c_new, n_new, m_new = _mlstm(main, gate, kt, p["a_bgate"][layer], g_a_hnorm[layer],
                                            m_c[layer], m_n[layer], m_m[layer], batch, seq, nm + "_mlstm")
            cs.append(c_new)
            ns.append(n_new)
            ms.append(m_new)
            x = _proj_norm_res(h, p["a_out"][layer], g[1], x, nm + "_out")
        else:
            if layer == N_A_LAYERS:
                cos, sin = _rope_tables(pos0, seq)
                zeros_hi = jnp.zeros((batch, 2 * KV_HEADS, CMP_HIDDEN), F32)
                if past is None:
                    kvt, wint, ks_b, kw_b, vst, vwt = _kv_rows_minor(x, g_kv, p["kv_t"], p["kv_k"], cos, sin,
                                                                     batch, seq, tag + "_kv")
                    kv_out = kvt.reshape(batch, 4, KV_HEADS, HEAD_DIM, seq).transpose(0, 4, 1, 2, 3)
                    win_out = wint.reshape(batch, 2, KV_HEADS, HEAD_DIM, seq).transpose(0, 4, 1, 2, 3)
                    n_pages = seq // PAGE
                    nb = seq // SEL_BLOCK
                    kc, vc, _ = _compress(kvt.reshape(batch, 4, gw, seq), None, batch, n_pages,
                                          min(CMP_PAGES, n_pages), zeros_hi, p["cmp"], None, tag + "_cmp")
                else:
                    kv4, win = _kv_rows(x, g_kv, p["kv"], jnp.tile(cos, (batch, 1)), jnp.tile(sin, (batch, 1)),
                                        tag + "_kv")
                    kv_out = kv4.reshape(batch, seq, 4, KV_HEADS, HEAD_DIM)
                    win_out = win.reshape(batch, seq, 2, KV_HEADS, HEAD_DIM)
                    pages, table, n_pages, cache_win_t = past
                    nb = (pos0 + seq + SEL_BLOCK - 1) // SEL_BLOCK
                    n_tail = nb * (SEL_BLOCK // CMP_STRIDE) - n_pages * (PAGE // CMP_STRIDE)
                    tail = jnp.pad(kv4[:, :2 * gw].reshape(batch, seq, 2, KV_HEADS, HEAD_DIM),
                                   ((0, 0), (0, PAGE - seq), (0, 0), (0, 0), (0, 0)))
                    kc_t, vc_t, hi_t = _compress(_token_minor(tail), jnp.arange(batch, dtype=I32), batch, 1, 1, zeros_hi,
                                                 p["cmp"], n_tail - 1, tag + "_cmpt")
                    kc_m, vc_m, _ = _compress(pages, table, batch, n_pages, min(CMP_PAGES, n_pages), hi_t,
                                              p["cmp"], None, tag + "_cmp")
                    kc = jnp.concatenate([kc_m, kc_t[:, :n_tail]], axis=1)
                    vc = jnp.concatenate([vc_m, vc_t[:, :n_tail]], axis=1)
                nbp = -(-nb // 32) * 32
                kc_l = _cmp_layout(kc, nb, nbp)
                vct_l = _cmp_layout(vc, nb, nbp).transpose(0, 2, 1)
            bl = layer - N_A_LAYERS
            q_raw, gate_pre = _norm_proj(x, g[0], [p["b_q"][bl], p["b_gate"][bl]], nm + "_in")
            if past is None:
                o = _attn_prompt(q_raw, gate_pre, p["b_bgate"][bl], cos, sin, kc_l, vct_l, ks_b, vst, kw_b, vwt,
                                 batch, seq, nm + "_attn")
            else:
                o = _attn_sample(q_raw, gate_pre, p["b_bgate"][bl], cos, sin, kc_l, vct_l, pages, table,
                                 kv4, cache_win_t, win, batch, seq, n_pages, nm + "_attn")
            x = _proj_norm_res(o, p["b_out"][bl], g[1], x, nm + "_out")
        x, conv_new = _ffn(x, g[2], p["up"][layer], ffn_conv_w[layer], ffn_conv_b[layer], p["down"][layer], g[3],
                           conv_prev[layer], batch, seq, nm + "_ffn")
        convs.append(conv_new)
    return (x.reshape(batch, seq, d), kv_out, win_out, jnp.stack(cs), jnp.stack(ns), jnp.stack(ms), jnp.stack(convs))


def _past_views(cache_kv, cache_win_kv, page_table):
    return (_token_minor(cache_kv), page_table.reshape(-1), page_table.shape[1], _token_minor(cache_win_kv))


def kernel(x_prompt, x_sample, cache_kv, cache_win_kv, state_mlstm_C, state_mlstm_n, state_mlstm_m, state_conv,
           page_table, g_norms, w_a_in, b_a_gate, g_a_hnorm, w_a_out, g_kv, w_kv, cmp_pos, cmp_w1, cmp_b1, cmp_w2,
           cmp_b2, w_b_in, b_b_gate, w_b_out, w_ffn_up, ffn_conv_w, ffn_conv_b, w_ffn_down):
    p = _prep_weights(w_a_in, b_a_gate, w_a_out, w_kv, cmp_pos, cmp_w1, cmp_b1, cmp_w2, cmp_b2,
                      w_b_in, b_b_gate, w_b_out, w_ffn_up, w_ffn_down)
    dt = x_prompt.dtype
    bp, tp, _ = x_prompt.shape
    bs, ts, _ = x_sample.shape
    past_len = page_table.shape[1] * PAGE
    gw = KV_HEADS * HEAD_DIM
    shared = (p, g_norms, g_a_hnorm, g_kv, ffn_conv_w, ffn_conv_b)

    y_p, kv_p, win_p, c_p, n_p, m_p, conv_p = _trunk(
        x_prompt, 0, None, *shared,
        jnp.zeros((DEPTH, bp, CONV_W - 1, 2 * D_FF), dt),
        jnp.zeros((N_A_LAYERS, bp, M_HEADS, M_QK_DIM, M_V_DIM), dt),
        jnp.zeros((N_A_LAYERS, bp, M_HEADS, M_QK_DIM), dt), jnp.zeros((N_A_LAYERS, bp, M_HEADS), dt), "p")
    win_p = win_p[:, tp - min(WINDOW, tp):]

    past = _past_views(cache_kv, cache_win_kv, page_table)
    y_s, kv_s, win_s, c_s, n_s, m_s, conv_s = _trunk(
        x_sample, past_len, past, *shared, state_conv, state_mlstm_C, state_mlstm_n, state_mlstm_m, "s")
    win_all = jnp.concatenate([cache_win_kv, win_s], axis=1)
    win_s = win_all[:, win_all.shape[1] - min(WINDOW, win_all.shape[1]):]
    return (y_p, y_s, kv_p, kv_s, win_p, win_s, c_p, c_s, n_p, n_s, m_p, m_s, conv_p, conv_s)
```

```python
import functools
import math

import jax
import jax.numpy as jnp
from jax import lax
from jax.experimental import pallas as pl
from jax.experimental.pallas import tpu as pltpu

F32 = jnp.float32
BF16 = jnp.bfloat16
I32 = jnp.int32

D_MODEL = 1024
DEPTH = 4
N_A_LAYERS = 2
M_HEADS = 8
M_QK_DIM = 64
M_V_DIM = 128
GATE_CAP = 15.0
N_HEADS = 16
HEAD_DIM = 64
KV_HEADS = 4
HPG = 4
CMP_BLOCK = 32
CMP_STRIDE = 16
CMP_HIDDEN = 256
SEL_BLOCK = 64
N_SELECT = 16
WINDOW = 512
ROT_DIM = 16
ROPE_THETA = 500000.0
D_FF = 2816
CONV_W = 3
EPS = 1e-6
BIG = 1e9
NEG = -1e30
PAGE = 128

LANES = 128
SUBLANES = 8
VMEM_LIMIT = 56 * 1024 * 1024

MLSTM_CHUNK = 128
KEY_TILE = 512
Q_TILE = 128
CMP_PAGES = 16
ATT_PAGES = 16
FFN_TM = 512
FFN_TN = 1408


def _cparams(sem):
    return pltpu.CompilerParams(dimension_semantics=sem, vmem_limit_bytes=VMEM_LIMIT)


def _rms(x, g):
    return x * lax.rsqrt(jnp.mean(x * x, axis=-1, keepdims=True) + EPS) * g


def _nt(a, b):
    return lax.dot_general(a, b, (((1,), (1,)), ((), ())), preferred_element_type=F32)


def _dot(a, b):
    return jnp.dot(a, b, preferred_element_type=F32)


def _row_tile(m, pref):
    t = min(m, pref)
    while m % t:
        t //= 2
    return t


def _norm_proj_kernel(x_ref, g_ref, *refs, n_w, n_t):
    xn = _rms(x_ref[...], g_ref[...]).astype(BF16)
    n_in = n_w + n_t
    for w_ref, o_ref in zip(refs[:n_w], refs[n_in:n_in + n_w]):
        o_ref[...] = _dot(xn, w_ref[...]).astype(o_ref.dtype)
    for w_ref, o_ref in zip(refs[n_w:n_in], refs[n_in + n_w:]):
        yt = _nt(w_ref[...], xn)
        for u in range(o_ref.shape[0]):
            o_ref[u] = yt[:, u * LANES:(u + 1) * LANES]


def _norm_proj(x, g, ws, name, wts=()):
    m, d = x.shape
    tm = _row_tile(m, 512)
    assert not wts or tm % LANES == 0
    n_w, n_t = len(ws), len(wts)
    in_specs = [pl.BlockSpec((tm, d), lambda i: (i, 0)), pl.BlockSpec((1, d), lambda i: (0, 0))]
    in_specs += [pl.BlockSpec(w.shape, lambda i: (0, 0)) for w in (*ws, *wts)]
    out_specs = [pl.BlockSpec((tm, w.shape[1]), lambda i: (i, 0)) for w in ws]
    out_specs += [pl.BlockSpec((tm // LANES, w.shape[0], LANES), lambda i: (i, 0, 0)) for w in wts]
    out_shape = [jax.ShapeDtypeStruct((m, w.shape[1]), F32) for w in ws]
    out_shape += [jax.ShapeDtypeStruct((m // LANES, w.shape[0], LANES), F32) for w in wts]
    return pl.pallas_call(
        functools.partial(_norm_proj_kernel, n_w=n_w, n_t=n_t), grid=(m // tm,), in_specs=in_specs,
        out_specs=out_specs, out_shape=out_shape, compiler_params=_cparams(("parallel",)),
        name=name)(x, g.reshape(1, d), *ws, *wts)


def _proj_norm_res_kernel(a_ref, w_ref, g_ref, res_ref, o_ref):
    y = _dot(a_ref[...].astype(BF16), w_ref[...])
    o_ref[...] = res_ref[...] + _rms(y, g_ref[...])


def _proj_norm_res(a, w, g, res, name):
    m, k = a.shape
    d = w.shape[1]
    tm = _row_tile(m, 512)
    return pl.pallas_call(
        _proj_norm_res_kernel, grid=(m // tm,),
        in_specs=[pl.BlockSpec((tm, k), lambda i: (i, 0)), pl.BlockSpec((k, d), lambda i: (0, 0)),
                  pl.BlockSpec((1, d), lambda i: (0, 0)), pl.BlockSpec((tm, d), lambda i: (i, 0))],
        out_specs=pl.BlockSpec((tm, d), lambda i: (i, 0)),
        out_shape=jax.ShapeDtypeStruct((m, d), F32),
        compiler_params=_cparams(("parallel",)), name=name)(a, w, g.reshape(1, d), res)


def _mlstm_kernel(main_ref, gate_ref, *refs, rows, nc, k_minor):
    if k_minor:
        kt_ref, *refs = refs
    bg_ref, gh_ref, c0_ref, n0_ref, m0_ref, h_ref, c_ref, n_ref, m_ref, c_s, n_s, m_s = refs
    L = MLSTM_CHUNK
    nh = M_HEADS
    a_q = nh * M_QK_DIM
    v0 = a_q if k_minor else 2 * a_q
    o0 = v0 + nh * M_V_DIM
    cidx = pl.program_id(1)

    @pl.when(cidx == 0)
    def _():
        c_s[...] = c0_ref[...]
        n_s[...] = n0_ref[...]
        m_s[...] = m0_ref[...]

    main = main_ref[...]
    gp = gate_ref[...] + bg_ref[...]
    if rows < L:
        main = jnp.concatenate([main, jnp.zeros((L - rows, main.shape[1]), F32)], axis=0)
        gp = jnp.concatenate([gp, jnp.zeros((L - rows, LANES), F32)], axis=0)
    capped = GATE_CAP * jnp.tanh(gp / GATE_CAP)
    row1 = lax.broadcasted_iota(I32, (L, LANES), 0)
    real = row1 < rows
    ilog = jnp.where(real, capped, -jnp.inf)
    logf = jnp.where(real, jnp.minimum(capped, 0.0) - jnp.log1p(jnp.exp(-jnp.abs(capped))), 0.0)
    bh = logf
    k = 1
    while k < L:
        bh = bh + jnp.where(row1 >= k, pltpu.roll(bh, k, axis=0), 0.0)
        k *= 2
    bh = pltpu.roll(bh, LANES - nh, axis=1)
    c_all = ilog - bh
    cm = c_all
    k = 1
    while k < L:
        cm = jnp.maximum(cm, jnp.where(row1 >= k, pltpu.roll(cm, k, axis=0), -jnp.inf))
        k *= 2
    m_row = m_s[...]
    mt = bh + jnp.maximum(m_row, cm)
    m_new = mt[L - 1:L, :]
    b_last = bh[L - 1:L, :]
    w_inter = jnp.exp(bh + m_row - mt)
    u_all = bh - mt
    emt = jnp.exp(-mt)
    ws_all = jnp.exp(b_last - bh + ilog - m_new)
    decay = jnp.exp(b_last + m_row - m_new)
    c_t = c_all.T
    ws_t = ws_all.T

    rr = lax.broadcasted_iota(I32, (L, L), 0)
    cc = lax.broadcasted_iota(I32, (L, L), 1)
    causal = cc <= rr
    lo_half = lax.broadcasted_iota(I32, (L, LANES), 1) < M_QK_DIM
    lo_rows = lax.broadcasted_iota(I32, (2 * M_QK_DIM, LANES), 0) < M_QK_DIM
    ones_b = jnp.ones((L, LANES), BF16)
    assert L == LANES

    def lanes(a, x):
        return jnp.broadcast_to(a[:, x:x + 1], (L, LANES))

    for p in range(nh // 2):
        qp = main[:, p * LANES:(p + 1) * LANES] * (M_QK_DIM ** -0.5)
        if k_minor:
            kt = kt_ref[0, p * LANES:(p + 1) * LANES, :]
        else:
            kt = main[:, a_q + p * LANES:a_q + (p + 1) * LANES].T
        c_pair = c_s[2 * p:2 * p + 2].reshape(2 * M_QK_DIM, M_V_DIM)
        n_pair = n_s[p]
        kt_b = kt.astype(BF16)
        state_b = jnp.concatenate([c_pair, n_pair], axis=1).astype(BF16)
        for e in range(2):
            x = 2 * p + e
            qx_b = jnp.where(lo_half if e == 0 else jnp.logical_not(lo_half), qp, 0.0).astype(BF16)
            vx = main[:, v0 + x * M_V_DIM:v0 + (x + 1) * M_V_DIM]
            ox = main[:, o0 + x * M_V_DIM:o0 + (x + 1) * M_V_DIM]
            a = jnp.exp(jnp.where(causal, lanes(u_all, x) + c_t[x:x + 1, :], -jnp.inf)) * _dot(qx_b, kt_b)
            inter = _dot(qx_b, state_b)
            intra = _dot(a.astype(BF16), jnp.concatenate([vx.astype(BF16), ones_b], axis=1))
            wi = lanes(w_inter, x)
            num = wi * inter[:, :M_V_DIM] + intra[:, :M_V_DIM]
            den = wi * inter[:, M_V_DIM:] + intra[:, M_V_DIM:]
            h = num / jnp.maximum(jnp.abs(den), lanes(emt, x))
            hn = h * lax.rsqrt(jnp.mean(h * h, axis=-1, keepdims=True) + EPS)
            hn = hn * gh_ref[:, x * M_V_DIM:(x + 1) * M_V_DIM] * jax.nn.sigmoid(ox)
            h_ref[:, x * M_V_DIM:(x + 1) * M_V_DIM] = hn[:rows]
        kwt = kt * jnp.where(lo_rows, ws_t[2 * p:2 * p + 1, :], ws_t[2 * p + 1:2 * p + 2, :])
        vcat = main[:, v0 + 2 * p * M_V_DIM:v0 + (2 * p + 2) * M_V_DIM]
        upd = _dot(kwt.astype(BF16), jnp.concatenate([vcat.astype(BF16), ones_b], axis=1))
        dec_e = decay[:, 2 * p:2 * p + 1]
        dec_o = decay[:, 2 * p + 1:2 * p + 2]
        c_s[2 * p] = dec_e * c_pair[:M_QK_DIM] + upd[:M_QK_DIM, :M_V_DIM]
        c_s[2 * p + 1] = dec_o * c_pair[M_QK_DIM:] + upd[M_QK_DIM:, M_V_DIM:2 * M_V_DIM]
        n_s[p] = jnp.where(lo_rows, dec_e, dec_o) * n_pair + upd[:, 2 * M_V_DIM:]
    m_s[...] = m_new

    @pl.when(cidx == nc - 1)
    def _():
        c_ref[...] = c_s[...]
        n_ref[...] = n_s[...]
        m_ref[...] = m_s[...]


def _mlstm(main, gate, kt, b_gate, g_hnorm, c0, n0, m0, batch, seq, name):
    rows = min(seq, MLSTM_CHUNK)
    nc = seq // rows
    a_v = M_HEADS * M_V_DIM
    hp = M_HEADS // 2
    n_in = jnp.broadcast_to(n0.reshape(batch, hp, 2 * M_QK_DIM, 1), (batch, hp, 2 * M_QK_DIM, LANES))
    m_in = jnp.pad(m0, ((0, 0), (0, LANES - M_HEADS)))[:, None, :]
    k_minor = kt is not None
    assert not k_minor or rows == MLSTM_CHUNK == LANES
    kt_specs = [pl.BlockSpec((1, kt.shape[1], LANES), lambda b, c: (b * nc + c, 0, 0))] if k_minor else []
    h, c, n, m = pl.pallas_call(
        functools.partial(_mlstm_kernel, rows=rows, nc=nc, k_minor=k_minor), grid=(batch, nc),
        in_specs=[pl.BlockSpec((rows, main.shape[1]), lambda b, c: (b * nc + c, 0)),
                  pl.BlockSpec((rows, LANES), lambda b, c: (b * nc + c, 0)), *kt_specs,
                  pl.BlockSpec((1, LANES), lambda b, c: (0, 0)),
                  pl.BlockSpec((1, a_v), lambda b, c: (0, 0)),
                  pl.BlockSpec((None, M_HEADS, M_QK_DIM, M_V_DIM), lambda b, c: (b, 0, 0, 0)),
                  pl.BlockSpec((None, hp, 2 * M_QK_DIM, LANES), lambda b, c: (b, 0, 0, 0)),
                  pl.BlockSpec((None, 1, LANES), lambda b, c: (b, 0, 0))],
        out_specs=[pl.BlockSpec((rows, a_v), lambda b, c: (b * nc + c, 0)),
                   pl.BlockSpec((None, M_HEADS, M_QK_DIM, M_V_DIM), lambda b, c: (b, 0, 0, 0)),
                   pl.BlockSpec((None, hp, 2 * M_QK_DIM, LANES), lambda b, c: (b, 0, 0, 0)),
                   pl.BlockSpec((None, 1, LANES), lambda b, c: (b, 0, 0))],
        out_shape=[jax.ShapeDtypeStruct((batch * seq, a_v), F32),
                   jax.ShapeDtypeStruct((batch, M_HEADS, M_QK_DIM, M_V_DIM), F32),
                   jax.ShapeDtypeStruct((batch, hp, 2 * M_QK_DIM, LANES), F32),
                   jax.ShapeDtypeStruct((batch, 1, LANES), F32)],
        scratch_shapes=[pltpu.VMEM((M_HEADS, M_QK_DIM, M_V_DIM), F32), pltpu.VMEM((hp, 2 * M_QK_DIM, LANES), F32),
                        pltpu.VMEM((1, LANES), F32)],
        compiler_params=_cparams(("parallel", "arbitrary")), name=name,
    )(main, gate, *([kt] if k_minor else []), b_gate, g_hnorm.reshape(1, a_v), c0, n_in, m_in)
    return h, c, n[..., 0].reshape(batch, M_HEADS, M_QK_DIM), m[:, 0, :M_HEADS]


def _ffn_kernel(*refs, carry, tm, n_j, tiles_per_seq, period, mixer):
    if mixer:
        a_ref, wm_ref, g1_ref, *refs = refs
    x_ref, g2_ref, wua_ref, wug_ref, cwa_ref, cwg_ref, cba_ref, cbg_ref, wd_ref, g3_ref = refs[:10]
    if carry and mixer:
        inita_ref, initg_ref, o_ref, sa_ref, sg_ref, xn_s, acc_s, ue_s, carry_s, x_s = refs[10:]
        branch_in = ((wua_ref, cwa_ref, cba_ref, inita_ref, sa_ref), (wug_ref, cwg_ref, cbg_ref, initg_ref, sg_ref))
    elif carry:
        inita_ref, initg_ref, o_ref, sa_ref, sg_ref, xn_s, acc_s, ue_s, carry_s = refs[10:]
        branch_in = ((wua_ref, cwa_ref, cba_ref, inita_ref, sa_ref), (wug_ref, cwg_ref, cbg_ref, initg_ref, sg_ref))
    else:
        t1a_ref, t1g_ref, t2a_ref, t2g_ref, o_ref, sa_ref, sg_ref, xn_s, acc_s, ue_s = refs[10:]
        branch_in = ((wua_ref, cwa_ref, cba_ref, (t1a_ref, t2a_ref), sa_ref),
                     (wug_ref, cwg_ref, cbg_ref, (t1g_ref, t2g_ref), sg_ref))
    i = pl.program_id(0)
    j = pl.program_id(1)
    tn = wd_ref.shape[0]

    @pl.when(j == 0)
    def _():
        x = x_ref[...]
        if mixer:
            x = x + _rms(_dot(a_ref[...].astype(BF16), wm_ref[...]), g1_ref[...])
            x_s[...] = x
        xn_s[...] = _rms(x, g2_ref[...]).astype(BF16)
        acc_s[...] = jnp.zeros_like(acc_s)

    xn = xn_s[...]
    conv = []
    for which, (w_ref, cw_ref, cb_ref, boundary, s_ref) in enumerate(branch_in):
        u = _dot(xn, w_ref[...])
        ue_s[SUBLANES:, :] = u
        if carry:
            slot = which * n_j + j
            first = (i % tiles_per_seq) == 0
            ue_s[SUBLANES - 2:SUBLANES, :] = jnp.where(first, boundary[...], carry_s[slot])
            tap1 = ue_s[pl.ds(SUBLANES - 1, tm), :]
            tap2 = ue_s[pl.ds(SUBLANES - 2, tm), :]
            last2 = u[tm - 2:tm, :]
            carry_s[slot] = last2
            s_ref[i // tiles_per_seq, j] = last2
        else:
            ue_s[0:SUBLANES, :] = jnp.zeros((SUBLANES, tn), F32)
            t = lax.broadcasted_iota(I32, (tm, tn), 0) & (period - 1)
            tap1 = jnp.where(t >= 1, ue_s[pl.ds(SUBLANES - 1, tm), :], boundary[0][...])
            tap2 = jnp.where(t >= 2, ue_s[pl.ds(SUBLANES - 2, tm), :], boundary[1][...])
            s_ref[...] = u
        conv.append(cb_ref[...] + tap2 * cw_ref[0:1, :] + tap1 * cw_ref[1:2, :] + u * cw_ref[2:3, :])
    y = jax.nn.gelu(conv[0], approximate=True) * conv[1]
    acc_s[...] += _dot(y.astype(BF16), wd_ref[...])

    @pl.when(j == n_j - 1)
    def _():
        o_ref[...] = (x_s[...] if mixer else x_ref[...]) + _rms(acc_s[...], g3_ref[...])


def _ffn(x, g2, w_up, conv_w, conv_b, w_down, g3, prev, batch, seq, name, mixer=None):
    m, d = x.shape
    nf = w_down.shape[0]
    tn = FFN_TN
    n_j = nf // tn
    carry = seq >= 256
    cb = conv_b.reshape(1, 2 * nf)
    col_a = lambda i, j: (0, j)
    col_g = lambda i, j: (0, j + n_j)
    common_specs = [
        None,
        pl.BlockSpec((1, d), lambda i, j: (0, 0)),
        pl.BlockSpec((d, tn), col_a), pl.BlockSpec((d, tn), col_g),
        pl.BlockSpec((CONV_W, tn), col_a), pl.BlockSpec((CONV_W, tn), col_g),
        pl.BlockSpec((1, tn), col_a), pl.BlockSpec((1, tn), col_g),
        pl.BlockSpec((tn, d), lambda i, j: (j, 0)),
        pl.BlockSpec((1, d), lambda i, j: (0, 0)),
    ]
    common_args = [x, g2.reshape(1, d), w_up, w_up, conv_w, conv_w, cb, cb, w_down, g3.reshape(1, d)]
    assert mixer is None or carry
    if carry:
        tm = _row_tile(seq, FFN_TM)
        tps = seq // tm
        common_specs[0] = pl.BlockSpec((tm, d), lambda i, j: (i, 0))
        mix_specs, mix_args, mix_scratch = [], [], []
        if mixer is not None:
            a, wm, g1 = mixer
            mix_specs = [pl.BlockSpec((tm, a.shape[1]), lambda i, j: (i, 0)), pl.BlockSpec(wm.shape, lambda i, j: (0, 0)),
                         pl.BlockSpec((1, d), lambda i, j: (0, 0))]
            mix_args = [a, wm, g1.reshape(1, d)]
            mix_scratch = [pltpu.VMEM((tm, d), F32)]
        st_a = lambda i, j: (i // tps, 0, j)
        st_g = lambda i, j: (i // tps, 0, j + n_j)
        out, sa, sg = pl.pallas_call(
            functools.partial(_ffn_kernel, carry=True, tm=tm, n_j=n_j, tiles_per_seq=tps, period=seq,
                              mixer=mixer is not None),
            grid=(m // tm, n_j),
            in_specs=mix_specs + common_specs + [pl.BlockSpec((None, 2, tn), st_a), pl.BlockSpec((None, 2, tn), st_g)],
            out_specs=[pl.BlockSpec((tm, d), lambda i, j: (i, 0)),
                       pl.BlockSpec((batch, n_j, 2, tn), lambda i, j: (0, 0, 0, 0)),
                       pl.BlockSpec((batch, n_j, 2, tn), lambda i, j: (0, 0, 0, 0))],
            out_shape=[jax.ShapeDtypeStruct((m, d), F32), jax.ShapeDtypeStruct((batch, n_j, 2, tn), F32),
                       jax.ShapeDtypeStruct((batch, n_j, 2, tn), F32)],
            scratch_shapes=[pltpu.VMEM((tm, d), BF16), pltpu.VMEM((tm, d), F32),
                            pltpu.VMEM((tm + SUBLANES, tn), F32), pltpu.VMEM((2 * n_j, 2, tn), F32)] + mix_scratch,
            compiler_params=_cparams(("arbitrary", "arbitrary")), name=name,
        )(*mix_args, *common_args, prev, prev)
        sa, sg = (s.transpose(0, 2, 1, 3).reshape(batch, 2, nf) for s in (sa, sg))
        return out, jnp.concatenate([sa, sg], axis=-1)
    tm = m
    assert seq >= 2 and seq & (seq - 1) == 0
    common_specs[0] = pl.BlockSpec((tm, d), lambda i, j: (i, 0))
    tap1 = jnp.pad(prev[:, 1:2], ((0, 0), (0, seq - 1), (0, 0))).reshape(m, 2 * nf)
    tap2 = jnp.pad(prev, ((0, 0), (0, seq - 2), (0, 0))).reshape(m, 2 * nf)
    row_a = lambda i, j: (i, j)
    row_g = lambda i, j: (i, j + n_j)
    out, ua, ug = pl.pallas_call(
        functools.partial(_ffn_kernel, carry=False, tm=tm, n_j=n_j, tiles_per_seq=1, period=seq, mixer=False),
        grid=(m // tm, n_j),
        in_specs=common_specs + [pl.BlockSpec((tm, tn), row_a), pl.BlockSpec((tm, tn), row_g),
                                 pl.BlockSpec((tm, tn), row_a), pl.BlockSpec((tm, tn), row_g)],
        out_specs=[pl.BlockSpec((tm, d), lambda i, j: (i, 0)), pl.BlockSpec((tm, tn), row_a),
                   pl.BlockSpec((tm, tn), row_a)],
        out_shape=[jax.ShapeDtypeStruct((m, d), F32), jax.ShapeDtypeStruct((m, nf), F32),
                   jax.ShapeDtypeStruct((m, nf), F32)],
        scratch_shapes=[pltpu.VMEM((tm, d), BF16), pltpu.VMEM((tm, d), F32), pltpu.VMEM((tm + SUBLANES, tn), F32)],
        compiler_params=_cparams(("arbitrary", "arbitrary")), name=name,
    )(*common_args, tap1, tap1, tap2, tap2)
    u = jnp.concatenate([ua, ug], axis=-1).reshape(batch, seq, 2 * nf)
    return out, u[:, seq - 2:]


def _rope_pair(x, cos, sin):
    half = ROT_DIM // 2
    lane = lax.broadcasted_iota(I32, x.shape, 1) & (HEAD_DIM - 1)
    partner = jnp.where(lane < half, pltpu.roll(x, LANES - half, axis=1), pltpu.roll(x, half, axis=1))
    return x * cos + partner * sin


def _rope(x, cos, sin):
    return jnp.concatenate(
        [_rope_pair(x[:, c * LANES:(c + 1) * LANES], cos, sin) for c in range(x.shape[1] // LANES)], axis=1)


def _rope_tables(pos0, seq):
    half = ROT_DIM // 2
    inv = jnp.power(jnp.float32(ROPE_THETA), -jnp.arange(0, ROT_DIM, 2, dtype=F32) / ROT_DIM)
    ang = (pos0 + jnp.arange(seq, dtype=I32)).astype(F32)[:, None] * inv[None, :]
    cos, sin = jnp.cos(ang), jnp.sin(ang)
    rest = HEAD_DIM - ROT_DIM
    cos_h = jnp.concatenate([cos, cos, jnp.ones((seq, rest), F32)], axis=1)
    sin_h = jnp.concatenate([-sin, sin, jnp.zeros((seq, rest), F32)], axis=1)
    return jnp.tile(cos_h, (1, LANES // HEAD_DIM)), jnp.tile(sin_h, (1, LANES // HEAD_DIM))


def _kv_kernel(x_ref, g_ref, w_ref, cos_ref, sin_ref, kv_ref, win_ref):
    xn = _rms(x_ref[...], g_ref[...]).astype(BF16)
    y = _dot(xn, w_ref[...])
    cos, sin = cos_ref[...], sin_ref[...]
    gw = KV_HEADS * HEAD_DIM
    ks = _rope(y[:, 2 * gw:3 * gw], cos, sin)
    kw = _rope(y[:, 4 * gw:5 * gw], cos, sin)
    kv_ref[:, 0:2 * gw] = y[:, 0:2 * gw]
    kv_ref[:, 2 * gw:3 * gw] = ks
    kv_ref[:, 3 * gw:4 * gw] = y[:, 3 * gw:4 * gw]
    win_ref[:, 0:gw] = kw
    win_ref[:, gw:2 * gw] = y[:, 5 * gw:6 * gw]


def _rope_rows(x, cos_t, sin_t):
    half = ROT_DIM // 2
    row = lax.broadcasted_iota(I32, x.shape, 0) & (HEAD_DIM - 1)
    partner = jnp.where(row < half, pltpu.roll(x, x.shape[0] - half, axis=0), pltpu.roll(x, half, axis=0))
    return x * cos_t + partner * sin_t


def _kv_minor_kernel(x_ref, g_ref, wt_ref, wk_ref, cos_ref, sin_ref, cost_ref, sint_ref,
                     kvt_ref, wint_ref, ks_ref, kw_ref, vst_ref, vwt_ref):
    xn = _rms(x_ref[...], g_ref[...]).astype(BF16)
    gw = KV_HEADS * HEAD_DIM
    yt = _nt(wt_ref[...], xn)
    cos_t, sin_t = cost_ref[...], sint_ref[...]

    def rope_t(a):
        return jnp.concatenate(
            [_rope_rows(a[c * LANES:(c + 1) * LANES], cos_t, sin_t) for c in range(gw // LANES)], axis=0)

    kvt_ref[0:2 * gw, :] = yt[0:2 * gw]
    kvt_ref[2 * gw:3 * gw, :] = rope_t(yt[2 * gw:3 * gw])
    kvt_ref[3 * gw:4 * gw, :] = yt[3 * gw:4 * gw]
    wint_ref[0:gw, :] = rope_t(yt[4 * gw:5 * gw])
    wint_ref[gw:2 * gw, :] = yt[5 * gw:6 * gw]
    vst_ref[...] = yt[3 * gw:4 * gw].astype(BF16)
    for u in range(vwt_ref.shape[0]):
        vwt_ref[u] = yt[5 * gw:6 * gw, u * PAGE:(u + 1) * PAGE].astype(BF16)
    yk = _dot(xn, wk_ref[...])
    cos, sin = cos_ref[...], sin_ref[...]
    ks_ref[...] = _rope(yk[:, 0:gw], cos, sin).astype(BF16)
    kw_ref[...] = _rope(yk[:, gw:2 * gw], cos, sin).astype(BF16)


def _kv_rows_minor(x, g_kv, w_t, w_k, cos, sin, batch, seq, name):
    m, d = x.shape
    tm = _row_tile(seq, KEY_TILE)
    gw = KV_HEADS * HEAD_DIM
    tt = seq // tm
    tok = lambda i: (i % tt, 0)
    tok_t = lambda i: (0, i % tt)
    return pl.pallas_call(
        _kv_minor_kernel, grid=(m // tm,),
        in_specs=[pl.BlockSpec((tm, d), lambda i: (i, 0)), pl.BlockSpec((1, d), lambda i: (0, 0)),
                  pl.BlockSpec(w_t.shape, lambda i: (0, 0)), pl.BlockSpec(w_k.shape, lambda i: (0, 0)),
                  pl.BlockSpec((tm, LANES), tok), pl.BlockSpec((tm, LANES), tok),
                  pl.BlockSpec((LANES, tm), tok_t), pl.BlockSpec((LANES, tm), tok_t)],
        out_specs=[pl.BlockSpec((None, 4 * gw, tm), lambda i: (i // tt, 0, i % tt)),
                   pl.BlockSpec((None, 2 * gw, tm), lambda i: (i // tt, 0, i % tt)),
                   pl.BlockSpec((tm, gw), lambda i: (i, 0)), pl.BlockSpec((tm, gw), lambda i: (i, 0)),
                   pl.BlockSpec((None, gw, tm), lambda i: (i, 0, 0)),
                   pl.BlockSpec((tm // PAGE, gw, PAGE), lambda i: (i, 0, 0))],
        out_shape=[jax.ShapeDtypeStruct((batch, 4 * gw, seq), F32), jax.ShapeDtypeStruct((batch, 2 * gw, seq), F32),
                   jax.ShapeDtypeStruct((m, gw), BF16), jax.ShapeDtypeStruct((m, gw), BF16),
                   jax.ShapeDtypeStruct((m // tm, gw, tm), BF16), jax.ShapeDtypeStruct((m // PAGE, gw, PAGE), BF16)],
        compiler_params=_cparams(("parallel",)), name=name)(x, g_kv.reshape(1, d), w_t, w_k, cos, sin, cos.T, sin.T)


def _kv_rows(x, g_kv, w_kv, cos, sin, name):
    m, d = x.shape
    gw = KV_HEADS * HEAD_DIM
    return pl.pallas_call(
        _kv_kernel, grid=(1,),
        in_specs=[pl.BlockSpec((m, d), lambda i: (0, 0)), pl.BlockSpec((1, d), lambda i: (0, 0)),
                  pl.BlockSpec(w_kv.shape, lambda i: (0, 0)),
                  pl.BlockSpec((m, LANES), lambda i: (0, 0)), pl.BlockSpec((m, LANES), lambda i: (0, 0))],
        out_specs=[pl.BlockSpec((m, 4 * gw), lambda i: (0, 0)), pl.BlockSpec((m, 2 * gw), lambda i: (0, 0))],
        out_shape=[jax.ShapeDtypeStruct((m, 4 * gw), F32), jax.ShapeDtypeStruct((m, 2 * gw), F32)],
        compiler_params=_cparams(("arbitrary",)), name=name)(x, g_kv.reshape(1, d), w_kv, cos, sin)


def _token_minor(rows):
    n_p, r, n = rows.shape[0], rows.shape[1], rows.shape[2]
    return rows.transpose(0, 2, 3, 4, 1).reshape(n_p, n, KV_HEADS * HEAD_DIM, r)


def _compress_kernel(tbl_ref, *refs, n_pg, zero_after):
    page_refs = refs[:n_pg]
    (pos_lo_ref, pos_hi_ref, w1lo_ref, w1hi_ref, b1_ref, w2_ref, b2_ref, hi_init_ref,
     kc_ref, vc_ref, hi_first_ref, carry_s, t_s, x_s) = refs[n_pg:]
    c = pl.program_id(1)
    njp = PAGE // CMP_STRIDE
    njc = n_pg * njp

    @pl.when(c == 0)
    def _():
        carry_s[...] = hi_init_ref[...]

    lo_lanes = lax.broadcasted_iota(I32, (njp, LANES), 1) < HEAD_DIM
    for u in range(n_pg):
        for br in range(2):
            for cp in range(KV_HEADS // 2):
                tb = t_s.at[(2 * u + br) % 2 * 2 + cp]
                tb[...] = page_refs[u][br, cp * LANES:(cp + 1) * LANES, :].T
                for q in range(CMP_STRIDE // 2):
                    p0 = tb[pl.ds(2 * q, njp, stride=CMP_STRIDE), :]
                    p1 = tb[pl.ds(2 * q + 1, njp, stride=CMP_STRIDE), :]
                    rows = slice(u * njp, (u + 1) * njp)
                    cols = slice(q * LANES, (q + 1) * LANES)
                    x_s[br * KV_HEADS + 2 * cp, rows, cols] = jnp.where(lo_lanes, p0, pltpu.roll(p1, HEAD_DIM, axis=1))
                    x_s[br * KV_HEADS + 2 * cp + 1, rows, cols] = jnp.where(lo_lanes, pltpu.roll(p0, HEAD_DIM, axis=1), p1)

    rowl = lax.broadcasted_iota(I32, (KV_HEADS * njc, CMP_HIDDEN), 0) & (njc - 1)
    for br, out_ref in enumerate((kc_ref, vc_ref)):
        x = x_s[br * KV_HEADS:(br + 1) * KV_HEADS].reshape(KV_HEADS * njc, CMP_STRIDE * HEAD_DIM)
        lo = _dot((x + pos_lo_ref[br]).astype(BF16), w1lo_ref[br])
        hi = _dot((x + pos_hi_ref[br]).astype(BF16), w1hi_ref[br])
        carry_rows = jnp.concatenate(
            [jnp.broadcast_to(carry_s[br * KV_HEADS + g:br * KV_HEADS + g + 1, :], (njc, CMP_HIDDEN))
             for g in range(KV_HEADS)], axis=0)
        hi_next = jnp.where(rowl == njc - 1, carry_rows, pltpu.roll(hi, KV_HEADS * njc - 1, axis=0))
        if zero_after is not None:
            hi_next = jnp.where(jnp.logical_and(rowl == zero_after, c == 0), 0.0, hi_next)
        for g in range(KV_HEADS):
            carry_s[br * KV_HEADS + g:br * KV_HEADS + g + 1, :] = hi[g * njc:g * njc + 1, :]
        h = jax.nn.gelu(lo + hi_next + b1_ref[br], approximate=True)
        o = _dot(h.astype(BF16), w2_ref[br]) + b2_ref[br]
        for g in range(KV_HEADS):
            out_ref[:, g * HEAD_DIM:(g + 1) * HEAD_DIM] = o[g * njc:(g + 1) * njc, :]
    hi_first_ref[...] = carry_s[...]


def _compress(pages, table, batch, n_pages, n_pg, hi_init, cw, zero_after, name):
    njp = PAGE // CMP_STRIDE
    njc = n_pg * njp
    assert njc & (njc - 1) == 0 and njc % SUBLANES == 0 and n_pages % n_pg == 0
    n_ch = n_pages // n_pg
    gw = KV_HEADS * HEAD_DIM
    flat = CMP_STRIDE * HEAD_DIM

    def page_map(u):
        if table is None:
            return lambda b, c, tbl: (b, 0, 0, (n_ch - 1 - c) * n_pg + u)
        return lambda b, c, tbl: (tbl[b * n_pages + (n_ch - 1 - c) * n_pg + u], 0, 0, 0)

    full3 = lambda b, c, tbl: (0, 0, 0)
    in_specs = [pl.BlockSpec((None, 2, gw, PAGE), page_map(u)) for u in range(n_pg)]
    in_specs += [pl.BlockSpec((2, 1, flat), full3), pl.BlockSpec((2, 1, flat), full3),
                 pl.BlockSpec((2, flat, CMP_HIDDEN), full3), pl.BlockSpec((2, flat, CMP_HIDDEN), full3),
                 pl.BlockSpec((2, 1, CMP_HIDDEN), full3), pl.BlockSpec((2, CMP_HIDDEN, HEAD_DIM), full3),
                 pl.BlockSpec((2, 1, HEAD_DIM), full3),
                 pl.BlockSpec((None, 2 * KV_HEADS, CMP_HIDDEN), lambda b, c, tbl: (b, 0, 0))]
    out_map = lambda b, c, tbl: (b, n_ch - 1 - c, 0)
    grid_spec = pltpu.PrefetchScalarGridSpec(
        num_scalar_prefetch=1, grid=(batch, n_ch), in_specs=in_specs,
        out_specs=[pl.BlockSpec((None, njc, gw), out_map), pl.BlockSpec((None, njc, gw), out_map),
                   pl.BlockSpec((None, 2 * KV_HEADS, CMP_HIDDEN), lambda b, c, tbl: (b, 0, 0))],
        scratch_shapes=[pltpu.VMEM((2 * KV_HEADS, CMP_HIDDEN), F32), pltpu.VMEM((4, PAGE, LANES), F32),
                        pltpu.VMEM((2 * KV_HEADS, njc, flat), F32)])
    return pl.pallas_call(
        functools.partial(_compress_kernel, n_pg=n_pg, zero_after=zero_after), grid_spec=grid_spec,
        out_shape=[jax.ShapeDtypeStruct((batch, n_ch * njc, gw), F32), jax.ShapeDtypeStruct((batch, n_ch * njc, gw), F32),
                   jax.ShapeDtypeStruct((batch, 2 * KV_HEADS, CMP_HIDDEN), F32)],
        compiler_params=_cparams(("parallel", "arbitrary")), name=name,
    )(jnp.zeros((1,), I32) if table is None else table, *([pages] * n_pg), cw["pos_lo"], cw["pos_hi"], cw["w1lo"], cw["w1hi"], cw["b1"], cw["w2"], cw["b2"], hi_init)


def _cmp_layout(kc_nat, nb, nbp):
    b = kc_nat.shape[0]
    x = kc_nat[:, :4 * nb].reshape(b, nb, 4, kc_nat.shape[-1]).transpose(0, 2, 1, 3)
    x = jnp.pad(x, ((0, 0), (0, 0), (0, nbp - nb), (0, 0)))
    return x.reshape(b, 4 * nbp, kc_nat.shape[-1]).astype(BF16)


SCALE = HEAD_DIM ** -0.5
QSCALE = SCALE * math.log2(math.e)
SUM_ROWS = 16
ACC_ROWS = HEAD_DIM + SUM_ROWS


def _group_queries(q, g):
    nq = q.shape[0]
    keep = (lax.broadcasted_iota(I32, (nq, LANES), 1) >> 6) == (g % 2)
    pieces = []
    for hh in range(HPG):
        h = HPG * g + hh
        chunk = q[:, (h // 2) * LANES:(h // 2 + 1) * LANES]
        if h % 2 != g % 2:
            chunk = pltpu.roll(chunk, HEAD_DIM, axis=1)
        pieces.append(jnp.where(keep, chunk, 0.0))
    return (jnp.concatenate(pieces, axis=0) * QSCALE).astype(BF16)


def _kchunk(k, g):
    return k[:, (g // 2) * LANES:(g // 2 + 1) * LANES]


def _values(vt, g):
    return jnp.concatenate([vt[g * HEAD_DIM:(g + 1) * HEAD_DIM, :], jnp.ones((SUM_ROWS, vt.shape[1]), BF16)], axis=0)


def _flash_groups(ss, vt, m_s, acc_s):
    ps, alphas = [], []
    for g in range(KV_HEADS):
        m_old = m_s[g]
        m_new = jnp.maximum(m_old, jnp.max(ss[g], axis=0, keepdims=True))
        alphas.append(jnp.exp2(m_old - m_new))
        ps.append(jnp.exp2(ss[g] - m_new).astype(BF16))
        m_s[g] = m_new
    for g in range(KV_HEADS):
        acc_s[g] = acc_s[g] * alphas[g] + _dot(_values(vt, g), ps[g])


def _one_shot_groups(ss, vt):
    ps = [jnp.exp2(s - jnp.max(s, axis=0, keepdims=True)).astype(BF16) for s in ss]
    return [_dot(_values(vt, g), ps[g]) for g in range(KV_HEADS)]


def _init_state(m_s, acc_s):
    m_s[...] = jnp.full(m_s.shape, NEG, F32)
    acc_s[...] = jnp.zeros(acc_s.shape, F32)


def _topk_bias(score, n_sel):
    n_iota = lax.broadcasted_iota(I32, score.shape, 0)

    def body(_, sc):
        mx = jnp.max(sc, axis=0, keepdims=True)
        idx = jnp.min(jnp.where(sc == mx, n_iota, score.shape[0]), axis=0, keepdims=True)
        return jnp.where(n_iota == idx, -jnp.inf, sc)

    left = lax.fori_loop(0, n_sel, body, score)
    return jnp.where(jnp.logical_and(left == -jnp.inf, score > -jnp.inf), 0.0, NEG)


def _cmp_valid(qpos_w, nb, nbp):
    n_w = lax.broadcasted_iota(I32, (nbp, qpos_w.shape[1]), 0)
    return jnp.concatenate(
        [jnp.logical_and((4 * n_w + c) * CMP_STRIDE + (CMP_BLOCK - 1) <= qpos_w, n_w < nb) for c in range(4)], axis=0)


def _cmp_branch(s, valid, vct_ref, g, qpos_w, qpos_q, nb, nbp):
    s = jnp.where(valid, s, NEG)
    e = jnp.exp2(s - jnp.max(s, axis=0, keepdims=True))
    some = (qpos_w >= CMP_BLOCK - 1).astype(F32)
    p = e * (some / jnp.maximum(jnp.sum(e, axis=0, keepdims=True), 1e-30))
    o_cmp = _dot(vct_ref[g * HEAD_DIM:(g + 1) * HEAD_DIM, :], p.astype(BF16))
    ps = ((p[:, 0:LANES] + p[:, LANES:2 * LANES]) + p[:, 2 * LANES:3 * LANES]) + p[:, 3 * LANES:4 * LANES]
    parts = [ps[c * nbp:(c + 1) * nbp] for c in range(4)]
    n_q = lax.broadcasted_iota(I32, (nbp, LANES), 0)
    prev = jnp.where(n_q >= 1, pltpu.roll(parts[3], 1, axis=0), 0.0)
    score = (((parts[0] + parts[1]) + parts[2]) + parts[3]) + prev
    cur = qpos_q >> 6
    forced = (n_q == 0) | (n_q == cur) | (n_q == cur - 1)
    score = jnp.where(forced, BIG, jnp.where(n_q * SEL_BLOCK <= qpos_q, score, -BIG))
    score = jnp.where(n_q < nb, score, -jnp.inf)
    return o_cmp, score


def _compressed_and_select(q, kc_ref, vct_ref, qpos_w, qpos_q, ocmp_s, score_s, bias_s, nb, nbp, n_eff):
    kc_all = jnp.concatenate([kc_ref[c * nbp:c * nbp + n_eff, :] for c in range(4)], axis=0)
    vct_all = jnp.concatenate([vct_ref[:, c * nbp:c * nbp + n_eff] for c in range(4)], axis=1)
    cs = [_nt(_kchunk(kc_all, g), _group_queries(q, g)) for g in range(KV_HEADS)]
    valid = _cmp_valid(qpos_w, nb, n_eff)
    for g in range(KV_HEADS):
        o_cmp, score = _cmp_branch(cs[g], valid, vct_all, g, qpos_w, qpos_q, nb, n_eff)
        ocmp_s[g] = o_cmp
        score_s[0:n_eff, g * LANES:(g + 1) * LANES] = score
    bias = _topk_bias(score_s[0:n_eff, :], min(N_SELECT, nb))
    for g in range(KV_HEADS):
        b = bias[:, g * LANES:(g + 1) * LANES]
        bias_s[g, 0:n_eff, :] = jnp.concatenate([b] * HPG, axis=1)
        if n_eff < nbp:
            bias_s[g, n_eff:nbp, :] = jnp.full((nbp - n_eff, HPG * LANES), NEG, F32)


def _block_bias(bias_rows, n_blk):
    w = bias_rows.shape[1]
    return jnp.concatenate([jnp.broadcast_to(bias_rows[u:u + 1, :], (SEL_BLOCK, w)) for u in range(n_blk)], axis=0)


def _gate_row(gates_t, c, g):
    return jnp.concatenate(
        [gates_t[(c * HPG + hh) * KV_HEADS + g:(c * HPG + hh) * KV_HEADS + g + 1, :] for hh in range(HPG)], axis=1)


def _finish(o_ref, acc_s, ocmp_s, wins, ot_s, gates_t):
    nq = o_ref.shape[0]
    for g in range(KV_HEADS):
        o = ocmp_s[g] * _gate_row(gates_t, 0, g)
        for c, acc in ((1, acc_s[g]), (2, wins[g])):
            o = o + acc[:HEAD_DIM] * (_gate_row(gates_t, c, g) / acc[HEAD_DIM:HEAD_DIM + 1])
        ot_s[g * HEAD_DIM:(g + 1) * HEAD_DIM, :] = o
    o_t = ot_s[...].T
    for hh in range(HPG):
        o_ref[:, hh * KV_HEADS * HEAD_DIM:(hh + 1) * KV_HEADS * HEAD_DIM] = o_t[hh * nq:(hh + 1) * nq, :]


def _attn_prompt_kernel(q_ref, gp_ref, bg_ref, cos_ref, sin_ref, kc_ref, vct_ref, ks_ref, vst_ref, kw_ref, vwt_ref,
                        o_ref, m_s, acc_s, score_s, bias_s, ocmp_s, ot_s, qzr_s, *, nb, nbp):
    i = pl.program_id(1)
    nq = Q_TILE
    w = HPG * nq
    s0 = i * nq
    q = q_ref[...]
    q_rot = _rope(q, cos_ref[...], sin_ref[...])
    gates_t = jax.nn.sigmoid(gp_ref[...] + bg_ref[...]).T
    qpos_w = s0 + (lax.broadcasted_iota(I32, (1, w), 1) & (nq - 1))
    qpos_q = s0 + lax.broadcasted_iota(I32, (1, LANES), 1)
    _init_state(m_s, acc_s)
    for g in range(KV_HEADS):
        qzr_s[g] = _group_queries(q_rot, g)
    n_cls = 4 if nbp % (4 * SUBLANES) == 0 and nbp // 4 >= N_SELECT else 1
    per_cls = nbp // n_cls
    cls = jnp.minimum((2 * (i + 1) + per_cls - 1) // per_cls, n_cls) - 1
    for k in range(n_cls):
        @pl.when(cls == k)
        def _(k=k):
            _compressed_and_select(q, kc_ref, vct_ref, qpos_w, qpos_q, ocmp_s, score_s, bias_s, nb, nbp,
                                   (k + 1) * per_cls)

    blk_per_tile = KEY_TILE // SEL_BLOCK

    def slc_tile(t, causal, n_tiles=1):
        keys = n_tiles * KEY_TILE
        k0 = pl.multiple_of(t * KEY_TILE, KEY_TILE)
        kt = ks_ref[pl.ds(k0, keys), :]
        vts = vst_ref[pl.ds(t, n_tiles)]
        vt = vts[0] if n_tiles == 1 else jnp.concatenate([vts[u] for u in range(n_tiles)], axis=1)
        ss = []
        for g in range(KV_HEADS):
            rows = bias_s[g, pl.ds(pl.multiple_of(t * blk_per_tile, blk_per_tile), n_tiles * blk_per_tile), :]
            s = _nt(_kchunk(kt, g), qzr_s[g]) + _block_bias(rows, n_tiles * blk_per_tile)
            if causal:
                s = jnp.where(k0 + lax.broadcasted_iota(I32, (keys, w), 0) <= qpos_w, s, NEG)
            ss.append(s)
        _flash_groups(ss, vt, m_s, acc_s)

    t_diag = s0 // KEY_TILE

    def tile_pair(t2, carry):
        slc_tile(2 * t2, False, 2)
        return carry

    lax.fori_loop(0, t_diag // 2, tile_pair, 0)

    @pl.when(t_diag % 2 == 1)
    def _():
        slc_tile(t_diag - 1, False)

    slc_tile(t_diag, True)

    n_wt = (WINDOW + nq) // PAGE
    t0 = jnp.maximum(i - WINDOW // PAGE, 0)
    k0 = pl.multiple_of(t0 * PAGE, PAGE)
    kwin = kw_ref[pl.ds(k0, n_wt * PAGE), :]
    vwin_tiles = vwt_ref[pl.ds(t0, n_wt)]
    vwin = jnp.concatenate([vwin_tiles[u] for u in range(n_wt)], axis=1)
    kpos = k0 + lax.broadcasted_iota(I32, (n_wt * PAGE, w), 0)
    wmask = jnp.logical_and(kpos <= qpos_w, qpos_w - kpos < WINDOW)
    ws = [jnp.where(wmask, _nt(_kchunk(kwin, g), qzr_s[g]), NEG) for g in range(KV_HEADS)]
    _finish(o_ref, acc_s, ocmp_s, _one_shot_groups(ws, vwin), ot_s, gates_t)


def _attn_prompt(q, gate_pre, b_gate, cos, sin, kc, vct, ks, vst, kw, vwt, batch, seq, name):
    assert seq % KEY_TILE == 0 and seq >= WINDOW + Q_TILE and Q_TILE == LANES
    m, d = q.shape
    nq = Q_TILE
    w = HPG * nq
    nqb = seq // nq
    nb = seq // SEL_BLOCK
    nbp = kc.shape[1] // 4
    gw = KV_HEADS * HEAD_DIM
    ntile = seq // KEY_TILE
    row_map = lambda b, i: (b * nqb + i, 0)
    per_b2 = lambda b, i: (b, 0)
    per_b3 = lambda b, i: (b, 0, 0)
    return pl.pallas_call(
        functools.partial(_attn_prompt_kernel, nb=nb, nbp=nbp), grid=(batch, nqb),
        in_specs=[pl.BlockSpec((nq, d), row_map), pl.BlockSpec((nq, LANES), row_map),
                  pl.BlockSpec((1, LANES), lambda b, i: (0, 0)),
                  pl.BlockSpec((nq, LANES), lambda b, i: (i, 0)), pl.BlockSpec((nq, LANES), lambda b, i: (i, 0)),
                  pl.BlockSpec((None, 4 * nbp, gw), per_b3), pl.BlockSpec((None, gw, 4 * nbp), per_b3),
                  pl.BlockSpec((seq, gw), per_b2), pl.BlockSpec((ntile, gw, KEY_TILE), per_b3),
                  pl.BlockSpec((seq, gw), per_b2), pl.BlockSpec((seq // PAGE, gw, PAGE), per_b3)],
        out_specs=pl.BlockSpec((nq, d), row_map),
        out_shape=jax.ShapeDtypeStruct((m, d), F32),
        scratch_shapes=[pltpu.VMEM((KV_HEADS, 1, w), F32), pltpu.VMEM((KV_HEADS, ACC_ROWS, w), F32),
                        pltpu.VMEM((nbp, KV_HEADS * LANES), F32), pltpu.VMEM((KV_HEADS, nbp, w), F32),
                        pltpu.VMEM((KV_HEADS, HEAD_DIM, w), F32), pltpu.VMEM((gw, w), F32),
                        pltpu.VMEM((KV_HEADS, w, LANES), BF16)],
        compiler_params=_cparams(("parallel", "arbitrary")), name=name,
    )(q, gate_pre, b_gate, cos, sin, kc, vct, ks, vst, kw, vwt)


def _pad_rows(x, n):
    return jnp.concatenate([x, jnp.zeros((n - x.shape[0], x.shape[1]), x.dtype)], axis=0)


def _sample_queries(q):
    n = q.shape[0]
    half = lax.broadcasted_iota(I32, (n, LANES), 1) >> 6
    zeros = jnp.zeros((n, LANES), F32)
    blocks = []
    for g in range(KV_HEADS):
        for hh in range(HPG):
            h = HPG * g + hh
            chunk = q[:, (h // 2) * LANES:(h // 2 + 1) * LANES]
            if h % 2 != g % 2:
                chunk = pltpu.roll(chunk, HEAD_DIM, axis=1)
            blk = jnp.where(half == (g % 2), chunk, 0.0)
            blocks.append(jnp.concatenate([blk, zeros] if g < 2 else [zeros, blk], axis=1))
    return (jnp.concatenate(blocks, axis=0) * QSCALE).astype(BF16)


def _with_ones(vt):
    return jnp.concatenate([vt, jnp.ones((SUM_ROWS, vt.shape[1]), BF16)], axis=0)


def _attn_sample_kernel(tbl_ref, *refs, n_pg, n_steps, nb, nbp, pos0, n_new):
    k_refs = refs[:n_pg]
    vt_refs = refs[n_pg:2 * n_pg]
    (q_ref, gp_ref, bg_ref, cos_ref, sin_ref, kc_ref, vct_ref, kvn_ref, cwin_ref, wnew_ref,
     o_ref, m_s, acc_s, bias_s, ocmp_s, qzr_s, gt_s) = refs[2 * n_pg:]
    step = pl.program_id(1)
    gw = KV_HEADS * HEAD_DIM
    per_g = HPG * n_new
    lane1 = lax.broadcasted_iota(I32, (1, LANES), 1)
    qpos = pos0 + (lane1 & (n_new - 1))

    @pl.when(step == 0)
    def _():
        q = q_ref[...]
        qzr_s[...] = _sample_queries(_rope(q, cos_ref[...], sin_ref[...]))
        gt_s[...] = _pad_rows(jax.nn.sigmoid(gp_ref[...] + bg_ref[...]), LANES).T
        m_s[...] = jnp.full(m_s.shape, NEG, F32)
        acc_s[...] = jnp.zeros(acc_s.shape, F32)
        valid = _cmp_valid(qpos, nb, nbp)
        sc = jnp.where(valid, _nt(kc_ref[...], _sample_queries(q)), NEG)
        e = jnp.exp2(sc - jnp.max(sc, axis=0, keepdims=True))
        some = (qpos >= CMP_BLOCK - 1).astype(F32)
        p = e * (some / jnp.maximum(jnp.sum(e, axis=0, keepdims=True), 1e-30))
        ocmp_s[...] = _dot(vct_ref[...], p.astype(BF16))
        in_strip = lax.broadcasted_iota(I32, p.shape, 1) & (per_g - 1)
        ps = p
        for k in range(1, HPG):
            sh = k * n_new
            ps = ps + jnp.where(in_strip >= sh, pltpu.roll(p, sh, axis=1), pltpu.roll(p, LANES - per_g + sh, axis=1))
        parts = [ps[c * nbp:(c + 1) * nbp] for c in range(4)]
        n_q = lax.broadcasted_iota(I32, (nbp, LANES), 0)
        prev = jnp.where(n_q >= 1, pltpu.roll(parts[3], 1, axis=0), 0.0)
        score = (((parts[0] + parts[1]) + parts[2]) + parts[3]) + prev
        cur = qpos >> 6
        forced = (n_q == 0) | (n_q == cur) | (n_q == cur - 1)
        score = jnp.where(forced, BIG, jnp.where(n_q * SEL_BLOCK <= qpos, score, -BIG))
        score = jnp.where(n_q < nb, score, -jnp.inf)
        bias_s[...] = _topk_bias(score, min(N_SELECT, nb))

    def flash(k_rows, vt, s_bias):
        s = _nt(k_rows, qzr_s[...]) + s_bias
        m_old = m_s[...]
        m_new = jnp.maximum(m_old, jnp.max(s, axis=0, keepdims=True))
        acc_s[...] = acc_s[...] * jnp.exp2(m_old - m_new) + _dot(_with_ones(vt), jnp.exp2(s - m_new).astype(BF16))
        m_s[...] = m_new

    blk_pp = PAGE // SEL_BLOCK
    blk_ps = n_pg * blk_pp
    k_all = jnp.concatenate(
        [jnp.concatenate([r[cp * LANES:(cp + 1) * LANES, :].T for r in k_refs], axis=0) for cp in range(gw // LANES)],
        axis=1).astype(BF16)
    vt = jnp.concatenate([r[...] for r in vt_refs], axis=1).astype(BF16)
    flash(k_all, vt, _block_bias(bias_s[pl.ds(pl.multiple_of(step * blk_ps, blk_ps), blk_ps), :], blk_ps))

    @pl.when(step == n_steps - 1)
    def _():
        krow = lax.broadcasted_iota(I32, (PAGE, LANES), 0)
        kvn = _pad_rows(kvn_ref[...], PAGE)
        nb0 = pos0 // SEL_BLOCK
        tail_bias = jnp.where(pos0 + krow <= qpos, _block_bias(bias_s[nb0:nb0 + blk_pp, :], blk_pp), NEG)
        flash(kvn[:, 2 * gw:3 * gw].astype(BF16), kvn[:, 3 * gw:4 * gw].T.astype(BF16), tail_bias)

        n_cached = cwin_ref.shape[2]
        wn = _pad_rows(wnew_ref[...], PAGE)
        kwin = jnp.concatenate([wn[:, 0:gw], cwin_ref[0].T], axis=0).astype(BF16)
        vwin = jnp.concatenate([wn[:, gw:2 * gw].T, cwin_ref[1]], axis=1).astype(BF16)
        kpos = jnp.concatenate([pos0 + krow, pos0 - n_cached + lax.broadcasted_iota(I32, (n_cached, LANES), 0)], axis=0)
        wmask = jnp.logical_and(jnp.logical_and(kpos <= qpos, qpos - kpos < WINDOW), kpos >= 0)
        ws = jnp.where(wmask, _nt(kwin, qzr_s[...]), NEG)
        wacc = _dot(_with_ones(vwin), jnp.exp2(ws - jnp.max(ws, axis=0, keepdims=True)).astype(BF16))

        gates_t = gt_s[...]
        grows = []
        for c in range(3):
            row = jnp.zeros((1, LANES), F32)
            for g in range(KV_HEADS):
                for hh in range(HPG):
                    col = (c * HPG + hh) * KV_HEADS + g
                    off = g * per_g + hh * n_new
                    src = gates_t[col:col + 1, :]
                    row = jnp.where((lane1 >= off) & (lane1 < off + n_new), src if off == 0 else pltpu.roll(src, off, axis=1), row)
            grows.append(row)
        acc = acc_s[...]
        o = (ocmp_s[...] * grows[0] + acc[0:gw] * (grows[1] / acc[gw:gw + 1])
             + wacc[0:gw] * (grows[2] / wacc[gw:gw + 1]))
        o_t = o.T
        for g in range(KV_HEADS):
            for hh in range(HPG):
                r0 = g * per_g + hh * n_new
                c0 = hh * gw + g * HEAD_DIM
                o_ref[:, c0:c0 + HEAD_DIM] = o_t[r0:r0 + n_new, g * HEAD_DIM:(g + 1) * HEAD_DIM]


def _attn_sample(q, gate_pre, b_gate, cos, sin, kc, vct, pages, table, kv_new, cache_win_t, win_new,
                 batch, n_new, n_pages, name):
    m, d = q.shape
    assert N_HEADS * n_new == LANES and n_new % SUBLANES == 0
    n_pg = ATT_PAGES
    assert n_pages % n_pg == 0
    n_steps = n_pages // n_pg
    pos0 = n_pages * PAGE
    nb = (pos0 + n_new + SEL_BLOCK - 1) // SEL_BLOCK
    nbp = kc.shape[1] // 4
    gw = KV_HEADS * HEAD_DIM
    n_cached = cache_win_t.shape[3]
    assert n_cached % PAGE == 0

    def page_map(u, branch):
        return lambda b, s, tbl: (tbl[b * n_pages + s * n_pg + u], branch, 0, 0)

    row_map = lambda b, s, tbl: (b, 0)
    per_b3 = lambda b, s, tbl: (b, 0, 0)
    const2 = lambda b, s, tbl: (0, 0)
    in_specs = [pl.BlockSpec((None, None, gw, PAGE), page_map(u, 2)) for u in range(n_pg)]
    in_specs += [pl.BlockSpec((None, None, gw, PAGE), page_map(u, 3)) for u in range(n_pg)]
    in_specs += [pl.BlockSpec((n_new, d), row_map), pl.BlockSpec((n_new, LANES), row_map),
                 pl.BlockSpec((1, LANES), const2), pl.BlockSpec((n_new, LANES), const2),
                 pl.BlockSpec((n_new, LANES), const2),
                 pl.BlockSpec((None, 4 * nbp, gw), per_b3), pl.BlockSpec((None, gw, 4 * nbp), per_b3),
                 pl.BlockSpec((n_new, 4 * gw), row_map),
                 pl.BlockSpec((None, 2, gw, n_cached), lambda b, s, tbl: (b, 0, 0, 0)),
                 pl.BlockSpec((n_new, 2 * gw), row_map)]
    grid_spec = pltpu.PrefetchScalarGridSpec(
        num_scalar_prefetch=1, grid=(batch, n_steps), in_specs=in_specs,
        out_specs=pl.BlockSpec((n_new, d), row_map),
        scratch_shapes=[pltpu.VMEM((1, LANES), F32), pltpu.VMEM((gw + SUM_ROWS, LANES), F32),
                        pltpu.VMEM((nbp, LANES), F32), pltpu.VMEM((gw, LANES), F32),
                        pltpu.VMEM((LANES, gw), BF16), pltpu.VMEM((LANES, LANES), F32)])
    return pl.pallas_call(
        functools.partial(_attn_sample_kernel, n_pg=n_pg, n_steps=n_steps, nb=nb, nbp=nbp, pos0=pos0, n_new=n_new),
        grid_spec=grid_spec, out_shape=jax.ShapeDtypeStruct((m, d), F32),
        compiler_params=_cparams(("parallel", "arbitrary")), name=name,
    )(table, *([pages] * 2 * n_pg), q, gate_pre, b_gate, cos, sin, kc, vct, kv_new, cache_win_t, win_new)


def _prep_weights(w_a_in, b_a_gate, w_a_out, w_kv, cmp_pos, cmp_w1, cmp_b1, cmp_w2, cmp_b2,
                  w_b_in, b_b_gate, w_b_out, w_ffn_up, w_ffn_down):
    a_q = M_HEADS * M_QK_DIM
    a_v = M_HEADS * M_V_DIM
    n_g = 2 * M_HEADS
    g0 = 2 * a_q + a_v
    gw = KV_HEADS * HEAD_DIM
    qd = N_HEADS * HEAD_DIM
    p = {}
    p["a_main"] = jnp.concatenate([w_a_in[:, :, :g0], w_a_in[:, :, g0 + n_g:]], axis=-1).astype(BF16)
    p["a_qvo"] = jnp.concatenate([w_a_in[:, :, :a_q], w_a_in[:, :, 2 * a_q:g0], w_a_in[:, :, g0 + n_g:]],
                                 axis=-1).astype(BF16)
    p["a_kt"] = w_a_in[:, :, a_q:2 * a_q].transpose(0, 2, 1).astype(BF16)
    p["a_gate"] = jnp.pad(w_a_in[:, :, g0:g0 + n_g], ((0, 0), (0, 0), (0, LANES - n_g))).astype(BF16)
    p["a_bgate"] = jnp.pad(b_a_gate, ((0, 0), (0, LANES - n_g)))[:, None, :]
    p["a_out"] = w_a_out.astype(BF16)
    p["kv"] = w_kv.astype(BF16)
    p["kv_t"] = w_kv.T.astype(BF16)
    p["kv_k"] = jnp.concatenate([w_kv[:, 2 * gw:3 * gw], w_kv[:, 4 * gw:5 * gw]], axis=1).astype(BF16)
    hh, g, c = jnp.meshgrid(jnp.arange(HPG), jnp.arange(KV_HEADS), jnp.arange(3), indexing="ij")
    old_col = ((HPG * g + hh) * 3 + c)
    new_col = ((c * HPG + hh) * KV_HEADS + g)
    order = jnp.zeros((3 * N_HEADS,), I32).at[new_col.reshape(-1)].set(old_col.reshape(-1))
    p["b_q"] = w_b_in[:, :, :qd].astype(BF16)
    p["b_gate"] = jnp.pad(w_b_in[:, :, qd:][:, :, order], ((0, 0), (0, 0), (0, LANES - 3 * N_HEADS))).astype(BF16)
    p["b_bgate"] = jnp.pad(b_b_gate[:, order], ((0, 0), (0, LANES - 3 * N_HEADS)))[:, None, :]
    wo = w_b_out.reshape(w_b_out.shape[0], KV_HEADS, HPG, HEAD_DIM, D_MODEL).transpose(0, 2, 1, 3, 4)
    p["b_out"] = wo.reshape(w_b_out.shape[0], qd, D_MODEL).astype(BF16)
    p["up"] = w_ffn_up.astype(BF16)
    p["down"] = w_ffn_down.astype(BF16)
    flat = CMP_STRIDE * HEAD_DIM
    p["cmp"] = {
        "pos_lo": cmp_pos[:, :CMP_STRIDE].reshape(2, 1, flat), "pos_hi": cmp_pos[:, CMP_STRIDE:].reshape(2, 1, flat),
        "w1lo": cmp_w1[:, :flat].astype(BF16), "w1hi": cmp_w1[:, flat:].astype(BF16),
        "b1": cmp_b1[:, None, :], "w2": cmp_w2.astype(BF16), "b2": cmp_b2[:, None, :]}
    return p


def _trunk(x3, pos0, past, p, g_norms, g_a_hnorm, g_kv, ffn_conv_w, ffn_conv_b, conv_prev, m_c, m_n, m_m, tag):
    batch, seq, d = x3.shape
    x = x3.reshape(batch * seq, d)
    cs, ns, ms, convs = [], [], [], []
    gw = KV_HEADS * HEAD_DIM
    for layer in range(DEPTH):
        g = g_norms[layer]
        nm = f"{tag}{layer}"
        if layer < N_A_LAYERS:
            if seq % MLSTM_CHUNK == 0:
                main, gate, kt = _norm_proj(x, g[0], [p["a_qvo"][layer], p["a_gate"][layer]], nm + "_in",
                                            wts=[p["a_kt"][layer]])
            else:
                main, gate = _norm_proj(x, g[0], [p["a_main"][layer], p["a_gate"][layer]], nm + "_in")
                kt = None
            h, c_new, n_new, m_new = _mlstm(main, gate, kt, p["a_bgate"][layer], g_a_hnorm[layer],
                                            m_c[layer], m_n[layer], m_m[layer], batch, seq, nm + "_mlstm")
            cs.append(c_new)
            ns.append(n_new)
            ms.append(m_new)
            mix = (h, p["a_out"][layer], g[1])
        else:
            if layer == N_A_LAYERS:
                cos, sin = _rope_tables(pos0, seq)
                zeros_hi = jnp.zeros((batch, 2 * KV_HEADS, CMP_HIDDEN), F32)
                if past is None:
                    kvt, wint, ks_b, kw_b, vst, vwt = _kv_rows_minor(x, g_kv, p["kv_t"], p["kv_k"], cos, sin,
                                                                     batch, seq, tag + "_kv")
                    kv_out = kvt.reshape(batch, 4, KV_HEADS, HEAD_DIM, seq).transpose(0, 4, 1, 2, 3)
                    win_out = wint.reshape(batch, 2, KV_HEADS, HEAD_DIM, seq).transpose(0, 4, 1, 2, 3)
                    n_pages = seq // PAGE
                    nb = seq // SEL_BLOCK
                    kc, vc, _ = _compress(kvt.reshape(batch, 4, gw, seq), None, batch, n_pages,
                                          min(CMP_PAGES, n_pages), zeros_hi, p["cmp"], None, tag + "_cmp")
                else:
                    kv4, win = _kv_rows(x, g_kv, p["kv"], jnp.tile(cos, (batch, 1)), jnp.tile(sin, (batch, 1)),
                                        tag + "_kv")
                    kv_out = kv4.reshape(batch, seq, 4, KV_HEADS, HEAD_DIM)
                    win_out = win.reshape(batch, seq, 2, KV_HEADS, HEAD_DIM)
                    pages, table, n_pages, cache_win_t = past
                    nb = (pos0 + seq + SEL_BLOCK - 1) // SEL_BLOCK
                    n_tail = nb * (SEL_BLOCK // CMP_STRIDE) - n_pages * (PAGE // CMP_STRIDE)
                    tail = jnp.pad(kv4[:, :2 * gw].reshape(batch, seq, 2, KV_HEADS, HEAD_DIM),
                                   ((0, 0), (0, PAGE - seq), (0, 0), (0, 0), (0, 0)))
                    kc_t, vc_t, hi_t = _compress(_token_minor(tail), jnp.arange(batch, dtype=I32), batch, 1, 1, zeros_hi,
                                                 p["cmp"], n_tail - 1, tag + "_cmpt")
                    kc_m, vc_m, _ = _compress(pages, table, batch, n_pages, min(CMP_PAGES, n_pages), hi_t,
                                              p["cmp"], None, tag + "_cmp")
                    kc = jnp.concatenate([kc_m, kc_t[:, :n_tail]], axis=1)
                    vc = jnp.concatenate([vc_m, vc_t[:, :n_tail]], axis=1)
                nbp = -(-nb // 32) * 32
                kc_l = _cmp_layout(kc, nb, nbp)
                vct_l = _cmp_layout(vc, nb, nbp).transpose(0, 2, 1)
            bl = layer - N_A_LAYERS
            q_raw, gate_pre = _norm_proj(x, g[0], [p["b_q"][bl], p["b_gate"][bl]], nm + "_in")
            if past is None:
                o = _attn_prompt(q_raw, gate_pre, p["b_bgate"][bl], cos, sin, kc_l, vct_l, ks_b, vst, kw_b, vwt,
                                 batch, seq, nm + "_attn")
            else:
                o = _attn_sample(q_raw, gate_pre, p["b_bgate"][bl], cos, sin, kc_l, vct_l, pages, table,
                                 kv4, cache_win_t, win, batch, seq, n_pages, nm + "_attn")
            mix = (o, p["b_out"][bl], g[1])
        if seq < 256:
            x = _proj_norm_res(*mix[:2], mix[2], x, nm + "_out")
            mix = None
        x, conv_new = _ffn(x, g[2], p["up"][layer], ffn_conv_w[layer], ffn_conv_b[layer], p["down"][layer], g[3],
                           conv_prev[layer], batch, seq, nm + "_ffn", mixer=mix)
        convs.append(conv_new)
    return (x.reshape(batch, seq, d), kv_out, win_out, jnp.stack(cs), jnp.stack(ns), jnp.stack(ms), jnp.stack(convs))


def _past_views(cache_kv, cache_win_kv, page_table):
    return (_token_minor(cache_kv), page_table.reshape(-1), page_table.shape[1], _token_minor(cache_win_kv))


def kernel(x_prompt, x_sample, cache_kv, cache_win_kv, state_mlstm_C, state_mlstm_n, state_mlstm_m, state_conv,
           page_table, g_norms, w_a_in, b_a_gate, g_a_hnorm, w_a_out, g_kv, w_kv, cmp_pos, cmp_w1, cmp_b1, cmp_w2,
           cmp_b2, w_b_in, b_b_gate, w_b_out, w_ffn_up, ffn_conv_w, ffn_conv_b, w_ffn_down):
    p = _prep_weights(w_a_in, b_a_gate, w_a_out, w_kv, cmp_pos, cmp_w1, cmp_b1, cmp_w2, cmp_b2,
                      w_b_in, b_b_gate, w_b_out, w_ffn_up, w_ffn_down)
    dt = x_prompt.dtype
    bp, tp, _ = x_prompt.shape
    bs, ts, _ = x_sample.shape
    past_len = page_table.shape[1] * PAGE
    gw = KV_HEADS * HEAD_DIM
    shared = (p, g_norms, g_a_hnorm, g_kv, ffn_conv_w, ffn_conv_b)

    y_p, kv_p, win_p, c_p, n_p, m_p, conv_p = _trunk(
        x_prompt, 0, None, *shared,
        jnp.zeros((DEPTH, bp, CONV_W - 1, 2 * D_FF), dt),
        jnp.zeros((N_A_LAYERS, bp, M_HEADS, M_QK_DIM, M_V_DIM), dt),
        jnp.zeros((N_A_LAYERS, bp, M_HEADS, M_QK_DIM), dt), jnp.zeros((N_A_LAYERS, bp, M_HEADS), dt), "p")
    win_p = win_p[:, tp - min(WINDOW, tp):]

    past = _past_views(cache_kv, cache_win_kv, page_table)
    y_s, kv_s, win_s, c_s, n_s, m_s, conv_s = _trunk(
        x_sample, past_len, past, *shared, state_conv, state_mlstm_C, state_mlstm_n, state_mlstm_m, "s")
    win_all = jnp.concatenate([cache_win_kv, win_s], axis=1)
    win_s = win_all[:, win_all.shape[1] - min(WINDOW, win_all.shape[1]):]
    return (y_p, y_s, kv_p, kv_s, win_p, win_s, c_p, c_s, n_p, n_s, m_p, m_s, conv_p, conv_s)
```

```python
import functools
import math

import jax
import jax.numpy as jnp
from jax import lax
from jax.experimental import pallas as pl
from jax.experimental.pallas import tpu as pltpu

F32 = jnp.float32
BF16 = jnp.bfloat16
I32 = jnp.int32

D_MODEL = 1024
DEPTH = 4
N_A_LAYERS = 2
M_HEADS = 8
M_QK_DIM = 64
M_V_DIM = 128
GATE_CAP = 15.0
N_HEADS = 16
HEAD_DIM = 64
KV_HEADS = 4
HPG = 4
CMP_BLOCK = 32
CMP_STRIDE = 16
CMP_HIDDEN = 256
SEL_BLOCK = 64
N_SELECT = 16
WINDOW = 512
ROT_DIM = 16
ROPE_THETA = 500000.0
D_FF = 2816
CONV_W = 3
EPS = 1e-6
BIG = 1e9
NEG = -1e30
PAGE = 128

LANES = 128
SUBLANES = 8
VMEM_LIMIT = 56 * 1024 * 1024

MLSTM_CHUNK = 128
KEY_TILE = 512
Q_TILE = 128
CMP_PAGES = 16
ATT_PAGES = 16
FFN_TM = 512
FFN_TN = 1408
LONG_SEQ = 256
HEAD_SHIFT = HEAD_DIM.bit_length() - 1
SEL_SHIFT = SEL_BLOCK.bit_length() - 1


def _cparams(sem):
    return pltpu.CompilerParams(dimension_semantics=sem, vmem_limit_bytes=VMEM_LIMIT)


def _rms(x, g):
    return x * lax.rsqrt(jnp.mean(x * x, axis=-1, keepdims=True) + EPS) * g


def _nt(a, b):
    return lax.dot_general(a, b, (((1,), (1,)), ((), ())), preferred_element_type=F32)


def _dot(a, b):
    return jnp.dot(a, b, preferred_element_type=F32)


def _row_tile(m, pref):
    t = min(m, pref)
    while m % t:
        t //= 2
    return t


def _norm_proj_kernel(x_ref, g_ref, *refs, n_w, n_t):
    xn = _rms(x_ref[...], g_ref[...]).astype(BF16)
    n_in = n_w + n_t
    for w_ref, o_ref in zip(refs[:n_w], refs[n_in:n_in + n_w]):
        o_ref[...] = _dot(xn, w_ref[...]).astype(o_ref.dtype)
    for w_ref, o_ref in zip(refs[n_w:n_in], refs[n_in + n_w:]):
        yt = _nt(w_ref[...], xn)
        for u in range(o_ref.shape[0]):
            o_ref[u] = yt[:, u * LANES:(u + 1) * LANES]


def _norm_proj(x, g, ws, name, wts=()):
    m, d = x.shape
    tm = _row_tile(m, 512)
    assert not wts or tm % LANES == 0
    n_w, n_t = len(ws), len(wts)
    in_specs = [pl.BlockSpec((tm, d), lambda i: (i, 0)), pl.BlockSpec((1, d), lambda i: (0, 0))]
    in_specs += [pl.BlockSpec(w.shape, lambda i: (0, 0)) for w in (*ws, *wts)]
    out_specs = [pl.BlockSpec((tm, w.shape[1]), lambda i: (i, 0)) for w in ws]
    out_specs += [pl.BlockSpec((tm // LANES, w.shape[0], LANES), lambda i: (i, 0, 0)) for w in wts]
    out_shape = [jax.ShapeDtypeStruct((m, w.shape[1]), F32) for w in ws]
    out_shape += [jax.ShapeDtypeStruct((m // LANES, w.shape[0], LANES), F32) for w in wts]
    return pl.pallas_call(
        functools.partial(_norm_proj_kernel, n_w=n_w, n_t=n_t), grid=(m // tm,), in_specs=in_specs,
        out_specs=out_specs, out_shape=out_shape, compiler_params=_cparams(("parallel",)),
        name=name)(x, g.reshape(1, d), *ws, *wts)


def _proj_norm_res_kernel(a_ref, w_ref, g_ref, res_ref, o_ref):
    y = _dot(a_ref[...].astype(BF16), w_ref[...])
    o_ref[...] = res_ref[...] + _rms(y, g_ref[...])


def _proj_norm_res(a, w, g, res, name):
    m, k = a.shape
    d = w.shape[1]
    tm = _row_tile(m, 512)
    return pl.pallas_call(
        _proj_norm_res_kernel, grid=(m // tm,),
        in_specs=[pl.BlockSpec((tm, k), lambda i: (i, 0)), pl.BlockSpec((k, d), lambda i: (0, 0)),
                  pl.BlockSpec((1, d), lambda i: (0, 0)), pl.BlockSpec((tm, d), lambda i: (i, 0))],
        out_specs=pl.BlockSpec((tm, d), lambda i: (i, 0)),
        out_shape=jax.ShapeDtypeStruct((m, d), F32),
        compiler_params=_cparams(("parallel",)), name=name)(a, w, g.reshape(1, d), res)


def _mlstm_kernel(main_ref, gate_ref, *refs, rows, nc, k_minor):
    if k_minor:
        kt_ref, *refs = refs
    bg_ref, gh_ref, c0_ref, n0_ref, m0_ref, h_ref, c_ref, n_ref, m_ref, c_s, n_s, m_s = refs
    L = MLSTM_CHUNK
    nh = M_HEADS
    a_q = nh * M_QK_DIM
    v0 = a_q if k_minor else 2 * a_q
    o0 = v0 + nh * M_V_DIM
    cidx = pl.program_id(1)

    @pl.when(cidx == 0)
    def _():
        c_s[...] = c0_ref[...]
        n_s[...] = n0_ref[...]
        m_s[...] = m0_ref[...]

    main = main_ref[...]
    gp = gate_ref[...] + bg_ref[...]
    if rows < L:
        main = jnp.concatenate([main, jnp.zeros((L - rows, main.shape[1]), F32)], axis=0)
        gp = jnp.concatenate([gp, jnp.zeros((L - rows, LANES), F32)], axis=0)
    capped = GATE_CAP * jnp.tanh(gp / GATE_CAP)
    row1 = lax.broadcasted_iota(I32, (L, LANES), 0)
    real = row1 < rows
    ilog = jnp.where(real, capped, -jnp.inf)
    logf = jnp.where(real, jnp.minimum(capped, 0.0) - jnp.log1p(jnp.exp(-jnp.abs(capped))), 0.0)
    bh = logf
    k = 1
    while k < L:
        bh = bh + jnp.where(row1 >= k, pltpu.roll(bh, k, axis=0), 0.0)
        k *= 2
    bh = pltpu.roll(bh, LANES - nh, axis=1)
    c_all = ilog - bh
    cm = c_all
    k = 1
    while k < L:
        cm = jnp.maximum(cm, jnp.where(row1 >= k, pltpu.roll(cm, k, axis=0), -jnp.inf))
        k *= 2
    m_row = m_s[...]
    mt = bh + jnp.maximum(m_row, cm)
    m_new = mt[L - 1:L, :]
    b_last = bh[L - 1:L, :]
    w_inter = jnp.exp(bh + m_row - mt)
    u_all = bh - mt
    emt = jnp.exp(-mt)
    ws_all = jnp.exp(b_last - bh + ilog - m_new)
    decay = jnp.exp(b_last + m_row - m_new)
    c_t = c_all.T
    ws_t = ws_all.T

    rr = lax.broadcasted_iota(I32, (L, L), 0)
    cc = lax.broadcasted_iota(I32, (L, L), 1)
    causal = cc <= rr
    lo_half = lax.broadcasted_iota(I32, (L, LANES), 1) < M_QK_DIM
    lo_rows = lax.broadcasted_iota(I32, (2 * M_QK_DIM, LANES), 0) < M_QK_DIM
    ones_b = jnp.ones((L, LANES), BF16)
    assert L == LANES

    def lanes(a, x):
        return jnp.broadcast_to(a[:, x:x + 1], (L, LANES))

    for p in range(nh // 2):
        qp = main[:, p * LANES:(p + 1) * LANES] * (M_QK_DIM ** -0.5)
        if k_minor:
            kt = kt_ref[0, p * LANES:(p + 1) * LANES, :]
        else:
            kt = main[:, a_q + p * LANES:a_q + (p + 1) * LANES].T
        c_pair = c_s[2 * p:2 * p + 2].reshape(2 * M_QK_DIM, M_V_DIM)
        n_pair = n_s[p]
        kt_b = kt.astype(BF16)
        state_b = jnp.concatenate([c_pair, n_pair], axis=1).astype(BF16)
        for e in range(2):
            x = 2 * p + e
            qx_b = jnp.where(lo_half if e == 0 else jnp.logical_not(lo_half), qp, 0.0).astype(BF16)
            vx = main[:, v0 + x * M_V_DIM:v0 + (x + 1) * M_V_DIM]
            ox = main[:, o0 + x * M_V_DIM:o0 + (x + 1) * M_V_DIM]
            a = jnp.exp(jnp.where(causal, lanes(u_all, x) + c_t[x:x + 1, :], -jnp.inf)) * _dot(qx_b, kt_b)
            inter = _dot(qx_b, state_b)
            intra = _dot(a.astype(BF16), jnp.concatenate([vx.astype(BF16), ones_b], axis=1))
            wi = lanes(w_inter, x)
            num = wi * inter[:, :M_V_DIM] + intra[:, :M_V_DIM]
            den = wi * inter[:, M_V_DIM:] + intra[:, M_V_DIM:]
            h = num / jnp.maximum(jnp.abs(den), lanes(emt, x))
            hn = h * lax.rsqrt(jnp.mean(h * h, axis=-1, keepdims=True) + EPS)
            hn = hn * gh_ref[:, x * M_V_DIM:(x + 1) * M_V_DIM] * jax.nn.sigmoid(ox)
            h_ref[:, x * M_V_DIM:(x + 1) * M_V_DIM] = hn[:rows]
        kwt = kt * jnp.where(lo_rows, ws_t[2 * p:2 * p + 1, :], ws_t[2 * p + 1:2 * p + 2, :])
        vcat = main[:, v0 + 2 * p * M_V_DIM:v0 + (2 * p + 2) * M_V_DIM]
        upd = _dot(kwt.astype(BF16), jnp.concatenate([vcat.astype(BF16), ones_b], axis=1))
        dec_e = decay[:, 2 * p:2 * p + 1]
        dec_o = decay[:, 2 * p + 1:2 * p + 2]
        c_s[2 * p] = dec_e * c_pair[:M_QK_DIM] + upd[:M_QK_DIM, :M_V_DIM]
        c_s[2 * p + 1] = dec_o * c_pair[M_QK_DIM:] + upd[M_QK_DIM:, M_V_DIM:2 * M_V_DIM]
        n_s[p] = jnp.where(lo_rows, dec_e, dec_o) * n_pair + upd[:, 2 * M_V_DIM:]
    m_s[...] = m_new

    @pl.when(cidx == nc - 1)
    def _():
        c_ref[...] = c_s[...]
        n_ref[...] = n_s[...]
        m_ref[...] = m_s[...]


def _mlstm(main, gate, kt, b_gate, g_hnorm, c0, n0, m0, batch, seq, name):
    rows = min(seq, MLSTM_CHUNK)
    nc = seq // rows
    a_v = M_HEADS * M_V_DIM
    hp = M_HEADS // 2
    n_in = jnp.broadcast_to(n0.reshape(batch, hp, 2 * M_QK_DIM, 1), (batch, hp, 2 * M_QK_DIM, LANES))
    m_in = jnp.pad(m0, ((0, 0), (0, LANES - M_HEADS)))[:, None, :]
    k_minor = kt is not None
    assert not k_minor or rows == MLSTM_CHUNK == LANES
    kt_specs = [pl.BlockSpec((1, kt.shape[1], LANES), lambda b, c: (b * nc + c, 0, 0))] if k_minor else []
    h, c, n, m = pl.pallas_call(
        functools.partial(_mlstm_kernel, rows=rows, nc=nc, k_minor=k_minor), grid=(batch, nc),
        in_specs=[pl.BlockSpec((rows, main.shape[1]), lambda b, c: (b * nc + c, 0)),
                  pl.BlockSpec((rows, LANES), lambda b, c: (b * nc + c, 0)), *kt_specs,
                  pl.BlockSpec((1, LANES), lambda b, c: (0, 0)),
                  pl.BlockSpec((1, a_v), lambda b, c: (0, 0)),
                  pl.BlockSpec((None, M_HEADS, M_QK_DIM, M_V_DIM), lambda b, c: (b, 0, 0, 0)),
                  pl.BlockSpec((None, hp, 2 * M_QK_DIM, LANES), lambda b, c: (b, 0, 0, 0)),
                  pl.BlockSpec((None, 1, LANES), lambda b, c: (b, 0, 0))],
        out_specs=[pl.BlockSpec((rows, a_v), lambda b, c: (b * nc + c, 0)),
                   pl.BlockSpec((None, M_HEADS, M_QK_DIM, M_V_DIM), lambda b, c: (b, 0, 0, 0)),
                   pl.BlockSpec((None, hp, 2 * M_QK_DIM, LANES), lambda b, c: (b, 0, 0, 0)),
                   pl.BlockSpec((None, 1, LANES), lambda b, c: (b, 0, 0))],
        out_shape=[jax.ShapeDtypeStruct((batch * seq, a_v), F32),
                   jax.ShapeDtypeStruct((batch, M_HEADS, M_QK_DIM, M_V_DIM), F32),
                   jax.ShapeDtypeStruct((batch, hp, 2 * M_QK_DIM, LANES), F32),
                   jax.ShapeDtypeStruct((batch, 1, LANES), F32)],
        scratch_shapes=[pltpu.VMEM((M_HEADS, M_QK_DIM, M_V_DIM), F32), pltpu.VMEM((hp, 2 * M_QK_DIM, LANES), F32),
                        pltpu.VMEM((1, LANES), F32)],
        compiler_params=_cparams(("parallel", "arbitrary")), name=name,
    )(main, gate, *([kt] if k_minor else []), b_gate, g_hnorm.reshape(1, a_v), c0, n_in, m_in)
    return h, c, n[..., 0].reshape(batch, M_HEADS, M_QK_DIM), m[:, 0, :M_HEADS]


def _ffn_kernel(*refs, carry, tm, n_j, tiles_per_seq, period, mixer):
    if mixer:
        a_ref, wm_ref, g1_ref, *refs = refs
    x_ref, g2_ref, wua_ref, wug_ref, cwa_ref, cwg_ref, cba_ref, cbg_ref, wd_ref, g3_ref = refs[:10]
    if carry and mixer:
        inita_ref, initg_ref, o_ref, sa_ref, sg_ref, xn_s, acc_s, ue_s, carry_s, x_s = refs[10:]
        branch_in = ((wua_ref, cwa_ref, cba_ref, inita_ref, sa_ref), (wug_ref, cwg_ref, cbg_ref, initg_ref, sg_ref))
    elif carry:
        inita_ref, initg_ref, o_ref, sa_ref, sg_ref, xn_s, acc_s, ue_s, carry_s = refs[10:]
        branch_in = ((wua_ref, cwa_ref, cba_ref, inita_ref, sa_ref), (wug_ref, cwg_ref, cbg_ref, initg_ref, sg_ref))
    else:
        t1a_ref, t1g_ref, t2a_ref, t2g_ref, o_ref, sa_ref, sg_ref, xn_s, acc_s, ue_s = refs[10:]
        branch_in = ((wua_ref, cwa_ref, cba_ref, (t1a_ref, t2a_ref), sa_ref),
                     (wug_ref, cwg_ref, cbg_ref, (t1g_ref, t2g_ref), sg_ref))
    i = pl.program_id(0)
    j = pl.program_id(1)
    tn = wd_ref.shape[0]

    @pl.when(j == 0)
    def _():
        x = x_ref[...]
        if mixer:
            x = x + _rms(_dot(a_ref[...].astype(BF16), wm_ref[...]), g1_ref[...])
            x_s[...] = x
        xn_s[...] = _rms(x, g2_ref[...]).astype(BF16)
        acc_s[...] = jnp.zeros_like(acc_s)

    xn = xn_s[...]
    conv = []
    for which, (w_ref, cw_ref, cb_ref, boundary, s_ref) in enumerate(branch_in):
        u = _dot(xn, w_ref[...])
        ue_s[SUBLANES:, :] = u
        if carry:
            slot = which * n_j + j
            first = (i % tiles_per_seq) == 0
            ue_s[SUBLANES - 2:SUBLANES, :] = jnp.where(first, boundary[...], carry_s[slot])
            tap1 = ue_s[pl.ds(SUBLANES - 1, tm), :]
            tap2 = ue_s[pl.ds(SUBLANES - 2, tm), :]
            last2 = u[tm - 2:tm, :]
            carry_s[slot] = last2
            s_ref[i // tiles_per_seq, j] = last2
        else:
            ue_s[0:SUBLANES, :] = jnp.zeros((SUBLANES, tn), F32)
            t = lax.broadcasted_iota(I32, (tm, tn), 0) & (period - 1)
            tap1 = jnp.where(t >= 1, ue_s[pl.ds(SUBLANES - 1, tm), :], boundary[0][...])
            tap2 = jnp.where(t >= 2, ue_s[pl.ds(SUBLANES - 2, tm), :], boundary[1][...])
            s_ref[...] = u
        conv.append(cb_ref[...] + tap2 * cw_ref[0:1, :] + tap1 * cw_ref[1:2, :] + u * cw_ref[2:3, :])
    y = jax.nn.gelu(conv[0], approximate=True) * conv[1]
    acc_s[...] += _dot(y.astype(BF16), wd_ref[...])

    @pl.when(j == n_j - 1)
    def _():
        o_ref[...] = (x_s[...] if mixer else x_ref[...]) + _rms(acc_s[...], g3_ref[...])


def _ffn(x, g2, w_up, conv_w, conv_b, w_down, g3, prev, batch, seq, name, mixer=None):
    m, d = x.shape
    nf = w_down.shape[0]
    tn = FFN_TN
    n_j = nf // tn
    carry = seq >= LONG_SEQ
    cb = conv_b.reshape(1, 2 * nf)
    col_a = lambda i, j: (0, j)
    col_g = lambda i, j: (0, j + n_j)
    common_specs = [
        None,
        pl.BlockSpec((1, d), lambda i, j: (0, 0)),
        pl.BlockSpec((d, tn), col_a), pl.BlockSpec((d, tn), col_g),
        pl.BlockSpec((CONV_W, tn), col_a), pl.BlockSpec((CONV_W, tn), col_g),
        pl.BlockSpec((1, tn), col_a), pl.BlockSpec((1, tn), col_g),
        pl.BlockSpec((tn, d), lambda i, j: (j, 0)),
        pl.BlockSpec((1, d), lambda i, j: (0, 0)),
    ]
    common_args = [x, g2.reshape(1, d), w_up, w_up, conv_w, conv_w, cb, cb, w_down, g3.reshape(1, d)]
    assert mixer is None or carry
    if carry:
        tm = _row_tile(seq, FFN_TM)
        tps = seq // tm
        common_specs[0] = pl.BlockSpec((tm, d), lambda i, j: (i, 0))
        mix_specs, mix_args, mix_scratch = [], [], []
        if mixer is not None:
            a, wm, g1 = mixer
            mix_specs = [pl.BlockSpec((tm, a.shape[1]), lambda i, j: (i, 0)), pl.BlockSpec(wm.shape, lambda i, j: (0, 0)),
                         pl.BlockSpec((1, d), lambda i, j: (0, 0))]
            mix_args = [a, wm, g1.reshape(1, d)]
            mix_scratch = [pltpu.VMEM((tm, d), F32)]
        st_a = lambda i, j: (i // tps, 0, j)
        st_g = lambda i, j: (i // tps, 0, j + n_j)
        out, sa, sg = pl.pallas_call(
            functools.partial(_ffn_kernel, carry=True, tm=tm, n_j=n_j, tiles_per_seq=tps, period=seq,
                              mixer=mixer is not None),
            grid=(m // tm, n_j),
            in_specs=mix_specs + common_specs + [pl.BlockSpec((None, 2, tn), st_a), pl.BlockSpec((None, 2, tn), st_g)],
            out_specs=[pl.BlockSpec((tm, d), lambda i, j: (i, 0)),
                       pl.BlockSpec((batch, n_j, 2, tn), lambda i, j: (0, 0, 0, 0)),
                       pl.BlockSpec((batch, n_j, 2, tn), lambda i, j: (0, 0, 0, 0))],
            out_shape=[jax.ShapeDtypeStruct((m, d), F32), jax.ShapeDtypeStruct((batch, n_j, 2, tn), F32),
                       jax.ShapeDtypeStruct((batch, n_j, 2, tn), F32)],
            scratch_shapes=[pltpu.VMEM((tm, d), BF16), pltpu.VMEM((tm, d), F32),
                            pltpu.VMEM((tm + SUBLANES, tn), F32), pltpu.VMEM((2 * n_j, 2, tn), F32)] + mix_scratch,
            compiler_params=_cparams(("arbitrary", "arbitrary")), name=name,
        )(*mix_args, *common_args, prev, prev)
        sa, sg = (s.transpose(0, 2, 1, 3).reshape(batch, 2, nf) for s in (sa, sg))
        return out, jnp.concatenate([sa, sg], axis=-1)
    tm = m
    assert seq >= 2 and seq & (seq - 1) == 0
    common_specs[0] = pl.BlockSpec((tm, d), lambda i, j: (i, 0))
    tap1 = jnp.pad(prev[:, 1:2], ((0, 0), (0, seq - 1), (0, 0))).reshape(m, 2 * nf)
    tap2 = jnp.pad(prev, ((0, 0), (0, seq - 2), (0, 0))).reshape(m, 2 * nf)
    row_a = lambda i, j: (i, j)
    row_g = lambda i, j: (i, j + n_j)
    out, ua, ug = pl.pallas_call(
        functools.partial(_ffn_kernel, carry=False, tm=tm, n_j=n_j, tiles_per_seq=1, period=seq, mixer=False),
        grid=(m // tm, n_j),
        in_specs=common_specs + [pl.BlockSpec((tm, tn), row_a), pl.BlockSpec((tm, tn), row_g),
                                 pl.BlockSpec((tm, tn), row_a), pl.BlockSpec((tm, tn), row_g)],
        out_specs=[pl.BlockSpec((tm, d), lambda i, j: (i, 0)), pl.BlockSpec((tm, tn), row_a),
                   pl.BlockSpec((tm, tn), row_a)],
        out_shape=[jax.ShapeDtypeStruct((m, d), F32), jax.ShapeDtypeStruct((m, nf), F32),
                   jax.ShapeDtypeStruct((m, nf), F32)],
        scratch_shapes=[pltpu.VMEM((tm, d), BF16), pltpu.VMEM((tm, d), F32), pltpu.VMEM((tm + SUBLANES, tn), F32)],
        compiler_params=_cparams(("arbitrary", "arbitrary")), name=name,
    )(*common_args, tap1, tap1, tap2, tap2)
    u = jnp.concatenate([ua, ug], axis=-1).reshape(batch, seq, 2 * nf)
    return out, u[:, seq - 2:]


def _rope_pair(x, cos, sin):
    half = ROT_DIM // 2
    lane = lax.broadcasted_iota(I32, x.shape, 1) & (HEAD_DIM - 1)
    partner = jnp.where(lane < half, pltpu.roll(x, LANES - half, axis=1), pltpu.roll(x, half, axis=1))
    return x * cos + partner * sin


def _rope(x, cos, sin):
    return jnp.concatenate(
        [_rope_pair(x[:, c * LANES:(c + 1) * LANES], cos, sin) for c in range(x.shape[1] // LANES)], axis=1)


def _rope_tables(pos0, seq):
    half = ROT_DIM // 2
    inv = jnp.power(jnp.float32(ROPE_THETA), -jnp.arange(0, ROT_DIM, 2, dtype=F32) / ROT_DIM)
    ang = (pos0 + jnp.arange(seq, dtype=I32)).astype(F32)[:, None] * inv[None, :]
    cos, sin = jnp.cos(ang), jnp.sin(ang)
    rest = HEAD_DIM - ROT_DIM
    cos_h = jnp.concatenate([cos, cos, jnp.ones((seq, rest), F32)], axis=1)
    sin_h = jnp.concatenate([-sin, sin, jnp.zeros((seq, rest), F32)], axis=1)
    return jnp.tile(cos_h, (1, LANES // HEAD_DIM)), jnp.tile(sin_h, (1, LANES // HEAD_DIM))


def _kv_kernel(x_ref, g_ref, w_ref, cos_ref, sin_ref, kv_ref, win_ref):
    xn = _rms(x_ref[...], g_ref[...]).astype(BF16)
    y = _dot(xn, w_ref[...])
    cos, sin = cos_ref[...], sin_ref[...]
    gw = KV_HEADS * HEAD_DIM
    ks = _rope(y[:, 2 * gw:3 * gw], cos, sin)
    kw = _rope(y[:, 4 * gw:5 * gw], cos, sin)
    kv_ref[:, 0:2 * gw] = y[:, 0:2 * gw]
    kv_ref[:, 2 * gw:3 * gw] = ks
    kv_ref[:, 3 * gw:4 * gw] = y[:, 3 * gw:4 * gw]
    win_ref[:, 0:gw] = kw
    win_ref[:, gw:2 * gw] = y[:, 5 * gw:6 * gw]


def _rope_rows(x, cos_t, sin_t):
    half = ROT_DIM // 2
    row = lax.broadcasted_iota(I32, x.shape, 0) & (HEAD_DIM - 1)
    partner = jnp.where(row < half, pltpu.roll(x, x.shape[0] - half, axis=0), pltpu.roll(x, half, axis=0))
    return x * cos_t + partner * sin_t


def _kv_minor_kernel(x_ref, g_ref, wt_ref, wk_ref, cos_ref, sin_ref, cost_ref, sint_ref,
                     kvt_ref, wint_ref, ks_ref, kw_ref, vst_ref, vwt_ref):
    xn = _rms(x_ref[...], g_ref[...]).astype(BF16)
    gw = KV_HEADS * HEAD_DIM
    yt = _nt(wt_ref[...], xn)
    cos_t, sin_t = cost_ref[...], sint_ref[...]

    def rope_t(a):
        return jnp.concatenate(
            [_rope_rows(a[c * LANES:(c + 1) * LANES], cos_t, sin_t) for c in range(gw // LANES)], axis=0)

    kvt_ref[0:2 * gw, :] = yt[0:2 * gw]
    kvt_ref[2 * gw:3 * gw, :] = rope_t(yt[2 * gw:3 * gw])
    kvt_ref[3 * gw:4 * gw, :] = yt[3 * gw:4 * gw]
    wint_ref[0:gw, :] = rope_t(yt[4 * gw:5 * gw])
    wint_ref[gw:2 * gw, :] = yt[5 * gw:6 * gw]
    vst_ref[...] = yt[3 * gw:4 * gw].astype(BF16)
    for u in range(vwt_ref.shape[0]):
        vwt_ref[u] = yt[5 * gw:6 * gw, u * PAGE:(u + 1) * PAGE].astype(BF16)
    yk = _dot(xn, wk_ref[...])
    cos, sin = cos_ref[...], sin_ref[...]
    ks_ref[...] = _rope(yk[:, 0:gw], cos, sin).astype(BF16)
    kw_ref[...] = _rope(yk[:, gw:2 * gw], cos, sin).astype(BF16)


def _kv_rows_minor(x, g_kv, w_t, w_k, cos, sin, batch, seq, name):
    m, d = x.shape
    tm = _row_tile(seq, KEY_TILE)
    gw = KV_HEADS * HEAD_DIM
    tt = seq // tm
    tok = lambda i: (i % tt, 0)
    tok_t = lambda i: (0, i % tt)
    return pl.pallas_call(
        _kv_minor_kernel, grid=(m // tm,),
        in_specs=[pl.BlockSpec((tm, d), lambda i: (i, 0)), pl.BlockSpec((1, d), lambda i: (0, 0)),
                  pl.BlockSpec(w_t.shape, lambda i: (0, 0)), pl.BlockSpec(w_k.shape, lambda i: (0, 0)),
                  pl.BlockSpec((tm, LANES), tok), pl.BlockSpec((tm, LANES), tok),
                  pl.BlockSpec((LANES, tm), tok_t), pl.BlockSpec((LANES, tm), tok_t)],
        out_specs=[pl.BlockSpec((None, 4 * gw, tm), lambda i: (i // tt, 0, i % tt)),
                   pl.BlockSpec((None, 2 * gw, tm), lambda i: (i // tt, 0, i % tt)),
                   pl.BlockSpec((tm, gw), lambda i: (i, 0)), pl.BlockSpec((tm, gw), lambda i: (i, 0)),
                   pl.BlockSpec((None, gw, tm), lambda i: (i, 0, 0)),
                   pl.BlockSpec((tm // PAGE, gw, PAGE), lambda i: (i, 0, 0))],
        out_shape=[jax.ShapeDtypeStruct((batch, 4 * gw, seq), F32), jax.ShapeDtypeStruct((batch, 2 * gw, seq), F32),
                   jax.ShapeDtypeStruct((m, gw), BF16), jax.ShapeDtypeStruct((m, gw), BF16),
                   jax.ShapeDtypeStruct((m // tm, gw, tm), BF16), jax.ShapeDtypeStruct((m // PAGE, gw, PAGE), BF16)],
        compiler_params=_cparams(("parallel",)), name=name)(x, g_kv.reshape(1, d), w_t, w_k, cos, sin, cos.T, sin.T)


def _kv_rows(x, g_kv, w_kv, cos, sin, name):
    m, d = x.shape
    gw = KV_HEADS * HEAD_DIM
    return pl.pallas_call(
        _kv_kernel, grid=(1,),
        in_specs=[pl.BlockSpec((m, d), lambda i: (0, 0)), pl.BlockSpec((1, d), lambda i: (0, 0)),
                  pl.BlockSpec(w_kv.shape, lambda i: (0, 0)),
                  pl.BlockSpec((m, LANES), lambda i: (0, 0)), pl.BlockSpec((m, LANES), lambda i: (0, 0))],
        out_specs=[pl.BlockSpec((m, 4 * gw), lambda i: (0, 0)), pl.BlockSpec((m, 2 * gw), lambda i: (0, 0))],
        out_shape=[jax.ShapeDtypeStruct((m, 4 * gw), F32), jax.ShapeDtypeStruct((m, 2 * gw), F32)],
        compiler_params=_cparams(("arbitrary",)), name=name)(x, g_kv.reshape(1, d), w_kv, cos, sin)


def _token_minor(rows):
    n_p, r, n = rows.shape[0], rows.shape[1], rows.shape[2]
    return rows.transpose(0, 2, 3, 4, 1).reshape(n_p, n, KV_HEADS * HEAD_DIM, r)


def _compress_kernel(tbl_ref, *refs, n_pg, zero_after):
    page_refs = refs[:n_pg]
    (pos_lo_ref, pos_hi_ref, w1lo_ref, w1hi_ref, b1_ref, w2_ref, b2_ref, hi_init_ref,
     kc_ref, vc_ref, hi_first_ref, carry_s, t_s, x_s) = refs[n_pg:]
    c = pl.program_id(1)
    njp = PAGE // CMP_STRIDE
    njc = n_pg * njp

    @pl.when(c == 0)
    def _():
        carry_s[...] = hi_init_ref[...]

    lo_lanes = lax.broadcasted_iota(I32, (njp, LANES), 1) < HEAD_DIM
    for u in range(n_pg):
        for br in range(2):
            for cp in range(KV_HEADS // 2):
                tb = t_s.at[(2 * u + br) % 2 * 2 + cp]
                tb[...] = page_refs[u][br, cp * LANES:(cp + 1) * LANES, :].T
                for q in range(CMP_STRIDE // 2):
                    p0 = tb[pl.ds(2 * q, njp, stride=CMP_STRIDE), :]
                    p1 = tb[pl.ds(2 * q + 1, njp, stride=CMP_STRIDE), :]
                    rows = slice(u * njp, (u + 1) * njp)
                    cols = slice(q * LANES, (q + 1) * LANES)
                    x_s[br * KV_HEADS + 2 * cp, rows, cols] = jnp.where(lo_lanes, p0, pltpu.roll(p1, HEAD_DIM, axis=1))
                    x_s[br * KV_HEADS + 2 * cp + 1, rows, cols] = jnp.where(lo_lanes, pltpu.roll(p0, HEAD_DIM, axis=1), p1)

    rowl = lax.broadcasted_iota(I32, (KV_HEADS * njc, CMP_HIDDEN), 0) & (njc - 1)
    for br, out_ref in enumerate((kc_ref, vc_ref)):
        x = x_s[br * KV_HEADS:(br + 1) * KV_HEADS].reshape(KV_HEADS * njc, CMP_STRIDE * HEAD_DIM)
        lo = _dot((x + pos_lo_ref[br]).astype(BF16), w1lo_ref[br])
        hi = _dot((x + pos_hi_ref[br]).astype(BF16), w1hi_ref[br])
        carry_rows = jnp.concatenate(
            [jnp.broadcast_to(carry_s[br * KV_HEADS + g:br * KV_HEADS + g + 1, :], (njc, CMP_HIDDEN))
             for g in range(KV_HEADS)], axis=0)
        hi_next = jnp.where(rowl == njc - 1, carry_rows, pltpu.roll(hi, KV_HEADS * njc - 1, axis=0))
        if zero_after is not None:
            hi_next = jnp.where(jnp.logical_and(rowl == zero_after, c == 0), 0.0, hi_next)
        for g in range(KV_HEADS):
            carry_s[br * KV_HEADS + g:br * KV_HEADS + g + 1, :] = hi[g * njc:g * njc + 1, :]
        h = jax.nn.gelu(lo + hi_next + b1_ref[br], approximate=True)
        o = _dot(h.astype(BF16), w2_ref[br]) + b2_ref[br]
        for g in range(KV_HEADS):
            out_ref[:, g * HEAD_DIM:(g + 1) * HEAD_DIM] = o[g * njc:(g + 1) * njc, :]
    hi_first_ref[...] = carry_s[...]


def _compress(pages, table, batch, n_pages, n_pg, hi_init, cw, zero_after, name):
    njp = PAGE // CMP_STRIDE
    njc = n_pg * njp
    assert njc & (njc - 1) == 0 and njc % SUBLANES == 0 and n_pages % n_pg == 0
    n_ch = n_pages // n_pg
    gw = KV_HEADS * HEAD_DIM
    flat = CMP_STRIDE * HEAD_DIM

    def page_map(u):
        if table is None:
            return lambda b, c, tbl: (b, 0, 0, (n_ch - 1 - c) * n_pg + u)
        return lambda b, c, tbl: (tbl[b * n_pages + (n_ch - 1 - c) * n_pg + u], 0, 0, 0)

    full3 = lambda b, c, tbl: (0, 0, 0)
    in_specs = [pl.BlockSpec((None, 2, gw, PAGE), page_map(u)) for u in range(n_pg)]
    in_specs += [pl.BlockSpec((2, 1, flat), full3), pl.BlockSpec((2, 1, flat), full3),
                 pl.BlockSpec((2, flat, CMP_HIDDEN), full3), pl.BlockSpec((2, flat, CMP_HIDDEN), full3),
                 pl.BlockSpec((2, 1, CMP_HIDDEN), full3), pl.BlockSpec((2, CMP_HIDDEN, HEAD_DIM), full3),
                 pl.BlockSpec((2, 1, HEAD_DIM), full3),
                 pl.BlockSpec((None, 2 * KV_HEADS, CMP_HIDDEN), lambda b, c, tbl: (b, 0, 0))]
    out_map = lambda b, c, tbl: (b, n_ch - 1 - c, 0)
    grid_spec = pltpu.PrefetchScalarGridSpec(
        num_scalar_prefetch=1, grid=(batch, n_ch), in_specs=in_specs,
        out_specs=[pl.BlockSpec((None, njc, gw), out_map), pl.BlockSpec((None, njc, gw), out_map),
                   pl.BlockSpec((None, 2 * KV_HEADS, CMP_HIDDEN), lambda b, c, tbl: (b, 0, 0))],
        scratch_shapes=[pltpu.VMEM((2 * KV_HEADS, CMP_HIDDEN), F32), pltpu.VMEM((4, PAGE, LANES), F32),
                        pltpu.VMEM((2 * KV_HEADS, njc, flat), F32)])
    return pl.pallas_call(
        functools.partial(_compress_kernel, n_pg=n_pg, zero_after=zero_after), grid_spec=grid_spec,
        out_shape=[jax.ShapeDtypeStruct((batch, n_ch * njc, gw), F32), jax.ShapeDtypeStruct((batch, n_ch * njc, gw), F32),
                   jax.ShapeDtypeStruct((batch, 2 * KV_HEADS, CMP_HIDDEN), F32)],
        compiler_params=_cparams(("parallel", "arbitrary")), name=name,
    )(jnp.zeros((1,), I32) if table is None else table, *([pages] * n_pg), cw["pos_lo"], cw["pos_hi"], cw["w1lo"], cw["w1hi"], cw["b1"], cw["w2"], cw["b2"], hi_init)


def _cmp_layout(kc_nat, nb, nbp):
    b = kc_nat.shape[0]
    x = kc_nat[:, :4 * nb].reshape(b, nb, 4, kc_nat.shape[-1]).transpose(0, 2, 1, 3)
    x = jnp.pad(x, ((0, 0), (0, 0), (0, nbp - nb), (0, 0)))
    return x.reshape(b, 4 * nbp, kc_nat.shape[-1]).astype(BF16)


SCALE = HEAD_DIM ** -0.5
QSCALE = SCALE * math.log2(math.e)
SUM_ROWS = 16
ACC_ROWS = HEAD_DIM + SUM_ROWS


def _group_queries(q, g):
    nq = q.shape[0]
    keep = (lax.broadcasted_iota(I32, (nq, LANES), 1) >> HEAD_SHIFT) == (g % 2)
    pieces = []
    for hh in range(HPG):
        h = HPG * g + hh
        chunk = q[:, (h // 2) * LANES:(h // 2 + 1) * LANES]
        if h % 2 != g % 2:
            chunk = pltpu.roll(chunk, HEAD_DIM, axis=1)
        pieces.append(jnp.where(keep, chunk, 0.0))
    return (jnp.concatenate(pieces, axis=0) * QSCALE).astype(BF16)


def _kchunk(k, g):
    return k[:, (g // 2) * LANES:(g // 2 + 1) * LANES]


def _values(vt, g):
    return jnp.concatenate([vt[g * HEAD_DIM:(g + 1) * HEAD_DIM, :], jnp.ones((SUM_ROWS, vt.shape[1]), BF16)], axis=0)


def _flash_groups(ss, vt, m_s, acc_s):
    ps, alphas = [], []
    for g in range(KV_HEADS):
        m_old = m_s[g]
        m_new = jnp.maximum(m_old, jnp.max(ss[g], axis=0, keepdims=True))
        alphas.append(jnp.exp2(m_old - m_new))
        ps.append(jnp.exp2(ss[g] - m_new).astype(BF16))
        m_s[g] = m_new
    for g in range(KV_HEADS):
        acc_s[g] = acc_s[g] * alphas[g] + _dot(_values(vt, g), ps[g])


def _one_shot_groups(ss, vt):
    ps = [jnp.exp2(s - jnp.max(s, axis=0, keepdims=True)).astype(BF16) for s in ss]
    return [_dot(_values(vt, g), ps[g]) for g in range(KV_HEADS)]


def _init_state(m_s, acc_s):
    m_s[...] = jnp.full(m_s.shape, NEG, F32)
    acc_s[...] = jnp.zeros(acc_s.shape, F32)


def _topk_bias(score, n_sel):
    n_iota = lax.broadcasted_iota(I32, score.shape, 0)

    def body(_, sc):
        mx = jnp.max(sc, axis=0, keepdims=True)
        idx = jnp.min(jnp.where(sc == mx, n_iota, score.shape[0]), axis=0, keepdims=True)
        return jnp.where(n_iota == idx, -jnp.inf, sc)

    left = lax.fori_loop(0, n_sel, body, score)
    return jnp.where(jnp.logical_and(left == -jnp.inf, score > -jnp.inf), 0.0, NEG)


def _cmp_valid(qpos_w, nb, nbp):
    n_w = lax.broadcasted_iota(I32, (nbp, qpos_w.shape[1]), 0)
    return jnp.concatenate(
        [jnp.logical_and((4 * n_w + c) * CMP_STRIDE + (CMP_BLOCK - 1) <= qpos_w, n_w < nb) for c in range(4)], axis=0)


def _cmp_branch(s, valid, vct_ref, g, qpos_w, qpos_q, nb, nbp):
    s = jnp.where(valid, s, NEG)
    e = jnp.exp2(s - jnp.max(s, axis=0, keepdims=True))
    some = (qpos_w >= CMP_BLOCK - 1).astype(F32)
    p = e * (some / jnp.maximum(jnp.sum(e, axis=0, keepdims=True), 1e-30))
    o_cmp = _dot(vct_ref[g * HEAD_DIM:(g + 1) * HEAD_DIM, :], p.astype(BF16))
    ps = ((p[:, 0:LANES] + p[:, LANES:2 * LANES]) + p[:, 2 * LANES:3 * LANES]) + p[:, 3 * LANES:4 * LANES]
    parts = [ps[c * nbp:(c + 1) * nbp] for c in range(4)]
    n_q = lax.broadcasted_iota(I32, (nbp, LANES), 0)
    prev = jnp.where(n_q >= 1, pltpu.roll(parts[3], 1, axis=0), 0.0)
    score = (((parts[0] + parts[1]) + parts[2]) + parts[3]) + prev
    cur = qpos_q >> SEL_SHIFT
    forced = (n_q == 0) | (n_q == cur) | (n_q == cur - 1)
    score = jnp.where(forced, BIG, jnp.where(n_q * SEL_BLOCK <= qpos_q, score, -BIG))
    score = jnp.where(n_q < nb, score, -jnp.inf)
    return o_cmp, score


def _compressed_and_select(q, kc_ref, vct_ref, qpos_w, qpos_q, ocmp_s, score_s, bias_s, nb, nbp, n_eff):
    kc_all = jnp.concatenate([kc_ref[c * nbp:c * nbp + n_eff, :] for c in range(4)], axis=0)
    vct_all = jnp.concatenate([vct_ref[:, c * nbp:c * nbp + n_eff] for c in range(4)], axis=1)
    cs = [_nt(_kchunk(kc_all, g), _group_queries(q, g)) for g in range(KV_HEADS)]
    valid = _cmp_valid(qpos_w, nb, n_eff)
    for g in range(KV_HEADS):
        o_cmp, score = _cmp_branch(cs[g], valid, vct_all, g, qpos_w, qpos_q, nb, n_eff)
        ocmp_s[g] = o_cmp
        score_s[0:n_eff, g * LANES:(g + 1) * LANES] = score
    bias = _topk_bias(score_s[0:n_eff, :], min(N_SELECT, nb))
    for g in range(KV_HEADS):
        b = bias[:, g * LANES:(g + 1) * LANES]
        bias_s[g, 0:n_eff, :] = jnp.concatenate([b] * HPG, axis=1)
        if n_eff < nbp:
            bias_s[g, n_eff:nbp, :] = jnp.full((nbp - n_eff, HPG * LANES), NEG, F32)


def _block_bias(bias_rows, n_blk):
    w = bias_rows.shape[1]
    return jnp.concatenate([jnp.broadcast_to(bias_rows[u:u + 1, :], (SEL_BLOCK, w)) for u in range(n_blk)], axis=0)


def _gate_row(gates_t, c, g):
    return jnp.concatenate(
        [gates_t[(c * HPG + hh) * KV_HEADS + g:(c * HPG + hh) * KV_HEADS + g + 1, :] for hh in range(HPG)], axis=1)


def _finish(o_ref, acc_s, ocmp_s, wins, ot_s, gates_t):
    nq = o_ref.shape[0]
    for g in range(KV_HEADS):
        o = ocmp_s[g] * _gate_row(gates_t, 0, g)
        for c, acc in ((1, acc_s[g]), (2, wins[g])):
            o = o + acc[:HEAD_DIM] * (_gate_row(gates_t, c, g) / acc[HEAD_DIM:HEAD_DIM + 1])
        ot_s[g * HEAD_DIM:(g + 1) * HEAD_DIM, :] = o
    o_t = ot_s[...].T
    for hh in range(HPG):
        o_ref[:, hh * KV_HEADS * HEAD_DIM:(hh + 1) * KV_HEADS * HEAD_DIM] = o_t[hh * nq:(hh + 1) * nq, :]


def _attn_prompt_kernel(q_ref, gp_ref, bg_ref, cos_ref, sin_ref, kc_ref, vct_ref, ks_ref, vst_ref, kw_ref, vwt_ref,
                        o_ref, m_s, acc_s, score_s, bias_s, ocmp_s, ot_s, qzr_s, *, nb, nbp):
    i = pl.program_id(1)
    nq = Q_TILE
    w = HPG * nq
    s0 = i * nq
    q = q_ref[...]
    q_rot = _rope(q, cos_ref[...], sin_ref[...])
    gates_t = jax.nn.sigmoid(gp_ref[...] + bg_ref[...]).T
    qpos_w = s0 + (lax.broadcasted_iota(I32, (1, w), 1) & (nq - 1))
    qpos_q = s0 + lax.broadcasted_iota(I32, (1, LANES), 1)
    _init_state(m_s, acc_s)
    for g in range(KV_HEADS):
        qzr_s[g] = _group_queries(q_rot, g)
    n_cls = 4 if nbp % (4 * SUBLANES) == 0 and nbp // 4 >= N_SELECT else 1
    per_cls = nbp // n_cls
    cls = jnp.minimum((2 * (i + 1) + per_cls - 1) // per_cls, n_cls) - 1
    for k in range(n_cls):
        @pl.when(cls == k)
        def _(k=k):
            _compressed_and_select(q, kc_ref, vct_ref, qpos_w, qpos_q, ocmp_s, score_s, bias_s, nb, nbp,
                                   (k + 1) * per_cls)

    blk_per_tile = KEY_TILE // SEL_BLOCK

    def slc_tile(t, causal, n_tiles=1):
        keys = n_tiles * KEY_TILE
        k0 = pl.multiple_of(t * KEY_TILE, KEY_TILE)
        kt = ks_ref[pl.ds(k0, keys), :]
        vts = vst_ref[pl.ds(t, n_tiles)]
        vt = vts[0] if n_tiles == 1 else jnp.concatenate([vts[u] for u in range(n_tiles)], axis=1)
        ss = []
        for g in range(KV_HEADS):
            rows = bias_s[g, pl.ds(pl.multiple_of(t * blk_per_tile, blk_per_tile), n_tiles * blk_per_tile), :]
            s = _nt(_kchunk(kt, g), qzr_s[g]) + _block_bias(rows, n_tiles * blk_per_tile)
            if causal:
                s = jnp.where(k0 + lax.broadcasted_iota(I32, (keys, w), 0) <= qpos_w, s, NEG)
            ss.append(s)
        _flash_groups(ss, vt, m_s, acc_s)

    t_diag = s0 // KEY_TILE

    def tile_pair(t2, carry):
        slc_tile(2 * t2, False, 2)
        return carry

    lax.fori_loop(0, t_diag // 2, tile_pair, 0)

    @pl.when(t_diag % 2 == 1)
    def _():
        slc_tile(t_diag - 1, False)

    slc_tile(t_diag, True)

    n_wt = (WINDOW + nq) // PAGE
    t0 = jnp.maximum(i - WINDOW // PAGE, 0)
    k0 = pl.multiple_of(t0 * PAGE, PAGE)
    kwin = kw_ref[pl.ds(k0, n_wt * PAGE), :]
    vwin_tiles = vwt_ref[pl.ds(t0, n_wt)]
    vwin = jnp.concatenate([vwin_tiles[u] for u in range(n_wt)], axis=1)
    kpos = k0 + lax.broadcasted_iota(I32, (n_wt * PAGE, w), 0)
    wmask = jnp.logical_and(kpos <= qpos_w, qpos_w - kpos < WINDOW)
    ws = [jnp.where(wmask, _nt(_kchunk(kwin, g), qzr_s[g]), NEG) for g in range(KV_HEADS)]
    _finish(o_ref, acc_s, ocmp_s, _one_shot_groups(ws, vwin), ot_s, gates_t)


def _attn_prompt(q, gate_pre, b_gate, cos, sin, kc, vct, ks, vst, kw, vwt, batch, seq, name):
    assert seq % KEY_TILE == 0 and seq >= WINDOW + Q_TILE and Q_TILE == LANES
    m, d = q.shape
    nq = Q_TILE
    w = HPG * nq
    nqb = seq // nq
    nb = seq // SEL_BLOCK
    nbp = kc.shape[1] // 4
    gw = KV_HEADS * HEAD_DIM
    ntile = seq // KEY_TILE
    row_map = lambda b, i: (b * nqb + i, 0)
    per_b2 = lambda b, i: (b, 0)
    per_b3 = lambda b, i: (b, 0, 0)
    return pl.pallas_call(
        functools.partial(_attn_prompt_kernel, nb=nb, nbp=nbp), grid=(batch, nqb),
        in_specs=[pl.BlockSpec((nq, d), row_map), pl.BlockSpec((nq, LANES), row_map),
                  pl.BlockSpec((1, LANES), lambda b, i: (0, 0)),
                  pl.BlockSpec((nq, LANES), lambda b, i: (i, 0)), pl.BlockSpec((nq, LANES), lambda b, i: (i, 0)),
                  pl.BlockSpec((None, 4 * nbp, gw), per_b3), pl.BlockSpec((None, gw, 4 * nbp), per_b3),
                  pl.BlockSpec((seq, gw), per_b2), pl.BlockSpec((ntile, gw, KEY_TILE), per_b3),
                  pl.BlockSpec((seq, gw), per_b2), pl.BlockSpec((seq // PAGE, gw, PAGE), per_b3)],
        out_specs=pl.BlockSpec((nq, d), row_map),
        out_shape=jax.ShapeDtypeStruct((m, d), F32),
        scratch_shapes=[pltpu.VMEM((KV_HEADS, 1, w), F32), pltpu.VMEM((KV_HEADS, ACC_ROWS, w), F32),
                        pltpu.VMEM((nbp, KV_HEADS * LANES), F32), pltpu.VMEM((KV_HEADS, nbp, w), F32),
                        pltpu.VMEM((KV_HEADS, HEAD_DIM, w), F32), pltpu.VMEM((gw, w), F32),
                        pltpu.VMEM((KV_HEADS, w, LANES), BF16)],
        compiler_params=_cparams(("parallel", "arbitrary")), name=name,
    )(q, gate_pre, b_gate, cos, sin, kc, vct, ks, vst, kw, vwt)


def _pad_rows(x, n):
    return jnp.concatenate([x, jnp.zeros((n - x.shape[0], x.shape[1]), x.dtype)], axis=0)


def _sample_queries(q):
    n = q.shape[0]
    half = lax.broadcasted_iota(I32, (n, LANES), 1) >> HEAD_SHIFT
    zeros = jnp.zeros((n, LANES), F32)
    blocks = []
    for g in range(KV_HEADS):
        for hh in range(HPG):
            h = HPG * g + hh
            chunk = q[:, (h // 2) * LANES:(h // 2 + 1) * LANES]
            if h % 2 != g % 2:
                chunk = pltpu.roll(chunk, HEAD_DIM, axis=1)
            blk = jnp.where(half == (g % 2), chunk, 0.0)
            blocks.append(jnp.concatenate([blk, zeros] if g < 2 else [zeros, blk], axis=1))
    return (jnp.concatenate(blocks, axis=0) * QSCALE).astype(BF16)


def _with_ones(vt):
    return jnp.concatenate([vt, jnp.ones((SUM_ROWS, vt.shape[1]), BF16)], axis=0)


def _attn_sample_kernel(tbl_ref, *refs, n_pg, n_steps, nb, nbp, pos0, n_new):
    k_refs = refs[:n_pg]
    vt_refs = refs[n_pg:2 * n_pg]
    (q_ref, gp_ref, bg_ref, cos_ref, sin_ref, kc_ref, vct_ref, kvn_ref, cwin_ref, wnew_ref,
     o_ref, m_s, acc_s, bias_s, ocmp_s, qzr_s, gt_s) = refs[2 * n_pg:]
    step = pl.program_id(1)
    gw = KV_HEADS * HEAD_DIM
    per_g = HPG * n_new
    lane1 = lax.broadcasted_iota(I32, (1, LANES), 1)
    qpos = pos0 + (lane1 & (n_new - 1))

    @pl.when(step == 0)
    def _():
        q = q_ref[...]
        qzr_s[...] = _sample_queries(_rope(q, cos_ref[...], sin_ref[...]))
        gt_s[...] = _pad_rows(jax.nn.sigmoid(gp_ref[...] + bg_ref[...]), LANES).T
        m_s[...] = jnp.full(m_s.shape, NEG, F32)
        acc_s[...] = jnp.zeros(acc_s.shape, F32)
        valid = _cmp_valid(qpos, nb, nbp)
        sc = jnp.where(valid, _nt(kc_ref[...], _sample_queries(q)), NEG)
        e = jnp.exp2(sc - jnp.max(sc, axis=0, keepdims=True))
        some = (qpos >= CMP_BLOCK - 1).astype(F32)
        p = e * (some / jnp.maximum(jnp.sum(e, axis=0, keepdims=True), 1e-30))
        ocmp_s[...] = _dot(vct_ref[...], p.astype(BF16))
        in_strip = lax.broadcasted_iota(I32, p.shape, 1) & (per_g - 1)
        ps = p
        for k in range(1, HPG):
            sh = k * n_new
            ps = ps + jnp.where(in_strip >= sh, pltpu.roll(p, sh, axis=1), pltpu.roll(p, LANES - per_g + sh, axis=1))
        parts = [ps[c * nbp:(c + 1) * nbp] for c in range(4)]
        n_q = lax.broadcasted_iota(I32, (nbp, LANES), 0)
        prev = jnp.where(n_q >= 1, pltpu.roll(parts[3], 1, axis=0), 0.0)
        score = (((parts[0] + parts[1]) + parts[2]) + parts[3]) + prev
        cur = qpos >> SEL_SHIFT
        forced = (n_q == 0) | (n_q == cur) | (n_q == cur - 1)
        score = jnp.where(forced, BIG, jnp.where(n_q * SEL_BLOCK <= qpos, score, -BIG))
        score = jnp.where(n_q < nb, score, -jnp.inf)
        bias_s[...] = _topk_bias(score, min(N_SELECT, nb))

    def flash(k_rows, vt, s_bias):
        s = _nt(k_rows, qzr_s[...]) + s_bias
        m_old = m_s[...]
        m_new = jnp.maximum(m_old, jnp.max(s, axis=0, keepdims=True))
        acc_s[...] = acc_s[...] * jnp.exp2(m_old - m_new) + _dot(_with_ones(vt), jnp.exp2(s - m_new).astype(BF16))
        m_s[...] = m_new

    blk_pp = PAGE // SEL_BLOCK
    blk_ps = n_pg * blk_pp
    k_all = jnp.concatenate(
        [jnp.concatenate([r[cp * LANES:(cp + 1) * LANES, :].T for r in k_refs], axis=0) for cp in range(gw // LANES)],
        axis=1).astype(BF16)
    vt = jnp.concatenate([r[...] for r in vt_refs], axis=1).astype(BF16)
    flash(k_all, vt, _block_bias(bias_s[pl.ds(pl.multiple_of(step * blk_ps, blk_ps), blk_ps), :], blk_ps))

    @pl.when(step == n_steps - 1)
    def _():
        krow = lax.broadcasted_iota(I32, (PAGE, LANES), 0)
        kvn = _pad_rows(kvn_ref[...], PAGE)
        nb0 = pos0 // SEL_BLOCK
        tail_bias = jnp.where(pos0 + krow <= qpos, _block_bias(bias_s[nb0:nb0 + blk_pp, :], blk_pp), NEG)
        flash(kvn[:, 2 * gw:3 * gw].astype(BF16), kvn[:, 3 * gw:4 * gw].T.astype(BF16), tail_bias)

        n_cached = cwin_ref.shape[2]
        wn = _pad_rows(wnew_ref[...], PAGE)
        kwin = jnp.concatenate([wn[:, 0:gw], cwin_ref[0].T], axis=0).astype(BF16)
        vwin = jnp.concatenate([wn[:, gw:2 * gw].T, cwin_ref[1]], axis=1).astype(BF16)
        kpos = jnp.concatenate([pos0 + krow, pos0 - n_cached + lax.broadcasted_iota(I32, (n_cached, LANES), 0)], axis=0)
        wmask = jnp.logical_and(jnp.logical_and(kpos <= qpos, qpos - kpos < WINDOW), kpos >= 0)
        ws = jnp.where(wmask, _nt(kwin, qzr_s[...]), NEG)
        wacc = _dot(_with_ones(vwin), jnp.exp2(ws - jnp.max(ws, axis=0, keepdims=True)).astype(BF16))

        gates_t = gt_s[...]
        grows = []
        for c in range(3):
            row = jnp.zeros((1, LANES), F32)
            for g in range(KV_HEADS):
                for hh in range(HPG):
                    col = (c * HPG + hh) * KV_HEADS + g
                    off = g * per_g + hh * n_new
                    src = gates_t[col:col + 1, :]
                    row = jnp.where((lane1 >= off) & (lane1 < off + n_new), src if off == 0 else pltpu.roll(src, off, axis=1), row)
            grows.append(row)
        acc = acc_s[...]
        o = (ocmp_s[...] * grows[0] + acc[0:gw] * (grows[1] / acc[gw:gw + 1])
             + wacc[0:gw] * (grows[2] / wacc[gw:gw + 1]))
        o_t = o.T
        for g in range(KV_HEADS):
            for hh in range(HPG):
                r0 = g * per_g + hh * n_new
                c0 = hh * gw + g * HEAD_DIM
                o_ref[:, c0:c0 + HEAD_DIM] = o_t[r0:r0 + n_new, g * HEAD_DIM:(g + 1) * HEAD_DIM]


def _attn_sample(q, gate_pre, b_gate, cos, sin, kc, vct, pages, table, kv_new, cache_win_t, win_new,
                 batch, n_new, n_pages, name):
    m, d = q.shape
    assert N_HEADS * n_new == LANES and n_new % SUBLANES == 0 and n_new & (n_new - 1) == 0
    n_pg = ATT_PAGES
    assert n_pages % n_pg == 0
    n_steps = n_pages // n_pg
    pos0 = n_pages * PAGE
    nb = (pos0 + n_new + SEL_BLOCK - 1) // SEL_BLOCK
    nbp = kc.shape[1] // 4
    gw = KV_HEADS * HEAD_DIM
    n_cached = cache_win_t.shape[3]
    assert n_cached % PAGE == 0

    def page_map(u, branch):
        return lambda b, s, tbl: (tbl[b * n_pages + s * n_pg + u], branch, 0, 0)

    row_map = lambda b, s, tbl: (b, 0)
    per_b3 = lambda b, s, tbl: (b, 0, 0)
    const2 = lambda b, s, tbl: (0, 0)
    in_specs = [pl.BlockSpec((None, None, gw, PAGE), page_map(u, 2)) for u in range(n_pg)]
    in_specs += [pl.BlockSpec((None, None, gw, PAGE), page_map(u, 3)) for u in range(n_pg)]
    in_specs += [pl.BlockSpec((n_new, d), row_map), pl.BlockSpec((n_new, LANES), row_map),
                 pl.BlockSpec((1, LANES), const2), pl.BlockSpec((n_new, LANES), const2),
                 pl.BlockSpec((n_new, LANES), const2),
                 pl.BlockSpec((None, 4 * nbp, gw), per_b3), pl.BlockSpec((None, gw, 4 * nbp), per_b3),
                 pl.BlockSpec((n_new, 4 * gw), row_map),
                 pl.BlockSpec((None, 2, gw, n_cached), lambda b, s, tbl: (b, 0, 0, 0)),
                 pl.BlockSpec((n_new, 2 * gw), row_map)]
    grid_spec = pltpu.PrefetchScalarGridSpec(
        num_scalar_prefetch=1, grid=(batch, n_steps), in_specs=in_specs,
        out_specs=pl.BlockSpec((n_new, d), row_map),
        scratch_shapes=[pltpu.VMEM((1, LANES), F32), pltpu.VMEM((gw + SUM_ROWS, LANES), F32),
                        pltpu.VMEM((nbp, LANES), F32), pltpu.VMEM((gw, LANES), F32),
                        pltpu.VMEM((LANES, gw), BF16), pltpu.VMEM((LANES, LANES), F32)])
    return pl.pallas_call(
        functools.partial(_attn_sample_kernel, n_pg=n_pg, n_steps=n_steps, nb=nb, nbp=nbp, pos0=pos0, n_new=n_new),
        grid_spec=grid_spec, out_shape=jax.ShapeDtypeStruct((m, d), F32),
        compiler_params=_cparams(("parallel", "arbitrary")), name=name,
    )(table, *([pages] * 2 * n_pg), q, gate_pre, b_gate, cos, sin, kc, vct, kv_new, cache_win_t, win_new)


def _prep_weights(w_a_in, b_a_gate, w_a_out, w_kv, cmp_pos, cmp_w1, cmp_b1, cmp_w2, cmp_b2,
                  w_b_in, b_b_gate, w_b_out, w_ffn_up, w_ffn_down):
    a_q = M_HEADS * M_QK_DIM
    a_v = M_HEADS * M_V_DIM
    n_g = 2 * M_HEADS
    g0 = 2 * a_q + a_v
    gw = KV_HEADS * HEAD_DIM
    qd = N_HEADS * HEAD_DIM
    p = {}
    p["a_main"] = jnp.concatenate([w_a_in[:, :, :g0], w_a_in[:, :, g0 + n_g:]], axis=-1).astype(BF16)
    p["a_qvo"] = jnp.concatenate([w_a_in[:, :, :a_q], w_a_in[:, :, 2 * a_q:g0], w_a_in[:, :, g0 + n_g:]],
                                 axis=-1).astype(BF16)
    p["a_kt"] = w_a_in[:, :, a_q:2 * a_q].transpose(0, 2, 1).astype(BF16)
    p["a_gate"] = jnp.pad(w_a_in[:, :, g0:g0 + n_g], ((0, 0), (0, 0), (0, LANES - n_g))).astype(BF16)
    p["a_bgate"] = jnp.pad(b_a_gate, ((0, 0), (0, LANES - n_g)))[:, None, :]
    p["a_out"] = w_a_out.astype(BF16)
    p["kv"] = w_kv.astype(BF16)
    p["kv_t"] = w_kv.T.astype(BF16)
    p["kv_k"] = jnp.concatenate([w_kv[:, 2 * gw:3 * gw], w_kv[:, 4 * gw:5 * gw]], axis=1).astype(BF16)
    hh, g, c = jnp.meshgrid(jnp.arange(HPG), jnp.arange(KV_HEADS), jnp.arange(3), indexing="ij")
    old_col = ((HPG * g + hh) * 3 + c)
    new_col = ((c * HPG + hh) * KV_HEADS + g)
    order = jnp.zeros((3 * N_HEADS,), I32).at[new_col.reshape(-1)].set(old_col.reshape(-1))
    p["b_q"] = w_b_in[:, :, :qd].astype(BF16)
    p["b_gate"] = jnp.pad(w_b_in[:, :, qd:][:, :, order], ((0, 0), (0, 0), (0, LANES - 3 * N_HEADS))).astype(BF16)
    p["b_bgate"] = jnp.pad(b_b_gate[:, order], ((0, 0), (0, LANES - 3 * N_HEADS)))[:, None, :]
    wo = w_b_out.reshape(w_b_out.shape[0], KV_HEADS, HPG, HEAD_DIM, D_MODEL).transpose(0, 2, 1, 3, 4)
    p["b_out"] = wo.reshape(w_b_out.shape[0], qd, D_MODEL).astype(BF16)
    p["up"] = w_ffn_up.astype(BF16)
    p["down"] = w_ffn_down.astype(BF16)
    flat = CMP_STRIDE * HEAD_DIM
    p["cmp"] = {
        "pos_lo": cmp_pos[:, :CMP_STRIDE].reshape(2, 1, flat), "pos_hi": cmp_pos[:, CMP_STRIDE:].reshape(2, 1, flat),
        "w1lo": cmp_w1[:, :flat].astype(BF16), "w1hi": cmp_w1[:, flat:].astype(BF16),
        "b1": cmp_b1[:, None, :], "w2": cmp_w2.astype(BF16), "b2": cmp_b2[:, None, :]}
    return p


def _trunk(x3, pos0, past, p, g_norms, g_a_hnorm, g_kv, ffn_conv_w, ffn_conv_b, conv_prev, m_c, m_n, m_m, tag):
    batch, seq, d = x3.shape
    x = x3.reshape(batch * seq, d)
    cs, ns, ms, convs = [], [], [], []
    gw = KV_HEADS * HEAD_DIM
    for layer in range(DEPTH):
        g = g_norms[layer]
        nm = f"{tag}{layer}"
        if layer < N_A_LAYERS:
            if seq % MLSTM_CHUNK == 0:
                main, gate, kt = _norm_proj(x, g[0], [p["a_qvo"][layer], p["a_gate"][layer]], nm + "_in",
                                            wts=[p["a_kt"][layer]])
            else:
                main, gate = _norm_proj(x, g[0], [p["a_main"][layer], p["a_gate"][layer]], nm + "_in")
                kt = None
            h, c_new, n_new, m_new = _mlstm(main, gate, kt, p["a_bgate"][layer], g_a_hnorm[layer],
                                            m_c[layer], m_n[layer], m_m[layer], batch, seq, nm + "_mlstm")
            cs.append(c_new)
            ns.append(n_new)
            ms.append(m_new)
            mix = (h, p["a_out"][layer], g[1])
        else:
            if layer == N_A_LAYERS:
                cos, sin = _rope_tables(pos0, seq)
                zeros_hi = jnp.zeros((batch, 2 * KV_HEADS, CMP_HIDDEN), F32)
                if past is None:
                    kvt, wint, ks_b, kw_b, vst, vwt = _kv_rows_minor(x, g_kv, p["kv_t"], p["kv_k"], cos, sin,
                                                                     batch, seq, tag + "_kv")
                    kv_out = kvt.reshape(batch, 4, KV_HEADS, HEAD_DIM, seq).transpose(0, 4, 1, 2, 3)
                    win_out = wint.reshape(batch, 2, KV_HEADS, HEAD_DIM, seq).transpose(0, 4, 1, 2, 3)
                    n_pages = seq // PAGE
                    nb = seq // SEL_BLOCK
                    kc, vc, _ = _compress(kvt.reshape(batch, 4, gw, seq), None, batch, n_pages,
                                          min(CMP_PAGES, n_pages), zeros_hi, p["cmp"], None, tag + "_cmp")
                else:
                    kv4, win = _kv_rows(x, g_kv, p["kv"], jnp.tile(cos, (batch, 1)), jnp.tile(sin, (batch, 1)),
                                        tag + "_kv")
                    kv_out = kv4.reshape(batch, seq, 4, KV_HEADS, HEAD_DIM)
                    win_out = win.reshape(batch, seq, 2, KV_HEADS, HEAD_DIM)
                    pages, table, n_pages, cache_win_t = past
                    nb = (pos0 + seq + SEL_BLOCK - 1) // SEL_BLOCK
                    n_tail = nb * (SEL_BLOCK // CMP_STRIDE) - n_pages * (PAGE // CMP_STRIDE)
                    tail = jnp.pad(kv4[:, :2 * gw].reshape(batch, seq, 2, KV_HEADS, HEAD_DIM),
                                   ((0, 0), (0, PAGE - seq), (0, 0), (0, 0), (0, 0)))
                    kc_t, vc_t, hi_t = _compress(_token_minor(tail), jnp.arange(batch, dtype=I32), batch, 1, 1, zeros_hi,
                                                 p["cmp"], n_tail - 1, tag + "_cmpt")
                    kc_m, vc_m, _ = _compress(pages, table, batch, n_pages, min(CMP_PAGES, n_pages), hi_t,
                                              p["cmp"], None, tag + "_cmp")
                    kc = jnp.concatenate([kc_m, kc_t[:, :n_tail]], axis=1)
                    vc = jnp.concatenate([vc_m, vc_t[:, :n_tail]], axis=1)
                nbp = -(-nb // 32) * 32
                kc_l = _cmp_layout(kc, nb, nbp)
                vct_l = _cmp_layout(vc, nb, nbp).transpose(0, 2, 1)
            bl = layer - N_A_LAYERS
            q_raw, gate_pre = _norm_proj(x, g[0], [p["b_q"][bl], p["b_gate"][bl]], nm + "_in")
            if past is None:
                o = _attn_prompt(q_raw, gate_pre, p["b_bgate"][bl], cos, sin, kc_l, vct_l, ks_b, vst, kw_b, vwt,
                                 batch, seq, nm + "_attn")
            else:
                o = _attn_sample(q_raw, gate_pre, p["b_bgate"][bl], cos, sin, kc_l, vct_l, pages, table,
                                 kv4, cache_win_t, win, batch, seq, n_pages, nm + "_attn")
            mix = (o, p["b_out"][bl], g[1])
        if seq < LONG_SEQ:
            x = _proj_norm_res(*mix[:2], mix[2], x, nm + "_out")
            mix = None
        x, conv_new = _ffn(x, g[2], p["up"][layer], ffn_conv_w[layer], ffn_conv_b[layer], p["down"][layer], g[3],
                           conv_prev[layer], batch, seq, nm + "_ffn", mixer=mix)
        convs.append(conv_new)
    return (x.reshape(batch, seq, d), kv_out, win_out, jnp.stack(cs), jnp.stack(ns), jnp.stack(ms), jnp.stack(convs))


def _past_views(cache_kv, cache_win_kv, page_table):
    return (_token_minor(cache_kv), page_table.reshape(-1), page_table.shape[1], _token_minor(cache_win_kv))


def kernel(x_prompt, x_sample, cache_kv, cache_win_kv, state_mlstm_C, state_mlstm_n, state_mlstm_m, state_conv,
           page_table, g_norms, w_a_in, b_a_gate, g_a_hnorm, w_a_out, g_kv, w_kv, cmp_pos, cmp_w1, cmp_b1, cmp_w2,
           cmp_b2, w_b_in, b_b_gate, w_b_out, w_ffn_up, ffn_conv_w, ffn_conv_b, w_ffn_down):
    p = _prep_weights(w_a_in, b_a_gate, w_a_out, w_kv, cmp_pos, cmp_w1, cmp_b1, cmp_w2, cmp_b2,
                      w_b_in, b_b_gate, w_b_out, w_ffn_up, w_ffn_down)
    dt = x_prompt.dtype
    bp, tp, _ = x_prompt.shape
    bs, ts, _ = x_sample.shape
    past_len = page_table.shape[1] * PAGE
    gw = KV_HEADS * HEAD_DIM
    shared = (p, g_norms, g_a_hnorm, g_kv, ffn_conv_w, ffn_conv_b)

    y_p, kv_p, win_p, c_p, n_p, m_p, conv_p = _trunk(
        x_prompt, 0, None, *shared,
        jnp.zeros((DEPTH, bp, CONV_W - 1, 2 * D_FF), dt),
        jnp.zeros((N_A_LAYERS, bp, M_HEADS, M_QK_DIM, M_V_DIM), dt),
        jnp.zeros((N_A_LAYERS, bp, M_HEADS, M_QK_DIM), dt), jnp.zeros((N_A_LAYERS, bp, M_HEADS), dt), "p")
    win_p = win_p[:, tp - min(WINDOW, tp):]

    past = _past_views(cache_kv, cache_win_kv, page_table)
    y_s, kv_s, win_s, c_s, n_s, m_s, conv_s = _trunk(
        x_sample, past_len, past, *shared, state_conv, state_mlstm_C, state_mlstm_n, state_mlstm_m, "s")
    win_all = jnp.concatenate([cache_win_kv, win_s], axis=1)
    win_s = win_all[:, win_all.shape[1] - min(WINDOW, win_all.shape[1]):]
    return (y_p, y_s, kv_p, kv_s, win_p, win_s, c_p, c_s, n_p, n_s, m_p, m_s, conv_p, conv_s)
```

```python
import functools
import math

import jax
import jax.numpy as jnp
from jax import lax
from jax.experimental import pallas as pl
from jax.experimental.pallas import tpu as pltpu

F32 = jnp.float32
BF16 = jnp.bfloat16
I32 = jnp.int32

D_MODEL = 1024
DEPTH = 4
N_A_LAYERS = 2
M_HEADS = 8
M_QK_DIM = 64
M_V_DIM = 128
GATE_CAP = 15.0
N_HEADS = 16
HEAD_DIM = 64
KV_HEADS = 4
HPG = 4
CMP_BLOCK = 32
CMP_STRIDE = 16
CMP_HIDDEN = 256
SEL_BLOCK = 64
N_SELECT = 16
WINDOW = 512
ROT_DIM = 16
ROPE_THETA = 500000.0
D_FF = 2816
CONV_W = 3
EPS = 1e-6
BIG = 1e9
NEG = -1e30
PAGE = 128

LANES = 128
SUBLANES = 8
VMEM_LIMIT = 56 * 1024 * 1024

MLSTM_CHUNK = 128
KEY_TILE = 512
Q_TILE = 128
CMP_PAGES = 16
ATT_PAGES = 16
FFN_TM = 512
FFN_TN = 1408


def _cparams(sem):
    return pltpu.CompilerParams(dimension_semantics=sem, vmem_limit_bytes=VMEM_LIMIT)


def _rms(x, g):
    return x * lax.rsqrt(jnp.mean(x * x, axis=-1, keepdims=True) + EPS) * g


def _nt(a, b):
    return lax.dot_general(a, b, (((1,), (1,)), ((), ())), preferred_element_type=F32)


def _dot(a, b):
    return jnp.dot(a, b, preferred_element_type=F32)


def _row_tile(m, pref):
    t = min(m, pref)
    while m % t:
        t //= 2
    return t


def _norm_proj_kernel(x_ref, g_ref, *refs, n_w, n_t):
    xn = _rms(x_ref[...], g_ref[...]).astype(BF16)
    n_in = n_w + n_t
    for w_ref, o_ref in zip(refs[:n_w], refs[n_in:n_in + n_w]):
        o_ref[...] = _dot(xn, w_ref[...]).astype(o_ref.dtype)
    for w_ref, o_ref in zip(refs[n_w:n_in], refs[n_in + n_w:]):
        yt = _nt(w_ref[...], xn)
        for u in range(o_ref.shape[0]):
            o_ref[u] = yt[:, u * LANES:(u + 1) * LANES]


def _norm_proj(x, g, ws, name, wts=()):
    m, d = x.shape
    tm = _row_tile(m, 512)
    assert not wts or tm % LANES == 0
    n_w, n_t = len(ws), len(wts)
    in_specs = [pl.BlockSpec((tm, d), lambda i: (i, 0)), pl.BlockSpec((1, d), lambda i: (0, 0))]
    in_specs += [pl.BlockSpec(w.shape, lambda i: (0, 0)) for w in (*ws, *wts)]
    out_specs = [pl.BlockSpec((tm, w.shape[1]), lambda i: (i, 0)) for w in ws]
    out_specs += [pl.BlockSpec((tm // LANES, w.shape[0], LANES), lambda i: (i, 0, 0)) for w in wts]
    out_shape = [jax.ShapeDtypeStruct((m, w.shape[1]), F32) for w in ws]
    out_shape += [jax.ShapeDtypeStruct((m // LANES, w.shape[0], LANES), F32) for w in wts]
    return pl.pallas_call(
        functools.partial(_norm_proj_kernel, n_w=n_w, n_t=n_t), grid=(m // tm,), in_specs=in_specs,
        out_specs=out_specs, out_shape=out_shape, compiler_params=_cparams(("parallel",)),
        name=name)(x, g.reshape(1, d), *ws, *wts)


def _proj_norm_res_kernel(a_ref, w_ref, g_ref, res_ref, o_ref):
    y = _dot(a_ref[...].astype(BF16), w_ref[...])
    o_ref[...] = res_ref[...] + _rms(y, g_ref[...])


def _proj_norm_res(a, w, g, res, name):
    m, k = a.shape
    d = w.shape[1]
    tm = _row_tile(m, 512)
    return pl.pallas_call(
        _proj_norm_res_kernel, grid=(m // tm,),
        in_specs=[pl.BlockSpec((tm, k), lambda i: (i, 0)), pl.BlockSpec((k, d), lambda i: (0, 0)),
                  pl.BlockSpec((1, d), lambda i: (0, 0)), pl.BlockSpec((tm, d), lambda i: (i, 0))],
        out_specs=pl.BlockSpec((tm, d), lambda i: (i, 0)),
        out_shape=jax.ShapeDtypeStruct((m, d), F32),
        compiler_params=_cparams(("parallel",)), name=name)(a, w, g.reshape(1, d), res)


def _mlstm_kernel(main_ref, gate_ref, *refs, rows, nc, k_minor):
    if k_minor:
        kt_ref, *refs = refs
    bg_ref, gh_ref, c0_ref, n0_ref, m0_ref, h_ref, c_ref, n_ref, m_ref, c_s, n_s, m_s = refs
    L = MLSTM_CHUNK
    nh = M_HEADS
    a_q = nh * M_QK_DIM
    v0 = a_q if k_minor else 2 * a_q
    o0 = v0 + nh * M_V_DIM
    cidx = pl.program_id(1)

    @pl.when(cidx == 0)
    def _():
        c_s[...] = c0_ref[...]
        n_s[...] = n0_ref[...]
        m_s[...] = m0_ref[...]

    main = main_ref[...]
    gp = gate_ref[...] + bg_ref[...]
    if rows < L:
        main = jnp.concatenate([main, jnp.zeros((L - rows, main.shape[1]), F32)], axis=0)
        gp = jnp.concatenate([gp, jnp.zeros((L - rows, LANES), F32)], axis=0)
    capped = GATE_CAP * jnp.tanh(gp / GATE_CAP)
    row1 = lax.broadcasted_iota(I32, (L, LANES), 0)
    real = row1 < rows
    ilog = jnp.where(real, capped, -jnp.inf)
    logf = jnp.where(real, jnp.minimum(capped, 0.0) - jnp.log1p(jnp.exp(-jnp.abs(capped))), 0.0)
    bh = logf
    k = 1
    while k < L:
        bh = bh + jnp.where(row1 >= k, pltpu.roll(bh, k, axis=0), 0.0)
        k *= 2
    bh = pltpu.roll(bh, LANES - nh, axis=1)
    c_all = ilog - bh
    cm = c_all
    k = 1
    while k < L:
        cm = jnp.maximum(cm, jnp.where(row1 >= k, pltpu.roll(cm, k, axis=0), -jnp.inf))
        k *= 2
    m_row = m_s[...]
    mt = bh + jnp.maximum(m_row, cm)
    m_new = mt[L - 1:L, :]
    b_last = bh[L - 1:L, :]
    w_inter = jnp.exp(bh + m_row - mt)
    u_all = bh - mt
    emt = jnp.exp(-mt)
    ws_all = jnp.exp(b_last - bh + ilog - m_new)
    decay = jnp.exp(b_last + m_row - m_new)
    c_t = c_all.T
    ws_t = ws_all.T

    rr = lax.broadcasted_iota(I32, (L, L), 0)
    cc = lax.broadcasted_iota(I32, (L, L), 1)
    causal = cc <= rr
    lo_half = lax.broadcasted_iota(I32, (L, LANES), 1) < M_QK_DIM
    lo_rows = lax.broadcasted_iota(I32, (2 * M_QK_DIM, LANES), 0) < M_QK_DIM
    ones_b = jnp.ones((L, LANES), BF16)
    assert L == LANES

    def lanes(a, x):
        return jnp.broadcast_to(a[:, x:x + 1], (L, LANES))

    for p in range(nh // 2):
        qp = main[:, p * LANES:(p + 1) * LANES] * (M_QK_DIM ** -0.5)
        if k_minor:
            kt = kt_ref[0, p * LANES:(p + 1) * LANES, :]
        else:
            kt = main[:, a_q + p * LANES:a_q + (p + 1) * LANES].T
        c_pair = c_s[2 * p:2 * p + 2].reshape(2 * M_QK_DIM, M_V_DIM)
        n_pair = n_s[p]
        kt_b = kt.astype(BF16)
        state_b = jnp.concatenate([c_pair, n_pair], axis=1).astype(BF16)
        for e in range(2):
            x = 2 * p + e
            qx_b = jnp.where(lo_half if e == 0 else jnp.logical_not(lo_half), qp, 0.0).astype(BF16)
            vx = main[:, v0 + x * M_V_DIM:v0 + (x + 1) * M_V_DIM]
            ox = main[:, o0 + x * M_V_DIM:o0 + (x + 1) * M_V_DIM]
            a = jnp.exp(jnp.where(causal, lanes(u_all, x) + c_t[x:x + 1, :], -jnp.inf)) * _dot(qx_b, kt_b)
            inter = _dot(qx_b, state_b)
            intra = _dot(a.astype(BF16), jnp.concatenate([vx.astype(BF16), ones_b], axis=1))
            wi = lanes(w_inter, x)
            num = wi * inter[:, :M_V_DIM] + intra[:, :M_V_DIM]
            den = wi * inter[:, M_V_DIM:] + intra[:, M_V_DIM:]
            h = num / jnp.maximum(jnp.abs(den), lanes(emt, x))
            hn = h * lax.rsqrt(jnp.mean(h * h, axis=-1, keepdims=True) + EPS)
            hn = hn * gh_ref[:, x * M_V_DIM:(x + 1) * M_V_DIM] * jax.nn.sigmoid(ox)
            h_ref[:, x * M_V_DIM:(x + 1) * M_V_DIM] = hn[:rows]
        kwt = kt * jnp.where(lo_rows, ws_t[2 * p:2 * p + 1, :], ws_t[2 * p + 1:2 * p + 2, :])
        vcat = main[:, v0 + 2 * p * M_V_DIM:v0 + (2 * p + 2) * M_V_DIM]
        upd = _dot(kwt.astype(BF16), jnp.concatenate([vcat.astype(BF16), ones_b], axis=1))
        dec_e = decay[:, 2 * p:2 * p + 1]
        dec_o = decay[:, 2 * p + 1:2 * p + 2]
        c_s[2 * p] = dec_e * c_pair[:M_QK_DIM] + upd[:M_QK_DIM, :M_V_DIM]
        c_s[2 * p + 1] = dec_o * c_pair[M_QK_DIM:] + upd[M_QK_DIM:, M_V_DIM:2 * M_V_DIM]
        n_s[p] = jnp.where(lo_rows, dec_e, dec_o) * n_pair + upd[:, 2 * M_V_DIM:]
    m_s[...] = m_new

    @pl.when(cidx == nc - 1)
    def _():
        c_ref[...] = c_s[...]
        n_ref[...] = n_s[...]
        m_ref[...] = m_s[...]


def _mlstm(main, gate, kt, b_gate, g_hnorm, c0, n0, m0, batch, seq, name):
    rows = min(seq, MLSTM_CHUNK)
    nc = seq // rows
    a_v = M_HEADS * M_V_DIM
    hp = M_HEADS // 2
    n_in = jnp.broadcast_to(n0.reshape(batch, hp, 2 * M_QK_DIM, 1), (batch, hp, 2 * M_QK_DIM, LANES))
    m_in = jnp.pad(m0, ((0, 0), (0, LANES - M_HEADS)))[:, None, :]
    k_minor = kt is not None
    assert not k_minor or rows == MLSTM_CHUNK == LANES
    kt_specs = [pl.BlockSpec((1, kt.shape[1], LANES), lambda b, c: (b * nc + c, 0, 0))] if k_minor else []
    h, c, n, m = pl.pallas_call(
        functools.partial(_mlstm_kernel, rows=rows, nc=nc, k_minor=k_minor), grid=(batch, nc),
        in_specs=[pl.BlockSpec((rows, main.shape[1]), lambda b, c: (b * nc + c, 0)),
                  pl.BlockSpec((rows, LANES), lambda b, c: (b * nc + c, 0)), *kt_specs,
                  pl.BlockSpec((1, LANES), lambda b, c: (0, 0)),
                  pl.BlockSpec((1, a_v), lambda b, c: (0, 0)),
                  pl.BlockSpec((None, M_HEADS, M_QK_DIM, M_V_DIM), lambda b, c: (b, 0, 0, 0)),
                  pl.BlockSpec((None, hp, 2 * M_QK_DIM, LANES), lambda b, c: (b, 0, 0, 0)),
                  pl.BlockSpec((None, 1, LANES), lambda b, c: (b, 0, 0))],
        out_specs=[pl.BlockSpec((rows, a_v), lambda b, c: (b * nc + c, 0)),
                   pl.BlockSpec((None, M_HEADS, M_QK_DIM, M_V_DIM), lambda b, c: (b, 0, 0, 0)),
                   pl.BlockSpec((None, hp, 2 * M_QK_DIM, LANES), lambda b, c: (b, 0, 0, 0)),
                   pl.BlockSpec((None, 1, LANES), lambda b, c: (b, 0, 0))],
        out_shape=[jax.ShapeDtypeStruct((batch * seq, a_v), F32),
                   jax.ShapeDtypeStruct((batch, M_HEADS, M_QK_DIM, M_V_DIM), F32),
                   jax.ShapeDtypeStruct((batch, hp, 2 * M_QK_DIM, LANES), F32),
                   jax.ShapeDtypeStruct((batch, 1, LANES), F32)],
        scratch_shapes=[pltpu.VMEM((M_HEADS, M_QK_DIM, M_V_DIM), F32), pltpu.VMEM((hp, 2 * M_QK_DIM, LANES), F32),
                        pltpu.VMEM((1, LANES), F32)],
        compiler_params=_cparams(("parallel", "arbitrary")), name=name,
    )(main, gate, *([kt] if k_minor else []), b_gate, g_hnorm.reshape(1, a_v), c0, n_in, m_in)
    return h, c, n[..., 0].reshape(batch, M_HEADS, M_QK_DIM), m[:, 0, :M_HEADS]


def _ffn_kernel(*refs, carry, tm, n_j, tiles_per_seq, period, mixer):
    if mixer:
        a_ref, wm_ref, g1_ref, *refs = refs
    x_ref, g2_ref, wua_ref, wug_ref, cwa_ref, cwg_ref, cba_ref, cbg_ref, wd_ref, g3_ref = refs[:10]
    if carry and mixer:
        inita_ref, initg_ref, o_ref, sa_ref, sg_ref, xn_s, acc_s, ue_s, carry_s, x_s = refs[10:]
        branch_in = ((wua_ref, cwa_ref, cba_ref, inita_ref, sa_ref), (wug_ref, cwg_ref, cbg_ref, initg_ref, sg_ref))
    elif carry:
        inita_ref, initg_ref, o_ref, sa_ref, sg_ref, xn_s, acc_s, ue_s, carry_s = refs[10:]
        branch_in = ((wua_ref, cwa_ref, cba_ref, inita_ref, sa_ref), (wug_ref, cwg_ref, cbg_ref, initg_ref, sg_ref))
    else:
        t1a_ref, t1g_ref, t2a_ref, t2g_ref, o_ref, sa_ref, sg_ref, xn_s, acc_s, ue_s = refs[10:]
        branch_in = ((wua_ref, cwa_ref, cba_ref, (t1a_ref, t2a_ref), sa_ref),
                     (wug_ref, cwg_ref, cbg_ref, (t1g_ref, t2g_ref), sg_ref))
    i = pl.program_id(0)
    j = pl.program_id(1)
    tn = wd_ref.shape[0]

    @pl.when(j == 0)
    def _():
        x = x_ref[...]
        if mixer:
            x = x + _rms(_dot(a_ref[...].astype(BF16), wm_ref[...]), g1_ref[...])
            x_s[...] = x
        xn_s[...] = _rms(x, g2_ref[...]).astype(BF16)
        acc_s[...] = jnp.zeros_like(acc_s)

    xn = xn_s[...]
    conv = []
    for which, (w_ref, cw_ref, cb_ref, boundary, s_ref) in enumerate(branch_in):
        u = _dot(xn, w_ref[...])
        ue_s[SUBLANES:, :] = u
        if carry:
            slot = which * n_j + j
            first = (i % tiles_per_seq) == 0
            ue_s[SUBLANES - 2:SUBLANES, :] = jnp.where(first, boundary[...], carry_s[slot])
            tap1 = ue_s[pl.ds(SUBLANES - 1, tm), :]
            tap2 = ue_s[pl.ds(SUBLANES - 2, tm), :]
            last2 = u[tm - 2:tm, :]
            carry_s[slot] = last2
            s_ref[i // tiles_per_seq, j] = last2
        else:
            ue_s[0:SUBLANES, :] = jnp.zeros((SUBLANES, tn), F32)
            t = lax.broadcasted_iota(I32, (tm, tn), 0) & (period - 1)
            tap1 = jnp.where(t >= 1, ue_s[pl.ds(SUBLANES - 1, tm), :], boundary[0][...])
            tap2 = jnp.where(t >= 2, ue_s[pl.ds(SUBLANES - 2, tm), :], boundary[1][...])
            s_ref[...] = u
        conv.append(cb_ref[...] + tap2 * cw_ref[0:1, :] + tap1 * cw_ref[1:2, :] + u * cw_ref[2:3, :])
    y = jax.nn.gelu(conv[0], approximate=True) * conv[1]
    acc_s[...] += _dot(y.astype(BF16), wd_ref[...])

    @pl.when(j == n_j - 1)
    def _():
        o_ref[...] = (x_s[...] if mixer else x_ref[...]) + _rms(acc_s[...], g3_ref[...])


def _ffn(x, g2, w_up, conv_w, conv_b, w_down, g3, prev, batch, seq, name, mixer=None):
    m, d = x.shape
    nf = w_down.shape[0]
    tn = FFN_TN
    n_j = nf // tn
    carry = seq >= 256
    cb = conv_b.reshape(1, 2 * nf)
    col_a = lambda i, j: (0, j)
    col_g = lambda i, j: (0, j + n_j)
    common_specs = [
        None,
        pl.BlockSpec((1, d), lambda i, j: (0, 0)),
        pl.BlockSpec((d, tn), col_a), pl.BlockSpec((d, tn), col_g),
        pl.BlockSpec((CONV_W, tn), col_a), pl.BlockSpec((CONV_W, tn), col_g),
        pl.BlockSpec((1, tn), col_a), pl.BlockSpec((1, tn), col_g),
        pl.BlockSpec((tn, d), lambda i, j: (j, 0)),
        pl.BlockSpec((1, d), lambda i, j: (0, 0)),
    ]
    common_args = [x, g2.reshape(1, d), w_up, w_up, conv_w, conv_w, cb, cb, w_down, g3.reshape(1, d)]
    assert mixer is None or carry
    if carry:
        tm = _row_tile(seq, FFN_TM)
        tps = seq // tm
        common_specs[0] = pl.BlockSpec((tm, d), lambda i, j: (i, 0))
        mix_specs, mix_args, mix_scratch = [], [], []
        if mixer is not None:
            a, wm, g1 = mixer
            mix_specs = [pl.BlockSpec((tm, a.shape[1]), lambda i, j: (i, 0)), pl.BlockSpec(wm.shape, lambda i, j: (0, 0)),
                         pl.BlockSpec((1, d), lambda i, j: (0, 0))]
            mix_args = [a, wm, g1.reshape(1, d)]
            mix_scratch = [pltpu.VMEM((tm, d), F32)]
        st_a = lambda i, j: (i // tps, 0, j)
        st_g = lambda i, j: (i // tps, 0, j + n_j)
        out, sa, sg = pl.pallas_call(
            functools.partial(_ffn_kernel, carry=True, tm=tm, n_j=n_j, tiles_per_seq=tps, period=seq,
                              mixer=mixer is not None),
            grid=(m // tm, n_j),
            in_specs=mix_specs + common_specs + [pl.BlockSpec((None, 2, tn), st_a), pl.BlockSpec((None, 2, tn), st_g)],
            out_specs=[pl.BlockSpec((tm, d), lambda i, j: (i, 0)),
                       pl.BlockSpec((batch, n_j, 2, tn), lambda i, j: (0, 0, 0, 0)),
                       pl.BlockSpec((batch, n_j, 2, tn), lambda i, j: (0, 0, 0, 0))],
            out_shape=[jax.ShapeDtypeStruct((m, d), F32), jax.ShapeDtypeStruct((batch, n_j, 2, tn), F32),
                       jax.ShapeDtypeStruct((batch, n_j, 2, tn), F32)],
            scratch_shapes=[pltpu.VMEM((tm, d), BF16), pltpu.VMEM((tm, d), F32),
                            pltpu.VMEM((tm + SUBLANES, tn), F32), pltpu.VMEM((2 * n_j, 2, tn), F32)] + mix_scratch,
            compiler_params=_cparams(("arbitrary", "arbitrary")), name=name,
        )(*mix_args, *common_args, prev, prev)
        sa, sg = (s.transpose(0, 2, 1, 3).reshape(batch, 2, nf) for s in (sa, sg))
        return out, jnp.concatenate([sa, sg], axis=-1)
    tm = m
    assert seq >= 2 and seq & (seq - 1) == 0
    common_specs[0] = pl.BlockSpec((tm, d), lambda i, j: (i, 0))
    tap1 = jnp.pad(prev[:, 1:2], ((0, 0), (0, seq - 1), (0, 0))).reshape(m, 2 * nf)
    tap2 = jnp.pad(prev, ((0, 0), (0, seq - 2), (0, 0))).reshape(m, 2 * nf)
    row_a = lambda i, j: (i, j)
    row_g = lambda i, j: (i, j + n_j)
    out, ua, ug = pl.pallas_call(
        functools.partial(_ffn_kernel, carry=False, tm=tm, n_j=n_j, tiles_per_seq=1, period=seq, mixer=False),
        grid=(m // tm, n_j),
        in_specs=common_specs + [pl.BlockSpec((tm, tn), row_a), pl.BlockSpec((tm, tn), row_g),
                                 pl.BlockSpec((tm, tn), row_a), pl.BlockSpec((tm, tn), row_g)],
        out_specs=[pl.BlockSpec((tm, d), lambda i, j: (i, 0)), pl.BlockSpec((tm, tn), row_a),
                   pl.BlockSpec((tm, tn), row_a)],
        out_shape=[jax.ShapeDtypeStruct((m, d), F32), jax.ShapeDtypeStruct((m, nf), F32),
                   jax.ShapeDtypeStruct((m, nf), F32)],
        scratch_shapes=[pltpu.VMEM((tm, d), BF16), pltpu.VMEM((tm, d), F32), pltpu.VMEM((tm + SUBLANES, tn), F32)],
        compiler_params=_cparams(("arbitrary", "arbitrary")), name=name,
    )(*common_args, tap1, tap1, tap2, tap2)
    u = jnp.concatenate([ua, ug], axis=-1).reshape(batch, seq, 2 * nf)
    return out, u[:, seq - 2:]


def _rope_pair(x, cos, sin):
    half = ROT_DIM // 2
    lane = lax.broadcasted_iota(I32, x.shape, 1) & (HEAD_DIM - 1)
    partner = jnp.where(lane < half, pltpu.roll(x, LANES - half, axis=1), pltpu.roll(x, half, axis=1))
    return x * cos + partner * sin


def _rope(x, cos, sin):
    return jnp.concatenate(
        [_rope_pair(x[:, c * LANES:(c + 1) * LANES], cos, sin) for c in range(x.shape[1] // LANES)], axis=1)


def _rope_tables(pos0, seq):
    half = ROT_DIM // 2
    inv = jnp.power(jnp.float32(ROPE_THETA), -jnp.arange(0, ROT_DIM, 2, dtype=F32) / ROT_DIM)
    ang = (pos0 + jnp.arange(seq, dtype=I32)).astype(F32)[:, None] * inv[None, :]
    cos, sin = jnp.cos(ang), jnp.sin(ang)
    rest = HEAD_DIM - ROT_DIM
    cos_h = jnp.concatenate([cos, cos, jnp.ones((seq, rest), F32)], axis=1)
    sin_h = jnp.concatenate([-sin, sin, jnp.zeros((seq, rest), F32)], axis=1)
    return jnp.tile(cos_h, (1, LANES // HEAD_DIM)), jnp.tile(sin_h, (1, LANES // HEAD_DIM))


def _kv_kernel(x_ref, g_ref, w_ref, cos_ref, sin_ref, kv_ref, win_ref):
    xn = _rms(x_ref[...], g_ref[...]).astype(BF16)
    y = _dot(xn, w_ref[...])
    cos, sin = cos_ref[...], sin_ref[...]
    gw = KV_HEADS * HEAD_DIM
    ks = _rope(y[:, 2 * gw:3 * gw], cos, sin)
    kw = _rope(y[:, 4 * gw:5 * gw], cos, sin)
    kv_ref[:, 0:2 * gw] = y[:, 0:2 * gw]
    kv_ref[:, 2 * gw:3 * gw] = ks
    kv_ref[:, 3 * gw:4 * gw] = y[:, 3 * gw:4 * gw]
    win_ref[:, 0:gw] = kw
    win_ref[:, gw:2 * gw] = y[:, 5 * gw:6 * gw]


def _rope_rows(x, cos_t, sin_t):
    half = ROT_DIM // 2
    row = lax.broadcasted_iota(I32, x.shape, 0) & (HEAD_DIM - 1)
    partner = jnp.where(row < half, pltpu.roll(x, x.shape[0] - half, axis=0), pltpu.roll(x, half, axis=0))
    return x * cos_t + partner * sin_t


def _kv_minor_kernel(x_ref, g_ref, wt_ref, wk_ref, cos_ref, sin_ref, cost_ref, sint_ref,
                     kvt_ref, wint_ref, ks_ref, kw_ref, vst_ref, vwt_ref):
    xn = _rms(x_ref[...], g_ref[...]).astype(BF16)
    gw = KV_HEADS * HEAD_DIM
    yt = _nt(wt_ref[...], xn)
    cos_t, sin_t = cost_ref[...], sint_ref[...]

    def rope_t(a):
        return jnp.concatenate(
            [_rope_rows(a[c * LANES:(c + 1) * LANES], cos_t, sin_t) for c in range(gw // LANES)], axis=0)

    kvt_ref[0:2 * gw, :] = yt[0:2 * gw]
    kvt_ref[2 * gw:3 * gw, :] = rope_t(yt[2 * gw:3 * gw])
    kvt_ref[3 * gw:4 * gw, :] = yt[3 * gw:4 * gw]
    wint_ref[0:gw, :] = rope_t(yt[4 * gw:5 * gw])
    wint_ref[gw:2 * gw, :] = yt[5 * gw:6 * gw]
    vst_ref[...] = yt[3 * gw:4 * gw].astype(BF16)
    for u in range(vwt_ref.shape[0]):
        vwt_ref[u] = yt[5 * gw:6 * gw, u * PAGE:(u + 1) * PAGE].astype(BF16)
    yk = _dot(xn, wk_ref[...])
    cos, sin = cos_ref[...], sin_ref[...]
    ks_ref[...] = _rope(yk[:, 0:gw], cos, sin).astype(BF16)
    kw_ref[...] = _rope(yk[:, gw:2 * gw], cos, sin).astype(BF16)


def _kv_rows_minor(x, g_kv, w_t, w_k, cos, sin, batch, seq, name):
    m, d = x.shape
    tm = _row_tile(seq, KEY_TILE)
    gw = KV_HEADS * HEAD_DIM
    tt = seq // tm
    tok = lambda i: (i % tt, 0)
    tok_t = lambda i: (0, i % tt)
    return pl.pallas_call(
        _kv_minor_kernel, grid=(m // tm,),
        in_specs=[pl.BlockSpec((tm, d), lambda i: (i, 0)), pl.BlockSpec((1, d), lambda i: (0, 0)),
                  pl.BlockSpec(w_t.shape, lambda i: (0, 0)), pl.BlockSpec(w_k.shape, lambda i: (0, 0)),
                  pl.BlockSpec((tm, LANES), tok), pl.BlockSpec((tm, LANES), tok),
                  pl.BlockSpec((LANES, tm), tok_t), pl.BlockSpec((LANES, tm), tok_t)],
        out_specs=[pl.BlockSpec((None, 4 * gw, tm), lambda i: (i // tt, 0, i % tt)),
                   pl.BlockSpec((None, 2 * gw, tm), lambda i: (i // tt, 0, i % tt)),
                   pl.BlockSpec((tm, gw), lambda i: (i, 0)), pl.BlockSpec((tm, gw), lambda i: (i, 0)),
                   pl.BlockSpec((None, gw, tm), lambda i: (i, 0, 0)),
                   pl.BlockSpec((tm // PAGE, gw, PAGE), lambda i: (i, 0, 0))],
        out_shape=[jax.ShapeDtypeStruct((batch, 4 * gw, seq), F32), jax.ShapeDtypeStruct((batch, 2 * gw, seq), F32),
                   jax.ShapeDtypeStruct((m, gw), BF16), jax.ShapeDtypeStruct((m, gw), BF16),
                   jax.ShapeDtypeStruct((m // tm, gw, tm), BF16), jax.ShapeDtypeStruct((m // PAGE, gw, PAGE), BF16)],
        compiler_params=_cparams(("parallel",)), name=name)(x, g_kv.reshape(1, d), w_t, w_k, cos, sin, cos.T, sin.T)


def _kv_rows(x, g_kv, w_kv, cos, sin, name):
    m, d = x.shape
    gw = KV_HEADS * HEAD_DIM
    return pl.pallas_call(
        _kv_kernel, grid=(1,),
        in_specs=[pl.BlockSpec((m, d), lambda i: (0, 0)), pl.BlockSpec((1, d), lambda i: (0, 0)),
                  pl.BlockSpec(w_kv.shape, lambda i: (0, 0)),
                  pl.BlockSpec((m, LANES), lambda i: (0, 0)), pl.BlockSpec((m, LANES), lambda i: (0, 0))],
        out_specs=[pl.BlockSpec((m, 4 * gw), lambda i: (0, 0)), pl.BlockSpec((m, 2 * gw), lambda i: (0, 0))],
        out_shape=[jax.ShapeDtypeStruct((m, 4 * gw), F32), jax.ShapeDtypeStruct((m, 2 * gw), F32)],
        compiler_params=_cparams(("arbitrary",)), name=name)(x, g_kv.reshape(1, d), w_kv, cos, sin)


def _token_minor(rows):
    n_p, r, n = rows.shape[0], rows.shape[1], rows.shape[2]
    return rows.transpose(0, 2, 3, 4, 1).reshape(n_p, n, KV_HEADS * HEAD_DIM, r)


def _compress_kernel(tbl_ref, *refs, n_pg, zero_after):
    page_refs = refs[:n_pg]
    (pos_lo_ref, pos_hi_ref, w1lo_ref, w1hi_ref, b1_ref, w2_ref, b2_ref, hi_init_ref,
     kc_ref, vc_ref, hi_first_ref, carry_s, t_s, x_s) = refs[n_pg:]
    c = pl.program_id(1)
    njp = PAGE // CMP_STRIDE
    njc = n_pg * njp

    @pl.when(c == 0)
    def _():
        carry_s[...] = hi_init_ref[...]

    lo_lanes = lax.broadcasted_iota(I32, (njp, LANES), 1) < HEAD_DIM
    for u in range(n_pg):
        for br in range(2):
            for cp in range(KV_HEADS // 2):
                tb = t_s.at[(2 * u + br) % 2 * 2 + cp]
                tb[...] = page_refs[u][br, cp * LANES:(cp + 1) * LANES, :].T
                for q in range(CMP_STRIDE // 2):
                    p0 = tb[pl.ds(2 * q, njp, stride=CMP_STRIDE), :]
                    p1 = tb[pl.ds(2 * q + 1, njp, stride=CMP_STRIDE), :]
                    rows = slice(u * njp, (u + 1) * njp)
                    cols = slice(q * LANES, (q + 1) * LANES)
                    x_s[br * KV_HEADS + 2 * cp, rows, cols] = jnp.where(lo_lanes, p0, pltpu.roll(p1, HEAD_DIM, axis=1))
                    x_s[br * KV_HEADS + 2 * cp + 1, rows, cols] = jnp.where(lo_lanes, pltpu.roll(p0, HEAD_DIM, axis=1), p1)

    rowl = lax.broadcasted_iota(I32, (KV_HEADS * njc, CMP_HIDDEN), 0) & (njc - 1)
    for br, out_ref in enumerate((kc_ref, vc_ref)):
        x = x_s[br * KV_HEADS:(br + 1) * KV_HEADS].reshape(KV_HEADS * njc, CMP_STRIDE * HEAD_DIM)
        lo = _dot((x + pos_lo_ref[br]).astype(BF16), w1lo_ref[br])
        hi = _dot((x + pos_hi_ref[br]).astype(BF16), w1hi_ref[br])
        carry_rows = jnp.concatenate(
            [jnp.broadcast_to(carry_s[br * KV_HEADS + g:br * KV_HEADS + g + 1, :], (njc, CMP_HIDDEN))
             for g in range(KV_HEADS)], axis=0)
        hi_next = jnp.where(rowl == njc - 1, carry_rows, pltpu.roll(hi, KV_HEADS * njc - 1, axis=0))
        if zero_after is not None:
            hi_next = jnp.where(jnp.logical_and(rowl == zero_after, c == 0), 0.0, hi_next)
        for g in range(KV_HEADS):
            carry_s[br * KV_HEADS + g:br * KV_HEADS + g + 1, :] = hi[g * njc:g * njc + 1, :]
        h = jax.nn.gelu(lo + hi_next + b1_ref[br], approximate=True)
        o = _dot(h.astype(BF16), w2_ref[br]) + b2_ref[br]
        for g in range(KV_HEADS):
            out_ref[:, g * HEAD_DIM:(g + 1) * HEAD_DIM] = o[g * njc:(g + 1) * njc, :]
    hi_first_ref[...] = carry_s[...]


def _compress(pages, table, batch, n_pages, n_pg, hi_init, cw, zero_after, name):
    njp = PAGE // CMP_STRIDE
    njc = n_pg * njp
    assert njc & (njc - 1) == 0 and njc % SUBLANES == 0 and n_pages % n_pg == 0
    n_ch = n_pages // n_pg
    gw = KV_HEADS * HEAD_DIM
    flat = CMP_STRIDE * HEAD_DIM

    def page_map(u):
        if table is None:
            return lambda b, c, tbl: (b, 0, 0, (n_ch - 1 - c) * n_pg + u)
        return lambda b, c, tbl: (tbl[b * n_pages + (n_ch - 1 - c) * n_pg + u], 0, 0, 0)

    full3 = lambda b, c, tbl: (0, 0, 0)
    in_specs = [pl.BlockSpec((None, 2, gw, PAGE), page_map(u)) for u in range(n_pg)]
    in_specs += [pl.BlockSpec((2, 1, flat), full3), pl.BlockSpec((2, 1, flat), full3),
                 pl.BlockSpec((2, flat, CMP_HIDDEN), full3), pl.BlockSpec((2, flat, CMP_HIDDEN), full3),
                 pl.BlockSpec((2, 1, CMP_HIDDEN), full3), pl.BlockSpec((2, CMP_HIDDEN, HEAD_DIM), full3),
                 pl.BlockSpec((2, 1, HEAD_DIM), full3),
                 pl.BlockSpec((None, 2 * KV_HEADS, CMP_HIDDEN), lambda b, c, tbl: (b, 0, 0))]
    out_map = lambda b, c, tbl: (b, n_ch - 1 - c, 0)
    grid_spec = pltpu.PrefetchScalarGridSpec(
        num_scalar_prefetch=1, grid=(batch, n_ch), in_specs=in_specs,
        out_specs=[pl.BlockSpec((None, njc, gw), out_map), pl.BlockSpec((None, njc, gw), out_map),
                   pl.BlockSpec((None, 2 * KV_HEADS, CMP_HIDDEN), lambda b, c, tbl: (b, 0, 0))],
        scratch_shapes=[pltpu.VMEM((2 * KV_HEADS, CMP_HIDDEN), F32), pltpu.VMEM((4, PAGE, LANES), F32),
                        pltpu.VMEM((2 * KV_HEADS, njc, flat), F32)])
    return pl.pallas_call(
        functools.partial(_compress_kernel, n_pg=n_pg, zero_after=zero_after), grid_spec=grid_spec,
        out_shape=[jax.ShapeDtypeStruct((batch, n_ch * njc, gw), F32), jax.ShapeDtypeStruct((batch, n_ch * njc, gw), F32),
                   jax.ShapeDtypeStruct((batch, 2 * KV_HEADS, CMP_HIDDEN), F32)],
        compiler_params=_cparams(("parallel", "arbitrary")), name=name,
    )(jnp.zeros((1,), I32) if table is None else table, *([pages] * n_pg), cw["pos_lo"], cw["pos_hi"], cw["w1lo"], cw["w1hi"], cw["b1"], cw["w2"], cw["b2"], hi_init)


def _cmp_layout(kc_nat, nb, nbp):
    b = kc_nat.shape[0]
    x = kc_nat[:, :4 * nb].reshape(b, nb, 4, kc_nat.shape[-1]).transpose(0, 2, 1, 3)
    x = jnp.pad(x, ((0, 0), (0, 0), (0, nbp - nb), (0, 0)))
    return x.reshape(b, 4 * nbp, kc_nat.shape[-1]).astype(BF16)


SCALE = HEAD_DIM ** -0.5
QSCALE = SCALE * math.log2(math.e)
SUM_ROWS = 16
ACC_ROWS = HEAD_DIM + SUM_ROWS


def _group_queries(q, g):
    nq = q.shape[0]
    keep = (lax.broadcasted_iota(I32, (nq, LANES), 1) >> 6) == (g % 2)
    pieces = []
    for hh in range(HPG):
        h = HPG * g + hh
        chunk = q[:, (h // 2) * LANES:(h // 2 + 1) * LANES]
        if h % 2 != g % 2:
            chunk = pltpu.roll(chunk, HEAD_DIM, axis=1)
        pieces.append(jnp.where(keep, chunk, 0.0))
    return (jnp.concatenate(pieces, axis=0) * QSCALE).astype(BF16)


def _kchunk(k, g):
    return k[:, (g // 2) * LANES:(g // 2 + 1) * LANES]


def _values(vt, g):
    return jnp.concatenate([vt[g * HEAD_DIM:(g + 1) * HEAD_DIM, :], jnp.ones((SUM_ROWS, vt.shape[1]), BF16)], axis=0)


def _flash_groups(ss, vt, m_s, acc_s):
    ps, alphas = [], []
    for g in range(KV_HEADS):
        m_old = m_s[g]
        m_new = jnp.maximum(m_old, jnp.max(ss[g], axis=0, keepdims=True))
        alphas.append(jnp.exp2(m_old - m_new))
        ps.append(jnp.exp2(ss[g] - m_new).astype(BF16))
        m_s[g] = m_new
    for g in range(KV_HEADS):
        acc_s[g] = acc_s[g] * alphas[g] + _dot(_values(vt, g), ps[g])


def _one_shot_groups(ss, vt):
    ps = [jnp.exp2(s - jnp.max(s, axis=0, keepdims=True)).astype(BF16) for s in ss]
    return [_dot(_values(vt, g), ps[g]) for g in range(KV_HEADS)]


def _init_state(m_s, acc_s):
    m_s[...] = jnp.full(m_s.shape, NEG, F32)
    acc_s[...] = jnp.zeros(acc_s.shape, F32)


def _topk_bias(score, n_sel):
    n_iota = lax.broadcasted_iota(I32, score.shape, 0)

    def body(_, sc):
        mx = jnp.max(sc, axis=0, keepdims=True)
        idx = jnp.min(jnp.where(sc == mx, n_iota, score.shape[0]), axis=0, keepdims=True)
        return jnp.where(n_iota == idx, -jnp.inf, sc)

    left = lax.fori_loop(0, n_sel, body, score)
    return jnp.where(jnp.logical_and(left == -jnp.inf, score > -jnp.inf), 0.0, NEG)


def _cmp_valid(qpos_w, nb, nbp):
    n_w = lax.broadcasted_iota(I32, (nbp, qpos_w.shape[1]), 0)
    return jnp.concatenate(
        [jnp.logical_and((4 * n_w + c) * CMP_STRIDE + (CMP_BLOCK - 1) <= qpos_w, n_w < nb) for c in range(4)], axis=0)


def _cmp_branch(s, valid, vct_ref, g, qpos_w, qpos_q, nb, nbp):
    s = jnp.where(valid, s, NEG)
    e = jnp.exp2(s - jnp.max(s, axis=0, keepdims=True))
    some = (qpos_w >= CMP_BLOCK - 1).astype(F32)
    p = e * (some / jnp.maximum(jnp.sum(e, axis=0, keepdims=True), 1e-30))
    o_cmp = _dot(vct_ref[g * HEAD_DIM:(g + 1) * HEAD_DIM, :], p.astype(BF16))
    ps = ((p[:, 0:LANES] + p[:, LANES:2 * LANES]) + p[:, 2 * LANES:3 * LANES]) + p[:, 3 * LANES:4 * LANES]
    parts = [ps[c * nbp:(c + 1) * nbp] for c in range(4)]
    n_q = lax.broadcasted_iota(I32, (nbp, LANES), 0)
    prev = jnp.where(n_q >= 1, pltpu.roll(parts[3], 1, axis=0), 0.0)
    score = (((parts[0] + parts[1]) + parts[2]) + parts[3]) + prev
    cur = qpos_q >> 6
    forced = (n_q == 0) | (n_q == cur) | (n_q == cur - 1)
    score = jnp.where(forced, BIG, jnp.where(n_q * SEL_BLOCK <= qpos_q, score, -BIG))
    score = jnp.where(n_q < nb, score, -jnp.inf)
    return o_cmp, score


def _compressed_and_select(q, kc_ref, vct_ref, qpos_w, qpos_q, ocmp_s, score_s, bias_s, nb, nbp, n_eff):
    kc_all = jnp.concatenate([kc_ref[c * nbp:c * nbp + n_eff, :] for c in range(4)], axis=0)
    vct_all = jnp.concatenate([vct_ref[:, c * nbp:c * nbp + n_eff] for c in range(4)], axis=1)
    cs = [_nt(_kchunk(kc_all, g), _group_queries(q, g)) for g in range(KV_HEADS)]
    valid = _cmp_valid(qpos_w, nb, n_eff)
    for g in range(KV_HEADS):
        o_cmp, score = _cmp_branch(cs[g], valid, vct_all, g, qpos_w, qpos_q, nb, n_eff)
        ocmp_s[g] = o_cmp
        score_s[0:n_eff, g * LANES:(g + 1) * LANES] = score
    bias = _topk_bias(score_s[0:n_eff, :], min(N_SELECT, nb))
    for g in range(KV_HEADS):
        b = bias[:, g * LANES:(g + 1) * LANES]
        bias_s[g, 0:n_eff, :] = jnp.concatenate([b] * HPG, axis=1)
        if n_eff < nbp:
            bias_s[g, n_eff:nbp, :] = jnp.full((nbp - n_eff, HPG * LANES), NEG, F32)


def _block_bias(bias_rows, n_blk):
    w = bias_rows.shape[1]
    return jnp.concatenate([jnp.broadcast_to(bias_rows[u:u + 1, :], (SEL_BLOCK, w)) for u in range(n_blk)], axis=0)


def _gate_row(gates_t, c, g):
    return jnp.concatenate(
        [gates_t[(c * HPG + hh) * KV_HEADS + g:(c * HPG + hh) * KV_HEADS + g + 1, :] for hh in range(HPG)], axis=1)


def _finish(o_ref, acc_s, ocmp_s, wins, ot_s, gates_t):
    nq = o_ref.shape[0]
    for g in range(KV_HEADS):
        o = ocmp_s[g] * _gate_row(gates_t, 0, g)
        for c, acc in ((1, acc_s[g]), (2, wins[g])):
            o = o + acc[:HEAD_DIM] * (_gate_row(gates_t, c, g) / acc[HEAD_DIM:HEAD_DIM + 1])
        ot_s[g * HEAD_DIM:(g + 1) * HEAD_DIM, :] = o
    o_t = ot_s[...].T
    for hh in range(HPG):
        o_ref[:, hh * KV_HEADS * HEAD_DIM:(hh + 1) * KV_HEADS * HEAD_DIM] = o_t[hh * nq:(hh + 1) * nq, :]


def _attn_prompt_kernel(q_ref, gp_ref, bg_ref, cos_ref, sin_ref, kc_ref, vct_ref, ks_ref, vst_ref, kw_ref, vwt_ref,
                        o_ref, m_s, acc_s, score_s, bias_s, ocmp_s, ot_s, qzr_s, *, nb, nbp):
    i = pl.program_id(1)
    nq = Q_TILE
    w = HPG * nq
    s0 = i * nq
    q = q_ref[...]
    q_rot = _rope(q, cos_ref[...], sin_ref[...])
    gates_t = jax.nn.sigmoid(gp_ref[...] + bg_ref[...]).T
    qpos_w = s0 + (lax.broadcasted_iota(I32, (1, w), 1) & (nq - 1))
    qpos_q = s0 + lax.broadcasted_iota(I32, (1, LANES), 1)
    _init_state(m_s, acc_s)
    for g in range(KV_HEADS):
        qzr_s[g] = _group_queries(q_rot, g)
    n_cls = 4 if nbp % (4 * SUBLANES) == 0 and nbp // 4 >= N_SELECT else 1
    per_cls = nbp // n_cls
    cls = jnp.minimum((2 * (i + 1) + per_cls - 1) // per_cls, n_cls) - 1
    for k in range(n_cls):
        @pl.when(cls == k)
        def _(k=k):
            _compressed_and_select(q, kc_ref, vct_ref, qpos_w, qpos_q, ocmp_s, score_s, bias_s, nb, nbp,
                                   (k + 1) * per_cls)

    blk_per_tile = KEY_TILE // SEL_BLOCK

    def slc_tile(t, causal, n_tiles=1):
        keys = n_tiles * KEY_TILE
        k0 = pl.multiple_of(t * KEY_TILE, KEY_TILE)
        kt = ks_ref[pl.ds(k0, keys), :]
        vts = vst_ref[pl.ds(t, n_tiles)]
        vt = vts[0] if n_tiles == 1 else jnp.concatenate([vts[u] for u in range(n_tiles)], axis=1)
        ss = []
        for g in range(KV_HEADS):
            rows = bias_s[g, pl.ds(pl.multiple_of(t * blk_per_tile, blk_per_tile), n_tiles * blk_per_tile), :]
            s = _nt(_kchunk(kt, g), qzr_s[g]) + _block_bias(rows, n_tiles * blk_per_tile)
            if causal:
                s = jnp.where(k0 + lax.broadcasted_iota(I32, (keys, w), 0) <= qpos_w, s, NEG)
            ss.append(s)
        _flash_groups(ss, vt, m_s, acc_s)

    t_diag = s0 // KEY_TILE

    def tile_quad(t4, carry):
        slc_tile(4 * t4, False, 4)
        return carry

    lax.fori_loop(0, t_diag // 4, tile_quad, 0)

    def tile_one(t, carry):
        slc_tile(t, False)
        return carry

    lax.fori_loop(4 * (t_diag // 4), t_diag, tile_one, 0)

    slc_tile(t_diag, True)

    n_wt = (WINDOW + nq) // PAGE
    t0 = jnp.maximum(i - WINDOW // PAGE, 0)
    k0 = pl.multiple_of(t0 * PAGE, PAGE)
    kwin = kw_ref[pl.ds(k0, n_wt * PAGE), :]
    vwin_tiles = vwt_ref[pl.ds(t0, n_wt)]
    vwin = jnp.concatenate([vwin_tiles[u] for u in range(n_wt)], axis=1)
    kpos = k0 + lax.broadcasted_iota(I32, (n_wt * PAGE, w), 0)
    wmask = jnp.logical_and(kpos <= qpos_w, qpos_w - kpos < WINDOW)
    ws = [jnp.where(wmask, _nt(_kchunk(kwin, g), qzr_s[g]), NEG) for g in range(KV_HEADS)]
    _finish(o_ref, acc_s, ocmp_s, _one_shot_groups(ws, vwin), ot_s, gates_t)


def _attn_prompt(q, gate_pre, b_gate, cos, sin, kc, vct, ks, vst, kw, vwt, batch, seq, name):
    assert seq % KEY_TILE == 0 and seq >= WINDOW + Q_TILE and Q_TILE == LANES
    m, d = q.shape
    nq = Q_TILE
    w = HPG * nq
    nqb = seq // nq
    nb = seq // SEL_BLOCK
    nbp = kc.shape[1] // 4
    gw = KV_HEADS * HEAD_DIM
    ntile = seq // KEY_TILE
    row_map = lambda b, i: (b * nqb + i, 0)
    per_b2 = lambda b, i: (b, 0)
    per_b3 = lambda b, i: (b, 0, 0)
    return pl.pallas_call(
        functools.partial(_attn_prompt_kernel, nb=nb, nbp=nbp), grid=(batch, nqb),
        in_specs=[pl.BlockSpec((nq, d), row_map), pl.BlockSpec((nq, LANES), row_map),
                  pl.BlockSpec((1, LANES), lambda b, i: (0, 0)),
                  pl.BlockSpec((nq, LANES), lambda b, i: (i, 0)), pl.BlockSpec((nq, LANES), lambda b, i: (i, 0)),
                  pl.BlockSpec((None, 4 * nbp, gw), per_b3), pl.BlockSpec((None, gw, 4 * nbp), per_b3),
                  pl.BlockSpec((seq, gw), per_b2, pipeline_mode=pl.Buffered(1)),
                  pl.BlockSpec((ntile, gw, KEY_TILE), per_b3, pipeline_mode=pl.Buffered(1)),
                  pl.BlockSpec((seq, gw), per_b2, pipeline_mode=pl.Buffered(1)),
                  pl.BlockSpec((seq // PAGE, gw, PAGE), per_b3, pipeline_mode=pl.Buffered(1))],
        out_specs=pl.BlockSpec((nq, d), row_map),
        out_shape=jax.ShapeDtypeStruct((m, d), F32),
        scratch_shapes=[pltpu.VMEM((KV_HEADS, 1, w), F32), pltpu.VMEM((KV_HEADS, ACC_ROWS, w), F32),
                        pltpu.VMEM((nbp, KV_HEADS * LANES), F32), pltpu.VMEM((KV_HEADS, nbp, w), F32),
                        pltpu.VMEM((KV_HEADS, HEAD_DIM, w), F32), pltpu.VMEM((gw, w), F32),
                        pltpu.VMEM((KV_HEADS, w, LANES), BF16)],
        compiler_params=_cparams(("parallel", "arbitrary")), name=name,
    )(q, gate_pre, b_gate, cos, sin, kc, vct, ks, vst, kw, vwt)


def _pad_rows(x, n):
    return jnp.concatenate([x, jnp.zeros((n - x.shape[0], x.shape[1]), x.dtype)], axis=0)


def _sample_queries(q):
    n = q.shape[0]
    half = lax.broadcasted_iota(I32, (n, LANES), 1) >> 6
    zeros = jnp.zeros((n, LANES), F32)
    blocks = []
    for g in range(KV_HEADS):
        for hh in range(HPG):
            h = HPG * g + hh
            chunk = q[:, (h // 2) * LANES:(h // 2 + 1) * LANES]
            if h % 2 != g % 2:
                chunk = pltpu.roll(chunk, HEAD_DIM, axis=1)
            blk = jnp.where(half == (g % 2), chunk, 0.0)
            blocks.append(jnp.concatenate([blk, zeros] if g < 2 else [zeros, blk], axis=1))
    return (jnp.concatenate(blocks, axis=0) * QSCALE).astype(BF16)


def _with_ones(vt):
    return jnp.concatenate([vt, jnp.ones((SUM_ROWS, vt.shape[1]), BF16)], axis=0)


def _attn_sample_kernel(tbl_ref, *refs, n_pg, n_steps, nb, nbp, pos0, n_new):
    k_refs = refs[:n_pg]
    vt_refs = refs[n_pg:2 * n_pg]
    (q_ref, gp_ref, bg_ref, cos_ref, sin_ref, kc_ref, vct_ref, kvn_ref, cwin_ref, wnew_ref,
     o_ref, m_s, acc_s, bias_s, ocmp_s, qzr_s, gt_s) = refs[2 * n_pg:]
    step = pl.program_id(1)
    gw = KV_HEADS * HEAD_DIM
    per_g = HPG * n_new
    lane1 = lax.broadcasted_iota(I32, (1, LANES), 1)
    qpos = pos0 + (lane1 & (n_new - 1))

    @pl.when(step == 0)
    def _():
        q = q_ref[...]
        qzr_s[...] = _sample_queries(_rope(q, cos_ref[...], sin_ref[...]))
        gt_s[...] = _pad_rows(jax.nn.sigmoid(gp_ref[...] + bg_ref[...]), LANES).T
        m_s[...] = jnp.full(m_s.shape, NEG, F32)
        acc_s[...] = jnp.zeros(acc_s.shape, F32)
        valid = _cmp_valid(qpos, nb, nbp)
        sc = jnp.where(valid, _nt(kc_ref[...], _sample_queries(q)), NEG)
        e = jnp.exp2(sc - jnp.max(sc, axis=0, keepdims=True))
        some = (qpos >= CMP_BLOCK - 1).astype(F32)
        p = e * (some / jnp.maximum(jnp.sum(e, axis=0, keepdims=True), 1e-30))
        ocmp_s[...] = _dot(vct_ref[...], p.astype(BF16))
        in_strip = lax.broadcasted_iota(I32, p.shape, 1) & (per_g - 1)
        ps = p
        for k in range(1, HPG):
            sh = k * n_new
            ps = ps + jnp.where(in_strip >= sh, pltpu.roll(p, sh, axis=1), pltpu.roll(p, LANES - per_g + sh, axis=1))
        parts = [ps[c * nbp:(c + 1) * nbp] for c in range(4)]
        n_q = lax.broadcasted_iota(I32, (nbp, LANES), 0)
        prev = jnp.where(n_q >= 1, pltpu.roll(parts[3], 1, axis=0), 0.0)
        score = (((parts[0] + parts[1]) + parts[2]) + parts[3]) + prev
        cur = qpos >> 6
        forced = (n_q == 0) | (n_q == cur) | (n_q == cur - 1)
        score = jnp.where(forced, BIG, jnp.where(n_q * SEL_BLOCK <= qpos, score, -BIG))
        score = jnp.where(n_q < nb, score, -jnp.inf)
        bias_s[...] = _topk_bias(score, min(N_SELECT, nb))

    def flash(k_rows, vt, s_bias):
        s = _nt(k_rows, qzr_s[...]) + s_bias
        m_old = m_s[...]
        m_new = jnp.maximum(m_old, jnp.max(s, axis=0, keepdims=True))
        acc_s[...] = acc_s[...] * jnp.exp2(m_old - m_new) + _dot(_with_ones(vt), jnp.exp2(s - m_new).astype(BF16))
        m_s[...] = m_new

    blk_pp = PAGE // SEL_BLOCK
    blk_ps = n_pg * blk_pp
    k_all = jnp.concatenate(
        [jnp.concatenate([r[cp * LANES:(cp + 1) * LANES, :].T for r in k_refs], axis=0) for cp in range(gw // LANES)],
        axis=1).astype(BF16)
    vt = jnp.concatenate([r[...] for r in vt_refs], axis=1).astype(BF16)
    flash(k_all, vt, _block_bias(bias_s[pl.ds(pl.multiple_of(step * blk_ps, blk_ps), blk_ps), :], blk_ps))

    @pl.when(step == n_steps - 1)
    def _():
        krow = lax.broadcasted_iota(I32, (PAGE, LANES), 0)
        kvn = _pad_rows(kvn_ref[...], PAGE)
        nb0 = pos0 // SEL_BLOCK
        tail_bias = jnp.where(pos0 + krow <= qpos, _block_bias(bias_s[nb0:nb0 + blk_pp, :], blk_pp), NEG)
        flash(kvn[:, 2 * gw:3 * gw].astype(BF16), kvn[:, 3 * gw:4 * gw].T.astype(BF16), tail_bias)

        n_cached = cwin_ref.shape[2]
        wn = _pad_rows(wnew_ref[...], PAGE)
        kwin = jnp.concatenate([wn[:, 0:gw], cwin_ref[0].T], axis=0).astype(BF16)
        vwin = jnp.concatenate([wn[:, gw:2 * gw].T, cwin_ref[1]], axis=1).astype(BF16)
        kpos = jnp.concatenate([pos0 + krow, pos0 - n_cached + lax.broadcasted_iota(I32, (n_cached, LANES), 0)], axis=0)
        wmask = jnp.logical_and(jnp.logical_and(kpos <= qpos, qpos - kpos < WINDOW), kpos >= 0)
        ws = jnp.where(wmask, _nt(kwin, qzr_s[...]), NEG)
        wacc = _dot(_with_ones(vwin), jnp.exp2(ws - jnp.max(ws, axis=0, keepdims=True)).astype(BF16))

        gates_t = gt_s[...]
        grows = []
        for c in range(3):
            row = jnp.zeros((1, LANES), F32)
            for g in range(KV_HEADS):
                for hh in range(HPG):
                    col = (c * HPG + hh) * KV_HEADS + g
                    off = g * per_g + hh * n_new
                    src = gates_t[col:col + 1, :]
                    row = jnp.where((lane1 >= off) & (lane1 < off + n_new), src if off == 0 else pltpu.roll(src, off, axis=1), row)
            grows.append(row)
        acc = acc_s[...]
        o = (ocmp_s[...] * grows[0] + acc[0:gw] * (grows[1] / acc[gw:gw + 1])
             + wacc[0:gw] * (grows[2] / wacc[gw:gw + 1]))
        o_t = o.T
        for g in range(KV_HEADS):
            for hh in range(HPG):
                r0 = g * per_g + hh * n_new
                c0 = hh * gw + g * HEAD_DIM
                o_ref[:, c0:c0 + HEAD_DIM] = o_t[r0:r0 + n_new, g * HEAD_DIM:(g + 1) * HEAD_DIM]


def _attn_sample(q, gate_pre, b_gate, cos, sin, kc, vct, pages, table, kv_new, cache_win_t, win_new,
                 batch, n_new, n_pages, name):
    m, d = q.shape
    assert N_HEADS * n_new == LANES and n_new % SUBLANES == 0
    n_pg = ATT_PAGES
    assert n_pages % n_pg == 0
    n_steps = n_pages // n_pg
    pos0 = n_pages * PAGE
    nb = (pos0 + n_new + SEL_BLOCK - 1) // SEL_BLOCK
    nbp = kc.shape[1] // 4
    gw = KV_HEADS * HEAD_DIM
    n_cached = cache_win_t.shape[3]
    assert n_cached % PAGE == 0

    def page_map(u, branch):
        return lambda b, s, tbl: (tbl[b * n_pages + s * n_pg + u], branch, 0, 0)

    row_map = lambda b, s, tbl: (b, 0)
    per_b3 = lambda b, s, tbl: (b, 0, 0)
    const2 = lambda b, s, tbl: (0, 0)
    in_specs = [pl.BlockSpec((None, None, gw, PAGE), page_map(u, 2)) for u in range(n_pg)]
    in_specs += [pl.BlockSpec((None, None, gw, PAGE), page_map(u, 3)) for u in range(n_pg)]
    in_specs += [pl.BlockSpec((n_new, d), row_map), pl.BlockSpec((n_new, LANES), row_map),
                 pl.BlockSpec((1, LANES), const2), pl.BlockSpec((n_new, LANES), const2),
                 pl.BlockSpec((n_new, LANES), const2),
                 pl.BlockSpec((None, 4 * nbp, gw), per_b3), pl.BlockSpec((None, gw, 4 * nbp), per_b3),
                 pl.BlockSpec((n_new, 4 * gw), row_map),
                 pl.BlockSpec((None, 2, gw, n_cached), lambda b, s, tbl: (b, 0, 0, 0)),
                 pl.BlockSpec((n_new, 2 * gw), row_map)]
    grid_spec = pltpu.PrefetchScalarGridSpec(
        num_scalar_prefetch=1, grid=(batch, n_steps), in_specs=in_specs,
        out_specs=pl.BlockSpec((n_new, d), row_map),
        scratch_shapes=[pltpu.VMEM((1, LANES), F32), pltpu.VMEM((gw + SUM_ROWS, LANES), F32),
                        pltpu.VMEM((nbp, LANES), F32), pltpu.VMEM((gw, LANES), F32),
                        pltpu.VMEM((LANES, gw), BF16), pltpu.VMEM((LANES, LANES), F32)])
    return pl.pallas_call(
        functools.partial(_attn_sample_kernel, n_pg=n_pg, n_steps=n_steps, nb=nb, nbp=nbp, pos0=pos0, n_new=n_new),
        grid_spec=grid_spec, out_shape=jax.ShapeDtypeStruct((m, d), F32),
        compiler_params=_cparams(("parallel", "arbitrary")), name=name,
    )(table, *([pages] * 2 * n_pg), q, gate_pre, b_gate, cos, sin, kc, vct, kv_new, cache_win_t, win_new)


def _prep_weights(w_a_in, b_a_gate, w_a_out, w_kv, cmp_pos, cmp_w1, cmp_b1, cmp_w2, cmp_b2,
                  w_b_in, b_b_gate, w_b_out, w_ffn_up, w_ffn_down):
    a_q = M_HEADS * M_QK_DIM
    a_v = M_HEADS * M_V_DIM
    n_g = 2 * M_HEADS
    g0 = 2 * a_q + a_v
    gw = KV_HEADS * HEAD_DIM
    qd = N_HEADS * HEAD_DIM
    p = {}
    p["a_main"] = jnp.concatenate([w_a_in[:, :, :g0], w_a_in[:, :, g0 + n_g:]], axis=-1).astype(BF16)
    p["a_qvo"] = jnp.concatenate([w_a_in[:, :, :a_q], w_a_in[:, :, 2 * a_q:g0], w_a_in[:, :, g0 + n_g:]],
                                 axis=-1).astype(BF16)
    p["a_kt"] = w_a_in[:, :, a_q:2 * a_q].transpose(0, 2, 1).astype(BF16)
    p["a_gate"] = jnp.pad(w_a_in[:, :, g0:g0 + n_g], ((0, 0), (0, 0), (0, LANES - n_g))).astype(BF16)
    p["a_bgate"] = jnp.pad(b_a_gate, ((0, 0), (0, LANES - n_g)))[:, None, :]
    p["a_out"] = w_a_out.astype(BF16)
    p["kv"] = w_kv.astype(BF16)
    p["kv_t"] = w_kv.T.astype(BF16)
    p["kv_k"] = jnp.concatenate([w_kv[:, 2 * gw:3 * gw], w_kv[:, 4 * gw:5 * gw]], axis=1).astype(BF16)
    hh, g, c = jnp.meshgrid(jnp.arange(HPG), jnp.arange(KV_HEADS), jnp.arange(3), indexing="ij")
    old_col = ((HPG * g + hh) * 3 + c)
    new_col = ((c * HPG + hh) * KV_HEADS + g)
    order = jnp.zeros((3 * N_HEADS,), I32).at[new_col.reshape(-1)].set(old_col.reshape(-1))
    p["b_q"] = w_b_in[:, :, :qd].astype(BF16)
    p["b_gate"] = jnp.pad(w_b_in[:, :, qd:][:, :, order], ((0, 0), (0, 0), (0, LANES - 3 * N_HEADS))).astype(BF16)
    p["b_bgate"] = jnp.pad(b_b_gate[:, order], ((0, 0), (0, LANES - 3 * N_HEADS)))[:, None, :]
    wo = w_b_out.reshape(w_b_out.shape[0], KV_HEADS, HPG, HEAD_DIM, D_MODEL).transpose(0, 2, 1, 3, 4)
    p["b_out"] = wo.reshape(w_b_out.shape[0], qd, D_MODEL).astype(BF16)
    p["up"] = w_ffn_up.astype(BF16)
    p["down"] = w_ffn_down.astype(BF16)
    flat = CMP_STRIDE * HEAD_DIM
    p["cmp"] = {
        "pos_lo": cmp_pos[:, :CMP_STRIDE].reshape(2, 1, flat), "pos_hi": cmp_pos[:, CMP_STRIDE:].reshape(2, 1, flat),
        "w1lo": cmp_w1[:, :flat].astype(BF16), "w1hi": cmp_w1[:, flat:].astype(BF16),
        "b1": cmp_b1[:, None, :], "w2": cmp_w2.astype(BF16), "b2": cmp_b2[:, None, :]}
    return p


def _trunk(x3, pos0, past, p, g_norms, g_a_hnorm, g_kv, ffn_conv_w, ffn_conv_b, conv_prev, m_c, m_n, m_m, tag):
    batch, seq, d = x3.shape
    x = x3.reshape(batch * seq, d)
    cs, ns, ms, convs = [], [], [], []
    gw = KV_HEADS * HEAD_DIM
    for layer in range(DEPTH):
        g = g_norms[layer]
        nm = f"{tag}{layer}"
        if layer < N_A_LAYERS:
            if seq % MLSTM_CHUNK == 0:
                main, gate, kt = _norm_proj(x, g[0], [p["a_qvo"][layer], p["a_gate"][layer]], nm + "_in",
                                            wts=[p["a_kt"][layer]])
            else:
                main, gate = _norm_proj(x, g[0], [p["a_main"][layer], p["a_gate"][layer]], nm + "_in")
                kt = None
            h, c_new, n_new, m_new = _mlstm(main, gate, kt, p["a_bgate"][layer], g_a_hnorm[layer],
                                            m_c[layer], m_n[layer], m_m[layer], batch, seq, nm + "_mlstm")
            cs.append(c_new)
            ns.append(n_new)
            ms.append(m_new)
            mix = (h, p["a_out"][layer], g[1])
        else:
            if layer == N_A_LAYERS:
                cos, sin = _rope_tables(pos0, seq)
                zeros_hi = jnp.zeros((batch, 2 * KV_HEADS, CMP_HIDDEN), F32)
                if past is None:
                    kvt, wint, ks_b, kw_b, vst, vwt = _kv_rows_minor(x, g_kv, p["kv_t"], p["kv_k"], cos, sin,
                                                                     batch, seq, tag + "_kv")
                    kv_out = kvt.reshape(batch, 4, KV_HEADS, HEAD_DIM, seq).transpose(0, 4, 1, 2, 3)
                    win_out = wint.reshape(batch, 2, KV_HEADS, HEAD_DIM, seq).transpose(0, 4, 1, 2, 3)
                    n_pages = seq // PAGE
                    nb = seq // SEL_BLOCK
                    kc, vc, _ = _compress(kvt.reshape(batch, 4, gw, seq), None, batch, n_pages,
                                          min(CMP_PAGES, n_pages), zeros_hi, p["cmp"], None, tag + "_cmp")
                else:
                    kv4, win = _kv_rows(x, g_kv, p["kv"], jnp.tile(cos, (batch, 1)), jnp.tile(sin, (batch, 1)),
                                        tag + "_kv")
                    kv_out = kv4.reshape(batch, seq, 4, KV_HEADS, HEAD_DIM)
                    win_out = win.reshape(batch, seq, 2, KV_HEADS, HEAD_DIM)
                    pages, table, n_pages, cache_win_t = past
                    nb = (pos0 + seq + SEL_BLOCK - 1) // SEL_BLOCK
                    n_tail = nb * (SEL_BLOCK // CMP_STRIDE) - n_pages * (PAGE // CMP_STRIDE)
                    tail = jnp.pad(kv4[:, :2 * gw].reshape(batch, seq, 2, KV_HEADS, HEAD_DIM),
                                   ((0, 0), (0, PAGE - seq), (0, 0), (0, 0), (0, 0)))
                    kc_t, vc_t, hi_t = _compress(_token_minor(tail), jnp.arange(batch, dtype=I32), batch, 1, 1, zeros_hi,
                                                 p["cmp"], n_tail - 1, tag + "_cmpt")
                    kc_m, vc_m, _ = _compress(pages, table, batch, n_pages, min(CMP_PAGES, n_pages), hi_t,
                                              p["cmp"], None, tag + "_cmp")
                    kc = jnp.concatenate([kc_m, kc_t[:, :n_tail]], axis=1)
                    vc = jnp.concatenate([vc_m, vc_t[:, :n_tail]], axis=1)
                nbp = -(-nb // 32) * 32
                kc_l = _cmp_layout(kc, nb, nbp)
                vct_l = _cmp_layout(vc, nb, nbp).transpose(0, 2, 1)
            bl = layer - N_A_LAYERS
            q_raw, gate_pre = _norm_proj(x, g[0], [p["b_q"][bl], p["b_gate"][bl]], nm + "_in")
            if past is None:
                o = _attn_prompt(q_raw, gate_pre, p["b_bgate"][bl], cos, sin, kc_l, vct_l, ks_b, vst, kw_b, vwt,
                                 batch, seq, nm + "_attn")
            else:
                o = _attn_sample(q_raw, gate_pre, p["b_bgate"][bl], cos, sin, kc_l, vct_l, pages, table,
                                 kv4, cache_win_t, win, batch, seq, n_pages, nm + "_attn")
            mix = (o, p["b_out"][bl], g[1])
        if seq < 256:
            x = _proj_norm_res(*mix[:2], mix[2], x, nm + "_out")
            mix = None
        x, conv_new = _ffn(x, g[2], p["up"][layer], ffn_conv_w[layer], ffn_conv_b[layer], p["down"][layer], g[3],
                           conv_prev[layer], batch, seq, nm + "_ffn", mixer=mix)
        convs.append(conv_new)
    return (x.reshape(batch, seq, d), kv_out, win_out, jnp.stack(cs), jnp.stack(ns), jnp.stack(ms), jnp.stack(convs))


def _past_views(cache_kv, cache_win_kv, page_table):
    return (_token_minor(cache_kv), page_table.reshape(-1), page_table.shape[1], _token_minor(cache_win_kv))


def kernel(x_prompt, x_sample, cache_kv, cache_win_kv, state_mlstm_C, state_mlstm_n, state_mlstm_m, state_conv,
           page_table, g_norms, w_a_in, b_a_gate, g_a_hnorm, w_a_out, g_kv, w_kv, cmp_pos, cmp_w1, cmp_b1, cmp_w2,
           cmp_b2, w_b_in, b_b_gate, w_b_out, w_ffn_up, ffn_conv_w, ffn_conv_b, w_ffn_down):
    p = _prep_weights(w_a_in, b_a_gate, w_a_out, w_kv, cmp_pos, cmp_w1, cmp_b1, cmp_w2, cmp_b2,
                      w_b_in, b_b_gate, w_b_out, w_ffn_up, w_ffn_down)
    dt = x_prompt.dtype
    bp, tp, _ = x_prompt.shape
    bs, ts, _ = x_sample.shape
    past_len = page_table.shape[1] * PAGE
    gw = KV_HEADS * HEAD_DIM
    shared = (p, g_norms, g_a_hnorm, g_kv, ffn_conv_w, ffn_conv_b)

    y_p, kv_p, win_p, c_p, n_p, m_p, conv_p = _trunk(
        x_prompt, 0, None, *shared,
        jnp.zeros((DEPTH, bp, CONV_W - 1, 2 * D_FF), dt),
        jnp.zeros((N_A_LAYERS, bp, M_HEADS, M_QK_DIM, M_V_DIM), dt),
        jnp.zeros((N_A_LAYERS, bp, M_HEADS, M_QK_DIM), dt), jnp.zeros((N_A_LAYERS, bp, M_HEADS), dt), "p")
    win_p = win_p[:, tp - min(WINDOW, tp):]

    past = _past_views(cache_kv, cache_win_kv, page_table)
    y_s, kv_s, win_s, c_s, n_s, m_s, conv_s = _trunk(
        x_sample, past_len, past, *shared, state_conv, state_mlstm_C, state_mlstm_n, state_mlstm_m, "s")
    win_all = jnp.concatenate([cache_win_kv, win_s], axis=1)
    win_s = win_all[:, win_all.shape[1] - min(WINDOW, win_all.shape[1]):]
    return (y_p, y_s, kv_p, kv_s, win_p, win_s, c_p, c_s, n_p, n_s, m_p, m_s, conv_p, conv_s)
```

```python
import functools
import math

import jax
import jax.numpy as jnp
from jax import lax
from jax.experimental import pallas as pl
from jax.experimental.pallas import tpu as pltpu

F32 = jnp.float32
BF16 = jnp.bfloat16
I32 = jnp.int32

D_MODEL = 1024
DEPTH = 4
N_A_LAYERS = 2
M_HEADS = 8
M_QK_DIM = 64
M_V_DIM = 128
GATE_CAP = 15.0
N_HEADS = 16
HEAD_DIM = 64
KV_HEADS = 4
HPG = 4
CMP_BLOCK = 32
CMP_STRIDE = 16
CMP_HIDDEN = 256
SEL_BLOCK = 64
N_SELECT = 16
WINDOW = 512
ROT_DIM = 16
ROPE_THETA = 500000.0
D_FF = 2816
CONV_W = 3
EPS = 1e-6
BIG = 1e9
NEG = -1e30
PAGE = 128

LANES = 128
SUBLANES = 8
VMEM_LIMIT = 56 * 1024 * 1024

MLSTM_CHUNK = 128
KEY_TILE = 512
Q_TILE = 128
CMP_PAGES = 32
ATT_PAGES = 16
FFN_TM = 512
FFN_TN = 1408


def _cparams(sem):
    return pltpu.CompilerParams(dimension_semantics=sem, vmem_limit_bytes=VMEM_LIMIT)


def _rms(x, g):
    return x * lax.rsqrt(jnp.mean(x * x, axis=-1, keepdims=True) + EPS) * g


def _nt(a, b):
    return lax.dot_general(a, b, (((1,), (1,)), ((), ())), preferred_element_type=F32)


def _dot(a, b):
    return jnp.dot(a, b, preferred_element_type=F32)


def _row_tile(m, pref):
    t = min(m, pref)
    while m % t:
        t //= 2
    return t


def _norm_proj_kernel(x_ref, g_ref, *refs, n_w, n_t):
    xn = _rms(x_ref[...], g_ref[...]).astype(BF16)
    n_in = n_w + n_t
    for w_ref, o_ref in zip(refs[:n_w], refs[n_in:n_in + n_w]):
        o_ref[...] = _dot(xn, w_ref[...]).astype(o_ref.dtype)
    for w_ref, o_ref in zip(refs[n_w:n_in], refs[n_in + n_w:]):
        yt = _nt(w_ref[...], xn)
        for u in range(o_ref.shape[0]):
            o_ref[u] = yt[:, u * LANES:(u + 1) * LANES]


def _norm_proj(x, g, ws, name, wts=()):
    m, d = x.shape
    tm = _row_tile(m, 512)
    assert not wts or tm % LANES == 0
    n_w, n_t = len(ws), len(wts)
    in_specs = [pl.BlockSpec((tm, d), lambda i: (i, 0)), pl.BlockSpec((1, d), lambda i: (0, 0))]
    in_specs += [pl.BlockSpec(w.shape, lambda i: (0, 0)) for w in (*ws, *wts)]
    out_specs = [pl.BlockSpec((tm, w.shape[1]), lambda i: (i, 0)) for w in ws]
    out_specs += [pl.BlockSpec((tm // LANES, w.shape[0], LANES), lambda i: (i, 0, 0)) for w in wts]
    out_shape = [jax.ShapeDtypeStruct((m, w.shape[1]), F32) for w in ws]
    out_shape += [jax.ShapeDtypeStruct((m // LANES, w.shape[0], LANES), F32) for w in wts]
    return pl.pallas_call(
        functools.partial(_norm_proj_kernel, n_w=n_w, n_t=n_t), grid=(m // tm,), in_specs=in_specs,
        out_specs=out_specs, out_shape=out_shape, compiler_params=_cparams(("parallel",)),
        name=name)(x, g.reshape(1, d), *ws, *wts)


def _proj_norm_res_kernel(a_ref, w_ref, g_ref, res_ref, o_ref):
    y = _dot(a_ref[...].astype(BF16), w_ref[...])
    o_ref[...] = res_ref[...] + _rms(y, g_ref[...])


def _proj_norm_res(a, w, g, res, name):
    m, k = a.shape
    d = w.shape[1]
    tm = _row_tile(m, 512)
    return pl.pallas_call(
        _proj_norm_res_kernel, grid=(m // tm,),
        in_specs=[pl.BlockSpec((tm, k), lambda i: (i, 0)), pl.BlockSpec((k, d), lambda i: (0, 0)),
                  pl.BlockSpec((1, d), lambda i: (0, 0)), pl.BlockSpec((tm, d), lambda i: (i, 0))],
        out_specs=pl.BlockSpec((tm, d), lambda i: (i, 0)),
        out_shape=jax.ShapeDtypeStruct((m, d), F32),
        compiler_params=_cparams(("parallel",)), name=name)(a, w, g.reshape(1, d), res)


def _mlstm_kernel(main_ref, gate_ref, *refs, rows, nc, k_minor):
    if k_minor:
        kt_ref, *refs = refs
    bg_ref, gh_ref, c0_ref, n0_ref, m0_ref, h_ref, c_ref, n_ref, m_ref, c_s, n_s, m_s = refs
    L = MLSTM_CHUNK
    nh = M_HEADS
    a_q = nh * M_QK_DIM
    v0 = a_q if k_minor else 2 * a_q
    o0 = v0 + nh * M_V_DIM
    cidx = pl.program_id(1)

    @pl.when(cidx == 0)
    def _():
        c_s[...] = c0_ref[...]
        n_s[...] = n0_ref[...]
        m_s[...] = m0_ref[...]

    main = main_ref[...]
    gp = gate_ref[...] + bg_ref[...]
    if rows < L:
        main = jnp.concatenate([main, jnp.zeros((L - rows, main.shape[1]), F32)], axis=0)
        gp = jnp.concatenate([gp, jnp.zeros((L - rows, LANES), F32)], axis=0)
    capped = GATE_CAP * jnp.tanh(gp / GATE_CAP)
    row1 = lax.broadcasted_iota(I32, (L, LANES), 0)
    real = row1 < rows
    ilog = jnp.where(real, capped, -jnp.inf)
    logf = jnp.where(real, jnp.minimum(capped, 0.0) - jnp.log1p(jnp.exp(-jnp.abs(capped))), 0.0)
    bh = logf
    k = 1
    while k < L:
        bh = bh + jnp.where(row1 >= k, pltpu.roll(bh, k, axis=0), 0.0)
        k *= 2
    bh = pltpu.roll(bh, LANES - nh, axis=1)
    c_all = ilog - bh
    cm = c_all
    k = 1
    while k < L:
        cm = jnp.maximum(cm, jnp.where(row1 >= k, pltpu.roll(cm, k, axis=0), -jnp.inf))
        k *= 2
    m_row = m_s[...]
    mt = bh + jnp.maximum(m_row, cm)
    m_new = mt[L - 1:L, :]
    b_last = bh[L - 1:L, :]
    w_inter = jnp.exp(bh + m_row - mt)
    u_all = bh - mt
    emt = jnp.exp(-mt)
    ws_all = jnp.exp(b_last - bh + ilog - m_new)
    decay = jnp.exp(b_last + m_row - m_new)
    c_t = c_all.T
    ws_t = ws_all.T

    rr = lax.broadcasted_iota(I32, (L, L), 0)
    cc = lax.broadcasted_iota(I32, (L, L), 1)
    causal = cc <= rr
    lo_half = lax.broadcasted_iota(I32, (L, LANES), 1) < M_QK_DIM
    lo_rows = lax.broadcasted_iota(I32, (2 * M_QK_DIM, LANES), 0) < M_QK_DIM
    ones_b = jnp.ones((L, LANES), BF16)
    assert L == LANES

    def lanes(a, x):
        return jnp.broadcast_to(a[:, x:x + 1], (L, LANES))

    for p in range(nh // 2):
        qp = main[:, p * LANES:(p + 1) * LANES] * (M_QK_DIM ** -0.5)
        if k_minor:
            kt = kt_ref[0, p * LANES:(p + 1) * LANES, :]
        else:
            kt = main[:, a_q + p * LANES:a_q + (p + 1) * LANES].T
        c_pair = c_s[2 * p:2 * p + 2].reshape(2 * M_QK_DIM, M_V_DIM)
        n_pair = n_s[p]
        kt_b = kt.astype(BF16)
        state_b = jnp.concatenate([c_pair, n_pair], axis=1).astype(BF16)
        for e in range(2):
            x = 2 * p + e
            qx_b = jnp.where(lo_half if e == 0 else jnp.logical_not(lo_half), qp, 0.0).astype(BF16)
            vx = main[:, v0 + x * M_V_DIM:v0 + (x + 1) * M_V_DIM]
            ox = main[:, o0 + x * M_V_DIM:o0 + (x + 1) * M_V_DIM]
            a = jnp.exp(jnp.where(causal, lanes(u_all, x) + c_t[x:x + 1, :], -jnp.inf)) * _dot(qx_b, kt_b)
            inter = _dot(qx_b, state_b)
            intra = _dot(a.astype(BF16), jnp.concatenate([vx.astype(BF16), ones_b], axis=1))
            wi = lanes(w_inter, x)
            num = wi * inter[:, :M_V_DIM] + intra[:, :M_V_DIM]
            den = wi * inter[:, M_V_DIM:] + intra[:, M_V_DIM:]
            h = num / jnp.maximum(jnp.abs(den), lanes(emt, x))
            hn = h * lax.rsqrt(jnp.mean(h * h, axis=-1, keepdims=True) + EPS)
            hn = hn * gh_ref[:, x * M_V_DIM:(x + 1) * M_V_DIM] * jax.nn.sigmoid(ox)
            h_ref[:, x * M_V_DIM:(x + 1) * M_V_DIM] = hn[:rows]
        kwt = kt * jnp.where(lo_rows, ws_t[2 * p:2 * p + 1, :], ws_t[2 * p + 1:2 * p + 2, :])
        vcat = main[:, v0 + 2 * p * M_V_DIM:v0 + (2 * p + 2) * M_V_DIM]
        upd = _dot(kwt.astype(BF16), jnp.concatenate([vcat.astype(BF16), ones_b], axis=1))
        dec_e = decay[:, 2 * p:2 * p + 1]
        dec_o = decay[:, 2 * p + 1:2 * p + 2]
        c_s[2 * p] = dec_e * c_pair[:M_QK_DIM] + upd[:M_QK_DIM, :M_V_DIM]
        c_s[2 * p + 1] = dec_o * c_pair[M_QK_DIM:] + upd[M_QK_DIM:, M_V_DIM:2 * M_V_DIM]
        n_s[p] = jnp.where(lo_rows, dec_e, dec_o) * n_pair + upd[:, 2 * M_V_DIM:]
    m_s[...] = m_new

    @pl.when(cidx == nc - 1)
    def _():
        c_ref[...] = c_s[...]
        n_ref[...] = n_s[...]
        m_ref[...] = m_s[...]


def _mlstm(main, gate, kt, b_gate, g_hnorm, c0, n0, m0, batch, seq, name):
    rows = min(seq, MLSTM_CHUNK)
    nc = seq // rows
    a_v = M_HEADS * M_V_DIM
    hp = M_HEADS // 2
    n_in = jnp.broadcast_to(n0.reshape(batch, hp, 2 * M_QK_DIM, 1), (batch, hp, 2 * M_QK_DIM, LANES))
    m_in = jnp.pad(m0, ((0, 0), (0, LANES - M_HEADS)))[:, None, :]
    k_minor = kt is not None
    assert not k_minor or rows == MLSTM_CHUNK == LANES
    kt_specs = [pl.BlockSpec((1, kt.shape[1], LANES), lambda b, c: (b * nc + c, 0, 0))] if k_minor else []
    h, c, n, m = pl.pallas_call(
        functools.partial(_mlstm_kernel, rows=rows, nc=nc, k_minor=k_minor), grid=(batch, nc),
        in_specs=[pl.BlockSpec((rows, main.shape[1]), lambda b, c: (b * nc + c, 0)),
                  pl.BlockSpec((rows, LANES), lambda b, c: (b * nc + c, 0)), *kt_specs,
                  pl.BlockSpec((1, LANES), lambda b, c: (0, 0)),
                  pl.BlockSpec((1, a_v), lambda b, c: (0, 0)),
                  pl.BlockSpec((None, M_HEADS, M_QK_DIM, M_V_DIM), lambda b, c: (b, 0, 0, 0)),
                  pl.BlockSpec((None, hp, 2 * M_QK_DIM, LANES), lambda b, c: (b, 0, 0, 0)),
                  pl.BlockSpec((None, 1, LANES), lambda b, c: (b, 0, 0))],
        out_specs=[pl.BlockSpec((rows, a_v), lambda b, c: (b * nc + c, 0)),
                   pl.BlockSpec((None, M_HEADS, M_QK_DIM, M_V_DIM), lambda b, c: (b, 0, 0, 0)),
                   pl.BlockSpec((None, hp, 2 * M_QK_DIM, LANES), lambda b, c: (b, 0, 0, 0)),
                   pl.BlockSpec((None, 1, LANES), lambda b, c: (b, 0, 0))],
        out_shape=[jax.ShapeDtypeStruct((batch * seq, a_v), F32),
                   jax.ShapeDtypeStruct((batch, M_HEADS, M_QK_DIM, M_V_DIM), F32),
                   jax.ShapeDtypeStruct((batch, hp, 2 * M_QK_DIM, LANES), F32),
                   jax.ShapeDtypeStruct((batch, 1, LANES), F32)],
        scratch_shapes=[pltpu.VMEM((M_HEADS, M_QK_DIM, M_V_DIM), F32), pltpu.VMEM((hp, 2 * M_QK_DIM, LANES), F32),
                        pltpu.VMEM((1, LANES), F32)],
        compiler_params=_cparams(("parallel", "arbitrary")), name=name,
    )(main, gate, *([kt] if k_minor else []), b_gate, g_hnorm.reshape(1, a_v), c0, n_in, m_in)
    return h, c, n[..., 0].reshape(batch, M_HEADS, M_QK_DIM), m[:, 0, :M_HEADS]


def _ffn_kernel(*refs, carry, tm, n_j, tiles_per_seq, period, mixer):
    if mixer:
        a_ref, wm_ref, g1_ref, *refs = refs
    x_ref, g2_ref, wua_ref, wug_ref, cwa_ref, cwg_ref, cba_ref, cbg_ref, wd_ref, g3_ref = refs[:10]
    if carry and mixer:
        inita_ref, initg_ref, o_ref, sa_ref, sg_ref, xn_s, acc_s, ue_s, carry_s, x_s = refs[10:]
        branch_in = ((wua_ref, cwa_ref, cba_ref, inita_ref, sa_ref), (wug_ref, cwg_ref, cbg_ref, initg_ref, sg_ref))
    elif carry:
        inita_ref, initg_ref, o_ref, sa_ref, sg_ref, xn_s, acc_s, ue_s, carry_s = refs[10:]
        branch_in = ((wua_ref, cwa_ref, cba_ref, inita_ref, sa_ref), (wug_ref, cwg_ref, cbg_ref, initg_ref, sg_ref))
    else:
        t1a_ref, t1g_ref, t2a_ref, t2g_ref, o_ref, sa_ref, sg_ref, xn_s, acc_s, ue_s = refs[10:]
        branch_in = ((wua_ref, cwa_ref, cba_ref, (t1a_ref, t2a_ref), sa_ref),
                     (wug_ref, cwg_ref, cbg_ref, (t1g_ref, t2g_ref), sg_ref))
    i = pl.program_id(0)
    j = pl.program_id(1)
    tn = wd_ref.shape[0]

    @pl.when(j == 0)
    def _():
        x = x_ref[...]
        if mixer:
            x = x + _rms(_dot(a_ref[...].astype(BF16), wm_ref[...]), g1_ref[...])
            x_s[...] = x
        xn_s[...] = _rms(x, g2_ref[...]).astype(BF16)
        acc_s[...] = jnp.zeros_like(acc_s)

    xn = xn_s[...]
    conv = []
    for which, (w_ref, cw_ref, cb_ref, boundary, s_ref) in enumerate(branch_in):
        u = _dot(xn, w_ref[...])
        ue_s[SUBLANES:, :] = u
        if carry:
            slot = which * n_j + j
            first = (i % tiles_per_seq) == 0
            ue_s[SUBLANES - 2:SUBLANES, :] = jnp.where(first, boundary[...], carry_s[slot])
            tap1 = ue_s[pl.ds(SUBLANES - 1, tm), :]
            tap2 = ue_s[pl.ds(SUBLANES - 2, tm), :]
            last2 = u[tm - 2:tm, :]
            carry_s[slot] = last2
            s_ref[i // tiles_per_seq, j] = last2
        else:
            ue_s[0:SUBLANES, :] = jnp.zeros((SUBLANES, tn), F32)
            t = lax.broadcasted_iota(I32, (tm, tn), 0) & (period - 1)
            tap1 = jnp.where(t >= 1, ue_s[pl.ds(SUBLANES - 1, tm), :], boundary[0][...])
            tap2 = jnp.where(t >= 2, ue_s[pl.ds(SUBLANES - 2, tm), :], boundary[1][...])
            s_ref[...] = u
        conv.append(cb_ref[...] + tap2 * cw_ref[0:1, :] + tap1 * cw_ref[1:2, :] + u * cw_ref[2:3, :])
    y = jax.nn.gelu(conv[0], approximate=True) * conv[1]
    acc_s[...] += _dot(y.astype(BF16), wd_ref[...])

    @pl.when(j == n_j - 1)
    def _():
        o_ref[...] = (x_s[...] if mixer else x_ref[...]) + _rms(acc_s[...], g3_ref[...])


def _ffn(x, g2, w_up, conv_w, conv_b, w_down, g3, prev, batch, seq, name, mixer=None):
    m, d = x.shape
    nf = w_down.shape[0]
    tn = FFN_TN
    n_j = nf // tn
    carry = seq >= 256
    cb = conv_b.reshape(1, 2 * nf)
    col_a = lambda i, j: (0, j)
    col_g = lambda i, j: (0, j + n_j)
    common_specs = [
        None,
        pl.BlockSpec((1, d), lambda i, j: (0, 0)),
        pl.BlockSpec((d, tn), col_a), pl.BlockSpec((d, tn), col_g),
        pl.BlockSpec((CONV_W, tn), col_a), pl.BlockSpec((CONV_W, tn), col_g),
        pl.BlockSpec((1, tn), col_a), pl.BlockSpec((1, tn), col_g),
        pl.BlockSpec((tn, d), lambda i, j: (j, 0)),
        pl.BlockSpec((1, d), lambda i, j: (0, 0)),
    ]
    common_args = [x, g2.reshape(1, d), w_up, w_up, conv_w, conv_w, cb, cb, w_down, g3.reshape(1, d)]
    assert mixer is None or carry
    if carry:
        tm = _row_tile(seq, FFN_TM)
        tps = seq // tm
        common_specs[0] = pl.BlockSpec((tm, d), lambda i, j: (i, 0))
        mix_specs, mix_args, mix_scratch = [], [], []
        if mixer is not None:
            a, wm, g1 = mixer
            mix_specs = [pl.BlockSpec((tm, a.shape[1]), lambda i, j: (i, 0)), pl.BlockSpec(wm.shape, lambda i, j: (0, 0)),
                         pl.BlockSpec((1, d), lambda i, j: (0, 0))]
            mix_args = [a, wm, g1.reshape(1, d)]
            mix_scratch = [pltpu.VMEM((tm, d), F32)]
        st_a = lambda i, j: (i // tps, 0, j)
        st_g = lambda i, j: (i // tps, 0, j + n_j)
        out, sa, sg = pl.pallas_call(
            functools.partial(_ffn_kernel, carry=True, tm=tm, n_j=n_j, tiles_per_seq=tps, period=seq,
                              mixer=mixer is not None),
            grid=(m // tm, n_j),
            in_specs=mix_specs + common_specs + [pl.BlockSpec((None, 2, tn), st_a), pl.BlockSpec((None, 2, tn), st_g)],
            out_specs=[pl.BlockSpec((tm, d), lambda i, j: (i, 0)),
                       pl.BlockSpec((batch, n_j, 2, tn), lambda i, j: (0, 0, 0, 0)),
                       pl.BlockSpec((batch, n_j, 2, tn), lambda i, j: (0, 0, 0, 0))],
            out_shape=[jax.ShapeDtypeStruct((m, d), F32), jax.ShapeDtypeStruct((batch, n_j, 2, tn), F32),
                       jax.ShapeDtypeStruct((batch, n_j, 2, tn), F32)],
            scratch_shapes=[pltpu.VMEM((tm, d), BF16), pltpu.VMEM((tm, d), F32),
                            pltpu.VMEM((tm + SUBLANES, tn), F32), pltpu.VMEM((2 * n_j, 2, tn), F32)] + mix_scratch,
            compiler_params=_cparams(("arbitrary", "arbitrary")), name=name,
        )(*mix_args, *common_args, prev, prev)
        sa, sg = (s.transpose(0, 2, 1, 3).reshape(batch, 2, nf) for s in (sa, sg))
        return out, jnp.concatenate([sa, sg], axis=-1)
    tm = m
    assert seq >= 2 and seq & (seq - 1) == 0
    common_specs[0] = pl.BlockSpec((tm, d), lambda i, j: (i, 0))
    tap1 = jnp.pad(prev[:, 1:2], ((0, 0), (0, seq - 1), (0, 0))).reshape(m, 2 * nf)
    tap2 = jnp.pad(prev, ((0, 0), (0, seq - 2), (0, 0))).reshape(m, 2 * nf)
    row_a = lambda i, j: (i, j)
    row_g = lambda i, j: (i, j + n_j)
    out, ua, ug = pl.pallas_call(
        functools.partial(_ffn_kernel, carry=False, tm=tm, n_j=n_j, tiles_per_seq=1, period=seq, mixer=False),
        grid=(m // tm, n_j),
        in_specs=common_specs + [pl.BlockSpec((tm, tn), row_a), pl.BlockSpec((tm, tn), row_g),
                                 pl.BlockSpec((tm, tn), row_a), pl.BlockSpec((tm, tn), row_g)],
        out_specs=[pl.BlockSpec((tm, d), lambda i, j: (i, 0)), pl.BlockSpec((tm, tn), row_a),
                   pl.BlockSpec((tm, tn), row_a)],
        out_shape=[jax.ShapeDtypeStruct((m, d), F32), jax.ShapeDtypeStruct((m, nf), F32),
                   jax.ShapeDtypeStruct((m, nf), F32)],
        scratch_shapes=[pltpu.VMEM((tm, d), BF16), pltpu.VMEM((tm, d), F32), pltpu.VMEM((tm + SUBLANES, tn), F32)],
        compiler_params=_cparams(("arbitrary", "arbitrary")), name=name,
    )(*common_args, tap1, tap1, tap2, tap2)
    u = jnp.concatenate([ua, ug], axis=-1).reshape(batch, seq, 2 * nf)
    return out, u[:, seq - 2:]


def _rope_pair(x, cos, sin):
    half = ROT_DIM // 2
    lane = lax.broadcasted_iota(I32, x.shape, 1) & (HEAD_DIM - 1)
    partner = jnp.where(lane < half, pltpu.roll(x, LANES - half, axis=1), pltpu.roll(x, half, axis=1))
    return x * cos + partner * sin


def _rope(x, cos, sin):
    return jnp.concatenate(
        [_rope_pair(x[:, c * LANES:(c + 1) * LANES], cos, sin) for c in range(x.shape[1] // LANES)], axis=1)


def _rope_tables(pos0, seq):
    half = ROT_DIM // 2
    inv = jnp.power(jnp.float32(ROPE_THETA), -jnp.arange(0, ROT_DIM, 2, dtype=F32) / ROT_DIM)
    ang = (pos0 + jnp.arange(seq, dtype=I32)).astype(F32)[:, None] * inv[None, :]
    cos, sin = jnp.cos(ang), jnp.sin(ang)
    rest = HEAD_DIM - ROT_DIM
    cos_h = jnp.concatenate([cos, cos, jnp.ones((seq, rest), F32)], axis=1)
    sin_h = jnp.concatenate([-sin, sin, jnp.zeros((seq, rest), F32)], axis=1)
    return jnp.tile(cos_h, (1, LANES // HEAD_DIM)), jnp.tile(sin_h, (1, LANES // HEAD_DIM))


def _kv_kernel(x_ref, g_ref, w_ref, cos_ref, sin_ref, kv_ref, win_ref):
    xn = _rms(x_ref[...], g_ref[...]).astype(BF16)
    y = _dot(xn, w_ref[...])
    cos, sin = cos_ref[...], sin_ref[...]
    gw = KV_HEADS * HEAD_DIM
    ks = _rope(y[:, 2 * gw:3 * gw], cos, sin)
    kw = _rope(y[:, 4 * gw:5 * gw], cos, sin)
    kv_ref[:, 0:2 * gw] = y[:, 0:2 * gw]
    kv_ref[:, 2 * gw:3 * gw] = ks
    kv_ref[:, 3 * gw:4 * gw] = y[:, 3 * gw:4 * gw]
    win_ref[:, 0:gw] = kw
    win_ref[:, gw:2 * gw] = y[:, 5 * gw:6 * gw]


def _rope_rows(x, cos_t, sin_t):
    half = ROT_DIM // 2
    row = lax.broadcasted_iota(I32, x.shape, 0) & (HEAD_DIM - 1)
    partner = jnp.where(row < half, pltpu.roll(x, x.shape[0] - half, axis=0), pltpu.roll(x, half, axis=0))
    return x * cos_t + partner * sin_t


def _kv_minor_kernel(x_ref, g_ref, wt_ref, wk_ref, cos_ref, sin_ref, cost_ref, sint_ref,
                     kvt_ref, wint_ref, ks_ref, kw_ref, vst_ref, vwt_ref):
    xn = _rms(x_ref[...], g_ref[...]).astype(BF16)
    gw = KV_HEADS * HEAD_DIM
    yt = _nt(wt_ref[...], xn)
    cos_t, sin_t = cost_ref[...], sint_ref[...]

    def rope_t(a):
        return jnp.concatenate(
            [_rope_rows(a[c * LANES:(c + 1) * LANES], cos_t, sin_t) for c in range(gw // LANES)], axis=0)

    kvt_ref[0:2 * gw, :] = yt[0:2 * gw]
    kvt_ref[2 * gw:3 * gw, :] = rope_t(yt[2 * gw:3 * gw])
    kvt_ref[3 * gw:4 * gw, :] = yt[3 * gw:4 * gw]
    wint_ref[0:gw, :] = rope_t(yt[4 * gw:5 * gw])
    wint_ref[gw:2 * gw, :] = yt[5 * gw:6 * gw]
    vst_ref[...] = yt[3 * gw:4 * gw].astype(BF16)
    for u in range(vwt_ref.shape[0]):
        vwt_ref[u] = yt[5 * gw:6 * gw, u * PAGE:(u + 1) * PAGE].astype(BF16)
    yk = _dot(xn, wk_ref[...])
    cos, sin = cos_ref[...], sin_ref[...]
    ks_ref[...] = _rope(yk[:, 0:gw], cos, sin).astype(BF16)
    kw_ref[...] = _rope(yk[:, gw:2 * gw], cos, sin).astype(BF16)


def _kv_rows_minor(x, g_kv, w_t, w_k, cos, sin, batch, seq, name):
    m, d = x.shape
    tm = _row_tile(seq, KEY_TILE)
    gw = KV_HEADS * HEAD_DIM
    tt = seq // tm
    tok = lambda i: (i % tt, 0)
    tok_t = lambda i: (0, i % tt)
    return pl.pallas_call(
        _kv_minor_kernel, grid=(m // tm,),
        in_specs=[pl.BlockSpec((tm, d), lambda i: (i, 0)), pl.BlockSpec((1, d), lambda i: (0, 0)),
                  pl.BlockSpec(w_t.shape, lambda i: (0, 0)), pl.BlockSpec(w_k.shape, lambda i: (0, 0)),
                  pl.BlockSpec((tm, LANES), tok), pl.BlockSpec((tm, LANES), tok),
                  pl.BlockSpec((LANES, tm), tok_t), pl.BlockSpec((LANES, tm), tok_t)],
        out_specs=[pl.BlockSpec((None, 4 * gw, tm), lambda i: (i // tt, 0, i % tt)),
                   pl.BlockSpec((None, 2 * gw, tm), lambda i: (i // tt, 0, i % tt)),
                   pl.BlockSpec((tm, gw), lambda i: (i, 0)), pl.BlockSpec((tm, gw), lambda i: (i, 0)),
                   pl.BlockSpec((None, gw, tm), lambda i: (i, 0, 0)),
                   pl.BlockSpec((tm // PAGE, gw, PAGE), lambda i: (i, 0, 0))],
        out_shape=[jax.ShapeDtypeStruct((batch, 4 * gw, seq), F32), jax.ShapeDtypeStruct((batch, 2 * gw, seq), F32),
                   jax.ShapeDtypeStruct((m, gw), BF16), jax.ShapeDtypeStruct((m, gw), BF16),
                   jax.ShapeDtypeStruct((m // tm, gw, tm), BF16), jax.ShapeDtypeStruct((m // PAGE, gw, PAGE), BF16)],
        compiler_params=_cparams(("parallel",)), name=name)(x, g_kv.reshape(1, d), w_t, w_k, cos, sin, cos.T, sin.T)


def _kv_rows(x, g_kv, w_kv, cos, sin, name):
    m, d = x.shape
    gw = KV_HEADS * HEAD_DIM
    return pl.pallas_call(
        _kv_kernel, grid=(1,),
        in_specs=[pl.BlockSpec((m, d), lambda i: (0, 0)), pl.BlockSpec((1, d), lambda i: (0, 0)),
                  pl.BlockSpec(w_kv.shape, lambda i: (0, 0)),
                  pl.BlockSpec((m, LANES), lambda i: (0, 0)), pl.BlockSpec((m, LANES), lambda i: (0, 0))],
        out_specs=[pl.BlockSpec((m, 4 * gw), lambda i: (0, 0)), pl.BlockSpec((m, 2 * gw), lambda i: (0, 0))],
        out_shape=[jax.ShapeDtypeStruct((m, 4 * gw), F32), jax.ShapeDtypeStruct((m, 2 * gw), F32)],
        compiler_params=_cparams(("arbitrary",)), name=name)(x, g_kv.reshape(1, d), w_kv, cos, sin)


def _token_minor(rows):
    n_p, r, n = rows.shape[0], rows.shape[1], rows.shape[2]
    return rows.transpose(0, 2, 3, 4, 1).reshape(n_p, n, KV_HEADS * HEAD_DIM, r)


def _compress_kernel(tbl_ref, *refs, n_pg, zero_after):
    page_refs = refs[:n_pg]
    (pos_lo_ref, pos_hi_ref, w1lo_ref, w1hi_ref, b1_ref, w2_ref, b2_ref, hi_init_ref,
     kc_ref, vc_ref, hi_first_ref, carry_s, t_s, x_s) = refs[n_pg:]
    c = pl.program_id(1)
    njp = PAGE // CMP_STRIDE
    njc = n_pg * njp

    @pl.when(c == 0)
    def _():
        carry_s[...] = hi_init_ref[...]

    lo_lanes = lax.broadcasted_iota(I32, (njp, LANES), 1) < HEAD_DIM
    for u in range(n_pg):
        for br in range(2):
            for cp in range(KV_HEADS // 2):
                tb = t_s.at[(4 * u + 2 * br + cp) % t_s.shape[0]]
                tb[...] = page_refs[u][br, cp * LANES:(cp + 1) * LANES, :].T
                for q in range(CMP_STRIDE // 2):
                    p0 = tb[pl.ds(2 * q, njp, stride=CMP_STRIDE), :]
                    p1 = tb[pl.ds(2 * q + 1, njp, stride=CMP_STRIDE), :]
                    rows = slice(u * njp, (u + 1) * njp)
                    cols = slice(q * LANES, (q + 1) * LANES)
                    x_s[br * KV_HEADS + 2 * cp, rows, cols] = jnp.where(lo_lanes, p0, pltpu.roll(p1, HEAD_DIM, axis=1))
                    x_s[br * KV_HEADS + 2 * cp + 1, rows, cols] = jnp.where(lo_lanes, pltpu.roll(p0, HEAD_DIM, axis=1), p1)

    rowl = lax.broadcasted_iota(I32, (KV_HEADS * njc, CMP_HIDDEN), 0) & (njc - 1)
    for br, out_ref in enumerate((kc_ref, vc_ref)):
        x = x_s[br * KV_HEADS:(br + 1) * KV_HEADS].reshape(KV_HEADS * njc, CMP_STRIDE * HEAD_DIM)
        lo = _dot((x + pos_lo_ref[br]).astype(BF16), w1lo_ref[br])
        hi = _dot((x + pos_hi_ref[br]).astype(BF16), w1hi_ref[br])
        carry_rows = jnp.concatenate(
            [jnp.broadcast_to(carry_s[br * KV_HEADS + g:br * KV_HEADS + g + 1, :], (njc, CMP_HIDDEN))
             for g in range(KV_HEADS)], axis=0)
        hi_next = jnp.where(rowl == njc - 1, carry_rows, pltpu.roll(hi, KV_HEADS * njc - 1, axis=0))
        if zero_after is not None:
            hi_next = jnp.where(jnp.logical_and(rowl == zero_after, c == 0), 0.0, hi_next)
        for g in range(KV_HEADS):
            carry_s[br * KV_HEADS + g:br * KV_HEADS + g + 1, :] = hi[g * njc:g * njc + 1, :]
        h = jax.nn.gelu(lo + hi_next + b1_ref[br], approximate=True)
        o = _dot(h.astype(BF16), w2_ref[br]) + b2_ref[br]
        for g in range(KV_HEADS):
            out_ref[:, g * HEAD_DIM:(g + 1) * HEAD_DIM] = o[g * njc:(g + 1) * njc, :]
    hi_first_ref[...] = carry_s[...]


def _compress(pages, table, batch, n_pages, n_pg, hi_init, cw, zero_after, name):
    njp = PAGE // CMP_STRIDE
    njc = n_pg * njp
    assert njc & (njc - 1) == 0 and njc % SUBLANES == 0 and n_pages % n_pg == 0
    n_ch = n_pages // n_pg
    gw = KV_HEADS * HEAD_DIM
    flat = CMP_STRIDE * HEAD_DIM

    def page_map(u):
        if table is None:
            return lambda b, c, tbl: (b, 0, 0, (n_ch - 1 - c) * n_pg + u)
        return lambda b, c, tbl: (tbl[b * n_pages + (n_ch - 1 - c) * n_pg + u], 0, 0, 0)

    full3 = lambda b, c, tbl: (0, 0, 0)
    in_specs = [pl.BlockSpec((None, 2, gw, PAGE), page_map(u)) for u in range(n_pg)]
    in_specs += [pl.BlockSpec((2, 1, flat), full3), pl.BlockSpec((2, 1, flat), full3),
                 pl.BlockSpec((2, flat, CMP_HIDDEN), full3), pl.BlockSpec((2, flat, CMP_HIDDEN), full3),
                 pl.BlockSpec((2, 1, CMP_HIDDEN), full3), pl.BlockSpec((2, CMP_HIDDEN, HEAD_DIM), full3),
                 pl.BlockSpec((2, 1, HEAD_DIM), full3),
                 pl.BlockSpec((None, 2 * KV_HEADS, CMP_HIDDEN), lambda b, c, tbl: (b, 0, 0))]
    out_map = lambda b, c, tbl: (b, n_ch - 1 - c, 0)
    grid_spec = pltpu.PrefetchScalarGridSpec(
        num_scalar_prefetch=1, grid=(batch, n_ch), in_specs=in_specs,
        out_specs=[pl.BlockSpec((None, njc, gw), out_map), pl.BlockSpec((None, njc, gw), out_map),
                   pl.BlockSpec((None, 2 * KV_HEADS, CMP_HIDDEN), lambda b, c, tbl: (b, 0, 0))],
        scratch_shapes=[pltpu.VMEM((2 * KV_HEADS, CMP_HIDDEN), F32), pltpu.VMEM((16, PAGE, LANES), F32),
                        pltpu.VMEM((2 * KV_HEADS, njc, flat), F32)])
    return pl.pallas_call(
        functools.partial(_compress_kernel, n_pg=n_pg, zero_after=zero_after), grid_spec=grid_spec,
        out_shape=[jax.ShapeDtypeStruct((batch, n_ch * njc, gw), F32), jax.ShapeDtypeStruct((batch, n_ch * njc, gw), F32),
                   jax.ShapeDtypeStruct((batch, 2 * KV_HEADS, CMP_HIDDEN), F32)],
        compiler_params=_cparams(("parallel", "arbitrary")), name=name,
    )(jnp.zeros((1,), I32) if table is None else table, *([pages] * n_pg), cw["pos_lo"], cw["pos_hi"], cw["w1lo"], cw["w1hi"], cw["b1"], cw["w2"], cw["b2"], hi_init)


def _cmp_layout(kc_nat, nb, nbp):
    b = kc_nat.shape[0]
    x = kc_nat[:, :4 * nb].reshape(b, nb, 4, kc_nat.shape[-1]).transpose(0, 2, 1, 3)
    x = jnp.pad(x, ((0, 0), (0, 0), (0, nbp - nb), (0, 0)))
    return x.reshape(b, 4 * nbp, kc_nat.shape[-1]).astype(BF16)


SCALE = HEAD_DIM ** -0.5
QSCALE = SCALE * math.log2(math.e)
SUM_ROWS = 16
ACC_ROWS = HEAD_DIM + SUM_ROWS


def _group_queries(q, g):
    nq = q.shape[0]
    keep = (lax.broadcasted_iota(I32, (nq, LANES), 1) >> 6) == (g % 2)
    pieces = []
    for hh in range(HPG):
        h = HPG * g + hh
        chunk = q[:, (h // 2) * LANES:(h // 2 + 1) * LANES]
        if h % 2 != g % 2:
            chunk = pltpu.roll(chunk, HEAD_DIM, axis=1)
        pieces.append(jnp.where(keep, chunk, 0.0))
    return (jnp.concatenate(pieces, axis=0) * QSCALE).astype(BF16)


def _kchunk(k, g):
    return k[:, (g // 2) * LANES:(g // 2 + 1) * LANES]


def _values(vt, g):
    return jnp.concatenate([vt[g * HEAD_DIM:(g + 1) * HEAD_DIM, :], jnp.ones((SUM_ROWS, vt.shape[1]), BF16)], axis=0)


def _flash_groups(ss, vt, m_s, acc_s):
    ps, alphas = [], []
    for g in range(KV_HEADS):
        m_old = m_s[g]
        m_new = jnp.maximum(m_old, jnp.max(ss[g], axis=0, keepdims=True))
        alphas.append(jnp.exp2(m_old - m_new))
        ps.append(jnp.exp2(ss[g] - m_new).astype(BF16))
        m_s[g] = m_new
    for g in range(KV_HEADS):
        acc_s[g] = acc_s[g] * alphas[g] + _dot(_values(vt, g), ps[g])


def _one_shot_groups(ss, vt):
    ps = [jnp.exp2(s - jnp.max(s, axis=0, keepdims=True)).astype(BF16) for s in ss]
    return [_dot(_values(vt, g), ps[g]) for g in range(KV_HEADS)]


def _init_state(m_s, acc_s):
    m_s[...] = jnp.full(m_s.shape, NEG, F32)
    acc_s[...] = jnp.zeros(acc_s.shape, F32)


def _topk_bias(score, n_sel):
    n_iota = lax.broadcasted_iota(I32, score.shape, 0)

    def body(_, sc):
        mx = jnp.max(sc, axis=0, keepdims=True)
        idx = jnp.min(jnp.where(sc == mx, n_iota, score.shape[0]), axis=0, keepdims=True)
        return jnp.where(n_iota == idx, -jnp.inf, sc)

    left = lax.fori_loop(0, n_sel, body, score)
    return jnp.where(jnp.logical_and(left == -jnp.inf, score > -jnp.inf), 0.0, NEG)


def _cmp_valid(qpos_w, nb, nbp):
    n_w = lax.broadcasted_iota(I32, (nbp, qpos_w.shape[1]), 0)
    return jnp.concatenate(
        [jnp.logical_and((4 * n_w + c) * CMP_STRIDE + (CMP_BLOCK - 1) <= qpos_w, n_w < nb) for c in range(4)], axis=0)


def _cmp_branch(s, valid, vct_ref, g, qpos_w, qpos_q, nb, nbp):
    s = jnp.where(valid, s, NEG)
    e = jnp.exp2(s - jnp.max(s, axis=0, keepdims=True))
    some = (qpos_w >= CMP_BLOCK - 1).astype(F32)
    p = e * (some / jnp.maximum(jnp.sum(e, axis=0, keepdims=True), 1e-30))
    o_cmp = _dot(vct_ref[g * HEAD_DIM:(g + 1) * HEAD_DIM, :], p.astype(BF16))
    ps = ((p[:, 0:LANES] + p[:, LANES:2 * LANES]) + p[:, 2 * LANES:3 * LANES]) + p[:, 3 * LANES:4 * LANES]
    parts = [ps[c * nbp:(c + 1) * nbp] for c in range(4)]
    n_q = lax.broadcasted_iota(I32, (nbp, LANES), 0)
    prev = jnp.where(n_q >= 1, pltpu.roll(parts[3], 1, axis=0), 0.0)
    score = (((parts[0] + parts[1]) + parts[2]) + parts[3]) + prev
    cur = qpos_q >> 6
    forced = (n_q == 0) | (n_q == cur) | (n_q == cur - 1)
    score = jnp.where(forced, BIG, jnp.where(n_q * SEL_BLOCK <= qpos_q, score, -BIG))
    score = jnp.where(n_q < nb, score, -jnp.inf)
    return o_cmp, score


def _compressed_and_select(q, kc_ref, vct_ref, qpos_w, qpos_q, ocmp_s, score_s, bias_s, nb, nbp, n_eff):
    kc_all = jnp.concatenate([kc_ref[c * nbp:c * nbp + n_eff, :] for c in range(4)], axis=0)
    vct_all = jnp.concatenate([vct_ref[:, c * nbp:c * nbp + n_eff] for c in range(4)], axis=1)
    cs = [_nt(_kchunk(kc_all, g), _group_queries(q, g)) for g in range(KV_HEADS)]
    valid = _cmp_valid(qpos_w, nb, n_eff)
    for g in range(KV_HEADS):
        o_cmp, score = _cmp_branch(cs[g], valid, vct_all, g, qpos_w, qpos_q, nb, n_eff)
        ocmp_s[g] = o_cmp
        score_s[0:n_eff, g * LANES:(g + 1) * LANES] = score
    bias = _topk_bias(score_s[0:n_eff, :], min(N_SELECT, nb))
    for g in range(KV_HEADS):
        b = bias[:, g * LANES:(g + 1) * LANES]
        bias_s[g, 0:n_eff, :] = jnp.concatenate([b] * HPG, axis=1)
        if n_eff < nbp:
            bias_s[g, n_eff:nbp, :] = jnp.full((nbp - n_eff, HPG * LANES), NEG, F32)


def _block_bias(bias_rows, n_blk):
    w = bias_rows.shape[1]
    return jnp.concatenate([jnp.broadcast_to(bias_rows[u:u + 1, :], (SEL_BLOCK, w)) for u in range(n_blk)], axis=0)


def _gate_row(gates_t, c, g):
    return jnp.concatenate(
        [gates_t[(c * HPG + hh) * KV_HEADS + g:(c * HPG + hh) * KV_HEADS + g + 1, :] for hh in range(HPG)], axis=1)


def _finish(o_ref, acc_s, ocmp_s, wins, ot_s, gates_t):
    nq = o_ref.shape[0]
    for g in range(KV_HEADS):
        o = ocmp_s[g] * _gate_row(gates_t, 0, g)
        for c, acc in ((1, acc_s[g]), (2, wins[g])):
            o = o + acc[:HEAD_DIM] * (_gate_row(gates_t, c, g) / acc[HEAD_DIM:HEAD_DIM + 1])
        ot_s[g * HEAD_DIM:(g + 1) * HEAD_DIM, :] = o
    o_t = ot_s[...].T
    for hh in range(HPG):
        o_ref[:, hh * KV_HEADS * HEAD_DIM:(hh + 1) * KV_HEADS * HEAD_DIM] = o_t[hh * nq:(hh + 1) * nq, :]


def _attn_prompt_kernel(q_ref, gp_ref, bg_ref, cos_ref, sin_ref, kc_ref, vct_ref, ks_ref, vst_ref, kw_ref, vwt_ref,
                        o_ref, m_s, acc_s, score_s, bias_s, ocmp_s, ot_s, qzr_s, *, nb, nbp):
    i = pl.program_id(1)
    nq = Q_TILE
    w = HPG * nq
    s0 = i * nq
    q = q_ref[...]
    q_rot = _rope(q, cos_ref[...], sin_ref[...])
    gates_t = jax.nn.sigmoid(gp_ref[...] + bg_ref[...]).T
    qpos_w = s0 + (lax.broadcasted_iota(I32, (1, w), 1) & (nq - 1))
    qpos_q = s0 + lax.broadcasted_iota(I32, (1, LANES), 1)
    _init_state(m_s, acc_s)
    for g in range(KV_HEADS):
        qzr_s[g] = _group_queries(q_rot, g)
    n_cls = 4 if nbp % (4 * SUBLANES) == 0 and nbp // 4 >= N_SELECT else 1
    per_cls = nbp // n_cls
    cls = jnp.minimum((2 * (i + 1) + per_cls - 1) // per_cls, n_cls) - 1
    for k in range(n_cls):
        @pl.when(cls == k)
        def _(k=k):
            _compressed_and_select(q, kc_ref, vct_ref, qpos_w, qpos_q, ocmp_s, score_s, bias_s, nb, nbp,
                                   (k + 1) * per_cls)

    blk_per_tile = KEY_TILE // SEL_BLOCK

    def slc_tile(t, causal, n_tiles=1):
        keys = n_tiles * KEY_TILE
        k0 = pl.multiple_of(t * KEY_TILE, KEY_TILE)
        kt = ks_ref[pl.ds(k0, keys), :]
        vts = vst_ref[pl.ds(t, n_tiles)]
        vt = vts[0] if n_tiles == 1 else jnp.concatenate([vts[u] for u in range(n_tiles)], axis=1)
        ss = []
        for g in range(KV_HEADS):
            rows = bias_s[g, pl.ds(pl.multiple_of(t * blk_per_tile, blk_per_tile), n_tiles * blk_per_tile), :]
            s = _nt(_kchunk(kt, g), qzr_s[g]) + _block_bias(rows, n_tiles * blk_per_tile)
            if causal:
                s = jnp.where(k0 + lax.broadcasted_iota(I32, (keys, w), 0) <= qpos_w, s, NEG)
            ss.append(s)
        _flash_groups(ss, vt, m_s, acc_s)

    t_diag = s0 // KEY_TILE

    def tile_quad(t4, carry):
        slc_tile(4 * t4, False, 4)
        return carry

    lax.fori_loop(0, t_diag // 4, tile_quad, 0)

    def tile_one(t, carry):
        slc_tile(t, False)
        return carry

    lax.fori_loop(4 * (t_diag // 4), t_diag, tile_one, 0)

    slc_tile(t_diag, True)

    n_wt = (WINDOW + nq) // PAGE
    t0 = jnp.maximum(i - WINDOW // PAGE, 0)
    k0 = pl.multiple_of(t0 * PAGE, PAGE)
    kwin = kw_ref[pl.ds(k0, n_wt * PAGE), :]
    vwin_tiles = vwt_ref[pl.ds(t0, n_wt)]
    vwin = jnp.concatenate([vwin_tiles[u] for u in range(n_wt)], axis=1)
    kpos = k0 + lax.broadcasted_iota(I32, (n_wt * PAGE, w), 0)
    wmask = jnp.logical_and(kpos <= qpos_w, qpos_w - kpos < WINDOW)
    ws = [jnp.where(wmask, _nt(_kchunk(kwin, g), qzr_s[g]), NEG) for g in range(KV_HEADS)]
    _finish(o_ref, acc_s, ocmp_s, _one_shot_groups(ws, vwin), ot_s, gates_t)


def _attn_prompt(q, gate_pre, b_gate, cos, sin, kc, vct, ks, vst, kw, vwt, batch, seq, name):
    assert seq % KEY_TILE == 0 and seq >= WINDOW + Q_TILE and Q_TILE == LANES
    m, d = q.shape
    nq = Q_TILE
    w = HPG * nq
    nqb = seq // nq
    nb = seq // SEL_BLOCK
    nbp = kc.shape[1] // 4
    gw = KV_HEADS * HEAD_DIM
    ntile = seq // KEY_TILE
    row_map = lambda b, i: (b * nqb + i, 0)
    per_b2 = lambda b, i: (b, 0)
    per_b3 = lambda b, i: (b, 0, 0)
    return pl.pallas_call(
        functools.partial(_attn_prompt_kernel, nb=nb, nbp=nbp), grid=(batch, nqb),
        in_specs=[pl.BlockSpec((nq, d), row_map), pl.BlockSpec((nq, LANES), row_map),
                  pl.BlockSpec((1, LANES), lambda b, i: (0, 0)),
                  pl.BlockSpec((nq, LANES), lambda b, i: (i, 0)), pl.BlockSpec((nq, LANES), lambda b, i: (i, 0)),
                  pl.BlockSpec((None, 4 * nbp, gw), per_b3), pl.BlockSpec((None, gw, 4 * nbp), per_b3),
                  pl.BlockSpec((seq, gw), per_b2, pipeline_mode=pl.Buffered(1)),
                  pl.BlockSpec((ntile, gw, KEY_TILE), per_b3, pipeline_mode=pl.Buffered(1)),
                  pl.BlockSpec((seq, gw), per_b2, pipeline_mode=pl.Buffered(1)),
                  pl.BlockSpec((seq // PAGE, gw, PAGE), per_b3, pipeline_mode=pl.Buffered(1))],
        out_specs=pl.BlockSpec((nq, d), row_map),
        out_shape=jax.ShapeDtypeStruct((m, d), F32),
        scratch_shapes=[pltpu.VMEM((KV_HEADS, 1, w), F32), pltpu.VMEM((KV_HEADS, ACC_ROWS, w), F32),
                        pltpu.VMEM((nbp, KV_HEADS * LANES), F32), pltpu.VMEM((KV_HEADS, nbp, w), F32),
                        pltpu.VMEM((KV_HEADS, HEAD_DIM, w), F32), pltpu.VMEM((gw, w), F32),
                        pltpu.VMEM((KV_HEADS, w, LANES), BF16)],
        compiler_params=_cparams(("parallel", "arbitrary")), name=name,
    )(q, gate_pre, b_gate, cos, sin, kc, vct, ks, vst, kw, vwt)


def _pad_rows(x, n):
    return jnp.concatenate([x, jnp.zeros((n - x.shape[0], x.shape[1]), x.dtype)], axis=0)


def _sample_queries(q):
    n = q.shape[0]
    half = lax.broadcasted_iota(I32, (n, LANES), 1) >> 6
    zeros = jnp.zeros((n, LANES), F32)
    blocks = []
    for g in range(KV_HEADS):
        for hh in range(HPG):
            h = HPG * g + hh
            chunk = q[:, (h // 2) * LANES:(h // 2 + 1) * LANES]
            if h % 2 != g % 2:
                chunk = pltpu.roll(chunk, HEAD_DIM, axis=1)
            blk = jnp.where(half == (g % 2), chunk, 0.0)
            blocks.append(jnp.concatenate([blk, zeros] if g < 2 else [zeros, blk], axis=1))
    return (jnp.concatenate(blocks, axis=0) * QSCALE).astype(BF16)


def _with_ones(vt):
    return jnp.concatenate([vt, jnp.ones((SUM_ROWS, vt.shape[1]), BF16)], axis=0)


def _attn_sample_kernel(tbl_ref, *refs, n_pg, n_steps, nb, nbp, pos0, n_new):
    k_refs = refs[:n_pg]
    vt_refs = refs[n_pg:2 * n_pg]
    (q_ref, gp_ref, bg_ref, cos_ref, sin_ref, kc_ref, vct_ref, kvn_ref, cwin_ref, wnew_ref,
     o_ref, m_s, acc_s, bias_s, ocmp_s, qzr_s, gt_s) = refs[2 * n_pg:]
    step = pl.program_id(1)
    gw = KV_HEADS * HEAD_DIM
    per_g = HPG * n_new
    lane1 = lax.broadcasted_iota(I32, (1, LANES), 1)
    qpos = pos0 + (lane1 & (n_new - 1))

    @pl.when(step == 0)
    def _():
        q = q_ref[...]
        qzr_s[...] = _sample_queries(_rope(q, cos_ref[...], sin_ref[...]))
        gt_s[...] = _pad_rows(jax.nn.sigmoid(gp_ref[...] + bg_ref[...]), LANES).T
        m_s[...] = jnp.full(m_s.shape, NEG, F32)
        acc_s[...] = jnp.zeros(acc_s.shape, F32)
        valid = _cmp_valid(qpos, nb, nbp)
        sc = jnp.where(valid, _nt(kc_ref[...], _sample_queries(q)), NEG)
        e = jnp.exp2(sc - jnp.max(sc, axis=0, keepdims=True))
        some = (qpos >= CMP_BLOCK - 1).astype(F32)
        p = e * (some / jnp.maximum(jnp.sum(e, axis=0, keepdims=True), 1e-30))
        ocmp_s[...] = _dot(vct_ref[...], p.astype(BF16))
        in_strip = lax.broadcasted_iota(I32, p.shape, 1) & (per_g - 1)
        ps = p
        for k in range(1, HPG):
            sh = k * n_new
            ps = ps + jnp.where(in_strip >= sh, pltpu.roll(p, sh, axis=1), pltpu.roll(p, LANES - per_g + sh, axis=1))
        parts = [ps[c * nbp:(c + 1) * nbp] for c in range(4)]
        n_q = lax.broadcasted_iota(I32, (nbp, LANES), 0)
        prev = jnp.where(n_q >= 1, pltpu.roll(parts[3], 1, axis=0), 0.0)
        score = (((parts[0] + parts[1]) + parts[2]) + parts[3]) + prev
        cur = qpos >> 6
        forced = (n_q == 0) | (n_q == cur) | (n_q == cur - 1)
        score = jnp.where(forced, BIG, jnp.where(n_q * SEL_BLOCK <= qpos, score, -BIG))
        score = jnp.where(n_q < nb, score, -jnp.inf)
        bias_s[...] = _topk_bias(score, min(N_SELECT, nb))

    def flash(k_rows, vt, s_bias):
        s = _nt(k_rows, qzr_s[...]) + s_bias
        m_old = m_s[...]
        m_new = jnp.maximum(m_old, jnp.max(s, axis=0, keepdims=True))
        acc_s[...] = acc_s[...] * jnp.exp2(m_old - m_new) + _dot(_with_ones(vt), jnp.exp2(s - m_new).astype(BF16))
        m_s[...] = m_new

    blk_pp = PAGE // SEL_BLOCK
    blk_ps = n_pg * blk_pp
    k_all = jnp.concatenate(
        [jnp.concatenate([r[cp * LANES:(cp + 1) * LANES, :].T for r in k_refs], axis=0) for cp in range(gw // LANES)],
        axis=1).astype(BF16)
    vt = jnp.concatenate([r[...] for r in vt_refs], axis=1).astype(BF16)
    flash(k_all, vt, _block_bias(bias_s[pl.ds(pl.multiple_of(step * blk_ps, blk_ps), blk_ps), :], blk_ps))

    @pl.when(step == n_steps - 1)
    def _():
        krow = lax.broadcasted_iota(I32, (PAGE, LANES), 0)
        kvn = _pad_rows(kvn_ref[...], PAGE)
        nb0 = pos0 // SEL_BLOCK
        tail_bias = jnp.where(pos0 + krow <= qpos, _block_bias(bias_s[nb0:nb0 + blk_pp, :], blk_pp), NEG)
        flash(kvn[:, 2 * gw:3 * gw].astype(BF16), kvn[:, 3 * gw:4 * gw].T.astype(BF16), tail_bias)

        n_cached = cwin_ref.shape[2]
        wn = _pad_rows(wnew_ref[...], PAGE)
        kwin = jnp.concatenate([wn[:, 0:gw], cwin_ref[0].T], axis=0).astype(BF16)
        vwin = jnp.concatenate([wn[:, gw:2 * gw].T, cwin_ref[1]], axis=1).astype(BF16)
        kpos = jnp.concatenate([pos0 + krow, pos0 - n_cached + lax.broadcasted_iota(I32, (n_cached, LANES), 0)], axis=0)
        wmask = jnp.logical_and(jnp.logical_and(kpos <= qpos, qpos - kpos < WINDOW), kpos >= 0)
        ws = jnp.where(wmask, _nt(kwin, qzr_s[...]), NEG)
        wacc = _dot(_with_ones(vwin), jnp.exp2(ws - jnp.max(ws, axis=0, keepdims=True)).astype(BF16))

        gates_t = gt_s[...]
        grows = []
        for c in range(3):
            row = jnp.zeros((1, LANES), F32)
            for g in range(KV_HEADS):
                for hh in range(HPG):
                    col = (c * HPG + hh) * KV_HEADS + g
                    off = g * per_g + hh * n_new
                    src = gates_t[col:col + 1, :]
                    row = jnp.where((lane1 >= off) & (lane1 < off + n_new), src if off == 0 else pltpu.roll(src, off, axis=1), row)
            grows.append(row)
        acc = acc_s[...]
        o = (ocmp_s[...] * grows[0] + acc[0:gw] * (grows[1] / acc[gw:gw + 1])
             + wacc[0:gw] * (grows[2] / wacc[gw:gw + 1]))
        o_t = o.T
        for g in range(KV_HEADS):
            for hh in range(HPG):
                r0 = g * per_g + hh * n_new
                c0 = hh * gw + g * HEAD_DIM
                o_ref[:, c0:c0 + HEAD_DIM] = o_t[r0:r0 + n_new, g * HEAD_DIM:(g + 1) * HEAD_DIM]


def _attn_sample(q, gate_pre, b_gate, cos, sin, kc, vct, pages, table, kv_new, cache_win_t, win_new,
                 batch, n_new, n_pages, name):
    m, d = q.shape
    assert N_HEADS * n_new == LANES and n_new % SUBLANES == 0
    n_pg = ATT_PAGES
    assert n_pages % n_pg == 0
    n_steps = n_pages // n_pg
    pos0 = n_pages * PAGE
    nb = (pos0 + n_new + SEL_BLOCK - 1) // SEL_BLOCK
    nbp = kc.shape[1] // 4
    gw = KV_HEADS * HEAD_DIM
    n_cached = cache_win_t.shape[3]
    assert n_cached % PAGE == 0

    def page_map(u, branch):
        return lambda b, s, tbl: (tbl[b * n_pages + s * n_pg + u], branch, 0, 0)

    row_map = lambda b, s, tbl: (b, 0)
    per_b3 = lambda b, s, tbl: (b, 0, 0)
    const2 = lambda b, s, tbl: (0, 0)
    in_specs = [pl.BlockSpec((None, None, gw, PAGE), page_map(u, 2)) for u in range(n_pg)]
    in_specs += [pl.BlockSpec((None, None, gw, PAGE), page_map(u, 3)) for u in range(n_pg)]
    in_specs += [pl.BlockSpec((n_new, d), row_map), pl.BlockSpec((n_new, LANES), row_map),
                 pl.BlockSpec((1, LANES), const2), pl.BlockSpec((n_new, LANES), const2),
                 pl.BlockSpec((n_new, LANES), const2),
                 pl.BlockSpec((None, 4 * nbp, gw), per_b3), pl.BlockSpec((None, gw, 4 * nbp), per_b3),
                 pl.BlockSpec((n_new, 4 * gw), row_map),
                 pl.BlockSpec((None, 2, gw, n_cached), lambda b, s, tbl: (b, 0, 0, 0)),
                 pl.BlockSpec((n_new, 2 * gw), row_map)]
    grid_spec = pltpu.PrefetchScalarGridSpec(
        num_scalar_prefetch=1, grid=(batch, n_steps), in_specs=in_specs,
        out_specs=pl.BlockSpec((n_new, d), row_map),
        scratch_shapes=[pltpu.VMEM((1, LANES), F32), pltpu.VMEM((gw + SUM_ROWS, LANES), F32),
                        pltpu.VMEM((nbp, LANES), F32), pltpu.VMEM((gw, LANES), F32),
                        pltpu.VMEM((LANES, gw), BF16), pltpu.VMEM((LANES, LANES), F32)])
    return pl.pallas_call(
        functools.partial(_attn_sample_kernel, n_pg=n_pg, n_steps=n_steps, nb=nb, nbp=nbp, pos0=pos0, n_new=n_new),
        grid_spec=grid_spec, out_shape=jax.ShapeDtypeStruct((m, d), F32),
        compiler_params=_cparams(("parallel", "arbitrary")), name=name,
    )(table, *([pages] * 2 * n_pg), q, gate_pre, b_gate, cos, sin, kc, vct, kv_new, cache_win_t, win_new)


def _prep_weights(w_a_in, b_a_gate, w_a_out, w_kv, cmp_pos, cmp_w1, cmp_b1, cmp_w2, cmp_b2,
                  w_b_in, b_b_gate, w_b_out, w_ffn_up, w_ffn_down):
    a_q = M_HEADS * M_QK_DIM
    a_v = M_HEADS * M_V_DIM
    n_g = 2 * M_HEADS
    g0 = 2 * a_q + a_v
    gw = KV_HEADS * HEAD_DIM
    qd = N_HEADS * HEAD_DIM
    p = {}
    p["a_main"] = jnp.concatenate([w_a_in[:, :, :g0], w_a_in[:, :, g0 + n_g:]], axis=-1).astype(BF16)
    p["a_qvo"] = jnp.concatenate([w_a_in[:, :, :a_q], w_a_in[:, :, 2 * a_q:g0], w_a_in[:, :, g0 + n_g:]],
                                 axis=-1).astype(BF16)
    p["a_kt"] = w_a_in[:, :, a_q:2 * a_q].transpose(0, 2, 1).astype(BF16)
    p["a_gate"] = jnp.pad(w_a_in[:, :, g0:g0 + n_g], ((0, 0), (0, 0), (0, LANES - n_g))).astype(BF16)
    p["a_bgate"] = jnp.pad(b_a_gate, ((0, 0), (0, LANES - n_g)))[:, None, :]
    p["a_out"] = w_a_out.astype(BF16)
    p["kv"] = w_kv.astype(BF16)
    p["kv_t"] = w_kv.T.astype(BF16)
    p["kv_k"] = jnp.concatenate([w_kv[:, 2 * gw:3 * gw], w_kv[:, 4 * gw:5 * gw]], axis=1).astype(BF16)
    hh, g, c = jnp.meshgrid(jnp.arange(HPG), jnp.arange(KV_HEADS), jnp.arange(3), indexing="ij")
    old_col = ((HPG * g + hh) * 3 + c)
    new_col = ((c * HPG + hh) * KV_HEADS + g)
    order = jnp.zeros((3 * N_HEADS,), I32).at[new_col.reshape(-1)].set(old_col.reshape(-1))
    p["b_q"] = w_b_in[:, :, :qd].astype(BF16)
    p["b_gate"] = jnp.pad(w_b_in[:, :, qd:][:, :, order], ((0, 0), (0, 0), (0, LANES - 3 * N_HEADS))).astype(BF16)
    p["b_bgate"] = jnp.pad(b_b_gate[:, order], ((0, 0), (0, LANES - 3 * N_HEADS)))[:, None, :]
    wo = w_b_out.reshape(w_b_out.shape[0], KV_HEADS, HPG, HEAD_DIM, D_MODEL).transpose(0, 2, 1, 3, 4)
    p["b_out"] = wo.reshape(w_b_out.shape[0], qd, D_MODEL).astype(BF16)
    p["up"] = w_ffn_up.astype(BF16)
    p["down"] = w_ffn_down.astype(BF16)
    flat = CMP_STRIDE * HEAD_DIM
    p["cmp"] = {
        "pos_lo": cmp_pos[:, :CMP_STRIDE].reshape(2, 1, flat), "pos_hi": cmp_pos[:, CMP_STRIDE:].reshape(2, 1, flat),
        "w1lo": cmp_w1[:, :flat].astype(BF16), "w1hi": cmp_w1[:, flat:].astype(BF16),
        "b1": cmp_b1[:, None, :], "w2": cmp_w2.astype(BF16), "b2": cmp_b2[:, None, :]}
    return p


def _trunk(x3, pos0, past, p, g_norms, g_a_hnorm, g_kv, ffn_conv_w, ffn_conv_b, conv_prev, m_c, m_n, m_m, tag):
    batch, seq, d = x3.shape
    x = x3.reshape(batch * seq, d)
    cs, ns, ms, convs = [], [], [], []
    gw = KV_HEADS * HEAD_DIM
    for layer in range(DEPTH):
        g = g_norms[layer]
        nm = f"{tag}{layer}"
        if layer < N_A_LAYERS:
            if seq % MLSTM_CHUNK == 0:
                main, gate, kt = _norm_proj(x, g[0], [p["a_qvo"][layer], p["a_gate"][layer]], nm + "_in",
                                            wts=[p["a_kt"][layer]])
            else:
                main, gate = _norm_proj(x, g[0], [p["a_main"][layer], p["a_gate"][layer]], nm + "_in")
                kt = None
            h, c_new, n_new, m_new = _mlstm(main, gate, kt, p["a_bgate"][layer], g_a_hnorm[layer],
                                            m_c[layer], m_n[layer], m_m[layer], batch, seq, nm + "_mlstm")
            cs.append(c_new)
            ns.append(n_new)
            ms.append(m_new)
            mix = (h, p["a_out"][layer], g[1])
        else:
            if layer == N_A_LAYERS:
                cos, sin = _rope_tables(pos0, seq)
                zeros_hi = jnp.zeros((batch, 2 * KV_HEADS, CMP_HIDDEN), F32)
                if past is None:
                    kvt, wint, ks_b, kw_b, vst, vwt = _kv_rows_minor(x, g_kv, p["kv_t"], p["kv_k"], cos, sin,
                                                                     batch, seq, tag + "_kv")
                    kv_out = kvt.reshape(batch, 4, KV_HEADS, HEAD_DIM, seq).transpose(0, 4, 1, 2, 3)
                    win_out = wint.reshape(batch, 2, KV_HEADS, HEAD_DIM, seq).transpose(0, 4, 1, 2, 3)
                    n_pages = seq // PAGE
                    nb = seq // SEL_BLOCK
                    kc, vc, _ = _compress(kvt.reshape(batch, 4, gw, seq), None, batch, n_pages,
                                          min(CMP_PAGES, n_pages), zeros_hi, p["cmp"], None, tag + "_cmp")
                else:
                    kv4, win = _kv_rows(x, g_kv, p["kv"], jnp.tile(cos, (batch, 1)), jnp.tile(sin, (batch, 1)),
                                        tag + "_kv")
                    kv_out = kv4.reshape(batch, seq, 4, KV_HEADS, HEAD_DIM)
                    win_out = win.reshape(batch, seq, 2, KV_HEADS, HEAD_DIM)
                    pages, table, n_pages, cache_win_t = past
                    nb = (pos0 + seq + SEL_BLOCK - 1) // SEL_BLOCK
                    n_tail = nb * (SEL_BLOCK // CMP_STRIDE) - n_pages * (PAGE // CMP_STRIDE)
                    tail = jnp.pad(kv4[:, :2 * gw].reshape(batch, seq, 2, KV_HEADS, HEAD_DIM),
                                   ((0, 0), (0, PAGE - seq), (0, 0), (0, 0), (0, 0)))
                    kc_t, vc_t, hi_t = _compress(_token_minor(tail), jnp.arange(batch, dtype=I32), batch, 1, 1, zeros_hi,
                                                 p["cmp"], n_tail - 1, tag + "_cmpt")
                    kc_m, vc_m, _ = _compress(pages, table, batch, n_pages, min(CMP_PAGES, n_pages), hi_t,
                                              p["cmp"], None, tag + "_cmp")
                    kc = jnp.concatenate([kc_m, kc_t[:, :n_tail]], axis=1)
                    vc = jnp.concatenate([vc_m, vc_t[:, :n_tail]], axis=1)
                nbp = -(-nb // 32) * 32
                kc_l = _cmp_layout(kc, nb, nbp)
                vct_l = _cmp_layout(vc, nb, nbp).transpose(0, 2, 1)
            bl = layer - N_A_LAYERS
            q_raw, gate_pre = _norm_proj(x, g[0], [p["b_q"][bl], p["b_gate"][bl]], nm + "_in")
            if past is None:
                o = _attn_prompt(q_raw, gate_pre, p["b_bgate"][bl], cos, sin, kc_l, vct_l, ks_b, vst, kw_b, vwt,
                                 batch, seq, nm + "_attn")
            else:
                o = _attn_sample(q_raw, gate_pre, p["b_bgate"][bl], cos, sin, kc_l, vct_l, pages, table,
                                 kv4, cache_win_t, win, batch, seq, n_pages, nm + "_attn")
            mix = (o, p["b_out"][bl], g[1])
        if seq < 256:
            x = _proj_norm_res(*mix[:2], mix[2], x, nm + "_out")
            mix = None
        x, conv_new = _ffn(x, g[2], p["up"][layer], ffn_conv_w[layer], ffn_conv_b[layer], p["down"][layer], g[3],
                           conv_prev[layer], batch, seq, nm + "_ffn", mixer=mix)
        convs.append(conv_new)
    return (x.reshape(batch, seq, d), kv_out, win_out, jnp.stack(cs), jnp.stack(ns), jnp.stack(ms), jnp.stack(convs))


def _past_views(cache_kv, cache_win_kv, page_table):
    return (_token_minor(cache_kv), page_table.reshape(-1), page_table.shape[1], _token_minor(cache_win_kv))


def kernel(x_prompt, x_sample, cache_kv, cache_win_kv, state_mlstm_C, state_mlstm_n, state_mlstm_m, state_conv,
           page_table, g_norms, w_a_in, b_a_gate, g_a_hnorm, w_a_out, g_kv, w_kv, cmp_pos, cmp_w1, cmp_b1, cmp_w2,
           cmp_b2, w_b_in, b_b_gate, w_b_out, w_ffn_up, ffn_conv_w, ffn_conv_b, w_ffn_down):
    p = _prep_weights(w_a_in, b_a_gate, w_a_out, w_kv, cmp_pos, cmp_w1, cmp_b1, cmp_w2, cmp_b2,
                      w_b_in, b_b_gate, w_b_out, w_ffn_up, w_ffn_down)
    dt = x_prompt.dtype
    bp, tp, _ = x_prompt.shape
    bs, ts, _ = x_sample.shape
    past_len = page_table.shape[1] * PAGE
    gw = KV_HEADS * HEAD_DIM
    shared = (p, g_norms, g_a_hnorm, g_kv, ffn_conv_w, ffn_conv_b)

    y_p, kv_p, win_p, c_p, n_p, m_p, conv_p = _trunk(
        x_prompt, 0, None, *shared,
        jnp.zeros((DEPTH, bp, CONV_W - 1, 2 * D_FF), dt),
        jnp.zeros((N_A_LAYERS, bp, M_HEADS, M_QK_DIM, M_V_DIM), dt),
        jnp.zeros((N_A_LAYERS, bp, M_HEADS, M_QK_DIM), dt), jnp.zeros((N_A_LAYERS, bp, M_HEADS), dt), "p")
    win_p = win_p[:, tp - min(WINDOW, tp):]

    past = _past_views(cache_kv, cache_win_kv, page_table)
    y_s, kv_s, win_s, c_s, n_s, m_s, conv_s = _trunk(
        x_sample, past_len, past, *shared, state_conv, state_mlstm_C, state_mlstm_n, state_mlstm_m, "s")
    win_all = jnp.concatenate([cache_win_kv, win_s], axis=1)
    win_s = win_all[:, win_all.shape[1] - min(WINDOW, win_all.shape[1]):]
    return (y_p, y_s, kv_p, kv_s, win_p, win_s, c_p, c_s, n_p, n_s, m_p, m_s, conv_p, conv_s)
```
